```python
import math
import jax, jax.numpy as jnp
from jax import lax
import numpy as np

D_MODEL = 1024
BATCH = 8
SEQ = 2048
DEPTH = 1
DEC_BATCH = 128
DEC_SEQ = 8
PAST_LEN = 16384
PAGE_SIZE = 128

D_MIX = D_MODEL
POOL_WIDTH = D_MIX // 2
POOL_WINDOWS = (2, 4, 8, 16)
POOL_GROUPS = len(POOL_WINDOWS)
POOL_GROUP_DIM = POOL_WIDTH // POOL_GROUPS
POOL_BUF = max(POOL_WINDOWS) - 1
GLA_WIDTH = D_MIX - POOL_WIDTH
GLA_HEADS = 4
GLA_DV = GLA_WIDTH // GLA_HEADS
GLA_DK = GLA_DV // 2
GLA_KEY_WIDTH = GLA_HEADS * GLA_DK
GLA_RANK = 16
GLA_TAU = 16.0
GLA_CHUNK = 64
LN_EPS = 1e-5
RMS_EPS = 1e-6
DEEPNORM_ALPHA = (2.0 * DEPTH) ** 0.25
DEEPNORM_BETA = (8.0 * DEPTH) ** -0.25
SPLITS = (POOL_WIDTH, POOL_WIDTH, GLA_KEY_WIDTH, GLA_KEY_WIDTH, GLA_WIDTH, GLA_WIDTH, GLA_RANK)
D_IN = sum(SPLITS)
SPLIT_IDX = [int(i) for i in np.cumsum(SPLITS)[:-1]]

kernel_name = "hybrid_pool_gla_deepnorm_step"


def pool_mix(u, buf, start_pos, w_map, scale):
    B, T, C = u.shape
    ext = jnp.concatenate([buf.astype(u.dtype), u], axis=1)
    ext32 = ext.astype(jnp.float32)
    cs = jnp.concatenate([jnp.zeros((B, 1, C), jnp.float32), jnp.cumsum(ext32, axis=1)], axis=1)
    top = cs[:, POOL_BUF + 1:POOL_BUF + 1 + T]
    pos = start_pos + jnp.arange(T, dtype=jnp.int32)
    self_rows = ext32[:, POOL_BUF:]
    outs = []
    for g, w in enumerate(POOL_WINDOWS):
        sl = slice(g * POOL_GROUP_DIM, (g + 1) * POOL_GROUP_DIM)
        lo = cs[:, POOL_BUF + 1 - w:POOL_BUF + 1 - w + T, sl]
        cnt = jnp.minimum(pos + 1, w).astype(jnp.float32)[None, :, None]
        outs.append((top[..., sl] - lo) / cnt - self_rows[..., sl])
    pooled = jnp.stack(outs, axis=2)
    mixed = jnp.einsum('btgc,gce->btge', pooled, w_map.astype(jnp.float32)).reshape(B, T, C)
    mixed = mixed * scale.astype(jnp.float32)
    return mixed.astype(u.dtype), ext[:, -POOL_BUF:]


def gla_chunked(q, k, v, log_a, S0):
    B, T, H, DK = q.shape
    DV = v.shape[-1]
    C = GLA_CHUNK if T % GLA_CHUNK == 0 else T
    N = T // C

    def to_chunks(a):
        return a.astype(jnp.float32).reshape(B, N, C, H, a.shape[-1]).transpose(1, 0, 3, 2, 4)

    qc, kc, vc, gc = to_chunks(q), to_chunks(k), to_chunks(v), to_chunks(log_a)
    mask = jnp.tril(jnp.ones((C, C), dtype=bool))[None, None, :, :, None]

    def step(S, inp):
        qb, kb, vb, gb = inp
        b = jnp.cumsum(gb, axis=2)
        inter = jnp.einsum('bhck,bhkv->bhcv', qb * jnp.exp(b), S)
        diff = b[:, :, :, None, :] - b[:, :, None, :, :]
        decay = jnp.exp(jnp.where(mask, diff, -jnp.inf))
        A = jnp.einsum('bhtk,bhsk,bhtsk->bhts', qb, kb, decay)
        intra = jnp.einsum('bhts,bhsv->bhtv', A, vb)
        bC = b[:, :, -1]
        S_new = jnp.exp(bC)[..., None] * S + jnp.einsum('bhsk,bhsv->bhkv', kb * jnp.exp(bC[:, :, None] - b), vb)
        return S_new, inter + intra

    S_fin, o = lax.scan(step, S0.astype(jnp.float32), (qc, kc, vc, gc))
    o = o.transpose(1, 0, 3, 2, 4).reshape(B, T, H, DV)
    return o, S_fin


def layer(x, buf, S, start_pos, w_in, w_map, pool_scale, w_a2, b_a, norm_g, w_out, ln_g, ln_b):
    B, T, _ = x.shape
    proj = jnp.einsum('btd,de->bte', x, w_in)
    u, gp, q, k, v, gg, fa = jnp.split(proj, SPLIT_IDX, axis=-1)
    pool_out, new_buf = pool_mix(u, buf, start_pos, w_map, pool_scale)
    g_logit = jnp.einsum('btr,rk->btk', fa, w_a2) + b_a
    log_a = jax.nn.log_sigmoid(g_logit.astype(jnp.float32)) / GLA_TAU
    qh = q.reshape(B, T, GLA_HEADS, GLA_DK) * (GLA_DK ** -0.5)
    kh = k.reshape(B, T, GLA_HEADS, GLA_DK)
    vh = v.reshape(B, T, GLA_HEADS, GLA_DV)
    o, S_new = gla_chunked(qh, kh, vh, log_a.reshape(B, T, GLA_HEADS, GLA_DK), S)
    o = o * lax.rsqrt(jnp.mean(o * o, axis=-1, keepdims=True) + RMS_EPS)
    o = (o * norm_g.astype(jnp.float32).reshape(GLA_HEADS, GLA_DV)).reshape(B, T, GLA_WIDTH)
    mix = jnp.concatenate([pool_out * jax.nn.silu(gp), o.astype(x.dtype) * jax.nn.silu(gg)], axis=-1)
    y = jnp.einsum('bte,ed->btd', mix, w_out)
    h = (DEEPNORM_ALPHA * x + y).astype(jnp.float32)
    mu = jnp.mean(h, axis=-1, keepdims=True)
    var = jnp.mean(jnp.square(h - mu), axis=-1, keepdims=True)
    out = (h - mu) * lax.rsqrt(var + LN_EPS) * ln_g.astype(jnp.float32) + ln_b.astype(jnp.float32)
    return out.astype(x.dtype), new_buf, S_new.astype(S.dtype)


def setup_inputs(seed: int = 0) -> dict:
    key = jax.random.key(seed)
    ks = jax.random.split(key, 13)
    f32 = jnp.float32
    x_prompt = jax.random.normal(ks[0], (BATCH, SEQ, D_MODEL), f32)
    x_sample = jax.random.normal(ks[1], (DEC_BATCH, DEC_SEQ, D_MODEL), f32)
    state_pool = jax.random.normal(ks[2], (DEPTH, DEC_BATCH, POOL_BUF, POOL_WIDTH), f32)
    state_gla = 0.5 * jax.random.normal(ks[3], (DEPTH, DEC_BATCH, GLA_HEADS, GLA_DK, GLA_DV), f32)
    w_in = jax.random.normal(ks[4], (DEPTH, D_MODEL, D_IN), f32) * D_MODEL ** -0.5
    w_pool_map = jax.random.normal(ks[5], (DEPTH, POOL_GROUPS, POOL_GROUP_DIM, POOL_GROUP_DIM), f32) * POOL_GROUP_DIM ** -0.5
    pool_scale = 1.0 + 0.02 * jax.random.normal(ks[6], (DEPTH, POOL_WIDTH), f32)
    w_gla_a2 = jax.random.normal(ks[7], (DEPTH, GLA_RANK, GLA_KEY_WIDTH), f32) * GLA_RANK ** -0.5
    b_gla_a = 0.01 * jax.random.normal(ks[8], (DEPTH, GLA_KEY_WIDTH), f32)
    gla_norm_g = 1.0 + 0.02 * jax.random.normal(ks[9], (DEPTH, GLA_WIDTH), f32)
    w_out = jax.random.normal(ks[10], (DEPTH, D_MIX, D_MODEL), f32) * (D_MIX ** -0.5) * DEEPNORM_BETA
    ln_g = 1.0 + 0.02 * jax.random.normal(ks[11], (DEPTH, D_MODEL), f32)
    ln_b = 0.02 * jax.random.normal(ks[12], (DEPTH, D_MODEL), f32)
    return {"x_prompt": x_prompt, "x_sample": x_sample, "state_pool": state_pool, "state_gla": state_gla,
            "w_in": w_in, "w_pool_map": w_pool_map, "pool_scale": pool_scale, "w_gla_a2": w_gla_a2,
            "b_gla_a": b_gla_a, "gla_norm_g": gla_norm_g, "w_out": w_out, "ln_g": ln_g, "ln_b": ln_b}


def reference(x_prompt, x_sample, state_pool, state_gla, w_in, w_pool_map, pool_scale, w_gla_a2,
              b_gla_a, gla_norm_g, w_out, ln_g, ln_b):
    hp, hs = x_prompt, x_sample
    pool_p, gla_p, pool_s, gla_s = [], [], [], []
    for l in range(DEPTH):
        params = (w_in[l], w_pool_map[l], pool_scale[l], w_gla_a2[l], b_gla_a[l], gla_norm_g[l],
                  w_out[l], ln_g[l], ln_b[l])
        buf0 = jnp.zeros((BATCH, POOL_BUF, POOL_WIDTH), state_pool.dtype)
        S0 = jnp.zeros((BATCH, GLA_HEADS, GLA_DK, GLA_DV), state_gla.dtype)
        hp, bp, Sp = layer(hp, buf0, S0, 0, *params)
        hs, bs, Ss = layer(hs, state_pool[l], state_gla[l], PAST_LEN, *params)
        pool_p.append(bp)
        gla_p.append(Sp)
        pool_s.append(bs)
        gla_s.append(Ss)
    new_pool_prompt = jnp.stack(pool_p, axis=0)
    new_gla_prompt = jnp.stack(gla_p, axis=0)
    new_pool_sample = jnp.stack(pool_s, axis=0)
    new_gla_sample = jnp.stack(gla_s, axis=0)
    return (hp, hs, new_pool_prompt, new_gla_prompt, new_pool_sample, new_gla_sample)
```

```python
import functools

import jax
import jax.numpy as jnp
from jax import lax
from jax.experimental import pallas as pl
from jax.experimental.pallas import tpu as pltpu

F32 = jnp.float32
BF16 = jnp.bfloat16

D_MODEL = 1024
POOL_WIDTH = 512
POOL_WINDOWS = (2, 4, 8, 16)
POOL_GROUP_DIM = 128
POOL_BUF = 15
HIST_ROWS = 16
GLA_HEADS = 4
GLA_DK = 64
GLA_DV = 128
GLA_KEY_WIDTH = GLA_HEADS * GLA_DK
GLA_WIDTH = GLA_HEADS * GLA_DV
GLA_RANK = 16
GLA_TAU = 16.0
LN_EPS = 1e-5
RMS_EPS = 1e-6
DEPTH = 1
DEEPNORM_ALPHA = (2.0 * DEPTH) ** 0.25

C_U, C_GP, C_Q, C_K, C_V, C_GG, C_GL, C_END = 0, 512, 1024, 1280, 1536, 2048, 2560, 2816

PROMPT_TILE = 256
GLA_CHUNK = 64
SAMPLE_SEQS = 16
VMEM_LIMIT = 48 * 1024 * 1024


def _dot(a, b):
    return jnp.dot(a, b, preferred_element_type=F32)


def _dot_nt(a, b):
    return lax.dot_general(a, b, (((1,), (1,)), ((), ())), preferred_element_type=F32)


def _dot_tn(a, b):
    return lax.dot_general(a, b, (((0,), (0,)), ((), ())), preferred_element_type=F32)


def _silu(x):
    return x * (1.0 / (1.0 + jnp.exp(-x)))


def _log_sigmoid(x):
    return jnp.minimum(x, 0.0) - jnp.log1p(jnp.exp(-jnp.abs(x)))


def _split_bf16(x):
    hi = x.astype(BF16)
    lo = (x - hi.astype(F32)).astype(BF16)
    return hi, lo


def _epilogue(x, pool_act, o, gg, ng_ref, wout_ref, lng_ref, lnb_ref):
    normed = []
    for h in range(GLA_HEADS):
        oh = o[:, h * GLA_DV:(h + 1) * GLA_DV]
        ms = jnp.mean(oh * oh, axis=-1, keepdims=True)
        normed.append(oh * lax.rsqrt(ms + RMS_EPS))
    gla_act = jnp.concatenate(normed, axis=1) * ng_ref[...] * _silu(gg)
    mix = jnp.concatenate([pool_act, gla_act], axis=1).astype(BF16)
    y = _dot(mix, wout_ref[...])
    h = DEEPNORM_ALPHA * x + y
    mu = jnp.mean(h, axis=-1, keepdims=True)
    d = h - mu
    var = jnp.mean(d * d, axis=-1, keepdims=True)
    return d * lax.rsqrt(var + LN_EPS) * lng_ref[...] + lnb_ref[...]


def _head_block_mask(rows_per_head, cols_per_head, shape):
    r = lax.broadcasted_iota(jnp.int32, shape, 0) // rows_per_head
    c = lax.broadcasted_iota(jnp.int32, shape, 1) // cols_per_head
    return r == c


def _pair_block_diag(a0, a1):
    z = jnp.zeros_like(a0)
    return jnp.concatenate(
        [jnp.concatenate([a0, z], axis=1), jnp.concatenate([z, a1], axis=1)], axis=0)


def _prompt_body(x_ref, win_ref, wmap_ref, pscale_ref, ba_ref, ng_ref, wout_ref, lng_ref, lnb_ref,
                 y_ref, pool_ref, s_out_ref, ext_ref, s_ref, *, tc, nj):
    j = pl.program_id(1)

    @pl.when(j == 0)
    def _():
        ext_ref[0:HIST_ROWS, :] = jnp.zeros((HIST_ROWS, POOL_WIDTH), F32)
        s_ref[...] = jnp.zeros_like(s_ref)

    x = x_ref[0]
    xb = x.astype(BF16)

    def proj(lo, hi):
        return _dot(xb, win_ref[:, lo:hi])

    u = proj(C_U, C_GP)
    ext_ref[HIST_ROWS:HIST_ROWS + tc, :] = u
    pos16 = j * tc + lax.broadcasted_iota(jnp.int32, (HIST_ROWS, POOL_GROUP_DIM), 0)
    mixed = []
    for g, w in enumerate(POOL_WINDOWS):
        sl = slice(g * POOL_GROUP_DIM, (g + 1) * POOL_GROUP_DIM)
        e = ext_ref[:, sl]
        s = e
        sh = 1
        while sh < w:
            s = s + pltpu.roll(s, sh, 0)
            sh *= 2
        wsum = s[HIST_ROWS:]
        self_rows = e[HIST_ROWS:]
        cnt = jnp.minimum(pos16 + 1, w).astype(F32)
        first = wsum[:HIST_ROWS] / cnt - self_rows[:HIST_ROWS]
        rest = wsum[HIST_ROWS:] * (1.0 / w) - self_rows[HIST_ROWS:]
        pooled = jnp.concatenate([first, rest], axis=0)
        mixed.append(_dot(pooled.astype(BF16), wmap_ref[g]))
    gp = proj(C_GP, C_Q)
    pool_act = jnp.concatenate(mixed, axis=1) * pscale_ref[...] * _silu(gp)

    @pl.when(j == nj - 1)
    def _():
        pool_ref[0] = ext_ref[tc + 1:tc + HIST_ROWS, :]

    ext_ref[0:HIST_ROWS, :] = ext_ref[tc:tc + HIST_ROWS, :]

    q = proj(C_Q, C_K) * (GLA_DK ** -0.5)
    k = proj(C_K, C_V)
    vb = proj(C_V, C_GG).astype(BF16)
    gg = proj(C_GG, C_GL)
    la = _log_sigmoid(proj(C_GL, C_END) + ba_ref[...]) * (1.0 / GLA_TAU)

    cc = GLA_CHUNK
    rr = lax.broadcasted_iota(jnp.int32, (cc, cc), 0)
    cl = lax.broadcasted_iota(jnp.int32, (cc, cc), 1)
    ltri = jnp.where(cl <= rr, 1.0, 0.0).astype(BF16)
    causal = (lax.broadcasted_iota(jnp.int32, (cc, GLA_HEADS * cc), 1) % cc
              <= lax.broadcasted_iota(jnp.int32, (cc, GLA_HEADS * cc), 0))
    blk = _head_block_mask(cc, GLA_DK, (GLA_HEADS * cc, GLA_KEY_WIDTH))

    o_chunks = []
    for c in range(tc // cc):
        rs = slice(c * cc, (c + 1) * cc)
        hi, lo = _split_bf16(la[rs])
        b = _dot(ltri, hi) + _dot(ltri, lo)
        b_end = b[cc - 1:cc, :]
        qs = (q[rs] * jnp.exp(b)).astype(BF16)
        ks = (k[rs] * jnp.exp(-b)).astype(BF16)
        kd = (k[rs] * jnp.exp(b_end - b)).astype(BF16)
        ks_heads = jnp.where(blk, jnp.concatenate([ks] * GLA_HEADS, axis=0), jnp.zeros((), BF16))
        a_all = jnp.where(causal, _dot_nt(qs, ks_heads), 0.0).astype(BF16)
        s_old = s_ref[...]
        s_bf = s_old.astype(BF16)
        vc = vb[rs]
        outs, upds = [], []
        for p in range(GLA_HEADS // 2):
            h0, h1 = 2 * p, 2 * p + 1
            lhs = jnp.concatenate([a_all[:, p * 2 * cc:(p + 1) * 2 * cc],
                                   qs[:, p * 2 * GLA_DK:(p + 1) * 2 * GLA_DK]], axis=1)
            rhs = jnp.concatenate([
                _pair_block_diag(vc[:, h0 * GLA_DV:(h0 + 1) * GLA_DV], vc[:, h1 * GLA_DV:(h1 + 1) * GLA_DV]),
                _pair_block_diag(s_bf[h0 * GLA_DK:(h0 + 1) * GLA_DK], s_bf[h1 * GLA_DK:(h1 + 1) * GLA_DK]),
            ], axis=0)
            outs.append(_dot(lhs, rhs))
            upd = _dot_tn(kd[:, p * 2 * GLA_DK:(p + 1) * 2 * GLA_DK], vc[:, p * 2 * GLA_DV:(p + 1) * 2 * GLA_DV])
            upds.append(upd[0:GLA_DK, 0:GLA_DV])
            upds.append(upd[GLA_DK:2 * GLA_DK, GLA_DV:2 * GLA_DV])
        o_chunks.append(jnp.concatenate(outs, axis=1))
        dcol = jnp.transpose(jnp.broadcast_to(jnp.exp(b_end), (GLA_DV, GLA_KEY_WIDTH)))
        s_ref[...] = s_old * dcol + jnp.concatenate(upds, axis=0)
    o = jnp.concatenate(o_chunks, axis=0)

    @pl.when(j == nj - 1)
    def _():
        s_out_ref[0] = s_ref[...]

    y_ref[0] = _epilogue(x, pool_act, o, gg, ng_ref, wout_ref, lng_ref, lnb_ref)


def _prompt_call(x, win, wmap, pscale, ba, ng, wout, lng, lnb):
    nb, t, _ = x.shape
    tc = PROMPT_TILE
    nj = t // tc
    const2 = lambda b, j: (0, 0)
    return pl.pallas_call(
        functools.partial(_prompt_body, tc=tc, nj=nj),
        grid=(nb, nj),
        in_specs=[
            pl.BlockSpec((1, tc, D_MODEL), lambda b, j: (b, j, 0)),
            pl.BlockSpec(win.shape, const2),
            pl.BlockSpec(wmap.shape, lambda b, j: (0, 0, 0)),
            pl.BlockSpec(pscale.shape, const2),
            pl.BlockSpec(ba.shape, const2),
            pl.BlockSpec(ng.shape, const2),
            pl.BlockSpec(wout.shape, const2),
            pl.BlockSpec(lng.shape, const2),
            pl.BlockSpec(lnb.shape, const2),
        ],
        out_specs=[
            pl.BlockSpec((1, tc, D_MODEL), lambda b, j: (b, j, 0)),
            pl.BlockSpec((1, POOL_BUF, POOL_WIDTH), lambda b, j: (b, 0, 0)),
            pl.BlockSpec((1, GLA_KEY_WIDTH, GLA_DV), lambda b, j: (b, 0, 0)),
        ],
        out_shape=[
            jax.ShapeDtypeStruct((nb, t, D_MODEL), F32),
            jax.ShapeDtypeStruct((nb, POOL_BUF, POOL_WIDTH), F32),
            jax.ShapeDtypeStruct((nb, GLA_KEY_WIDTH, GLA_DV), F32),
        ],
        scratch_shapes=[
            pltpu.VMEM((HIST_ROWS + tc, POOL_WIDTH), F32),
            pltpu.VMEM((GLA_KEY_WIDTH, GLA_DV), F32),
        ],
        compiler_params=pltpu.CompilerParams(
            dimension_semantics=("arbitrary", "arbitrary"), vmem_limit_bytes=VMEM_LIMIT),
        name="prompt_layer",
    )(x, win, wmap, pscale, ba, ng, wout, lng, lnb)


def _sample_body(x_ref, pool_in_ref, s_in_ref, win_ref, wmap_ref, pscale_ref, ba_ref, ng_ref, wout_ref,
                 lng_ref, lnb_ref, y_ref, pool_ref, s_out_ref, ext_ref, *, bs, t, start_pos):
    rows = bs * t
    x = x_ref[...].reshape(rows, D_MODEL)
    xb = x.astype(BF16)

    def proj(lo, hi):
        return _dot(xb, win_ref[:, lo:hi])

    u = proj(C_U, C_GP)
    u3 = u.reshape(bs, t, POOL_WIDTH)
    ext_ref[:, 1:HIST_ROWS, :] = pool_in_ref[...]
    ext_ref[:, HIST_ROWS:HIST_ROWS + t, :] = u3
    pool_ref[...] = ext_ref[:, t + 1:t + HIST_ROWS, :]
    pos = start_pos + lax.broadcasted_iota(jnp.int32, (bs, t, POOL_GROUP_DIM), 1)
    mixed = []
    for g, w in enumerate(POOL_WINDOWS):
        sl = slice(g * POOL_GROUP_DIM, (g + 1) * POOL_GROUP_DIM)
        acc = u3[:, :, sl]
        for back in range(1, w):
            acc = acc + ext_ref[:, HIST_ROWS - back:HIST_ROWS - back + t, sl]
        cnt = jnp.minimum(pos + 1, w).astype(F32)
        pooled = (acc / cnt - u3[:, :, sl]).reshape(rows, POOL_GROUP_DIM)
        mixed.append(_dot(pooled.astype(BF16), wmap_ref[g]))
    gp = proj(C_GP, C_Q)
    pool_act = jnp.concatenate(mixed, axis=1) * pscale_ref[...] * _silu(gp)

    q = proj(C_Q, C_K) * (GLA_DK ** -0.5)
    k = proj(C_K, C_V)
    vb = proj(C_V, C_GG).astype(BF16)
    gg = proj(C_GG, C_GL)
    la = _log_sigmoid(proj(C_GL, C_END) + ba_ref[...]) * (1.0 / GLA_TAU)

    rr = lax.broadcasted_iota(jnp.int32, (rows, rows), 0)
    cl = lax.broadcasted_iota(jnp.int32, (rows, rows), 1)
    same = (rr // t) == (cl // t)
    ltri = jnp.where(same & (cl <= rr), 1.0, 0.0).astype(BF16)
    lall = jnp.where(same, 1.0, 0.0).astype(BF16)
    hi, lo = _split_bf16(la)
    b = _dot(ltri, hi) + _dot(ltri, lo)
    b_end = _dot(lall, hi) + _dot(lall, lo)
    qs = (q * jnp.exp(b)).astype(BF16)
    ks = (k * jnp.exp(-b)).astype(BF16)
    kd = (k * jnp.exp(b_end - b)).astype(BF16)

    blk = _head_block_mask(rows, GLA_DK, (GLA_HEADS * rows, GLA_KEY_WIDTH))
    ks_heads = jnp.where(blk, jnp.concatenate([ks] * GLA_HEADS, axis=0), jnp.zeros((), BF16))
    ra = lax.broadcasted_iota(jnp.int32, (rows, GLA_HEADS * rows), 0)
    ca = lax.broadcasted_iota(jnp.int32, (rows, GLA_HEADS * rows), 1) % rows
    valid = ((ra // t) == (ca // t)) & (ca <= ra)
    a_all = jnp.where(valid, _dot_nt(qs, ks_heads), 0.0).astype(BF16)

    s_old = s_in_ref[...]
    s_bf = s_old.astype(BF16)
    qs3 = qs.reshape(bs, t, GLA_KEY_WIDTH)
    kd3 = kd.reshape(bs, t, GLA_KEY_WIDTH)
    v3 = vb.reshape(bs, t, GLA_WIDTH)
    hi3 = hi.reshape(bs, t, GLA_KEY_WIDTH)
    lo3 = lo.reshape(bs, t, GLA_KEY_WIDTH)
    ones3 = jnp.ones((bs, t, GLA_DV), BF16)
    bdims_nn = (((2,), (1,)), ((0,), (0,)))
    bdims_tn = (((1,), (1,)), ((0,), (0,)))
    dsum = (lax.dot_general(hi3, ones3, bdims_tn, preferred_element_type=F32)
            + lax.dot_general(lo3, ones3, bdims_tn, preferred_element_type=F32))
    outs, upds = [], []
    for p in range(GLA_HEADS // 2):
        h0, h1 = 2 * p, 2 * p + 1
        intra = _dot(a_all[:, p * 2 * rows:(p + 1) * 2 * rows],
                     _pair_block_diag(vb[:, h0 * GLA_DV:(h0 + 1) * GLA_DV], vb[:, h1 * GLA_DV:(h1 + 1) * GLA_DV]))
        s0 = s_bf[:, h0 * GLA_DK:(h0 + 1) * GLA_DK, :]
        s1 = s_bf[:, h1 * GLA_DK:(h1 + 1) * GLA_DK, :]
        z = jnp.zeros_like(s0)
        s_pair = jnp.concatenate(
            [jnp.concatenate([s0, z], axis=2), jnp.concatenate([z, s1], axis=2)], axis=1)
        inter = lax.dot_general(qs3[:, :, p * 2 * GLA_DK:(p + 1) * 2 * GLA_DK], s_pair, bdims_nn,
                                preferred_element_type=F32)
        outs.append(intra + inter.reshape(rows, 2 * GLA_DV))
        upd = lax.dot_general(kd3[:, :, p * 2 * GLA_DK:(p + 1) * 2 * GLA_DK],
                              v3[:, :, p * 2 * GLA_DV:(p + 1) * 2 * GLA_DV], bdims_tn,
                              preferred_element_type=F32)
        upds.append(upd[:, 0:GLA_DK, 0:GLA_DV])
        upds.append(upd[:, GLA_DK:2 * GLA_DK, GLA_DV:2 * GLA_DV])
    o = jnp.concatenate(outs, axis=1)
    s_out_ref[...] = s_old * jnp.exp(dsum) + jnp.concatenate(upds, axis=1)

    y = _epilogue(x, pool_act, o, gg, ng_ref, wout_ref, lng_ref, lnb_ref)
    y_ref[...] = y.reshape(bs, t, D_MODEL)


def _sample_call(x, pool_in, s_in, win, wmap, pscale, ba, ng, wout, lng, lnb, start_pos):
    nb, t, _ = x.shape
    bs = SAMPLE_SEQS
    const2 = lambda i: (0, 0)
    seq3 = lambda i: (i, 0, 0)
    return pl.pallas_call(
        functools.partial(_sample_body, bs=bs, t=t, start_pos=start_pos),
        grid=(nb // bs,),
        in_specs=[
            pl.BlockSpec((bs, t, D_MODEL), seq3),
            pl.BlockSpec((bs, POOL_BUF, POOL_WIDTH), seq3),
            pl.BlockSpec((bs, GLA_KEY_WIDTH, GLA_DV), seq3),
            pl.BlockSpec(win.shape, const2),
            pl.BlockSpec(wmap.shape, lambda i: (0, 0, 0)),
            pl.BlockSpec(pscale.shape, const2),
            pl.BlockSpec(ba.shape, const2),
            pl.BlockSpec(ng.shape, const2),
            pl.BlockSpec(wout.shape, const2),
            pl.BlockSpec(lng.shape, const2),
            pl.BlockSpec(lnb.shape, const2),
        ],
        out_specs=[
            pl.BlockSpec((bs, t, D_MODEL), seq3),
            pl.BlockSpec((bs, POOL_BUF, POOL_WIDTH), seq3),
            pl.BlockSpec((bs, GLA_KEY_WIDTH, GLA_DV), seq3),
        ],
        out_shape=[
            jax.ShapeDtypeStruct((nb, t, D_MODEL), F32),
            jax.ShapeDtypeStruct((nb, POOL_BUF, POOL_WIDTH), F32),
            jax.ShapeDtypeStruct((nb, GLA_KEY_WIDTH, GLA_DV), F32),
        ],
        scratch_shapes=[pltpu.VMEM((bs, HIST_ROWS + t, POOL_WIDTH), F32)],
        compiler_params=pltpu.CompilerParams(
            dimension_semantics=("arbitrary",), vmem_limit_bytes=VMEM_LIMIT),
        name="sample_layer",
    )(x, pool_in, s_in, win, wmap, pscale, ba, ng, wout, lng, lnb)


def _fold_body(wfa_ref, wa2_ref, out_ref):
    wfa = wfa_ref[...]
    wa2 = wa2_ref[...]
    acc = wfa[:, 0:1] * wa2[0:1, :]
    for r in range(1, GLA_RANK):
        acc = acc + wfa[:, r:r + 1] * wa2[r:r + 1, :]
    out_ref[...] = acc


def _fold_forget_weights(wfa, wa2):
    return pl.pallas_call(
        _fold_body,
        out_shape=jax.ShapeDtypeStruct((D_MODEL, GLA_KEY_WIDTH), F32),
        name="fold_forget_weights",
    )(wfa, wa2)


def kernel(x_prompt, x_sample, state_pool, state_gla, w_in, w_pool_map, pool_scale, w_gla_a2, b_gla_a,
           gla_norm_g, w_out, ln_g, ln_b):
    assert w_in.shape[0] == DEPTH
    past_len = 16384
    w_in0 = w_in[0]
    w_forget = _fold_forget_weights(w_in0[:, C_GL:], w_gla_a2[0])
    win = jnp.concatenate([w_in0[:, :C_GL], w_forget], axis=1).astype(BF16)
    wmap = w_pool_map[0].astype(BF16)
    wout = w_out[0].astype(BF16)
    pscale = pool_scale[0].reshape(1, POOL_WIDTH)
    ba = b_gla_a[0].reshape(1, GLA_KEY_WIDTH)
    ng = gla_norm_g[0].reshape(1, GLA_WIDTH)
    lng = ln_g[0].reshape(1, D_MODEL)
    lnb = ln_b[0].reshape(1, D_MODEL)
    params = (win, wmap, pscale, ba, ng, wout, lng, lnb)

    y_p, pool_p, gla_p = _prompt_call(x_prompt, *params)
    nsamp = x_sample.shape[0]
    y_s, pool_s, gla_s = _sample_call(
        x_sample, state_pool[0], state_gla[0].reshape(nsamp, GLA_KEY_WIDTH, GLA_DV), *params,
        start_pos=past_len)

    nb = x_prompt.shape[0]
    return (y_p, y_s, pool_p[None],
            gla_p.reshape(1, nb, GLA_HEADS, GLA_DK, GLA_DV),
            pool_s[None],
            gla_s.reshape(1, nsamp, GLA_HEADS, GLA_DK, GLA_DV))
```

```python
import functools

import jax
import jax.numpy as jnp
from jax import lax
from jax.experimental import pallas as pl
from jax.experimental.pallas import tpu as pltpu

F32 = jnp.float32
BF16 = jnp.bfloat16

D_MODEL = 1024
POOL_WIDTH = 512
POOL_WINDOWS = (2, 4, 8, 16)
POOL_GROUP_DIM = 128
POOL_BUF = 15
HIST_ROWS = 16
GLA_HEADS = 4
GLA_DK = 64
GLA_DV = 128
GLA_KEY_WIDTH = GLA_HEADS * GLA_DK
GLA_WIDTH = GLA_HEADS * GLA_DV
GLA_RANK = 16
GLA_TAU = 16.0
LN_EPS = 1e-5
RMS_EPS = 1e-6
DEPTH = 1
DEEPNORM_ALPHA = (2.0 * DEPTH) ** 0.25

C_U, C_GP, C_Q, C_K, C_V, C_GG, C_GL, C_END = 0, 512, 1024, 1280, 1536, 2048, 2560, 2816

MXU_COLS = 256
PROMPT_TILE = 256
GLA_CHUNK = 64
SAMPLE_SEQS = 16
VMEM_LIMIT = 48 * 1024 * 1024


def _dot(a, b):
    return jnp.dot(a, b, preferred_element_type=F32)


def _dot_nt(a, b):
    return lax.dot_general(a, b, (((1,), (1,)), ((), ())), preferred_element_type=F32)


def _dot_tn(a, b):
    return lax.dot_general(a, b, (((0,), (0,)), ((), ())), preferred_element_type=F32)


def _silu(x):
    return x * (1.0 / (1.0 + jnp.exp(-x)))


def _log_sigmoid(x):
    return jnp.minimum(x, 0.0) - jnp.log1p(jnp.exp(-jnp.abs(x)))


def _split_bf16(x):
    hi = x.astype(BF16)
    lo = (x - hi.astype(F32)).astype(BF16)
    return hi, lo


def _epilogue(x, pool_act, o, gg, ng_ref, wout_ref, lng_ref, lnb_ref, pump=lambda n: None):
    pump(1)
    normed = []
    for h in range(GLA_HEADS):
        oh = o[:, h * GLA_DV:(h + 1) * GLA_DV]
        ms = jnp.mean(oh * oh, axis=-1, keepdims=True)
        normed.append(oh * lax.rsqrt(ms + RMS_EPS))
    gla_act = jnp.concatenate(normed, axis=1) * ng_ref[...] * _silu(gg)
    mix = jnp.concatenate([pool_act, gla_act], axis=1).astype(BF16)
    y = _dot(mix, wout_ref[...])
    pump(len(range(0, C_END, MXU_COLS)))
    h = DEEPNORM_ALPHA * x + y
    mu = jnp.mean(h, axis=-1, keepdims=True)
    d = h - mu
    var = jnp.mean(d * d, axis=-1, keepdims=True)
    return d * lax.rsqrt(var + LN_EPS) * lng_ref[...] + lnb_ref[...]


def _head_block_mask(rows_per_head, cols_per_head, shape):
    r = lax.broadcasted_iota(jnp.int32, shape, 0) // rows_per_head
    c = lax.broadcasted_iota(jnp.int32, shape, 1) // cols_per_head
    return r == c


def _pair_block_diag(a0, a1):
    z = jnp.zeros_like(a0)
    return jnp.concatenate(
        [jnp.concatenate([a0, z], axis=1), jnp.concatenate([z, a1], axis=1)], axis=0)


def _prompt_stage_a(x_ref, win_ref, proj_ref, xkeep_ref, xb_ref):
    x = x_ref[0]
    xkeep_ref[...] = x
    xb_ref[...] = x.astype(BF16)
    slabs = iter(range(0, C_END, MXU_COLS))

    def pump(n):
        for _ in range(n):
            lo = next(slabs, None)
            if lo is not None:
                proj_ref[:, lo:lo + MXU_COLS] = _dot(xb_ref[...], win_ref[:, lo:lo + MXU_COLS])

    return pump


def _prompt_stage_b(jb, proj_ref, xkeep_ref, wmap_ref, pscale_ref, ba_ref, ng_ref, wout_ref, lng_ref, lnb_ref,
                    y_ref, ext_ref, s_ref, pump, *, tc):
    pump(1)
    u = proj_ref[:, C_U:C_GP]
    ext_ref[HIST_ROWS:HIST_ROWS + tc, :] = u
    pos16 = jb * tc + lax.broadcasted_iota(jnp.int32, (HIST_ROWS, POOL_GROUP_DIM), 0)
    mixed = []
    for g, w in enumerate(POOL_WINDOWS):
        sl = slice(g * POOL_GROUP_DIM, (g + 1) * POOL_GROUP_DIM)
        e = ext_ref[:, sl]
        s = e
        sh = 1
        while sh < w:
            s = s + pltpu.roll(s, sh, 0)
            sh *= 2
        wsum = s[HIST_ROWS:]
        self_rows = e[HIST_ROWS:]
        cnt = jnp.minimum(pos16 + 1, w).astype(F32)
        first = wsum[:HIST_ROWS] / cnt - self_rows[:HIST_ROWS]
        rest = wsum[HIST_ROWS:] * (1.0 / w) - self_rows[HIST_ROWS:]
        pooled = jnp.concatenate([first, rest], axis=0)
        mixed.append(_dot(pooled.astype(BF16), wmap_ref[g]))
        pump(g % 2)
    pool_act = jnp.concatenate(mixed, axis=1) * pscale_ref[...] * _silu(proj_ref[:, C_GP:C_Q])

    la = _log_sigmoid(proj_ref[:, C_GL:C_END] + ba_ref[...]) * (1.0 / GLA_TAU)

    cc = GLA_CHUNK
    rr = lax.broadcasted_iota(jnp.int32, (cc, cc), 0)
    cl = lax.broadcasted_iota(jnp.int32, (cc, cc), 1)
    ltri = jnp.where(cl <= rr, 1.0, 0.0).astype(BF16)
    causal = (lax.broadcasted_iota(jnp.int32, (cc, GLA_HEADS * cc), 1) % cc
              <= lax.broadcasted_iota(jnp.int32, (cc, GLA_HEADS * cc), 0))
    blk = _head_block_mask(cc, GLA_DK, (GLA_HEADS * cc, GLA_KEY_WIDTH))

    o_chunks = []
    for c in range(tc // cc):
        rs = slice(c * cc, (c + 1) * cc)
        hi, lo = _split_bf16(la[rs])
        b = _dot(ltri, hi) + _dot(ltri, lo)
        pump(1)
        b_end = b[cc - 1:cc, :]
        q_c = proj_ref[rs, C_Q:C_K] * (GLA_DK ** -0.5)
        k_c = proj_ref[rs, C_K:C_V]
        qs = (q_c * jnp.exp(b)).astype(BF16)
        ks = (k_c * jnp.exp(-b)).astype(BF16)
        kd = (k_c * jnp.exp(b_end - b)).astype(BF16)
        ks_heads = jnp.where(blk, jnp.concatenate([ks] * GLA_HEADS, axis=0), jnp.zeros((), BF16))
        a_all = jnp.where(causal, _dot_nt(qs, ks_heads), 0.0).astype(BF16)
        s_old = s_ref[...]
        s_bf = s_old.astype(BF16)
        vc = proj_ref[rs, C_V:C_GG].astype(BF16)
        outs, upds = [], []
        for p in range(GLA_HEADS // 2):
            h0, h1 = 2 * p, 2 * p + 1
            lhs = jnp.concatenate([a_all[:, p * 2 * cc:(p + 1) * 2 * cc],
                                   qs[:, p * 2 * GLA_DK:(p + 1) * 2 * GLA_DK]], axis=1)
            rhs = jnp.concatenate([
                _pair_block_diag(vc[:, h0 * GLA_DV:(h0 + 1) * GLA_DV], vc[:, h1 * GLA_DV:(h1 + 1) * GLA_DV]),
                _pair_block_diag(s_bf[h0 * GLA_DK:(h0 + 1) * GLA_DK], s_bf[h1 * GLA_DK:(h1 + 1) * GLA_DK]),
            ], axis=0)
            outs.append(_dot(lhs, rhs))
            upd = _dot_tn(kd[:, p * 2 * GLA_DK:(p + 1) * 2 * GLA_DK], vc[:, p * 2 * GLA_DV:(p + 1) * 2 * GLA_DV])
            upds.append(upd[0:GLA_DK, 0:GLA_DV])
            upds.append(upd[GLA_DK:2 * GLA_DK, GLA_DV:2 * GLA_DV])
        o_chunks.append(jnp.concatenate(outs, axis=1))
        pump(1 - c % 2)
        dcol = jnp.transpose(jnp.broadcast_to(jnp.exp(b_end), (GLA_DV, GLA_KEY_WIDTH)))
        s_ref[...] = s_old * dcol + jnp.concatenate(upds, axis=0)
    o = jnp.concatenate(o_chunks, axis=0)

    y_ref[0] = _epilogue(xkeep_ref[...], pool_act, o, proj_ref[:, C_GG:C_GL], ng_ref, wout_ref, lng_ref, lnb_ref,
                         pump)


def _prompt_body(x_ref, win_ref, wmap_ref, pscale_ref, ba_ref, ng_ref, wout_ref, lng_ref, lnb_ref,
                 y_ref, pool_ref, s_out_ref,
                 proj0_ref, proj1_ref, xkeep0_ref, xkeep1_ref, xb_ref, ext_ref, s_ref, *, tc, nj):
    i = pl.program_id(0)
    jb = (i + nj - 1) % nj

    @pl.when(i == 0)
    def _():
        proj1_ref[...] = jnp.zeros_like(proj1_ref)
        xkeep1_ref[...] = jnp.zeros_like(xkeep1_ref)

    @pl.when((i == 0) | (jb == 0))
    def _():
        ext_ref[0:HIST_ROWS, :] = jnp.zeros((HIST_ROWS, POOL_WIDTH), F32)
        s_ref[...] = jnp.zeros_like(s_ref)

    def step(proj_a, xkeep_a, proj_b, xkeep_b):
        pump = _prompt_stage_a(x_ref, win_ref, proj_a, xkeep_a, xb_ref)
        _prompt_stage_b(jb, proj_b, xkeep_b, wmap_ref, pscale_ref, ba_ref, ng_ref, wout_ref, lng_ref, lnb_ref,
                        y_ref, ext_ref, s_ref, pump, tc=tc)

    @pl.when(i % 2 == 0)
    def _():
        step(proj0_ref, xkeep0_ref, proj1_ref, xkeep1_ref)

    @pl.when(i % 2 == 1)
    def _():
        step(proj1_ref, xkeep1_ref, proj0_ref, xkeep0_ref)

    @pl.when((i > 0) & (jb == nj - 1))
    def _():
        pool_ref[0] = ext_ref[tc + 1:tc + HIST_ROWS, :]
        s_out_ref[0] = s_ref[...]

    ext_ref[0:HIST_ROWS, :] = ext_ref[tc:tc + HIST_ROWS, :]


def _prompt_call(x, win, wmap, pscale, ba, ng, wout, lng, lnb):
    nb, t, _ = x.shape
    tc = PROMPT_TILE
    nj = t // tc
    ntiles = nb * nj
    const2 = lambda i: (0, 0)

    def tile_a(i):
        ia = jnp.minimum(i, ntiles - 1)
        return (ia // nj, ia % nj, 0)

    def tile_b(i):
        ib = jnp.maximum(i - 1, 0)
        return (ib // nj, ib % nj, 0)

    def seq_b(i):
        return (jnp.maximum(i - 1, 0) // nj, 0, 0)

    return pl.pallas_call(
        functools.partial(_prompt_body, tc=tc, nj=nj),
        grid=(ntiles + 1,),
        in_specs=[
            pl.BlockSpec((1, tc, D_MODEL), tile_a),
            pl.BlockSpec(win.shape, const2),
            pl.BlockSpec(wmap.shape, lambda i: (0, 0, 0)),
            pl.BlockSpec(pscale.shape, const2),
            pl.BlockSpec(ba.shape, const2),
            pl.BlockSpec(ng.shape, const2),
            pl.BlockSpec(wout.shape, const2),
            pl.BlockSpec(lng.shape, const2),
            pl.BlockSpec(lnb.shape, const2),
        ],
        out_specs=[
            pl.BlockSpec((1, tc, D_MODEL), tile_b),
            pl.BlockSpec((1, POOL_BUF, POOL_WIDTH), seq_b),
            pl.BlockSpec((1, GLA_KEY_WIDTH, GLA_DV), seq_b),
        ],
        out_shape=[
            jax.ShapeDtypeStruct((nb, t, D_MODEL), F32),
            jax.ShapeDtypeStruct((nb, POOL_BUF, POOL_WIDTH), F32),
            jax.ShapeDtypeStruct((nb, GLA_KEY_WIDTH, GLA_DV), F32),
        ],
        scratch_shapes=[
            pltpu.VMEM((tc, C_END), F32),
            pltpu.VMEM((tc, C_END), F32),
            pltpu.VMEM((tc, D_MODEL), F32),
            pltpu.VMEM((tc, D_MODEL), F32),
            pltpu.VMEM((tc, D_MODEL), BF16),
            pltpu.VMEM((HIST_ROWS + tc, POOL_WIDTH), F32),
            pltpu.VMEM((GLA_KEY_WIDTH, GLA_DV), F32),
        ],
        compiler_params=pltpu.CompilerParams(
            dimension_semantics=("arbitrary",), vmem_limit_bytes=VMEM_LIMIT),
        name="prompt_layer",
    )(x, win, wmap, pscale, ba, ng, wout, lng, lnb)


def _sample_body(x_ref, pool_in_ref, s_in_ref, win_ref, wmap_ref, pscale_ref, ba_ref, ng_ref, wout_ref,
                 lng_ref, lnb_ref, y_ref, pool_ref, s_out_ref, ext_ref, *, bs, t, start_pos):
    rows = bs * t
    x = x_ref[...].reshape(rows, D_MODEL)
    xb = x.astype(BF16)

    def proj(lo, hi):
        return _dot(xb, win_ref[:, lo:hi])

    u = proj(C_U, C_GP)
    u3 = u.reshape(bs, t, POOL_WIDTH)
    ext_ref[:, 1:HIST_ROWS, :] = pool_in_ref[...]
    ext_ref[:, HIST_ROWS:HIST_ROWS + t, :] = u3
    pool_ref[...] = ext_ref[:, t + 1:t + HIST_ROWS, :]
    pos = start_pos + lax.broadcasted_iota(jnp.int32, (bs, t, POOL_GROUP_DIM), 1)
    mixed = []
    for g, w in enumerate(POOL_WINDOWS):
        sl = slice(g * POOL_GROUP_DIM, (g + 1) * POOL_GROUP_DIM)
        acc = u3[:, :, sl]
        for back in range(1, w):
            acc = acc + ext_ref[:, HIST_ROWS - back:HIST_ROWS - back + t, sl]
        cnt = jnp.minimum(pos + 1, w).astype(F32)
        pooled = (acc / cnt - u3[:, :, sl]).reshape(rows, POOL_GROUP_DIM)
        mixed.append(_dot(pooled.astype(BF16), wmap_ref[g]))
    gp = proj(C_GP, C_Q)
    pool_act = jnp.concatenate(mixed, axis=1) * pscale_ref[...] * _silu(gp)

    q = proj(C_Q, C_K) * (GLA_DK ** -0.5)
    k = proj(C_K, C_V)
    vb = proj(C_V, C_GG).astype(BF16)
    gg = proj(C_GG, C_GL)
    la = _log_sigmoid(proj(C_GL, C_END) + ba_ref[...]) * (1.0 / GLA_TAU)

    rr = lax.broadcasted_iota(jnp.int32, (rows, rows), 0)
    cl = lax.broadcasted_iota(jnp.int32, (rows, rows), 1)
    same = (rr // t) == (cl // t)
    ltri = jnp.where(same & (cl <= rr), 1.0, 0.0).astype(BF16)
    lall = jnp.where(same, 1.0, 0.0).astype(BF16)
    hi, lo = _split_bf16(la)
    b = _dot(ltri, hi) + _dot(ltri, lo)
    b_end = _dot(lall, hi) + _dot(lall, lo)
    qs = (q * jnp.exp(b)).astype(BF16)
    ks = (k * jnp.exp(-b)).astype(BF16)
    kd = (k * jnp.exp(b_end - b)).astype(BF16)

    blk = _head_block_mask(rows, GLA_DK, (GLA_HEADS * rows, GLA_KEY_WIDTH))
    ks_heads = jnp.where(blk, jnp.concatenate([ks] * GLA_HEADS, axis=0), jnp.zeros((), BF16))
    ra = lax.broadcasted_iota(jnp.int32, (rows, GLA_HEADS * rows), 0)
    ca = lax.broadcasted_iota(jnp.int32, (rows, GLA_HEADS * rows), 1) % rows
    valid = ((ra // t) == (ca // t)) & (ca <= ra)
    a_all = jnp.where(valid, _dot_nt(qs, ks_heads), 0.0).astype(BF16)

    s_old = s_in_ref[...]
    s_bf = s_old.astype(BF16)
    qs3 = qs.reshape(bs, t, GLA_KEY_WIDTH)
    kd3 = kd.reshape(bs, t, GLA_KEY_WIDTH)
    v3 = vb.reshape(bs, t, GLA_WIDTH)
    hi3 = hi.reshape(bs, t, GLA_KEY_WIDTH)
    lo3 = lo.reshape(bs, t, GLA_KEY_WIDTH)
    ones3 = jnp.ones((bs, t, GLA_DV), BF16)
    bdims_nn = (((2,), (1,)), ((0,), (0,)))
    bdims_tn = (((1,), (1,)), ((0,), (0,)))
    dsum = (lax.dot_general(hi3, ones3, bdims_tn, preferred_element_type=F32)
            + lax.dot_general(lo3, ones3, bdims_tn, preferred_element_type=F32))
    outs, upds = [], []
    for p in range(GLA_HEADS // 2):
        h0, h1 = 2 * p, 2 * p + 1
        intra = _dot(a_all[:, p * 2 * rows:(p + 1) * 2 * rows],
                     _pair_block_diag(vb[:, h0 * GLA_DV:(h0 + 1) * GLA_DV], vb[:, h1 * GLA_DV:(h1 + 1) * GLA_DV]))
        s0 = s_bf[:, h0 * GLA_DK:(h0 + 1) * GLA_DK, :]
        s1 = s_bf[:, h1 * GLA_DK:(h1 + 1) * GLA_DK, :]
        z = jnp.zeros_like(s0)
        s_pair = jnp.concatenate(
            [jnp.concatenate([s0, z], axis=2), jnp.concatenate([z, s1], axis=2)], axis=1)
        inter = lax.dot_general(qs3[:, :, p * 2 * GLA_DK:(p + 1) * 2 * GLA_DK], s_pair, bdims_nn,
                                preferred_element_type=F32)
        outs.append(intra + inter.reshape(rows, 2 * GLA_DV))
        upd = lax.dot_general(kd3[:, :, p * 2 * GLA_DK:(p + 1) * 2 * GLA_DK],
                              v3[:, :, p * 2 * GLA_DV:(p + 1) * 2 * GLA_DV], bdims_tn,
                              preferred_element_type=F32)
        upds.append(upd[:, 0:GLA_DK, 0:GLA_DV])
        upds.append(upd[:, GLA_DK:2 * GLA_DK, GLA_DV:2 * GLA_DV])
    o = jnp.concatenate(outs, axis=1)
    s_out_ref[...] = s_old * jnp.exp(dsum) + jnp.concatenate(upds, axis=1)

    y = _epilogue(x, pool_act, o, gg, ng_ref, wout_ref, lng_ref, lnb_ref)
    y_ref[...] = y.reshape(bs, t, D_MODEL)


def _sample_call(x, pool_in, s_in, win, wmap, pscale, ba, ng, wout, lng, lnb, start_pos):
    nb, t, _ = x.shape
    bs = SAMPLE_SEQS
    const2 = lambda i: (0, 0)
    seq3 = lambda i: (i, 0, 0)
    return pl.pallas_call(
        functools.partial(_sample_body, bs=bs, t=t, start_pos=start_pos),
        grid=(nb // bs,),
        in_specs=[
            pl.BlockSpec((bs, t, D_MODEL), seq3),
            pl.BlockSpec((bs, POOL_BUF, POOL_WIDTH), seq3),
            pl.BlockSpec((bs, GLA_KEY_WIDTH, GLA_DV), seq3),
            pl.BlockSpec(win.shape, const2),
            pl.BlockSpec(wmap.shape, lambda i: (0, 0, 0)),
            pl.BlockSpec(pscale.shape, const2),
            pl.BlockSpec(ba.shape, const2),
            pl.BlockSpec(ng.shape, const2),
            pl.BlockSpec(wout.shape, const2),
            pl.BlockSpec(lng.shape, const2),
            pl.BlockSpec(lnb.shape, const2),
        ],
        out_specs=[
            pl.BlockSpec((bs, t, D_MODEL), seq3),
            pl.BlockSpec((bs, POOL_BUF, POOL_WIDTH), seq3),
            pl.BlockSpec((bs, GLA_KEY_WIDTH, GLA_DV), seq3),
        ],
        out_shape=[
            jax.ShapeDtypeStruct((nb, t, D_MODEL), F32),
            jax.ShapeDtypeStruct((nb, POOL_BUF, POOL_WIDTH), F32),
            jax.ShapeDtypeStruct((nb, GLA_KEY_WIDTH, GLA_DV), F32),
        ],
        scratch_shapes=[pltpu.VMEM((bs, HIST_ROWS + t, POOL_WIDTH), F32)],
        compiler_params=pltpu.CompilerParams(
            dimension_semantics=("arbitrary",), vmem_limit_bytes=VMEM_LIMIT),
        name="sample_layer",
    )(x, pool_in, s_in, win, wmap, pscale, ba, ng, wout, lng, lnb)


def _fold_body(wfa_ref, wa2_ref, out_ref):
    wfa = wfa_ref[...]
    wa2 = wa2_ref[...]
    acc = wfa[:, 0:1] * wa2[0:1, :]
    for r in range(1, GLA_RANK):
        acc = acc + wfa[:, r:r + 1] * wa2[r:r + 1, :]
    out_ref[...] = acc


def _fold_forget_weights(wfa, wa2):
    return pl.pallas_call(
        _fold_body,
        out_shape=jax.ShapeDtypeStruct((D_MODEL, GLA_KEY_WIDTH), F32),
        name="fold_forget_weights",
    )(wfa, wa2)


def kernel(x_prompt, x_sample, state_pool, state_gla, w_in, w_pool_map, pool_scale, w_gla_a2, b_gla_a,
           gla_norm_g, w_out, ln_g, ln_b):
    assert w_in.shape[0] == DEPTH
    past_len = 16384
    w_in0 = w_in[0]
    w_forget = _fold_forget_weights(w_in0[:, C_GL:], w_gla_a2[0])
    win = jnp.concatenate([w_in0[:, :C_GL], w_forget], axis=1).astype(BF16)
    wmap = w_pool_map[0].astype(BF16)
    wout = w_out[0].astype(BF16)
    pscale = pool_scale[0].reshape(1, POOL_WIDTH)
    ba = b_gla_a[0].reshape(1, GLA_KEY_WIDTH)
    ng = gla_norm_g[0].reshape(1, GLA_WIDTH)
    lng = ln_g[0].reshape(1, D_MODEL)
    lnb = ln_b[0].reshape(1, D_MODEL)
    params = (win, wmap, pscale, ba, ng, wout, lng, lnb)

    y_p, pool_p, gla_p = _prompt_call(x_prompt, *params)
    nsamp = x_sample.shape[0]
    y_s, pool_s, gla_s = _sample_call(
        x_sample, state_pool[0], state_gla[0].reshape(nsamp, GLA_KEY_WIDTH, GLA_DV), *params,
        start_pos=past_len)

    nb = x_prompt.shape[0]
    return (y_p, y_s, pool_p[None],
            gla_p.reshape(1, nb, GLA_HEADS, GLA_DK, GLA_DV),
            pool_s[None],
            gla_s.reshape(1, nsamp, GLA_HEADS, GLA_DK, GLA_DV))
```

```python
import functools

import jax
import jax.numpy as jnp
from jax import lax
from jax.experimental import pallas as pl
from jax.experimental.pallas import tpu as pltpu

F32 = jnp.float32
BF16 = jnp.bfloat16

D_MODEL = 1024
POOL_WIDTH = 512
POOL_WINDOWS = (2, 4, 8, 16)
POOL_GROUP_DIM = 128
POOL_BUF = 15
HIST_ROWS = 16
GLA_HEADS = 4
GLA_DK = 64
GLA_DV = 128
GLA_KEY_WIDTH = GLA_HEADS * GLA_DK
GLA_WIDTH = GLA_HEADS * GLA_DV
GLA_RANK = 16
GLA_TAU = 16.0
LN_EPS = 1e-5
RMS_EPS = 1e-6
DEPTH = 1
DEEPNORM_ALPHA = (2.0 * DEPTH) ** 0.25
PAST_LEN = 16384

C_U, C_GP, C_Q, C_K, C_V, C_GG, C_GL, C_END = 0, 512, 1024, 1280, 1536, 2048, 2560, 2816

MXU_COLS = 256
PROMPT_TILE = 256
GLA_CHUNK = 64
SAMPLE_SEQS = 16
VMEM_LIMIT = 48 * 1024 * 1024


def _dot(a, b):
    return jnp.dot(a, b, preferred_element_type=F32)


def _dot_nt(a, b):
    return lax.dot_general(a, b, (((1,), (1,)), ((), ())), preferred_element_type=F32)


def _dot_tn(a, b):
    return lax.dot_general(a, b, (((0,), (0,)), ((), ())), preferred_element_type=F32)


def _silu(x):
    return x * (1.0 / (1.0 + jnp.exp(-x)))


def _log_sigmoid(x):
    return jnp.minimum(x, 0.0) - jnp.log1p(jnp.exp(-jnp.abs(x)))


def _split_bf16(x):
    hi = x.astype(BF16)
    lo = (x - hi.astype(F32)).astype(BF16)
    return hi, lo


def _epilogue(x, pool_act, o, gg, ng_ref, wout_ref, lng_ref, lnb_ref, pump=lambda n: None):
    pump(1)
    normed = []
    for h in range(GLA_HEADS):
        oh = o[:, h * GLA_DV:(h + 1) * GLA_DV]
        ms = jnp.mean(oh * oh, axis=-1, keepdims=True)
        normed.append(oh * lax.rsqrt(ms + RMS_EPS))
    gla_act = jnp.concatenate(normed, axis=1) * ng_ref[...] * _silu(gg)
    mix = jnp.concatenate([pool_act, gla_act], axis=1).astype(BF16)
    y = _dot(mix, wout_ref[...])
    pump(len(range(0, C_END, MXU_COLS)))
    h = DEEPNORM_ALPHA * x + y
    mu = jnp.mean(h, axis=-1, keepdims=True)
    d = h - mu
    var = jnp.mean(d * d, axis=-1, keepdims=True)
    return d * lax.rsqrt(var + LN_EPS) * lng_ref[...] + lnb_ref[...]


def _head_block_mask(rows_per_head, cols_per_head, shape):
    r = lax.broadcasted_iota(jnp.int32, shape, 0) // rows_per_head
    c = lax.broadcasted_iota(jnp.int32, shape, 1) // cols_per_head
    return r == c


def _pair_block_diag(a0, a1):
    z = jnp.zeros_like(a0)
    return jnp.concatenate(
        [jnp.concatenate([a0, z], axis=1), jnp.concatenate([z, a1], axis=1)], axis=0)


def _prompt_stage_a(x_ref, win_ref, proj_ref, xkeep_ref, xb_ref):
    x = x_ref[0]
    xkeep_ref[...] = x
    xb_ref[...] = x.astype(BF16)
    slabs = iter(range(0, C_END, MXU_COLS))

    def pump(n):
        for _ in range(n):
            lo = next(slabs, None)
            if lo is not None:
                proj_ref[:, lo:lo + MXU_COLS] = _dot(xb_ref[...], win_ref[:, lo:lo + MXU_COLS])

    return pump


def _prompt_stage_b(jb, proj_ref, xkeep_ref, wmap_ref, pscale_ref, ba_ref, ng_ref, wout_ref, lng_ref, lnb_ref,
                    y_ref, ext_ref, s_ref, pump, *, tc):
    pump(1)
    u = proj_ref[:, C_U:C_GP]
    ext_ref[HIST_ROWS:HIST_ROWS + tc, :] = u
    pos16 = jb * tc + lax.broadcasted_iota(jnp.int32, (HIST_ROWS, POOL_GROUP_DIM), 0)
    mixed = []
    for g, w in enumerate(POOL_WINDOWS):
        sl = slice(g * POOL_GROUP_DIM, (g + 1) * POOL_GROUP_DIM)
        e = ext_ref[:, sl]
        s = e
        sh = 1
        while sh < w:
            s = s + pltpu.roll(s, sh, 0)
            sh *= 2
        wsum = s[HIST_ROWS:]
        self_rows = e[HIST_ROWS:]
        cnt = jnp.minimum(pos16 + 1, w).astype(F32)
        first = wsum[:HIST_ROWS] / cnt - self_rows[:HIST_ROWS]
        rest = wsum[HIST_ROWS:] * (1.0 / w) - self_rows[HIST_ROWS:]
        pooled = jnp.concatenate([first, rest], axis=0)
        mixed.append(_dot(pooled.astype(BF16), wmap_ref[g]))
        pump(g % 2)
    pool_act = jnp.concatenate(mixed, axis=1) * pscale_ref[...] * _silu(proj_ref[:, C_GP:C_Q])

    la = _log_sigmoid(proj_ref[:, C_GL:C_END] + ba_ref[...]) * (1.0 / GLA_TAU)

    cc = GLA_CHUNK
    rr = lax.broadcasted_iota(jnp.int32, (cc, cc), 0)
    cl = lax.broadcasted_iota(jnp.int32, (cc, cc), 1)
    ltri = jnp.where(cl <= rr, 1.0, 0.0).astype(BF16)
    causal = (lax.broadcasted_iota(jnp.int32, (cc, GLA_HEADS * cc), 1) % cc
              <= lax.broadcasted_iota(jnp.int32, (cc, GLA_HEADS * cc), 0))
    blk = _head_block_mask(cc, GLA_DK, (GLA_HEADS * cc, GLA_KEY_WIDTH))

    o_chunks = []
    for c in range(tc // cc):
        rs = slice(c * cc, (c + 1) * cc)
        hi, lo = _split_bf16(la[rs])
        b = _dot(ltri, hi) + _dot(ltri, lo)
        pump(1)
        b_end = b[cc - 1:cc, :]
        q_c = proj_ref[rs, C_Q:C_K] * (GLA_DK ** -0.5)
        k_c = proj_ref[rs, C_K:C_V]
        qs = (q_c * jnp.exp(b)).astype(BF16)
        ks = (k_c * jnp.exp(-b)).astype(BF16)
        kd = (k_c * jnp.exp(b_end - b)).astype(BF16)
        ks_heads = jnp.where(blk, jnp.concatenate([ks] * GLA_HEADS, axis=0), jnp.zeros((), BF16))
        a_all = jnp.where(causal, _dot_nt(qs, ks_heads), 0.0).astype(BF16)
        s_old = s_ref[...]
        s_bf = s_old.astype(BF16)
        vc = proj_ref[rs, C_V:C_GG].astype(BF16)
        outs, upds = [], []
        for p in range(GLA_HEADS // 2):
            h0, h1 = 2 * p, 2 * p + 1
            lhs = jnp.concatenate([a_all[:, p * 2 * cc:(p + 1) * 2 * cc],
                                   qs[:, p * 2 * GLA_DK:(p + 1) * 2 * GLA_DK]], axis=1)
            rhs = jnp.concatenate([
                _pair_block_diag(vc[:, h0 * GLA_DV:(h0 + 1) * GLA_DV], vc[:, h1 * GLA_DV:(h1 + 1) * GLA_DV]),
                _pair_block_diag(s_bf[h0 * GLA_DK:(h0 + 1) * GLA_DK], s_bf[h1 * GLA_DK:(h1 + 1) * GLA_DK]),
            ], axis=0)
            outs.append(_dot(lhs, rhs))
            upd = _dot_tn(kd[:, p * 2 * GLA_DK:(p + 1) * 2 * GLA_DK], vc[:, p * 2 * GLA_DV:(p + 1) * 2 * GLA_DV])
            upds.append(upd[0:GLA_DK, 0:GLA_DV])
            upds.append(upd[GLA_DK:2 * GLA_DK, GLA_DV:2 * GLA_DV])
        o_chunks.append(jnp.concatenate(outs, axis=1))
        pump(1 - c % 2)
        dcol = jnp.transpose(jnp.broadcast_to(jnp.exp(b_end), (GLA_DV, GLA_KEY_WIDTH)))
        s_ref[...] = s_old * dcol + jnp.concatenate(upds, axis=0)
    o = jnp.concatenate(o_chunks, axis=0)

    y_ref[0] = _epilogue(xkeep_ref[...], pool_act, o, proj_ref[:, C_GG:C_GL], ng_ref, wout_ref, lng_ref, lnb_ref,
                         pump)


def _prompt_body(x_ref, win_ref, wmap_ref, pscale_ref, ba_ref, ng_ref, wout_ref, lng_ref, lnb_ref,
                 y_ref, pool_ref, s_out_ref,
                 proj0_ref, proj1_ref, xkeep0_ref, xkeep1_ref, xb_ref, ext_ref, s_ref, *, tc, nj):
    i = pl.program_id(0)
    jb = (i + nj - 1) % nj

    @pl.when(i == 0)
    def _():
        proj1_ref[...] = jnp.zeros_like(proj1_ref)
        xkeep1_ref[...] = jnp.zeros_like(xkeep1_ref)

    @pl.when((i == 0) | (jb == 0))
    def _():
        ext_ref[0:HIST_ROWS, :] = jnp.zeros((HIST_ROWS, POOL_WIDTH), F32)
        s_ref[...] = jnp.zeros_like(s_ref)

    def step(proj_a, xkeep_a, proj_b, xkeep_b):
        pump = _prompt_stage_a(x_ref, win_ref, proj_a, xkeep_a, xb_ref)
        _prompt_stage_b(jb, proj_b, xkeep_b, wmap_ref, pscale_ref, ba_ref, ng_ref, wout_ref, lng_ref, lnb_ref,
                        y_ref, ext_ref, s_ref, pump, tc=tc)

    @pl.when(i % 2 == 0)
    def _():
        step(proj0_ref, xkeep0_ref, proj1_ref, xkeep1_ref)

    @pl.when(i % 2 == 1)
    def _():
        step(proj1_ref, xkeep1_ref, proj0_ref, xkeep0_ref)

    @pl.when((i > 0) & (jb == nj - 1))
    def _():
        pool_ref[0] = ext_ref[tc + 1:tc + HIST_ROWS, :]
        s_out_ref[0] = s_ref[...]

    ext_ref[0:HIST_ROWS, :] = ext_ref[tc:tc + HIST_ROWS, :]


def _prompt_call(x, win, wmap, pscale, ba, ng, wout, lng, lnb):
    nb, t, _ = x.shape
    tc = PROMPT_TILE
    nj = t // tc
    ntiles = nb * nj
    const2 = lambda i: (0, 0)

    def tile_a(i):
        ia = jnp.minimum(i, ntiles - 1)
        return (ia // nj, ia % nj, 0)

    def tile_b(i):
        ib = jnp.maximum(i - 1, 0)
        return (ib // nj, ib % nj, 0)

    def seq_b(i):
        return (jnp.maximum(i - 1, 0) // nj, 0, 0)

    return pl.pallas_call(
        functools.partial(_prompt_body, tc=tc, nj=nj),
        grid=(ntiles + 1,),
        in_specs=[
            pl.BlockSpec((1, tc, D_MODEL), tile_a),
            pl.BlockSpec(win.shape, const2),
            pl.BlockSpec(wmap.shape, lambda i: (0, 0, 0)),
            pl.BlockSpec(pscale.shape, const2),
            pl.BlockSpec(ba.shape, const2),
            pl.BlockSpec(ng.shape, const2),
            pl.BlockSpec(wout.shape, const2),
            pl.BlockSpec(lng.shape, const2),
            pl.BlockSpec(lnb.shape, const2),
        ],
        out_specs=[
            pl.BlockSpec((1, tc, D_MODEL), tile_b),
            pl.BlockSpec((1, POOL_BUF, POOL_WIDTH), seq_b),
            pl.BlockSpec((1, GLA_KEY_WIDTH, GLA_DV), seq_b),
        ],
        out_shape=[
            jax.ShapeDtypeStruct((nb, t, D_MODEL), F32),
            jax.ShapeDtypeStruct((nb, POOL_BUF, POOL_WIDTH), F32),
            jax.ShapeDtypeStruct((nb, GLA_KEY_WIDTH, GLA_DV), F32),
        ],
        scratch_shapes=[
            pltpu.VMEM((tc, C_END), F32),
            pltpu.VMEM((tc, C_END), F32),
            pltpu.VMEM((tc, D_MODEL), F32),
            pltpu.VMEM((tc, D_MODEL), F32),
            pltpu.VMEM((tc, D_MODEL), BF16),
            pltpu.VMEM((HIST_ROWS + tc, POOL_WIDTH), F32),
            pltpu.VMEM((GLA_KEY_WIDTH, GLA_DV), F32),
        ],
        compiler_params=pltpu.CompilerParams(
            dimension_semantics=("arbitrary",), vmem_limit_bytes=VMEM_LIMIT),
        name="prompt_layer",
    )(x, win, wmap, pscale, ba, ng, wout, lng, lnb)


def _sample_body(x_ref, pool_in_ref, s_in_ref, win_ref, wmap_ref, pscale_ref, ba_ref, ng_ref, wout_ref,
                 lng_ref, lnb_ref, y_ref, pool_ref, s_out_ref, u_ref, pooled_ref, *, bs, t, start_pos):
    rows = bs * t
    x = x_ref[...].reshape(rows, D_MODEL)
    xb = x.astype(BF16)

    def proj(lo, hi):
        return _dot(xb, win_ref[:, lo:hi])

    u = proj(C_U, C_GP)
    mixed = []
    for g, w in enumerate(POOL_WINDOWS):
        sl = slice(g * POOL_GROUP_DIM, (g + 1) * POOL_GROUP_DIM)
        u_ref[g] = u[:, sl]
        ext = [pool_in_ref[r, :, sl] for r in range(POOL_BUF)]
        ext += [u_ref[g, pl.ds(tok, bs, stride=t), :] for tok in range(t)]
        for r in range(POOL_BUF):
            pool_ref[r, :, sl] = ext[t + r]
        sums = ext
        sh = 1
        while sh < w:
            first = 2 * sh - 1
            sums = [None] * first + [sums[r] + sums[r - sh] for r in range(first, len(sums))]
            sh *= 2
        for tok in range(t):
            cnt = float(min(start_pos + tok + 1, w))
            pooled_ref[g, pl.ds(tok, bs, stride=t), :] = sums[POOL_BUF + tok] / cnt - ext[POOL_BUF + tok]
        mixed.append(_dot(pooled_ref[g].astype(BF16), wmap_ref[g]))
    gp = proj(C_GP, C_Q)
    pool_act = jnp.concatenate(mixed, axis=1) * pscale_ref[...] * _silu(gp)

    q = proj(C_Q, C_K) * (GLA_DK ** -0.5)
    k = proj(C_K, C_V)
    vb = proj(C_V, C_GG).astype(BF16)
    gg = proj(C_GG, C_GL)
    la = _log_sigmoid(proj(C_GL, C_END) + ba_ref[...]) * (1.0 / GLA_TAU)

    rr = lax.broadcasted_iota(jnp.int32, (rows, rows), 0)
    cl = lax.broadcasted_iota(jnp.int32, (rows, rows), 1)
    same = (rr // t) == (cl // t)
    ltri = jnp.where(same & (cl <= rr), 1.0, 0.0).astype(BF16)
    lall = jnp.where(same, 1.0, 0.0).astype(BF16)
    hi, lo = _split_bf16(la)
    b = _dot(ltri, hi) + _dot(ltri, lo)
    b_end = _dot(lall, hi) + _dot(lall, lo)
    qs = (q * jnp.exp(b)).astype(BF16)
    ks = (k * jnp.exp(-b)).astype(BF16)
    kd = (k * jnp.exp(b_end - b)).astype(BF16)

    blk = _head_block_mask(rows, GLA_DK, (GLA_HEADS * rows, GLA_KEY_WIDTH))
    ks_heads = jnp.where(blk, jnp.concatenate([ks] * GLA_HEADS, axis=0), jnp.zeros((), BF16))
    ra = lax.broadcasted_iota(jnp.int32, (rows, GLA_HEADS * rows), 0)
    ca = lax.broadcasted_iota(jnp.int32, (rows, GLA_HEADS * rows), 1) % rows
    valid = ((ra // t) == (ca // t)) & (ca <= ra)
    a_all = jnp.where(valid, _dot_nt(qs, ks_heads), 0.0).astype(BF16)

    s_old = s_in_ref[...]
    s_bf = s_old.astype(BF16)
    qs3 = qs.reshape(bs, t, GLA_KEY_WIDTH)
    kd3 = kd.reshape(bs, t, GLA_KEY_WIDTH)
    v3 = vb.reshape(bs, t, GLA_WIDTH)
    hi3 = hi.reshape(bs, t, GLA_KEY_WIDTH)
    lo3 = lo.reshape(bs, t, GLA_KEY_WIDTH)
    ones3 = jnp.ones((bs, t, GLA_DV), BF16)
    bdims_nn = (((2,), (1,)), ((0,), (0,)))
    bdims_tn = (((1,), (1,)), ((0,), (0,)))
    dsum = (lax.dot_general(hi3, ones3, bdims_tn, preferred_element_type=F32)
            + lax.dot_general(lo3, ones3, bdims_tn, preferred_element_type=F32))
    outs, upds = [], []
    for p in range(GLA_HEADS // 2):
        h0, h1 = 2 * p, 2 * p + 1
        intra = _dot(a_all[:, p * 2 * rows:(p + 1) * 2 * rows],
                     _pair_block_diag(vb[:, h0 * GLA_DV:(h0 + 1) * GLA_DV], vb[:, h1 * GLA_DV:(h1 + 1) * GLA_DV]))
        s0 = s_bf[:, h0 * GLA_DK:(h0 + 1) * GLA_DK, :]
        s1 = s_bf[:, h1 * GLA_DK:(h1 + 1) * GLA_DK, :]
        z = jnp.zeros_like(s0)
        s_pair = jnp.concatenate(
            [jnp.concatenate([s0, z], axis=2), jnp.concatenate([z, s1], axis=2)], axis=1)
        inter = lax.dot_general(qs3[:, :, p * 2 * GLA_DK:(p + 1) * 2 * GLA_DK], s_pair, bdims_nn,
                                preferred_element_type=F32)
        outs.append(intra + inter.reshape(rows, 2 * GLA_DV))
        upd = lax.dot_general(kd3[:, :, p * 2 * GLA_DK:(p + 1) * 2 * GLA_DK],
                              v3[:, :, p * 2 * GLA_DV:(p + 1) * 2 * GLA_DV], bdims_tn,
                              preferred_element_type=F32)
        upds.append(upd[:, 0:GLA_DK, 0:GLA_DV])
        upds.append(upd[:, GLA_DK:2 * GLA_DK, GLA_DV:2 * GLA_DV])
    o = jnp.concatenate(outs, axis=1)
    s_out_ref[...] = s_old * jnp.exp(dsum) + jnp.concatenate(upds, axis=1)

    y = _epilogue(x, pool_act, o, gg, ng_ref, wout_ref, lng_ref, lnb_ref)
    y_ref[...] = y.reshape(bs, t, D_MODEL)


def _sample_call(x, pool_in, s_in, win, wmap, pscale, ba, ng, wout, lng, lnb, start_pos):
    nb, t, _ = x.shape
    bs = SAMPLE_SEQS
    const2 = lambda i: (0, 0)
    seq3 = lambda i: (i, 0, 0)
    hist3 = lambda i: (0, i, 0)
    return pl.pallas_call(
        functools.partial(_sample_body, bs=bs, t=t, start_pos=start_pos),
        grid=(nb // bs,),
        in_specs=[
            pl.BlockSpec((bs, t, D_MODEL), seq3),
            pl.BlockSpec((POOL_BUF, bs, POOL_WIDTH), hist3),
            pl.BlockSpec((bs, GLA_KEY_WIDTH, GLA_DV), seq3),
            pl.BlockSpec(win.shape, const2),
            pl.BlockSpec(wmap.shape, lambda i: (0, 0, 0)),
            pl.BlockSpec(pscale.shape, const2),
            pl.BlockSpec(ba.shape, const2),
            pl.BlockSpec(ng.shape, const2),
            pl.BlockSpec(wout.shape, const2),
            pl.BlockSpec(lng.shape, const2),
            pl.BlockSpec(lnb.shape, const2),
        ],
        out_specs=[
            pl.BlockSpec((bs, t, D_MODEL), seq3),
            pl.BlockSpec((POOL_BUF, bs, POOL_WIDTH), hist3),
            pl.BlockSpec((bs, GLA_KEY_WIDTH, GLA_DV), seq3),
        ],
        out_shape=[
            jax.ShapeDtypeStruct((nb, t, D_MODEL), F32),
            jax.ShapeDtypeStruct((POOL_BUF, nb, POOL_WIDTH), F32),
            jax.ShapeDtypeStruct((nb, GLA_KEY_WIDTH, GLA_DV), F32),
        ],
        scratch_shapes=[pltpu.VMEM((len(POOL_WINDOWS), bs * t, POOL_GROUP_DIM), F32),
                        pltpu.VMEM((len(POOL_WINDOWS), bs * t, POOL_GROUP_DIM), F32)],
        compiler_params=pltpu.CompilerParams(
            dimension_semantics=("arbitrary",), vmem_limit_bytes=VMEM_LIMIT),
        name="sample_layer",
    )(x, pool_in, s_in, win, wmap, pscale, ba, ng, wout, lng, lnb)


def _prep_body(wt_ref, wfat_ref, wa2t_ref, out_ref, *, n_plain):
    i = pl.program_id(0)

    @pl.when(i < n_plain)
    def _():
        out_ref[...] = jnp.transpose(wt_ref[...]).astype(BF16)

    @pl.when(i == n_plain)
    def _():
        wfat = wfat_ref[...]
        wa2t = wa2t_ref[...]
        acc = wa2t[:, 0:1] * wfat[0:1, :]
        for r in range(1, GLA_RANK):
            acc = acc + wa2t[:, r:r + 1] * wfat[r:r + 1, :]
        out_ref[...] = jnp.transpose(acc).astype(BF16)


def _prep_input_weights(wt, wa2t):
    n_plain = C_GL // MXU_COLS
    return pl.pallas_call(
        functools.partial(_prep_body, n_plain=n_plain),
        grid=(n_plain + 1,),
        in_specs=[
            pl.BlockSpec((MXU_COLS, D_MODEL), lambda i: (jnp.minimum(i, n_plain - 1), 0)),
            pl.BlockSpec((GLA_RANK, D_MODEL), lambda i: (0, 0)),
            pl.BlockSpec((GLA_KEY_WIDTH, GLA_RANK), lambda i: (0, 0)),
        ],
        out_specs=pl.BlockSpec((D_MODEL, MXU_COLS), lambda i: (0, i)),
        out_shape=jax.ShapeDtypeStruct((D_MODEL, C_END), BF16),
        compiler_params=pltpu.CompilerParams(dimension_semantics=("arbitrary",)),
        name="prep_input_weights",
    )(wt, wt[C_GL:], wa2t)


def kernel(x_prompt, x_sample, state_pool, state_gla, w_in, w_pool_map, pool_scale, w_gla_a2, b_gla_a,
           gla_norm_g, w_out, ln_g, ln_b):
    assert w_in.shape[0] == DEPTH
    win = _prep_input_weights(jnp.transpose(w_in[0]), jnp.transpose(w_gla_a2[0]))
    wmap = w_pool_map[0].astype(BF16)
    wout = w_out[0].astype(BF16)
    pscale = pool_scale[0].reshape(1, POOL_WIDTH)
    ba = b_gla_a[0].reshape(1, GLA_KEY_WIDTH)
    ng = gla_norm_g[0].reshape(1, GLA_WIDTH)
    lng = ln_g[0].reshape(1, D_MODEL)
    lnb = ln_b[0].reshape(1, D_MODEL)
    params = (win, wmap, pscale, ba, ng, wout, lng, lnb)

    y_p, pool_p, gla_p = _prompt_call(x_prompt, *params)
    nsamp = x_sample.shape[0]
    y_s, pool_s, gla_s = _sample_call(
        x_sample, jnp.transpose(state_pool[0], (1, 0, 2)),
        state_gla[0].reshape(nsamp, GLA_KEY_WIDTH, GLA_DV), *params, start_pos=PAST_LEN)

    nb = x_prompt.shape[0]
    return (y_p, y_s, pool_p[None],
            gla_p.reshape(1, nb, GLA_HEADS, GLA_DK, GLA_DV),
            jnp.transpose(pool_s, (1, 0, 2))[None],
            gla_s.reshape(1, nsamp, GLA_HEADS, GLA_DK, GLA_DV))
```

```python
import functools

import jax
import jax.numpy as jnp
from jax import lax
from jax.experimental import pallas as pl
from jax.experimental.pallas import tpu as pltpu

F32 = jnp.float32
BF16 = jnp.bfloat16

D_MODEL = 1024
POOL_WIDTH = 512
POOL_WINDOWS = (2, 4, 8, 16)
POOL_GROUP_DIM = 128
POOL_BUF = 15
HIST_ROWS = 16
GLA_HEADS = 4
GLA_DK = 64
GLA_DV = 128
GLA_KEY_WIDTH = GLA_HEADS * GLA_DK
GLA_WIDTH = GLA_HEADS * GLA_DV
GLA_RANK = 16
GLA_TAU = 16.0
LN_EPS = 1e-5
RMS_EPS = 1e-6
DEPTH = 1
DEEPNORM_ALPHA = (2.0 * DEPTH) ** 0.25
PAST_LEN = 16384

C_U, C_GP, C_Q, C_K, C_V, C_GG, C_GL, C_END = 0, 512, 1024, 1280, 1536, 2048, 2560, 2816

MXU_COLS = 256
PROMPT_TILE = 256
PUMP_ROWS = 128
GLA_CHUNK = 64
SAMPLE_SEQS = 16
VMEM_LIMIT = 48 * 1024 * 1024


def _dot(a, b):
    return jnp.dot(a, b, preferred_element_type=F32)


def _dot_nt(a, b):
    return lax.dot_general(a, b, (((1,), (1,)), ((), ())), preferred_element_type=F32)


def _dot_tn(a, b):
    return lax.dot_general(a, b, (((0,), (0,)), ((), ())), preferred_element_type=F32)


def _silu(x):
    return x * (1.0 / (1.0 + jnp.exp(-x)))


def _log_sigmoid(x):
    return jnp.minimum(x, 0.0) - jnp.log1p(jnp.exp(-jnp.abs(x)))


def _split_bf16(x):
    hi = x.astype(BF16)
    lo = (x - hi.astype(F32)).astype(BF16)
    return hi, lo


def _epilogue(x, pool_act, o, gg, ng_ref, wout_ref, lng_ref, lnb_ref, pump=lambda n: None):
    pump(1)
    normed = []
    for h in range(GLA_HEADS):
        oh = o[:, h * GLA_DV:(h + 1) * GLA_DV]
        ms = jnp.mean(oh * oh, axis=-1, keepdims=True)
        normed.append(oh * lax.rsqrt(ms + RMS_EPS))
    gla_act = jnp.concatenate(normed, axis=1) * ng_ref[...] * _silu(gg)
    mix = jnp.concatenate([pool_act, gla_act], axis=1).astype(BF16)
    y = _dot(mix, wout_ref[...])
    pump(1 << 30)
    h = DEEPNORM_ALPHA * x + y
    mu = jnp.mean(h, axis=-1, keepdims=True)
    d = h - mu
    var = jnp.mean(d * d, axis=-1, keepdims=True)
    return d * lax.rsqrt(var + LN_EPS) * lng_ref[...] + lnb_ref[...]


def _head_block_mask(rows_per_head, cols_per_head, shape):
    r = lax.broadcasted_iota(jnp.int32, shape, 0) // rows_per_head
    c = lax.broadcasted_iota(jnp.int32, shape, 1) // cols_per_head
    return r == c


def _score_masks(rows, n_levels):
    shape = (rows, GLA_HEADS * rows)
    t_idx = lax.broadcasted_iota(jnp.int32, shape, 0)
    s_idx = lax.broadcasted_iota(jnp.int32, shape, 1) % rows
    masks = [t_idx == s_idx]
    for lvl in range(n_levels):
        half = 1 << lvl
        blk = 2 * half
        masks.append((t_idx // blk == s_idx // blk) & (t_idx % blk >= half) & (s_idx % blk < half))
    return masks


def _decayed_scores(q, k, la, b, masks, head_blk, pump=lambda n: None):
    rows = q.shape[0]
    r_idx = lax.broadcasted_iota(jnp.int32, q.shape, 0)

    def scores(qt, kt):
        kh = jnp.where(head_blk, jnp.concatenate([kt.astype(BF16)] * GLA_HEADS, axis=0), jnp.zeros((), BF16))
        return _dot_nt(qt.astype(BF16), kh)

    acc = jnp.where(masks[0], scores(q, k), 0.0)
    for lvl in range(len(masks) - 1):
        half = 1 << lvl
        blk = 2 * half
        if lvl == 0:
            neg = jnp.where(r_idx % 2 == 1, la, 0.0)
        elif lvl == 1:
            r4 = r_idx % 4
            nxt = pltpu.roll(la, rows - 1, 0)
            prv = pltpu.roll(la, 1, 0)
            neg = jnp.where(r4 == 0, nxt, jnp.where(r4 == 1, 0.0, jnp.where(r4 == 2, la, la + prv)))
        else:
            ref = jnp.concatenate(
                [jnp.broadcast_to(b[j * blk + half - 1:j * blk + half, :], (blk, b.shape[1]))
                 for j in range(rows // blk)], axis=0)
            d = b - ref
            neg = jnp.minimum(d, -d)
        w = jnp.exp(neg)
        acc = jnp.where(masks[lvl + 1], scores(q * w, k * w), acc)
        pump(lvl % 2)
    return acc.astype(BF16)


def _pair_block_diag(a0, a1):
    z = jnp.zeros_like(a0)
    return jnp.concatenate(
        [jnp.concatenate([a0, z], axis=1), jnp.concatenate([z, a1], axis=1)], axis=0)


def _prompt_stage_a(x_ref, win_ref, proj_ref, xkeep_ref, xb_ref):
    x = x_ref[0]
    xkeep_ref[...] = x
    xb_ref[...] = x.astype(BF16)
    units = iter([(r, lo) for lo in range(0, C_END, MXU_COLS) for r in range(0, x.shape[0], PUMP_ROWS)])

    def pump(n):
        for _ in range(n):
            unit = next(units, None)
            if unit is not None:
                r, lo = unit
                proj_ref[r:r + PUMP_ROWS, lo:lo + MXU_COLS] = _dot(
                    xb_ref[r:r + PUMP_ROWS, :], win_ref[:, lo:lo + MXU_COLS])

    return pump


def _prompt_stage_b(jb, proj_ref, xkeep_ref, wmap_ref, pscale_ref, ba_ref, ng_ref, wout_ref, lng_ref, lnb_ref,
                    y_ref, ext_ref, s_ref, pump, *, tc):
    pump(1)
    u = proj_ref[:, C_U:C_GP]
    ext_ref[HIST_ROWS:HIST_ROWS + tc, :] = u
    pos16 = jb * tc + lax.broadcasted_iota(jnp.int32, (HIST_ROWS, POOL_GROUP_DIM), 0)
    mixed = []
    for g, w in enumerate(POOL_WINDOWS):
        sl = slice(g * POOL_GROUP_DIM, (g + 1) * POOL_GROUP_DIM)
        e = ext_ref[:, sl]
        s = e
        sh = 1
        while sh < w:
            s = s + pltpu.roll(s, sh, 0)
            sh *= 2
        wsum = s[HIST_ROWS:]
        self_rows = e[HIST_ROWS:]
        cnt = jnp.minimum(pos16 + 1, w).astype(F32)
        first = wsum[:HIST_ROWS] / cnt - self_rows[:HIST_ROWS]
        rest = wsum[HIST_ROWS:] * (1.0 / w) - self_rows[HIST_ROWS:]
        pooled = jnp.concatenate([first, rest], axis=0)
        mixed.append(_dot(pooled.astype(BF16), wmap_ref[g]))
        pump(1)
    pool_act = jnp.concatenate(mixed, axis=1) * pscale_ref[...] * _silu(proj_ref[:, C_GP:C_Q])

    la = _log_sigmoid(proj_ref[:, C_GL:C_END] + ba_ref[...]) * (1.0 / GLA_TAU)

    cc = GLA_CHUNK
    rr = lax.broadcasted_iota(jnp.int32, (cc, cc), 0)
    cl = lax.broadcasted_iota(jnp.int32, (cc, cc), 1)
    ltri = jnp.where(cl <= rr, 1.0, 0.0).astype(BF16)
    masks = _score_masks(cc, cc.bit_length() - 1)
    blk = _head_block_mask(cc, GLA_DK, (GLA_HEADS * cc, GLA_KEY_WIDTH))

    o_chunks = []
    for c in range(tc // cc):
        rs = slice(c * cc, (c + 1) * cc)
        hi, lo = _split_bf16(la[rs])
        b = _dot(ltri, hi) + _dot(ltri, lo)
        pump(1)
        b_end = b[cc - 1:cc, :]
        q_c = proj_ref[rs, C_Q:C_K] * (GLA_DK ** -0.5)
        k_c = proj_ref[rs, C_K:C_V]
        qs = (q_c * jnp.exp(b)).astype(BF16)
        kd = (k_c * jnp.exp(b_end - b)).astype(BF16)
        a_all = _decayed_scores(q_c, k_c, la[rs], b, masks, blk, pump)
        s_old = s_ref[...]
        s_bf = s_old.astype(BF16)
        vc = proj_ref[rs, C_V:C_GG].astype(BF16)
        outs, upds = [], []
        for p in range(GLA_HEADS // 2):
            h0, h1 = 2 * p, 2 * p + 1
            lhs = jnp.concatenate([a_all[:, p * 2 * cc:(p + 1) * 2 * cc],
                                   qs[:, p * 2 * GLA_DK:(p + 1) * 2 * GLA_DK]], axis=1)
            rhs = jnp.concatenate([
                _pair_block_diag(vc[:, h0 * GLA_DV:(h0 + 1) * GLA_DV], vc[:, h1 * GLA_DV:(h1 + 1) * GLA_DV]),
                _pair_block_diag(s_bf[h0 * GLA_DK:(h0 + 1) * GLA_DK], s_bf[h1 * GLA_DK:(h1 + 1) * GLA_DK]),
            ], axis=0)
            outs.append(_dot(lhs, rhs))
            upd = _dot_tn(kd[:, p * 2 * GLA_DK:(p + 1) * 2 * GLA_DK], vc[:, p * 2 * GLA_DV:(p + 1) * 2 * GLA_DV])
            upds.append(upd[0:GLA_DK, 0:GLA_DV])
            upds.append(upd[GLA_DK:2 * GLA_DK, GLA_DV:2 * GLA_DV])
        o_chunks.append(jnp.concatenate(outs, axis=1))
        dcol = jnp.transpose(jnp.broadcast_to(jnp.exp(b_end), (GLA_DV, GLA_KEY_WIDTH)))
        s_ref[...] = s_old * dcol + jnp.concatenate(upds, axis=0)
    o = jnp.concatenate(o_chunks, axis=0)

    y_ref[0] = _epilogue(xkeep_ref[...], pool_act, o, proj_ref[:, C_GG:C_GL], ng_ref, wout_ref, lng_ref, lnb_ref,
                         pump)


def _prompt_body(x_ref, win_ref, wmap_ref, pscale_ref, ba_ref, ng_ref, wout_ref, lng_ref, lnb_ref,
                 y_ref, pool_ref, s_out_ref,
                 proj0_ref, proj1_ref, xkeep0_ref, xkeep1_ref, xb_ref, ext_ref, s_ref, *, tc, nj):
    i = pl.program_id(0)
    jb = (i + nj - 1) % nj

    @pl.when(i == 0)
    def _():
        proj1_ref[...] = jnp.zeros_like(proj1_ref)
        xkeep1_ref[...] = jnp.zeros_like(xkeep1_ref)

    @pl.when((i == 0) | (jb == 0))
    def _():
        ext_ref[0:HIST_ROWS, :] = jnp.zeros((HIST_ROWS, POOL_WIDTH), F32)
        s_ref[...] = jnp.zeros_like(s_ref)

    def step(proj_a, xkeep_a, proj_b, xkeep_b):
        pump = _prompt_stage_a(x_ref, win_ref, proj_a, xkeep_a, xb_ref)
        _prompt_stage_b(jb, proj_b, xkeep_b, wmap_ref, pscale_ref, ba_ref, ng_ref, wout_ref, lng_ref, lnb_ref,
                        y_ref, ext_ref, s_ref, pump, tc=tc)

    @pl.when(i % 2 == 0)
    def _():
        step(proj0_ref, xkeep0_ref, proj1_ref, xkeep1_ref)

    @pl.when(i % 2 == 1)
    def _():
        step(proj1_ref, xkeep1_ref, proj0_ref, xkeep0_ref)

    @pl.when((i > 0) & (jb == nj - 1))
    def _():
        pool_ref[0] = ext_ref[tc + 1:tc + HIST_ROWS, :]
        s_out_ref[0] = s_ref[...]

    ext_ref[0:HIST_ROWS, :] = ext_ref[tc:tc + HIST_ROWS, :]


def _prompt_call(x, win, wmap, pscale, ba, ng, wout, lng, lnb):
    nb, t, _ = x.shape
    tc = PROMPT_TILE
    nj = t // tc
    ntiles = nb * nj
    const2 = lambda i: (0, 0)

    def tile_a(i):
        ia = jnp.minimum(i, ntiles - 1)
        return (ia // nj, ia % nj, 0)

    def tile_b(i):
        ib = jnp.maximum(i - 1, 0)
        return (ib // nj, ib % nj, 0)

    def seq_b(i):
        return (jnp.maximum(i - 1, 0) // nj, 0, 0)

    return pl.pallas_call(
        functools.partial(_prompt_body, tc=tc, nj=nj),
        grid=(ntiles + 1,),
        in_specs=[
            pl.BlockSpec((1, tc, D_MODEL), tile_a),
            pl.BlockSpec(win.shape, const2),
            pl.BlockSpec(wmap.shape, lambda i: (0, 0, 0)),
            pl.BlockSpec(pscale.shape, const2),
            pl.BlockSpec(ba.shape, const2),
            pl.BlockSpec(ng.shape, const2),
            pl.BlockSpec(wout.shape, const2),
            pl.BlockSpec(lng.shape, const2),
            pl.BlockSpec(lnb.shape, const2),
        ],
        out_specs=[
            pl.BlockSpec((1, tc, D_MODEL), tile_b),
            pl.BlockSpec((1, POOL_BUF, POOL_WIDTH), seq_b),
            pl.BlockSpec((1, GLA_KEY_WIDTH, GLA_DV), seq_b),
        ],
        out_shape=[
            jax.ShapeDtypeStruct((nb, t, D_MODEL), F32),
            jax.ShapeDtypeStruct((nb, POOL_BUF, POOL_WIDTH), F32),
            jax.ShapeDtypeStruct((nb, GLA_KEY_WIDTH, GLA_DV), F32),
        ],
        scratch_shapes=[
            pltpu.VMEM((tc, C_END), F32),
            pltpu.VMEM((tc, C_END), F32),
            pltpu.VMEM((tc, D_MODEL), F32),
            pltpu.VMEM((tc, D_MODEL), F32),
            pltpu.VMEM((tc, D_MODEL), BF16),
            pltpu.VMEM((HIST_ROWS + tc, POOL_WIDTH), F32),
            pltpu.VMEM((GLA_KEY_WIDTH, GLA_DV), F32),
        ],
        compiler_params=pltpu.CompilerParams(
            dimension_semantics=("arbitrary",), vmem_limit_bytes=VMEM_LIMIT),
        name="prompt_layer",
    )(x, win, wmap, pscale, ba, ng, wout, lng, lnb)


def _sample_body(x_ref, pool_in_ref, s_in_ref, win_ref, wmap_ref, pscale_ref, ba_ref, ng_ref, wout_ref,
                 lng_ref, lnb_ref, y_ref, pool_ref, s_out_ref, u_ref, pooled_ref, *, bs, t, start_pos):
    rows = bs * t
    x = x_ref[...].reshape(rows, D_MODEL)
    xb = x.astype(BF16)

    def proj(lo, hi):
        return _dot(xb, win_ref[:, lo:hi])

    u = proj(C_U, C_GP)
    mixed = []
    for g, w in enumerate(POOL_WINDOWS):
        sl = slice(g * POOL_GROUP_DIM, (g + 1) * POOL_GROUP_DIM)
        u_ref[g] = u[:, sl]
        ext = [pool_in_ref[r, :, sl] for r in range(POOL_BUF)]
        ext += [u_ref[g, pl.ds(tok, bs, stride=t), :] for tok in range(t)]
        for r in range(POOL_BUF):
            pool_ref[r, :, sl] = ext[t + r]
        sums = ext
        sh = 1
        while sh < w:
            first = 2 * sh - 1
            sums = [None] * first + [sums[r] + sums[r - sh] for r in range(first, len(sums))]
            sh *= 2
        for tok in range(t):
            cnt = float(min(start_pos + tok + 1, w))
            pooled_ref[g, pl.ds(tok, bs, stride=t), :] = sums[POOL_BUF + tok] / cnt - ext[POOL_BUF + tok]
        mixed.append(_dot(pooled_ref[g].astype(BF16), wmap_ref[g]))
    gp = proj(C_GP, C_Q)
    pool_act = jnp.concatenate(mixed, axis=1) * pscale_ref[...] * _silu(gp)

    q = proj(C_Q, C_K) * (GLA_DK ** -0.5)
    k = proj(C_K, C_V)
    vb = proj(C_V, C_GG).astype(BF16)
    gg = proj(C_GG, C_GL)
    la = _log_sigmoid(proj(C_GL, C_END) + ba_ref[...]) * (1.0 / GLA_TAU)

    rr = lax.broadcasted_iota(jnp.int32, (rows, rows), 0)
    cl = lax.broadcasted_iota(jnp.int32, (rows, rows), 1)
    same = (rr // t) == (cl // t)
    ltri = jnp.where(same & (cl <= rr), 1.0, 0.0).astype(BF16)
    lall = jnp.where(same, 1.0, 0.0).astype(BF16)
    hi, lo = _split_bf16(la)
    b = _dot(ltri, hi) + _dot(ltri, lo)
    b_end = _dot(lall, hi) + _dot(lall, lo)
    qs = (q * jnp.exp(b)).astype(BF16)
    kd = (k * jnp.exp(b_end - b)).astype(BF16)

    assert t & (t - 1) == 0, "tokens per sequence must be a power of two"
    blk = _head_block_mask(rows, GLA_DK, (GLA_HEADS * rows, GLA_KEY_WIDTH))
    a_all = _decayed_scores(q, k, la, b, _score_masks(rows, t.bit_length() - 1), blk)

    s_old = s_in_ref[...]
    s_bf = s_old.astype(BF16)
    qs3 = qs.reshape(bs, t, GLA_KEY_WIDTH)
    kd3 = kd.reshape(bs, t, GLA_KEY_WIDTH)
    v3 = vb.reshape(bs, t, GLA_WIDTH)
    hi3 = hi.reshape(bs, t, GLA_KEY_WIDTH)
    lo3 = lo.reshape(bs, t, GLA_KEY_WIDTH)
    ones3 = jnp.ones((bs, t, GLA_DV), BF16)
    bdims_nn = (((2,), (1,)), ((0,), (0,)))
    bdims_tn = (((1,), (1,)), ((0,), (0,)))
    dsum = (lax.dot_general(hi3, ones3, bdims_tn, preferred_element_type=F32)
            + lax.dot_general(lo3, ones3, bdims_tn, preferred_element_type=F32))
    outs, upds = [], []
    for p in range(GLA_HEADS // 2):
        h0, h1 = 2 * p, 2 * p + 1
        intra = _dot(a_all[:, p * 2 * rows:(p + 1) * 2 * rows],
                     _pair_block_diag(vb[:, h0 * GLA_DV:(h0 + 1) * GLA_DV], vb[:, h1 * GLA_DV:(h1 + 1) * GLA_DV]))
        s0 = s_bf[:, h0 * GLA_DK:(h0 + 1) * GLA_DK, :]
        s1 = s_bf[:, h1 * GLA_DK:(h1 + 1) * GLA_DK, :]
        z = jnp.zeros_like(s0)
        s_pair = jnp.concatenate(
            [jnp.concatenate([s0, z], axis=2), jnp.concatenate([z, s1], axis=2)], axis=1)
        inter = lax.dot_general(qs3[:, :, p * 2 * GLA_DK:(p + 1) * 2 * GLA_DK], s_pair, bdims_nn,
                                preferred_element_type=F32)
        outs.append(intra + inter.reshape(rows, 2 * GLA_DV))
        upd = lax.dot_general(kd3[:, :, p * 2 * GLA_DK:(p + 1) * 2 * GLA_DK],
                              v3[:, :, p * 2 * GLA_DV:(p + 1) * 2 * GLA_DV], bdims_tn,
                              preferred_element_type=F32)
        upds.append(upd[:, 0:GLA_DK, 0:GLA_DV])
        upds.append(upd[:, GLA_DK:2 * GLA_DK, GLA_DV:2 * GLA_DV])
    o = jnp.concatenate(outs, axis=1)
    s_out_ref[...] = s_old * jnp.exp(dsum) + jnp.concatenate(upds, axis=1)

    y = _epilogue(x, pool_act, o, gg, ng_ref, wout_ref, lng_ref, lnb_ref)
    y_ref[...] = y.reshape(bs, t, D_MODEL)


def _sample_call(x, pool_in, s_in, win, wmap, pscale, ba, ng, wout, lng, lnb, start_pos):
    nb, t, _ = x.shape
    bs = SAMPLE_SEQS
    const2 = lambda i: (0, 0)
    seq3 = lambda i: (i, 0, 0)
    hist3 = lambda i: (0, i, 0)
    return pl.pallas_call(
        functools.partial(_sample_body, bs=bs, t=t, start_pos=start_pos),
        grid=(nb // bs,),
        in_specs=[
            pl.BlockSpec((bs, t, D_MODEL), seq3),
            pl.BlockSpec((POOL_BUF, bs, POOL_WIDTH), hist3),
            pl.BlockSpec((bs, GLA_KEY_WIDTH, GLA_DV), seq3),
            pl.BlockSpec(win.shape, const2),
            pl.BlockSpec(wmap.shape, lambda i: (0, 0, 0)),
            pl.BlockSpec(pscale.shape, const2),
            pl.BlockSpec(ba.shape, const2),
            pl.BlockSpec(ng.shape, const2),
            pl.BlockSpec(wout.shape, const2),
            pl.BlockSpec(lng.shape, const2),
            pl.BlockSpec(lnb.shape, const2),
        ],
        out_specs=[
            pl.BlockSpec((bs, t, D_MODEL), seq3),
            pl.BlockSpec((POOL_BUF, bs, POOL_WIDTH), hist3),
            pl.BlockSpec((bs, GLA_KEY_WIDTH, GLA_DV), seq3),
        ],
        out_shape=[
            jax.ShapeDtypeStruct((nb, t, D_MODEL), F32),
            jax.ShapeDtypeStruct((POOL_BUF, nb, POOL_WIDTH), F32),
            jax.ShapeDtypeStruct((nb, GLA_KEY_WIDTH, GLA_DV), F32),
        ],
        scratch_shapes=[pltpu.VMEM((len(POOL_WINDOWS), bs * t, POOL_GROUP_DIM), F32),
                        pltpu.VMEM((len(POOL_WINDOWS), bs * t, POOL_GROUP_DIM), F32)],
        compiler_params=pltpu.CompilerParams(
            dimension_semantics=("arbitrary",), vmem_limit_bytes=VMEM_LIMIT),
        name="sample_layer",
    )(x, pool_in, s_in, win, wmap, pscale, ba, ng, wout, lng, lnb)


def _prep_body(wt_ref, wfat_ref, wa2t_ref, out_ref, *, n_plain):
    i = pl.program_id(0)

    @pl.when(i < n_plain)
    def _():
        out_ref[...] = jnp.transpose(wt_ref[...]).astype(BF16)

    @pl.when(i == n_plain)
    def _():
        wfat = wfat_ref[...]
        wa2t = wa2t_ref[...]
        acc = wa2t[:, 0:1] * wfat[0:1, :]
        for r in range(1, GLA_RANK):
            acc = acc + wa2t[:, r:r + 1] * wfat[r:r + 1, :]
        out_ref[...] = jnp.transpose(acc).astype(BF16)


def _prep_input_weights(wt, wa2t):
    n_plain = C_GL // MXU_COLS
    return pl.pallas_call(
        functools.partial(_prep_body, n_plain=n_plain),
        grid=(n_plain + 1,),
        in_specs=[
            pl.BlockSpec((MXU_COLS, D_MODEL), lambda i: (jnp.minimum(i, n_plain - 1), 0)),
            pl.BlockSpec((GLA_RANK, D_MODEL), lambda i: (0, 0)),
            pl.BlockSpec((GLA_KEY_WIDTH, GLA_RANK), lambda i: (0, 0)),
        ],
        out_specs=pl.BlockSpec((D_MODEL, MXU_COLS), lambda i: (0, i)),
        out_shape=jax.ShapeDtypeStruct((D_MODEL, C_END), BF16),
        compiler_params=pltpu.CompilerParams(dimension_semantics=("arbitrary",)),
        name="prep_input_weights",
    )(wt, wt[C_GL:], wa2t)


def kernel(x_prompt, x_sample, state_pool, state_gla, w_in, w_pool_map, pool_scale, w_gla_a2, b_gla_a,
           gla_norm_g, w_out, ln_g, ln_b):
    assert w_in.shape[0] == DEPTH
    win = _prep_input_weights(jnp.transpose(w_in[0]), jnp.transpose(w_gla_a2[0]))
    wmap = w_pool_map[0].astype(BF16)
    wout = w_out[0].astype(BF16)
    pscale = pool_scale[0].reshape(1, POOL_WIDTH)
    ba = b_gla_a[0].reshape(1, GLA_KEY_WIDTH)
    ng = gla_norm_g[0].reshape(1, GLA_WIDTH)
    lng = ln_g[0].reshape(1, D_MODEL)
    lnb = ln_b[0].reshape(1, D_MODEL)
    params = (win, wmap, pscale, ba, ng, wout, lng, lnb)

    y_p, pool_p, gla_p = _prompt_call(x_prompt, *params)
    nsamp = x_sample.shape[0]
    y_s, pool_s, gla_s = _sample_call(
        x_sample, jnp.transpose(state_pool[0], (1, 0, 2)),
        state_gla[0].reshape(nsamp, GLA_KEY_WIDTH, GLA_DV), *params, start_pos=PAST_LEN)

    nb = x_prompt.shape[0]
    return (y_p, y_s, pool_p[None],
            gla_p.reshape(1, nb, GLA_HEADS, GLA_DK, GLA_DV),
            jnp.transpose(pool_s, (1, 0, 2))[None],
            gla_s.reshape(1, nsamp, GLA_HEADS, GLA_DK, GLA_DV))
```

```python
import functools

import jax
import jax.numpy as jnp
from jax import lax
from jax.experimental import pallas as pl
from jax.experimental.pallas import tpu as pltpu

F32 = jnp.float32
BF16 = jnp.bfloat16

D_MODEL = 1024
POOL_WIDTH = 512
POOL_WINDOWS = (2, 4, 8, 16)
POOL_GROUP_DIM = 128
POOL_BUF = 15
HIST_ROWS = 16
GLA_HEADS = 4
GLA_DK = 64
GLA_DV = 128
GLA_KEY_WIDTH = GLA_HEADS * GLA_DK
GLA_WIDTH = GLA_HEADS * GLA_DV
GLA_RANK = 16
GLA_TAU = 16.0
LN_EPS = 1e-5
RMS_EPS = 1e-6
DEPTH = 1
DEEPNORM_ALPHA = (2.0 * DEPTH) ** 0.25
PAST_LEN = 16384

C_U, C_GP, C_Q, C_K, C_V, C_GG, C_GL, C_END = 0, 512, 1024, 1280, 1536, 2048, 2560, 2816

MXU_COLS = 256
PROMPT_TILE = 256
PUMP_ROWS = 128
SAFE_LOG_DECAY = 60.0
GLA_CHUNK = 64
SAMPLE_SEQS = 16
VMEM_LIMIT = 48 * 1024 * 1024


def _dot(a, b):
    return jnp.dot(a, b, preferred_element_type=F32)


def _dot_nt(a, b):
    return lax.dot_general(a, b, (((1,), (1,)), ((), ())), preferred_element_type=F32)


def _dot_tn(a, b):
    return lax.dot_general(a, b, (((0,), (0,)), ((), ())), preferred_element_type=F32)


def _silu(x):
    return x * (1.0 / (1.0 + jnp.exp(-x)))


def _log_sigmoid(x):
    return jnp.minimum(x, 0.0) - jnp.log1p(jnp.exp(-jnp.abs(x)))


def _split_bf16(x):
    hi = x.astype(BF16)
    lo = (x - hi.astype(F32)).astype(BF16)
    return hi, lo


def _epilogue(x, pool_act, o, gg, ng_ref, wout_ref, lng_ref, lnb_ref, pump=lambda n: None):
    pump(1)
    normed = []
    for h in range(GLA_HEADS):
        oh = o[:, h * GLA_DV:(h + 1) * GLA_DV]
        ms = jnp.mean(oh * oh, axis=-1, keepdims=True)
        normed.append(oh * lax.rsqrt(ms + RMS_EPS))
    gla_act = jnp.concatenate(normed, axis=1) * ng_ref[...] * _silu(gg)
    mix = jnp.concatenate([pool_act, gla_act], axis=1).astype(BF16)
    y = _dot(mix, wout_ref[...])
    pump(1 << 30)
    h = DEEPNORM_ALPHA * x + y
    mu = jnp.mean(h, axis=-1, keepdims=True)
    d = h - mu
    var = jnp.mean(d * d, axis=-1, keepdims=True)
    return d * lax.rsqrt(var + LN_EPS) * lng_ref[...] + lnb_ref[...]


def _head_block_mask(rows_per_head, cols_per_head, shape):
    r = lax.broadcasted_iota(jnp.int32, shape, 0) // rows_per_head
    c = lax.broadcasted_iota(jnp.int32, shape, 1) // cols_per_head
    return r == c


def _score_masks(rows, n_levels):
    shape = (rows, GLA_HEADS * rows)
    t_idx = lax.broadcasted_iota(jnp.int32, shape, 0)
    s_idx = lax.broadcasted_iota(jnp.int32, shape, 1) % rows
    masks = [t_idx == s_idx]
    for lvl in range(n_levels):
        half = 1 << lvl
        blk = 2 * half
        masks.append((t_idx // blk == s_idx // blk) & (t_idx % blk >= half) & (s_idx % blk < half))
    return masks


def _decayed_scores(q, k, la, b, masks, head_blk, pump=lambda n: None):
    rows = q.shape[0]
    r_idx = lax.broadcasted_iota(jnp.int32, q.shape, 0)

    def scores(qt, kt):
        kh = jnp.where(head_blk, jnp.concatenate([kt.astype(BF16)] * GLA_HEADS, axis=0), jnp.zeros((), BF16))
        return _dot_nt(qt.astype(BF16), kh)

    acc = jnp.where(masks[0], scores(q, k), 0.0)
    for lvl in range(len(masks) - 1):
        half = 1 << lvl
        blk = 2 * half
        if lvl == 0:
            neg = jnp.where(r_idx % 2 == 1, la, 0.0)
        elif lvl == 1:
            r4 = r_idx % 4
            nxt = pltpu.roll(la, rows - 1, 0)
            prv = pltpu.roll(la, 1, 0)
            neg = jnp.where(r4 == 0, nxt, jnp.where(r4 == 1, 0.0, jnp.where(r4 == 2, la, la + prv)))
        else:
            ref = jnp.concatenate(
                [jnp.broadcast_to(b[j * blk + half - 1:j * blk + half, :], (blk, b.shape[1]))
                 for j in range(rows // blk)], axis=0)
            d = b - ref
            neg = jnp.minimum(d, -d)
        w = jnp.exp(neg)
        acc = jnp.where(masks[lvl + 1], scores(q * w, k * w), acc)
        pump(lvl % 2)
    return acc.astype(BF16)


def _pair_block_diag(a0, a1):
    z = jnp.zeros_like(a0)
    return jnp.concatenate(
        [jnp.concatenate([a0, z], axis=1), jnp.concatenate([z, a1], axis=1)], axis=0)


def _make_pump(units, xb_ref, win_ref, proj_ref):
    units = iter(units)

    def pump(n):
        for _ in range(n):
            unit = next(units, None)
            if unit is not None:
                r, lo = unit
                proj_ref[r:r + PUMP_ROWS, lo:lo + MXU_COLS] = _dot(
                    xb_ref[r:r + PUMP_ROWS, :], win_ref[:, lo:lo + MXU_COLS])

    return pump


def _naive_scores(q, k, b, causal, head_blk):
    ks = (k * jnp.exp(-b)).astype(BF16)
    kh = jnp.where(head_blk, jnp.concatenate([ks] * GLA_HEADS, axis=0), jnp.zeros((), BF16))
    return jnp.where(causal, _dot_nt((q * jnp.exp(b)).astype(BF16), kh), 0.0).astype(BF16)


def _prompt_stage_b(jb, proj_ref, xkeep_ref, la_ref, wmap_ref, pscale_ref, ba_ref, ng_ref, wout_ref, lng_ref,
                    lnb_ref, y_ref, ext_ref, s_ref, pump, next_decay, *, tc, robust):
    u = proj_ref[:, C_U:C_GP]
    ext_ref[HIST_ROWS:HIST_ROWS + tc, :] = u
    pos16 = jb * tc + lax.broadcasted_iota(jnp.int32, (HIST_ROWS, POOL_GROUP_DIM), 0)
    mixed = []
    for g, w in enumerate(POOL_WINDOWS):
        sl = slice(g * POOL_GROUP_DIM, (g + 1) * POOL_GROUP_DIM)
        e = ext_ref[:, sl]
        s = e
        sh = 1
        while sh < w:
            s = s + pltpu.roll(s, sh, 0)
            sh *= 2
        wsum = s[HIST_ROWS:]
        self_rows = e[HIST_ROWS:]
        cnt = jnp.minimum(pos16 + 1, w).astype(F32)
        first = wsum[:HIST_ROWS] / cnt - self_rows[:HIST_ROWS]
        rest = wsum[HIST_ROWS:] * (1.0 / w) - self_rows[HIST_ROWS:]
        pooled = jnp.concatenate([first, rest], axis=0)
        mixed.append(_dot(pooled.astype(BF16), wmap_ref[g]))
        pump(1)
    pool_act = jnp.concatenate(mixed, axis=1) * pscale_ref[...] * _silu(proj_ref[:, C_GP:C_Q])
    next_decay()

    cc = GLA_CHUNK
    rr = lax.broadcasted_iota(jnp.int32, (cc, cc), 0)
    cl = lax.broadcasted_iota(jnp.int32, (cc, cc), 1)
    ltri = jnp.where(cl <= rr, 1.0, 0.0).astype(BF16)
    masks = _score_masks(cc, cc.bit_length() - 1 if robust else 0)
    causal = (lax.broadcasted_iota(jnp.int32, (cc, GLA_HEADS * cc), 1) % cc
              <= lax.broadcasted_iota(jnp.int32, (cc, GLA_HEADS * cc), 0))
    blk = _head_block_mask(cc, GLA_DK, (GLA_HEADS * cc, GLA_KEY_WIDTH))

    o_chunks = []
    for c in range(tc // cc):
        rs = slice(c * cc, (c + 1) * cc)
        la_c = la_ref[rs, :]
        hi, lo = _split_bf16(la_c)
        b = _dot(ltri, hi) + _dot(ltri, lo)
        pump(1)
        b_end = b[cc - 1:cc, :]
        q_c = proj_ref[rs, C_Q:C_K] * (GLA_DK ** -0.5)
        k_c = proj_ref[rs, C_K:C_V]
        qs = (q_c * jnp.exp(b)).astype(BF16)
        kd = (k_c * jnp.exp(b_end - b)).astype(BF16)
        if robust:
            a_all = _decayed_scores(q_c, k_c, la_c, b, masks, blk, pump)
        else:
            a_all = _naive_scores(q_c, k_c, b, causal, blk)
            pump(2)
        s_old = s_ref[...]
        s_bf = s_old.astype(BF16)
        vc = proj_ref[rs, C_V:C_GG].astype(BF16)
        outs, upds = [], []
        for p in range(GLA_HEADS // 2):
            h0, h1 = 2 * p, 2 * p + 1
            lhs = jnp.concatenate([a_all[:, p * 2 * cc:(p + 1) * 2 * cc],
                                   qs[:, p * 2 * GLA_DK:(p + 1) * 2 * GLA_DK]], axis=1)
            rhs = jnp.concatenate([
                _pair_block_diag(vc[:, h0 * GLA_DV:(h0 + 1) * GLA_DV], vc[:, h1 * GLA_DV:(h1 + 1) * GLA_DV]),
                _pair_block_diag(s_bf[h0 * GLA_DK:(h0 + 1) * GLA_DK], s_bf[h1 * GLA_DK:(h1 + 1) * GLA_DK]),
            ], axis=0)
            outs.append(_dot(lhs, rhs))
            upd = _dot_tn(kd[:, p * 2 * GLA_DK:(p + 1) * 2 * GLA_DK], vc[:, p * 2 * GLA_DV:(p + 1) * 2 * GLA_DV])
            upds.append(upd[0:GLA_DK, 0:GLA_DV])
            upds.append(upd[GLA_DK:2 * GLA_DK, GLA_DV:2 * GLA_DV])
        o_chunks.append(jnp.concatenate(outs, axis=1))
        dcol = jnp.transpose(jnp.broadcast_to(jnp.exp(b_end), (GLA_DV, GLA_KEY_WIDTH)))
        s_ref[...] = s_old * dcol + jnp.concatenate(upds, axis=0)
    o = jnp.concatenate(o_chunks, axis=0)

    y_ref[0] = _epilogue(xkeep_ref[...], pool_act, o, proj_ref[:, C_GG:C_GL], ng_ref, wout_ref, lng_ref, lnb_ref,
                         pump)


def _prompt_body(x_ref, win_ref, wmap_ref, pscale_ref, ba_ref, ng_ref, wout_ref, lng_ref, lnb_ref,
                 y_ref, pool_ref, s_out_ref,
                 proj0_ref, proj1_ref, xkeep0_ref, xkeep1_ref, la0_ref, la1_ref, xb_ref, ext_ref, s_ref, steep_ref,
                 *, tc, nj):
    i = pl.program_id(0)
    jb = (i + nj - 1) % nj

    @pl.when(i == 0)
    def _():
        proj1_ref[...] = jnp.zeros_like(proj1_ref)
        xkeep1_ref[...] = jnp.zeros_like(xkeep1_ref)
        la1_ref[...] = jnp.zeros_like(la1_ref)
        steep_ref[0] = 0

    @pl.when((i == 0) | (jb == 0))
    def _():
        ext_ref[0:HIST_ROWS, :] = jnp.zeros((HIST_ROWS, POOL_WIDTH), F32)
        s_ref[...] = jnp.zeros_like(s_ref)

    units = [(r, lo) for lo in (C_GL, *range(0, C_GL, MXU_COLS)) for r in range(0, tc, PUMP_ROWS)]

    def step(proj_a, xkeep_a, la_a, proj_b, xkeep_b, la_b, robust):
        x = x_ref[0]
        xkeep_a[...] = x
        xb_ref[...] = x.astype(BF16)
        pump = _make_pump(units, xb_ref, win_ref, proj_a)
        pump(tc // PUMP_ROWS)

        def next_decay():
            la = _log_sigmoid(proj_a[:, C_GL:C_END] + ba_ref[...]) * (1.0 / GLA_TAU)
            la_a[...] = la
            chunk_decay = [jnp.sum(la[c * GLA_CHUNK:(c + 1) * GLA_CHUNK], axis=0, keepdims=True)
                           for c in range(tc // GLA_CHUNK)]
            lowest = jnp.min(functools.reduce(jnp.minimum, chunk_decay))
            steep_ref[0] = jnp.where(lowest < -SAFE_LOG_DECAY, 1, 0).astype(jnp.int32)

        _prompt_stage_b(jb, proj_b, xkeep_b, la_b, wmap_ref, pscale_ref, ba_ref, ng_ref, wout_ref, lng_ref,
                        lnb_ref, y_ref, ext_ref, s_ref, pump, next_decay, tc=tc, robust=robust)

    set0 = (proj0_ref, xkeep0_ref, la0_ref)
    set1 = (proj1_ref, xkeep1_ref, la1_ref)
    steep = steep_ref[0] != 0
    for parity, (set_a, set_b) in enumerate(((set0, set1), (set1, set0))):
        for robust in (False, True):
            @pl.when((i % 2 == parity) & (steep == robust))
            def _(set_a=set_a, set_b=set_b, robust=robust):
                step(*set_a, *set_b, robust)

    @pl.when((i > 0) & (jb == nj - 1))
    def _():
        pool_ref[0] = ext_ref[tc + 1:tc + HIST_ROWS, :]
        s_out_ref[0] = s_ref[...]

    ext_ref[0:HIST_ROWS, :] = ext_ref[tc:tc + HIST_ROWS, :]


def _prompt_call(x, win, wmap, pscale, ba, ng, wout, lng, lnb):
    nb, t, _ = x.shape
    tc = PROMPT_TILE
    nj = t // tc
    ntiles = nb * nj
    const2 = lambda i: (0, 0)

    def tile_a(i):
        ia = jnp.minimum(i, ntiles - 1)
        return (ia // nj, ia % nj, 0)

    def tile_b(i):
        ib = jnp.maximum(i - 1, 0)
        return (ib // nj, ib % nj, 0)

    def seq_b(i):
        return (jnp.maximum(i - 1, 0) // nj, 0, 0)

    return pl.pallas_call(
        functools.partial(_prompt_body, tc=tc, nj=nj),
        grid=(ntiles + 1,),
        in_specs=[
            pl.BlockSpec((1, tc, D_MODEL), tile_a),
            pl.BlockSpec(win.shape, const2),
            pl.BlockSpec(wmap.shape, lambda i: (0, 0, 0)),
            pl.BlockSpec(pscale.shape, const2),
            pl.BlockSpec(ba.shape, const2),
            pl.BlockSpec(ng.shape, const2),
            pl.BlockSpec(wout.shape, const2),
            pl.BlockSpec(lng.shape, const2),
            pl.BlockSpec(lnb.shape, const2),
        ],
        out_specs=[
            pl.BlockSpec((1, tc, D_MODEL), tile_b),
            pl.BlockSpec((1, POOL_BUF, POOL_WIDTH), seq_b),
            pl.BlockSpec((1, GLA_KEY_WIDTH, GLA_DV), seq_b),
        ],
        out_shape=[
            jax.ShapeDtypeStruct((nb, t, D_MODEL), F32),
            jax.ShapeDtypeStruct((nb, POOL_BUF, POOL_WIDTH), F32),
            jax.ShapeDtypeStruct((nb, GLA_KEY_WIDTH, GLA_DV), F32),
        ],
        scratch_shapes=[
            pltpu.VMEM((tc, C_END), F32),
            pltpu.VMEM((tc, C_END), F32),
            pltpu.VMEM((tc, D_MODEL), F32),
            pltpu.VMEM((tc, D_MODEL), F32),
            pltpu.VMEM((tc, GLA_KEY_WIDTH), F32),
            pltpu.VMEM((tc, GLA_KEY_WIDTH), F32),
            pltpu.VMEM((tc, D_MODEL), BF16),
            pltpu.VMEM((HIST_ROWS + tc, POOL_WIDTH), F32),
            pltpu.VMEM((GLA_KEY_WIDTH, GLA_DV), F32),
            pltpu.SMEM((1,), jnp.int32),
        ],
        compiler_params=pltpu.CompilerParams(
            dimension_semantics=("arbitrary",), vmem_limit_bytes=VMEM_LIMIT),
        name="prompt_layer",
    )(x, win, wmap, pscale, ba, ng, wout, lng, lnb)


def _sample_body(x_ref, pool_in_ref, s_in_ref, win_ref, wmap_ref, pscale_ref, ba_ref, ng_ref, wout_ref,
                 lng_ref, lnb_ref, y_ref, pool_ref, s_out_ref, u_ref, pooled_ref, *, bs, t, start_pos):
    rows = bs * t
    x = x_ref[...].reshape(rows, D_MODEL)
    xb = x.astype(BF16)

    def proj(lo, hi):
        return _dot(xb, win_ref[:, lo:hi])

    u = proj(C_U, C_GP)
    mixed = []
    for g, w in enumerate(POOL_WINDOWS):
        sl = slice(g * POOL_GROUP_DIM, (g + 1) * POOL_GROUP_DIM)
        u_ref[g] = u[:, sl]
        ext = [pool_in_ref[r, :, sl] for r in range(POOL_BUF)]
        ext += [u_ref[g, pl.ds(tok, bs, stride=t), :] for tok in range(t)]
        for r in range(POOL_BUF):
            pool_ref[r, :, sl] = ext[t + r]
        sums = ext
        sh = 1
        while sh < w:
            first = 2 * sh - 1
            sums = [None] * first + [sums[r] + sums[r - sh] for r in range(first, len(sums))]
            sh *= 2
        for tok in range(t):
            cnt = float(min(start_pos + tok + 1, w))
            pooled_ref[g, pl.ds(tok, bs, stride=t), :] = sums[POOL_BUF + tok] / cnt - ext[POOL_BUF + tok]
        mixed.append(_dot(pooled_ref[g].astype(BF16), wmap_ref[g]))
    gp = proj(C_GP, C_Q)
    pool_act = jnp.concatenate(mixed, axis=1) * pscale_ref[...] * _silu(gp)

    q = proj(C_Q, C_K) * (GLA_DK ** -0.5)
    k = proj(C_K, C_V)
    vb = proj(C_V, C_GG).astype(BF16)
    gg = proj(C_GG, C_GL)
    la = _log_sigmoid(proj(C_GL, C_END) + ba_ref[...]) * (1.0 / GLA_TAU)

    rr = lax.broadcasted_iota(jnp.int32, (rows, rows), 0)
    cl = lax.broadcasted_iota(jnp.int32, (rows, rows), 1)
    same = (rr // t) == (cl // t)
    ltri = jnp.where(same & (cl <= rr), 1.0, 0.0).astype(BF16)
    lall = jnp.where(same, 1.0, 0.0).astype(BF16)
    hi, lo = _split_bf16(la)
    b = _dot(ltri, hi) + _dot(ltri, lo)
    b_end = _dot(lall, hi) + _dot(lall, lo)
    qs = (q * jnp.exp(b)).astype(BF16)
    kd = (k * jnp.exp(b_end - b)).astype(BF16)

    assert t & (t - 1) == 0, "tokens per sequence must be a power of two"
    blk = _head_block_mask(rows, GLA_DK, (GLA_HEADS * rows, GLA_KEY_WIDTH))
    a_all = _decayed_scores(q, k, la, b, _score_masks(rows, t.bit_length() - 1), blk)

    s_old = s_in_ref[...]
    s_bf = s_old.astype(BF16)
    qs3 = qs.reshape(bs, t, GLA_KEY_WIDTH)
    kd3 = kd.reshape(bs, t, GLA_KEY_WIDTH)
    v3 = vb.reshape(bs, t, GLA_WIDTH)
    hi3 = hi.reshape(bs, t, GLA_KEY_WIDTH)
    lo3 = lo.reshape(bs, t, GLA_KEY_WIDTH)
    ones3 = jnp.ones((bs, t, GLA_DV), BF16)
    bdims_nn = (((2,), (1,)), ((0,), (0,)))
    bdims_tn = (((1,), (1,)), ((0,), (0,)))
    dsum = (lax.dot_general(hi3, ones3, bdims_tn, preferred_element_type=F32)
            + lax.dot_general(lo3, ones3, bdims_tn, preferred_element_type=F32))
    outs, upds = [], []
    for p in range(GLA_HEADS // 2):
        h0, h1 = 2 * p, 2 * p + 1
        intra = _dot(a_all[:, p * 2 * rows:(p + 1) * 2 * rows],
                     _pair_block_diag(vb[:, h0 * GLA_DV:(h0 + 1) * GLA_DV], vb[:, h1 * GLA_DV:(h1 + 1) * GLA_DV]))
        s0 = s_bf[:, h0 * GLA_DK:(h0 + 1) * GLA_DK, :]
        s1 = s_bf[:, h1 * GLA_DK:(h1 + 1) * GLA_DK, :]
        z = jnp.zeros_like(s0)
        s_pair = jnp.concatenate(
            [jnp.concatenate([s0, z], axis=2), jnp.concatenate([z, s1], axis=2)], axis=1)
        inter = lax.dot_general(qs3[:, :, p * 2 * GLA_DK:(p + 1) * 2 * GLA_DK], s_pair, bdims_nn,
                                preferred_element_type=F32)
        outs.append(intra + inter.reshape(rows, 2 * GLA_DV))
        upd = lax.dot_general(kd3[:, :, p * 2 * GLA_DK:(p + 1) * 2 * GLA_DK],
                              v3[:, :, p * 2 * GLA_DV:(p + 1) * 2 * GLA_DV], bdims_tn,
                              preferred_element_type=F32)
        upds.append(upd[:, 0:GLA_DK, 0:GLA_DV])
        upds.append(upd[:, GLA_DK:2 * GLA_DK, GLA_DV:2 * GLA_DV])
    o = jnp.concatenate(outs, axis=1)
    s_out_ref[...] = s_old * jnp.exp(dsum) + jnp.concatenate(upds, axis=1)

    y = _epilogue(x, pool_act, o, gg, ng_ref, wout_ref, lng_ref, lnb_ref)
    y_ref[...] = y.reshape(bs, t, D_MODEL)


def _sample_call(x, pool_in, s_in, win, wmap, pscale, ba, ng, wout, lng, lnb, start_pos):
    nb, t, _ = x.shape
    bs = SAMPLE_SEQS
    const2 = lambda i: (0, 0)
    seq3 = lambda i: (i, 0, 0)
    hist3 = lambda i: (0, i, 0)
    return pl.pallas_call(
        functools.partial(_sample_body, bs=bs, t=t, start_pos=start_pos),
        grid=(nb // bs,),
        in_specs=[
            pl.BlockSpec((bs, t, D_MODEL), seq3),
            pl.BlockSpec((POOL_BUF, bs, POOL_WIDTH), hist3),
            pl.BlockSpec((bs, GLA_KEY_WIDTH, GLA_DV), seq3),
            pl.BlockSpec(win.shape, const2),
            pl.BlockSpec(wmap.shape, lambda i: (0, 0, 0)),
            pl.BlockSpec(pscale.shape, const2),
            pl.BlockSpec(ba.shape, const2),
            pl.BlockSpec(ng.shape, const2),
            pl.BlockSpec(wout.shape, const2),
            pl.BlockSpec(lng.shape, const2),
            pl.BlockSpec(lnb.shape, const2),
        ],
        out_specs=[
            pl.BlockSpec((bs, t, D_MODEL), seq3),
            pl.BlockSpec((POOL_BUF, bs, POOL_WIDTH), hist3),
            pl.BlockSpec((bs, GLA_KEY_WIDTH, GLA_DV), seq3),
        ],
        out_shape=[
            jax.ShapeDtypeStruct((nb, t, D_MODEL), F32),
            jax.ShapeDtypeStruct((POOL_BUF, nb, POOL_WIDTH), F32),
            jax.ShapeDtypeStruct((nb, GLA_KEY_WIDTH, GLA_DV), F32),
        ],
        scratch_shapes=[pltpu.VMEM((len(POOL_WINDOWS), bs * t, POOL_GROUP_DIM), F32),
                        pltpu.VMEM((len(POOL_WINDOWS), bs * t, POOL_GROUP_DIM), F32)],
        compiler_params=pltpu.CompilerParams(
            dimension_semantics=("arbitrary",), vmem_limit_bytes=VMEM_LIMIT),
        name="sample_layer",
    )(x, pool_in, s_in, win, wmap, pscale, ba, ng, wout, lng, lnb)


def _prep_body(wt_ref, wfat_ref, wa2t_ref, out_ref, *, n_plain):
    i = pl.program_id(0)

    @pl.when(i < n_plain)
    def _():
        out_ref[...] = jnp.transpose(wt_ref[...]).astype(BF16)

    @pl.when(i == n_plain)
    def _():
        wfat = wfat_ref[...]
        wa2t = wa2t_ref[...]
        acc = wa2t[:, 0:1] * wfat[0:1, :]
        for r in range(1, GLA_RANK):
            acc = acc + wa2t[:, r:r + 1] * wfat[r:r + 1, :]
        out_ref[...] = jnp.transpose(acc).astype(BF16)


def _prep_input_weights(wt, wa2t):
    n_plain = C_GL // MXU_COLS
    return pl.pallas_call(
        functools.partial(_prep_body, n_plain=n_plain),
        grid=(n_plain + 1,),
        in_specs=[
            pl.BlockSpec((MXU_COLS, D_MODEL), lambda i: (jnp.minimum(i, n_plain - 1), 0)),
            pl.BlockSpec((GLA_RANK, D_MODEL), lambda i: (0, 0)),
            pl.BlockSpec((GLA_KEY_WIDTH, GLA_RANK), lambda i: (0, 0)),
        ],
        out_specs=pl.BlockSpec((D_MODEL, MXU_COLS), lambda i: (0, i)),
        out_shape=jax.ShapeDtypeStruct((D_MODEL, C_END), BF16),
        compiler_params=pltpu.CompilerParams(dimension_semantics=("arbitrary",)),
        name="prep_input_weights",
    )(wt, wt[C_GL:], wa2t)


def kernel(x_prompt, x_sample, state_pool, state_gla, w_in, w_pool_map, pool_scale, w_gla_a2, b_gla_a,
           gla_norm_g, w_out, ln_g, ln_b):
    assert w_in.shape[0] == DEPTH
    win = _prep_input_weights(jnp.transpose(w_in[0]), jnp.transpose(w_gla_a2[0]))
    wmap = w_pool_map[0].astype(BF16)
    wout = w_out[0].astype(BF16)
    pscale = pool_scale[0].reshape(1, POOL_WIDTH)
    ba = b_gla_a[0].reshape(1, GLA_KEY_WIDTH)
    ng = gla_norm_g[0].reshape(1, GLA_WIDTH)
    lng = ln_g[0].reshape(1, D_MODEL)
    lnb = ln_b[0].reshape(1, D_MODEL)
    params = (win, wmap, pscale, ba, ng, wout, lng, lnb)

    y_p, pool_p, gla_p = _prompt_call(x_prompt, *params)
    nsamp = x_sample.shape[0]
    y_s, pool_s, gla_s = _sample_call(
        x_sample, jnp.transpose(state_pool[0], (1, 0, 2)),
        state_gla[0].reshape(nsamp, GLA_KEY_WIDTH, GLA_DV), *params, start_pos=PAST_LEN)

    nb = x_prompt.shape[0]
    return (y_p, y_s, pool_p[None],
            gla_p.reshape(1, nb, GLA_HEADS, GLA_DK, GLA_DV),
            jnp.transpose(pool_s, (1, 0, 2))[None],
            gla_s.reshape(1, nsamp, GLA_HEADS, GLA_DK, GLA_DV))
```

```python
import functools
import itertools

import jax
import jax.numpy as jnp
from jax import lax
from jax.experimental import pallas as pl
from jax.experimental.pallas import tpu as pltpu

F32 = jnp.float32
BF16 = jnp.bfloat16

D_MODEL = 1024
POOL_WIDTH = 512
POOL_WINDOWS = (2, 4, 8, 16)
POOL_GROUP_DIM = 128
POOL_BUF = 15
HIST_ROWS = 16
GLA_HEADS = 4
GLA_DK = 64
GLA_DV = 128
GLA_KEY_WIDTH = GLA_HEADS * GLA_DK
GLA_WIDTH = GLA_HEADS * GLA_DV
GLA_RANK = 16
GLA_TAU = 16.0
LN_EPS = 1e-5
RMS_EPS = 1e-6
DEPTH = 1
DEEPNORM_ALPHA = (2.0 * DEPTH) ** 0.25
PAST_LEN = 16384

C_U, C_GP, C_Q, C_K, C_V, C_GG, C_GL, C_END = 0, 512, 1024, 1280, 1536, 2048, 2560, 2816

MXU_COLS = 256
PROMPT_TILE = 512
PROMPT_SUBTILE = 256
SUBTILE_TAIL_UNITS = 3
PUMP_ROWS = 128
SAFE_LOG_DECAY = 60.0
GLA_CHUNK = 64
SAMPLE_SEQS = 16
VMEM_LIMIT = 56 * 1024 * 1024


def _dot(a, b):
    return jnp.dot(a, b, preferred_element_type=F32)


def _dot_nt(a, b):
    return lax.dot_general(a, b, (((1,), (1,)), ((), ())), preferred_element_type=F32)


def _dot_tn(a, b):
    return lax.dot_general(a, b, (((0,), (0,)), ((), ())), preferred_element_type=F32)


def _silu(x):
    return x * (1.0 / (1.0 + jnp.exp(-x)))


def _log_sigmoid(x):
    return jnp.minimum(x, 0.0) - jnp.log1p(jnp.exp(-jnp.abs(x)))


def _split_bf16(x):
    hi = x.astype(BF16)
    lo = (x - hi.astype(F32)).astype(BF16)
    return hi, lo


def _epilogue(x, pool_act, o, gg, ng_ref, wout_ref, lng_ref, lnb_ref, pump=lambda n=None: None, tail=None):
    pump(1)
    normed = []
    for h in range(GLA_HEADS):
        oh = o[:, h * GLA_DV:(h + 1) * GLA_DV]
        ms = jnp.mean(oh * oh, axis=-1, keepdims=True)
        normed.append(oh * lax.rsqrt(ms + RMS_EPS))
    gla_act = jnp.concatenate(normed, axis=1) * ng_ref[...] * _silu(gg)
    mix = jnp.concatenate([pool_act, gla_act], axis=1).astype(BF16)
    y = _dot(mix, wout_ref[...])
    pump(tail)
    h = DEEPNORM_ALPHA * x + y
    mu = jnp.mean(h, axis=-1, keepdims=True)
    d = h - mu
    var = jnp.mean(d * d, axis=-1, keepdims=True)
    return d * lax.rsqrt(var + LN_EPS) * lng_ref[...] + lnb_ref[...]


def _head_block_mask(rows_per_head, cols_per_head, shape):
    r = lax.broadcasted_iota(jnp.int32, shape, 0) // rows_per_head
    c = lax.broadcasted_iota(jnp.int32, shape, 1) // cols_per_head
    return r == c


def _score_masks(rows, n_levels):
    shape = (rows, GLA_HEADS * rows)
    t_idx = lax.broadcasted_iota(jnp.int32, shape, 0)
    s_idx = lax.broadcasted_iota(jnp.int32, shape, 1) % rows
    masks = [t_idx == s_idx]
    for lvl in range(n_levels):
        half = 1 << lvl
        blk = 2 * half
        masks.append((t_idx // blk == s_idx // blk) & (t_idx % blk >= half) & (s_idx % blk < half))
    return masks


def _decayed_scores(q, k, la, b, masks, head_blk, pump=lambda n=None: None):
    rows = q.shape[0]
    r_idx = lax.broadcasted_iota(jnp.int32, q.shape, 0)

    def scores(qt, kt):
        kh = jnp.where(head_blk, jnp.concatenate([kt.astype(BF16)] * GLA_HEADS, axis=0), jnp.zeros((), BF16))
        return _dot_nt(qt.astype(BF16), kh)

    acc = jnp.where(masks[0], scores(q, k), 0.0)
    for lvl in range(len(masks) - 1):
        half = 1 << lvl
        blk = 2 * half
        if lvl == 0:
            neg = jnp.where(r_idx % 2 == 1, la, 0.0)
        elif lvl == 1:
            r4 = r_idx % 4
            nxt = pltpu.roll(la, rows - 1, 0)
            prv = pltpu.roll(la, 1, 0)
            neg = jnp.where(r4 == 0, nxt, jnp.where(r4 == 1, 0.0, jnp.where(r4 == 2, la, la + prv)))
        else:
            ref = jnp.concatenate(
                [jnp.broadcast_to(b[j * blk + half - 1:j * blk + half, :], (blk, b.shape[1]))
                 for j in range(rows // blk)], axis=0)
            d = b - ref
            neg = jnp.minimum(d, -d)
        w = jnp.exp(neg)
        acc = jnp.where(masks[lvl + 1], scores(q * w, k * w), acc)
        pump(lvl % 2)
    return acc.astype(BF16)


def _pair_block_diag(a0, a1):
    z = jnp.zeros_like(a0)
    return jnp.concatenate(
        [jnp.concatenate([a0, z], axis=1), jnp.concatenate([z, a1], axis=1)], axis=0)


def _make_pump(units, xb_ref, win_ref, proj_ref):
    units = iter(units)

    def pump(n=None):
        for r, lo in (units if n is None else itertools.islice(units, n)):
            proj_ref[r:r + PUMP_ROWS, lo:lo + MXU_COLS] = _dot(
                xb_ref[r:r + PUMP_ROWS, :], win_ref[:, lo:lo + MXU_COLS])

    return pump


def _naive_scores(q, k, b, causal, head_blk):
    ks = (k * jnp.exp(-b)).astype(BF16)
    kh = jnp.where(head_blk, jnp.concatenate([ks] * GLA_HEADS, axis=0), jnp.zeros((), BF16))
    return jnp.where(causal, _dot_nt((q * jnp.exp(b)).astype(BF16), kh), 0.0).astype(BF16)


def _prompt_stage_b(pos0, proj_ref, xkeep_ref, la_ref, wmap_ref, pscale_ref, ba_ref, ng_ref, wout_ref, lng_ref,
                    lnb_ref, y_ref, ext_ref, s_ref, pump, next_decay, *, tc, robust, tail):
    u = proj_ref[:, C_U:C_GP]
    ext_ref[HIST_ROWS:HIST_ROWS + tc, :] = u
    pos16 = pos0 + lax.broadcasted_iota(jnp.int32, (HIST_ROWS, POOL_GROUP_DIM), 0)
    mixed = []
    for g, w in enumerate(POOL_WINDOWS):
        sl = slice(g * POOL_GROUP_DIM, (g + 1) * POOL_GROUP_DIM)
        e = ext_ref[:, sl]
        s = e
        sh = 1
        while sh < w:
            s = s + pltpu.roll(s, sh, 0)
            sh *= 2
        wsum = s[HIST_ROWS:]
        self_rows = e[HIST_ROWS:]
        cnt = jnp.minimum(pos16 + 1, w).astype(F32)
        first = wsum[:HIST_ROWS] / cnt - self_rows[:HIST_ROWS]
        rest = wsum[HIST_ROWS:] * (1.0 / w) - self_rows[HIST_ROWS:]
        pooled = jnp.concatenate([first, rest], axis=0)
        mixed.append(_dot(pooled.astype(BF16), wmap_ref[g]))
        pump(1)
    pool_act = jnp.concatenate(mixed, axis=1) * pscale_ref[...] * _silu(proj_ref[:, C_GP:C_Q])
    next_decay()

    cc = GLA_CHUNK
    rr = lax.broadcasted_iota(jnp.int32, (cc, cc), 0)
    cl = lax.broadcasted_iota(jnp.int32, (cc, cc), 1)
    ltri = jnp.where(cl <= rr, 1.0, 0.0).astype(BF16)
    masks = _score_masks(cc, cc.bit_length() - 1 if robust else 0)
    causal = (lax.broadcasted_iota(jnp.int32, (cc, GLA_HEADS * cc), 1) % cc
              <= lax.broadcasted_iota(jnp.int32, (cc, GLA_HEADS * cc), 0))
    blk = _head_block_mask(cc, GLA_DK, (GLA_HEADS * cc, GLA_KEY_WIDTH))

    o_chunks = []
    for c in range(tc // cc):
        rs = slice(c * cc, (c + 1) * cc)
        la_c = la_ref[rs, :]
        hi, lo = _split_bf16(la_c)
        b = _dot(ltri, hi) + _dot(ltri, lo)
        pump(1)
        b_end = b[cc - 1:cc, :]
        q_c = proj_ref[rs, C_Q:C_K] * (GLA_DK ** -0.5)
        k_c = proj_ref[rs, C_K:C_V]
        qs = (q_c * jnp.exp(b)).astype(BF16)
        kd = (k_c * jnp.exp(b_end - b)).astype(BF16)
        if robust:
            a_all = _decayed_scores(q_c, k_c, la_c, b, masks, blk, pump)
        else:
            a_all = _naive_scores(q_c, k_c, b, causal, blk)
            pump(2)
        s_old = s_ref[...]
        s_bf = s_old.astype(BF16)
        vc = proj_ref[rs, C_V:C_GG].astype(BF16)
        outs, upds = [], []
        for p in range(GLA_HEADS // 2):
            h0, h1 = 2 * p, 2 * p + 1
            lhs = jnp.concatenate([a_all[:, p * 2 * cc:(p + 1) * 2 * cc],
                                   qs[:, p * 2 * GLA_DK:(p + 1) * 2 * GLA_DK]], axis=1)
            rhs = jnp.concatenate([
                _pair_block_diag(vc[:, h0 * GLA_DV:(h0 + 1) * GLA_DV], vc[:, h1 * GLA_DV:(h1 + 1) * GLA_DV]),
                _pair_block_diag(s_bf[h0 * GLA_DK:(h0 + 1) * GLA_DK], s_bf[h1 * GLA_DK:(h1 + 1) * GLA_DK]),
            ], axis=0)
            outs.append(_dot(lhs, rhs))
            upd = _dot_tn(kd[:, p * 2 * GLA_DK:(p + 1) * 2 * GLA_DK], vc[:, p * 2 * GLA_DV:(p + 1) * 2 * GLA_DV])
            upds.append(upd[0:GLA_DK, 0:GLA_DV])
            upds.append(upd[GLA_DK:2 * GLA_DK, GLA_DV:2 * GLA_DV])
        o_chunks.append(jnp.concatenate(outs, axis=1))
        dcol = jnp.transpose(jnp.broadcast_to(jnp.exp(b_end), (GLA_DV, GLA_KEY_WIDTH)))
        s_ref[...] = s_old * dcol + jnp.concatenate(upds, axis=0)
    o = jnp.concatenate(o_chunks, axis=0)

    y_ref[...] = _epilogue(xkeep_ref[...], pool_act, o, proj_ref[:, C_GG:C_GL], ng_ref, wout_ref, lng_ref, lnb_ref,
                           pump, tail)


def _prompt_body(x_ref, win_ref, wmap_ref, pscale_ref, ba_ref, ng_ref, wout_ref, lng_ref, lnb_ref,
                 y_ref, pool_ref, s_out_ref,
                 proj0_ref, proj1_ref, xkeep0_ref, xkeep1_ref, la0_ref, la1_ref, xb_ref, ext_ref, s_ref, steep_ref,
                 *, tc, sub, nj):
    i = pl.program_id(0)
    jb = (i + nj - 1) % nj

    @pl.when(i == 0)
    def _():
        proj1_ref[...] = jnp.zeros_like(proj1_ref)
        xkeep1_ref[...] = jnp.zeros_like(xkeep1_ref)
        la1_ref[...] = jnp.zeros_like(la1_ref)
        steep_ref[0] = 0

    @pl.when((i == 0) | (jb == 0))
    def _():
        ext_ref[0:HIST_ROWS, :] = jnp.zeros((HIST_ROWS, POOL_WIDTH), F32)
        s_ref[...] = jnp.zeros_like(s_ref)

    units = [(r, lo) for lo in (C_GL, *range(0, C_GL, MXU_COLS)) for r in range(0, tc, PUMP_ROWS)]

    def step(proj_a, xkeep_a, la_a, proj_b, xkeep_b, la_b, robust):
        x = x_ref[0]
        xkeep_a[...] = x
        xb_ref[...] = x.astype(BF16)
        pump = _make_pump(units, xb_ref, win_ref, proj_a)
        pump(tc // PUMP_ROWS)

        def next_decay():
            la = _log_sigmoid(proj_a[:, C_GL:C_END] + ba_ref[...]) * (1.0 / GLA_TAU)
            la_a[...] = la
            chunk_decay = [jnp.sum(la[c * GLA_CHUNK:(c + 1) * GLA_CHUNK], axis=0, keepdims=True)
                           for c in range(tc // GLA_CHUNK)]
            lowest = jnp.min(functools.reduce(jnp.minimum, chunk_decay))
            steep_ref[0] = jnp.where(lowest < -SAFE_LOG_DECAY, 1, 0).astype(jnp.int32)

        n_sub = tc // sub
        for k in range(n_sub):
            rows = slice(k * sub, (k + 1) * sub)
            last = k == n_sub - 1
            _prompt_stage_b((jb * n_sub + k) * sub, proj_b.at[rows, :], xkeep_b.at[rows, :], la_b.at[rows, :],
                            wmap_ref, pscale_ref, ba_ref, ng_ref, wout_ref, lng_ref, lnb_ref,
                            y_ref.at[0, rows, :], ext_ref, s_ref, pump, next_decay if k == 0 else (lambda: None),
                            tc=sub, robust=robust, tail=None if last else SUBTILE_TAIL_UNITS)
            if not last:
                ext_ref[0:HIST_ROWS, :] = ext_ref[sub:sub + HIST_ROWS, :]

    set0 = (proj0_ref, xkeep0_ref, la0_ref)
    set1 = (proj1_ref, xkeep1_ref, la1_ref)
    steep = steep_ref[0] != 0
    for parity, (set_a, set_b) in enumerate(((set0, set1), (set1, set0))):
        for robust in (False, True):
            @pl.when((i % 2 == parity) & (steep == robust))
            def _(set_a=set_a, set_b=set_b, robust=robust):
                step(*set_a, *set_b, robust)

    @pl.when((i > 0) & (jb == nj - 1))
    def _():
        pool_ref[0] = ext_ref[sub + 1:sub + HIST_ROWS, :]
        s_out_ref[0] = s_ref[...]

    ext_ref[0:HIST_ROWS, :] = ext_ref[sub:sub + HIST_ROWS, :]


def _prompt_call(x, win, wmap, pscale, ba, ng, wout, lng, lnb):
    nb, t, _ = x.shape
    tc = PROMPT_TILE
    nj = t // tc
    ntiles = nb * nj
    const2 = lambda i: (0, 0)

    def tile_a(i):
        ia = jnp.minimum(i, ntiles - 1)
        return (ia // nj, ia % nj, 0)

    def tile_b(i):
        ib = jnp.maximum(i - 1, 0)
        return (ib // nj, ib % nj, 0)

    def seq_b(i):
        return (jnp.maximum(i - 1, 0) // nj, 0, 0)

    return pl.pallas_call(
        functools.partial(_prompt_body, tc=tc, sub=PROMPT_SUBTILE, nj=nj),
        grid=(ntiles + 1,),
        in_specs=[
            pl.BlockSpec((1, tc, D_MODEL), tile_a),
            pl.BlockSpec(win.shape, const2),
            pl.BlockSpec(wmap.shape, lambda i: (0, 0, 0)),
            pl.BlockSpec(pscale.shape, const2),
            pl.BlockSpec(ba.shape, const2),
            pl.BlockSpec(ng.shape, const2),
            pl.BlockSpec(wout.shape, const2),
            pl.BlockSpec(lng.shape, const2),
            pl.BlockSpec(lnb.shape, const2),
        ],
        out_specs=[
            pl.BlockSpec((1, tc, D_MODEL), tile_b),
            pl.BlockSpec((1, POOL_BUF, POOL_WIDTH), seq_b),
            pl.BlockSpec((1, GLA_KEY_WIDTH, GLA_DV), seq_b),
        ],
        out_shape=[
            jax.ShapeDtypeStruct((nb, t, D_MODEL), F32),
            jax.ShapeDtypeStruct((nb, POOL_BUF, POOL_WIDTH), F32),
            jax.ShapeDtypeStruct((nb, GLA_KEY_WIDTH, GLA_DV), F32),
        ],
        scratch_shapes=[
            pltpu.VMEM((tc, C_END), F32),
            pltpu.VMEM((tc, C_END), F32),
            pltpu.VMEM((tc, D_MODEL), F32),
            pltpu.VMEM((tc, D_MODEL), F32),
            pltpu.VMEM((tc, GLA_KEY_WIDTH), F32),
            pltpu.VMEM((tc, GLA_KEY_WIDTH), F32),
            pltpu.VMEM((tc, D_MODEL), BF16),
            pltpu.VMEM((HIST_ROWS + PROMPT_SUBTILE, POOL_WIDTH), F32),
            pltpu.VMEM((GLA_KEY_WIDTH, GLA_DV), F32),
            pltpu.SMEM((1,), jnp.int32),
        ],
        compiler_params=pltpu.CompilerParams(
            dimension_semantics=("arbitrary",), vmem_limit_bytes=VMEM_LIMIT),
        name="prompt_layer",
    )(x, win, wmap, pscale, ba, ng, wout, lng, lnb)


def _sample_body(x_ref, pool_in_ref, s_in_ref, win_ref, wmap_ref, pscale_ref, ba_ref, ng_ref, wout_ref,
                 lng_ref, lnb_ref, y_ref, pool_ref, s_out_ref, u_ref, pooled_ref, *, bs, t, start_pos):
    rows = bs * t
    x = x_ref[...].reshape(rows, D_MODEL)
    xb = x.astype(BF16)

    def proj(lo, hi):
        return _dot(xb, win_ref[:, lo:hi])

    u = proj(C_U, C_GP)
    mixed = []
    for g, w in enumerate(POOL_WINDOWS):
        sl = slice(g * POOL_GROUP_DIM, (g + 1) * POOL_GROUP_DIM)
        u_ref[g] = u[:, sl]
        ext = [pool_in_ref[r, :, sl] for r in range(POOL_BUF)]
        ext += [u_ref[g, pl.ds(tok, bs, stride=t), :] for tok in range(t)]
        for r in range(POOL_BUF):
            pool_ref[r, :, sl] = ext[t + r]
        sums = ext
        sh = 1
        while sh < w:
            first = 2 * sh - 1
            sums = [None] * first + [sums[r] + sums[r - sh] for r in range(first, len(sums))]
            sh *= 2
        for tok in range(t):
            cnt = float(min(start_pos + tok + 1, w))
            pooled_ref[g, pl.ds(tok, bs, stride=t), :] = sums[POOL_BUF + tok] / cnt - ext[POOL_BUF + tok]
        mixed.append(_dot(pooled_ref[g].astype(BF16), wmap_ref[g]))
    gp = proj(C_GP, C_Q)
    pool_act = jnp.concatenate(mixed, axis=1) * pscale_ref[...] * _silu(gp)

    q = proj(C_Q, C_K) * (GLA_DK ** -0.5)
    k = proj(C_K, C_V)
    vb = proj(C_V, C_GG).astype(BF16)
    gg = proj(C_GG, C_GL)
    la = _log_sigmoid(proj(C_GL, C_END) + ba_ref[...]) * (1.0 / GLA_TAU)

    rr = lax.broadcasted_iota(jnp.int32, (rows, rows), 0)
    cl = lax.broadcasted_iota(jnp.int32, (rows, rows), 1)
    same = (rr // t) == (cl // t)
    ltri = jnp.where(same & (cl <= rr), 1.0, 0.0).astype(BF16)
    lall = jnp.where(same, 1.0, 0.0).astype(BF16)
    hi, lo = _split_bf16(la)
    b = _dot(ltri, hi) + _dot(ltri, lo)
    b_end = _dot(lall, hi) + _dot(lall, lo)
    qs = (q * jnp.exp(b)).astype(BF16)
    kd = (k * jnp.exp(b_end - b)).astype(BF16)

    assert t & (t - 1) == 0, "tokens per sequence must be a power of two"
    blk = _head_block_mask(rows, GLA_DK, (GLA_HEADS * rows, GLA_KEY_WIDTH))
    a_all = _decayed_scores(q, k, la, b, _score_masks(rows, t.bit_length() - 1), blk)

    s_old = s_in_ref[...]
    s_bf = s_old.astype(BF16)
    qs3 = qs.reshape(bs, t, GLA_KEY_WIDTH)
    kd3 = kd.reshape(bs, t, GLA_KEY_WIDTH)
    v3 = vb.reshape(bs, t, GLA_WIDTH)
    hi3 = hi.reshape(bs, t, GLA_KEY_WIDTH)
    lo3 = lo.reshape(bs, t, GLA_KEY_WIDTH)
    ones3 = jnp.ones((bs, t, GLA_DV), BF16)
    bdims_nn = (((2,), (1,)), ((0,), (0,)))
    bdims_tn = (((1,), (1,)), ((0,), (0,)))
    dsum = (lax.dot_general(hi3, ones3, bdims_tn, preferred_element_type=F32)
            + lax.dot_general(lo3, ones3, bdims_tn, preferred_element_type=F32))
    outs, upds = [], []
    for p in range(GLA_HEADS // 2):
        h0, h1 = 2 * p, 2 * p + 1
        intra = _dot(a_all[:, p * 2 * rows:(p + 1) * 2 * rows],
                     _pair_block_diag(vb[:, h0 * GLA_DV:(h0 + 1) * GLA_DV], vb[:, h1 * GLA_DV:(h1 + 1) * GLA_DV]))
        s0 = s_bf[:, h0 * GLA_DK:(h0 + 1) * GLA_DK, :]
        s1 = s_bf[:, h1 * GLA_DK:(h1 + 1) * GLA_DK, :]
        z = jnp.zeros_like(s0)
        s_pair = jnp.concatenate(
            [jnp.concatenate([s0, z], axis=2), jnp.concatenate([z, s1], axis=2)], axis=1)
        inter = lax.dot_general(qs3[:, :, p * 2 * GLA_DK:(p + 1) * 2 * GLA_DK], s_pair, bdims_nn,
                                preferred_element_type=F32)
        outs.append(intra + inter.reshape(rows, 2 * GLA_DV))
        upd = lax.dot_general(kd3[:, :, p * 2 * GLA_DK:(p + 1) * 2 * GLA_DK],
                              v3[:, :, p * 2 * GLA_DV:(p + 1) * 2 * GLA_DV], bdims_tn,
                              preferred_element_type=F32)
        upds.append(upd[:, 0:GLA_DK, 0:GLA_DV])
        upds.append(upd[:, GLA_DK:2 * GLA_DK, GLA_DV:2 * GLA_DV])
    o = jnp.concatenate(outs, axis=1)
    s_out_ref[...] = s_old * jnp.exp(dsum) + jnp.concatenate(upds, axis=1)

    y = _epilogue(x, pool_act, o, gg, ng_ref, wout_ref, lng_ref, lnb_ref)
    y_ref[...] = y.reshape(bs, t, D_MODEL)


def _sample_call(x, pool_in, s_in, win, wmap, pscale, ba, ng, wout, lng, lnb, start_pos):
    nb, t, _ = x.shape
    bs = SAMPLE_SEQS
    const2 = lambda i: (0, 0)
    seq3 = lambda i: (i, 0, 0)
    hist3 = lambda i: (0, i, 0)
    return pl.pallas_call(
        functools.partial(_sample_body, bs=bs, t=t, start_pos=start_pos),
        grid=(nb // bs,),
        in_specs=[
            pl.BlockSpec((bs, t, D_MODEL), seq3),
            pl.BlockSpec((POOL_BUF, bs, POOL_WIDTH), hist3),
            pl.BlockSpec((bs, GLA_KEY_WIDTH, GLA_DV), seq3),
            pl.BlockSpec(win.shape, const2),
            pl.BlockSpec(wmap.shape, lambda i: (0, 0, 0)),
            pl.BlockSpec(pscale.shape, const2),
            pl.BlockSpec(ba.shape, const2),
            pl.BlockSpec(ng.shape, const2),
            pl.BlockSpec(wout.shape, const2),
            pl.BlockSpec(lng.shape, const2),
            pl.BlockSpec(lnb.shape, const2),
        ],
        out_specs=[
            pl.BlockSpec((bs, t, D_MODEL), seq3),
            pl.BlockSpec((POOL_BUF, bs, POOL_WIDTH), hist3),
            pl.BlockSpec((bs, GLA_KEY_WIDTH, GLA_DV), seq3),
        ],
        out_shape=[
            jax.ShapeDtypeStruct((nb, t, D_MODEL), F32),
            jax.ShapeDtypeStruct((POOL_BUF, nb, POOL_WIDTH), F32),
            jax.ShapeDtypeStruct((nb, GLA_KEY_WIDTH, GLA_DV), F32),
        ],
        scratch_shapes=[pltpu.VMEM((len(POOL_WINDOWS), bs * t, POOL_GROUP_DIM), F32),
                        pltpu.VMEM((len(POOL_WINDOWS), bs * t, POOL_GROUP_DIM), F32)],
        compiler_params=pltpu.CompilerParams(
            dimension_semantics=("arbitrary",), vmem_limit_bytes=VMEM_LIMIT),
        name="sample_layer",
    )(x, pool_in, s_in, win, wmap, pscale, ba, ng, wout, lng, lnb)


def _prep_body(wt_ref, wfat_ref, wa2t_ref, out_ref, *, n_plain):
    i = pl.program_id(0)

    @pl.when(i < n_plain)
    def _():
        out_ref[...] = jnp.transpose(wt_ref[...]).astype(BF16)

    @pl.when(i == n_plain)
    def _():
        wfat = wfat_ref[...]
        wa2t = wa2t_ref[...]
        acc = wa2t[:, 0:1] * wfat[0:1, :]
        for r in range(1, GLA_RANK):
            acc = acc + wa2t[:, r:r + 1] * wfat[r:r + 1, :]
        out_ref[...] = jnp.transpose(acc).astype(BF16)


def _prep_input_weights(wt, wa2t):
    n_plain = C_GL // MXU_COLS
    return pl.pallas_call(
        functools.partial(_prep_body, n_plain=n_plain),
        grid=(n_plain + 1,),
        in_specs=[
            pl.BlockSpec((MXU_COLS, D_MODEL), lambda i: (jnp.minimum(i, n_plain - 1), 0)),
            pl.BlockSpec((GLA_RANK, D_MODEL), lambda i: (0, 0)),
            pl.BlockSpec((GLA_KEY_WIDTH, GLA_RANK), lambda i: (0, 0)),
        ],
        out_specs=pl.BlockSpec((D_MODEL, MXU_COLS), lambda i: (0, i)),
        out_shape=jax.ShapeDtypeStruct((D_MODEL, C_END), BF16),
        compiler_params=pltpu.CompilerParams(dimension_semantics=("arbitrary",)),
        name="prep_input_weights",
    )(wt, wt[C_GL:], wa2t)


def kernel(x_prompt, x_sample, state_pool, state_gla, w_in, w_pool_map, pool_scale, w_gla_a2, b_gla_a,
           gla_norm_g, w_out, ln_g, ln_b):
    assert w_in.shape[0] == DEPTH
    win = _prep_input_weights(jnp.transpose(w_in[0]), jnp.transpose(w_gla_a2[0]))
    wmap = w_pool_map[0].astype(BF16)
    wout = w_out[0].astype(BF16)
    pscale = pool_scale[0].reshape(1, POOL_WIDTH)
    ba = b_gla_a[0].reshape(1, GLA_KEY_WIDTH)
    ng = gla_norm_g[0].reshape(1, GLA_WIDTH)
    lng = ln_g[0].reshape(1, D_MODEL)
    lnb = ln_b[0].reshape(1, D_MODEL)
    params = (win, wmap, pscale, ba, ng, wout, lng, lnb)

    y_p, pool_p, gla_p = _prompt_call(x_prompt, *params)
    nsamp = x_sample.shape[0]
    y_s, pool_s, gla_s = _sample_call(
        x_sample, jnp.transpose(state_pool[0], (1, 0, 2)),
        state_gla[0].reshape(nsamp, GLA_KEY_WIDTH, GLA_DV), *params, start_pos=PAST_LEN)

    nb = x_prompt.shape[0]
    return (y_p, y_s, pool_p[None],
            gla_p.reshape(1, nb, GLA_HEADS, GLA_DK, GLA_DV),
            jnp.transpose(pool_s, (1, 0, 2))[None],
            gla_s.reshape(1, nsamp, GLA_HEADS, GLA_DK, GLA_DV))
```

```python
import functools
import itertools

import jax
import jax.numpy as jnp
from jax import lax
from jax.experimental import pallas as pl
from jax.experimental.pallas import tpu as pltpu

F32 = jnp.float32
BF16 = jnp.bfloat16

D_MODEL = 1024
POOL_WIDTH = 512
POOL_WINDOWS = (2, 4, 8, 16)
POOL_GROUP_DIM = 128
POOL_BUF = 15
HIST_ROWS = 16
GLA_HEADS = 4
GLA_DK = 64
GLA_DV = 128
GLA_KEY_WIDTH = GLA_HEADS * GLA_DK
GLA_WIDTH = GLA_HEADS * GLA_DV
GLA_RANK = 16
GLA_TAU = 16.0
LN_EPS = 1e-5
RMS_EPS = 1e-6
DEPTH = 1
DEEPNORM_ALPHA = (2.0 * DEPTH) ** 0.25
PAST_LEN = 16384

C_U, C_GP, C_Q, C_K, C_V, C_GG, C_GL, C_END = 0, 512, 1024, 1280, 1536, 2048, 2560, 2816

MXU_COLS = 256
PROMPT_TILE = 256
PROMPT_SUBTILE = 256
SUBTILE_TAIL_UNITS = 3
PUMP_ROWS = 128
SAFE_LOG_DECAY = 60.0
GLA_CHUNK = 64
SAMPLE_SEQS = 16
VMEM_LIMIT = 56 * 1024 * 1024


def _dot(a, b):
    return jnp.dot(a, b, preferred_element_type=F32)


def _dot_nt(a, b):
    return lax.dot_general(a, b, (((1,), (1,)), ((), ())), preferred_element_type=F32)


def _dot_tn(a, b):
    return lax.dot_general(a, b, (((0,), (0,)), ((), ())), preferred_element_type=F32)


def _silu(x):
    return x * (1.0 / (1.0 + jnp.exp(-x)))


def _log_sigmoid(x):
    return jnp.minimum(x, 0.0) - jnp.log1p(jnp.exp(-jnp.abs(x)))


def _split_bf16(x):
    hi = x.astype(BF16)
    lo = (x - hi.astype(F32)).astype(BF16)
    return hi, lo


def _epilogue(x, pool_act, o, gg, ng_ref, wout_ref, lng_ref, lnb_ref, pump=lambda n=None: None, tail=None,
              row_blocks=1):
    pump(1)
    normed = []
    for h in range(GLA_HEADS):
        oh = o[:, h * GLA_DV:(h + 1) * GLA_DV]
        ms = jnp.mean(oh * oh, axis=-1, keepdims=True)
        normed.append(oh * lax.rsqrt(ms + RMS_EPS))
    gla_act = jnp.concatenate(normed, axis=1) * ng_ref[...] * _silu(gg)
    mix = jnp.concatenate([pool_act, gla_act], axis=1).astype(BF16)
    y = _dot(mix, wout_ref[...])
    n = x.shape[0] // row_blocks
    outs = []
    for r in range(row_blocks):
        rows = slice(r * n, (r + 1) * n)
        pump(1)
        h = DEEPNORM_ALPHA * x[rows] + y[rows]
        mu = jnp.mean(h, axis=-1, keepdims=True)
        d = h - mu
        var = jnp.mean(d * d, axis=-1, keepdims=True)
        outs.append(d * lax.rsqrt(var + LN_EPS) * lng_ref[...] + lnb_ref[...])
    pump(tail)
    return outs[0] if row_blocks == 1 else jnp.concatenate(outs, axis=0)


def _head_block_mask(rows_per_head, cols_per_head, shape):
    r = lax.broadcasted_iota(jnp.int32, shape, 0) // rows_per_head
    c = lax.broadcasted_iota(jnp.int32, shape, 1) // cols_per_head
    return r == c


def _score_masks(rows, n_levels):
    shape = (rows, GLA_HEADS * rows)
    t_idx = lax.broadcasted_iota(jnp.int32, shape, 0)
    s_idx = lax.broadcasted_iota(jnp.int32, shape, 1) % rows
    masks = [t_idx == s_idx]
    for lvl in range(n_levels):
        half = 1 << lvl
        blk = 2 * half
        masks.append((t_idx // blk == s_idx // blk) & (t_idx % blk >= half) & (s_idx % blk < half))
    return masks


def _decayed_scores(q, k, la, b, masks, head_blk, pump=lambda n=None: None):
    rows = q.shape[0]
    r_idx = lax.broadcasted_iota(jnp.int32, q.shape, 0)

    def scores(qt, kt):
        kh = jnp.where(head_blk, jnp.concatenate([kt.astype(BF16)] * GLA_HEADS, axis=0), jnp.zeros((), BF16))
        return _dot_nt(qt.astype(BF16), kh)

    acc = jnp.where(masks[0], scores(q, k), 0.0)
    for lvl in range(len(masks) - 1):
        half = 1 << lvl
        blk = 2 * half
        if lvl == 0:
            neg = jnp.where(r_idx % 2 == 1, la, 0.0)
        elif lvl == 1:
            r4 = r_idx % 4
            nxt = pltpu.roll(la, rows - 1, 0)
            prv = pltpu.roll(la, 1, 0)
            neg = jnp.where(r4 == 0, nxt, jnp.where(r4 == 1, 0.0, jnp.where(r4 == 2, la, la + prv)))
        else:
            ref = jnp.concatenate(
                [jnp.broadcast_to(b[j * blk + half - 1:j * blk + half, :], (blk, b.shape[1]))
                 for j in range(rows // blk)], axis=0)
            d = b - ref
            neg = jnp.minimum(d, -d)
        w = jnp.exp(neg)
        acc = jnp.where(masks[lvl + 1], scores(q * w, k * w), acc)
        pump(lvl % 2)
    return acc.astype(BF16)


def _pair_block_diag(a0, a1):
    z = jnp.zeros_like(a0)
    return jnp.concatenate(
        [jnp.concatenate([a0, z], axis=1), jnp.concatenate([z, a1], axis=1)], axis=0)


def _make_pump(units, xb_ref, win_ref, proj_ref):
    units = iter(units)

    def pump(n=None):
        for r, lo in (units if n is None else itertools.islice(units, n)):
            proj_ref[r:r + PUMP_ROWS, lo:lo + MXU_COLS] = _dot(
                xb_ref[r:r + PUMP_ROWS, :], win_ref[:, lo:lo + MXU_COLS])

    return pump


def _naive_scores(q, k, b, causal, head_blk):
    ks = (k * jnp.exp(-b)).astype(BF16)
    kh = jnp.where(head_blk, jnp.concatenate([ks] * GLA_HEADS, axis=0), jnp.zeros((), BF16))
    return jnp.where(causal, _dot_nt((q * jnp.exp(b)).astype(BF16), kh), 0.0).astype(BF16)


def _prompt_stage_b(pos0, proj_ref, xkeep_ref, la_ref, wmap_ref, pscale_ref, ba_ref, ng_ref, wout_ref, lng_ref,
                    lnb_ref, y_ref, ext_ref, s_ref, pump, next_decay, *, tc, robust, tail):
    u = proj_ref[:, C_U:C_GP]
    ext_ref[HIST_ROWS:HIST_ROWS + tc, :] = u
    pos16 = pos0 + lax.broadcasted_iota(jnp.int32, (HIST_ROWS, POOL_GROUP_DIM), 0)
    mixed = []
    for g, w in enumerate(POOL_WINDOWS):
        sl = slice(g * POOL_GROUP_DIM, (g + 1) * POOL_GROUP_DIM)
        e = ext_ref[:, sl]
        s = e
        sh = 1
        while sh < w:
            s = s + pltpu.roll(s, sh, 0)
            sh *= 2
        wsum = s[HIST_ROWS:]
        self_rows = e[HIST_ROWS:]
        cnt = jnp.minimum(pos16 + 1, w).astype(F32)
        first = wsum[:HIST_ROWS] / cnt - self_rows[:HIST_ROWS]
        rest = wsum[HIST_ROWS:] * (1.0 / w) - self_rows[HIST_ROWS:]
        pooled = jnp.concatenate([first, rest], axis=0)
        mixed.append(_dot(pooled.astype(BF16), wmap_ref[g]))
        pump(1)
    pool_act = jnp.concatenate(mixed, axis=1) * pscale_ref[...] * _silu(proj_ref[:, C_GP:C_Q])
    next_decay()

    cc = GLA_CHUNK
    n_chunks = tc // cc
    rr = lax.broadcasted_iota(jnp.int32, (tc, tc), 0)
    cl = lax.broadcasted_iota(jnp.int32, (tc, tc), 1)
    ltri = jnp.where((rr // cc == cl // cc) & (cl <= rr), 1.0, 0.0).astype(BF16)
    masks = _score_masks(cc, cc.bit_length() - 1 if robust else 0)
    causal = (lax.broadcasted_iota(jnp.int32, (cc, GLA_HEADS * cc), 1) % cc
              <= lax.broadcasted_iota(jnp.int32, (cc, GLA_HEADS * cc), 0))
    blk = _head_block_mask(cc, GLA_DK, (GLA_HEADS * cc, GLA_KEY_WIDTH))

    hi, lo = _split_bf16(la_ref[...])
    b_all = _dot(ltri, hi) + _dot(ltri, lo)
    pump(1)
    lhs_list, v_list, upd_list, dcol_list = [], [], [], []
    for c in range(n_chunks):
        rs = slice(c * cc, (c + 1) * cc)
        b = b_all[rs]
        b_end = b[cc - 1:cc, :]
        q_c = proj_ref[rs, C_Q:C_K] * (GLA_DK ** -0.5)
        k_c = proj_ref[rs, C_K:C_V]
        qs = (q_c * jnp.exp(b)).astype(BF16)
        kd = (k_c * jnp.exp(b_end - b)).astype(BF16)
        if robust:
            a_all = _decayed_scores(q_c, k_c, la_ref[rs, :], b, masks, blk, pump)
        else:
            a_all = _naive_scores(q_c, k_c, b, causal, blk)
            pump(2)
        vc = proj_ref[rs, C_V:C_GG].astype(BF16)
        upds = []
        for p in range(GLA_HEADS // 2):
            upd = _dot_tn(kd[:, p * 2 * GLA_DK:(p + 1) * 2 * GLA_DK], vc[:, p * 2 * GLA_DV:(p + 1) * 2 * GLA_DV])
            upds.append(upd[0:GLA_DK, 0:GLA_DV])
            upds.append(upd[GLA_DK:2 * GLA_DK, GLA_DV:2 * GLA_DV])
        lhs_list.append([jnp.concatenate([a_all[:, p * 2 * cc:(p + 1) * 2 * cc],
                                          qs[:, p * 2 * GLA_DK:(p + 1) * 2 * GLA_DK]], axis=1)
                         for p in range(GLA_HEADS // 2)])
        v_list.append(vc)
        upd_list.append(jnp.concatenate(upds, axis=0))
        dcol_list.append(jnp.transpose(jnp.broadcast_to(jnp.exp(b_end), (GLA_DV, GLA_KEY_WIDTH))))

    o_chunks = []
    state = s_ref[...]
    for c in range(n_chunks):
        s_bf = state.astype(BF16)
        vc = v_list[c]
        outs = []
        for p in range(GLA_HEADS // 2):
            h0, h1 = 2 * p, 2 * p + 1
            rhs = jnp.concatenate([
                _pair_block_diag(vc[:, h0 * GLA_DV:(h0 + 1) * GLA_DV], vc[:, h1 * GLA_DV:(h1 + 1) * GLA_DV]),
                _pair_block_diag(s_bf[h0 * GLA_DK:(h0 + 1) * GLA_DK], s_bf[h1 * GLA_DK:(h1 + 1) * GLA_DK]),
            ], axis=0)
            outs.append(_dot(lhs_list[c][p], rhs))
        o_chunks.append(jnp.concatenate(outs, axis=1))
        state = state * dcol_list[c] + upd_list[c]
        pump(c % 2)
    s_ref[...] = state
    o = jnp.concatenate(o_chunks, axis=0)

    y_ref[...] = _epilogue(xkeep_ref[...], pool_act, o, proj_ref[:, C_GG:C_GL], ng_ref, wout_ref, lng_ref, lnb_ref,
                           pump, tail, row_blocks=4)


def _prompt_body(x_ref, win_ref, wmap_ref, pscale_ref, ba_ref, ng_ref, wout_ref, lng_ref, lnb_ref,
                 y_ref, pool_ref, s_out_ref,
                 proj0_ref, proj1_ref, xkeep0_ref, xkeep1_ref, la0_ref, la1_ref, xb_ref, ext_ref, s_ref, steep_ref,
                 *, tc, sub, nj):
    i = pl.program_id(0)
    jb = (i + nj - 1) % nj

    @pl.when(i == 0)
    def _():
        proj1_ref[...] = jnp.zeros_like(proj1_ref)
        xkeep1_ref[...] = jnp.zeros_like(xkeep1_ref)
        la1_ref[...] = jnp.zeros_like(la1_ref)
        steep_ref[0] = 0

    @pl.when((i == 0) | (jb == 0))
    def _():
        ext_ref[0:HIST_ROWS, :] = jnp.zeros((HIST_ROWS, POOL_WIDTH), F32)
        s_ref[...] = jnp.zeros_like(s_ref)

    units = [(r, lo) for lo in (C_GL, *range(0, C_GL, MXU_COLS)) for r in range(0, tc, PUMP_ROWS)]

    def step(proj_a, xkeep_a, la_a, proj_b, xkeep_b, la_b, robust):
        x = x_ref[0]
        xkeep_a[...] = x
        xb_ref[...] = x.astype(BF16)
        pump = _make_pump(units, xb_ref, win_ref, proj_a)
        pump(tc // PUMP_ROWS)

        def next_decay():
            la = _log_sigmoid(proj_a[:, C_GL:C_END] + ba_ref[...]) * (1.0 / GLA_TAU)
            la_a[...] = la
            chunk_decay = [jnp.sum(la[c * GLA_CHUNK:(c + 1) * GLA_CHUNK], axis=0, keepdims=True)
                           for c in range(tc // GLA_CHUNK)]
            lowest = jnp.min(functools.reduce(jnp.minimum, chunk_decay))
            steep_ref[0] = jnp.where(lowest < -SAFE_LOG_DECAY, 1, 0).astype(jnp.int32)

        n_sub = tc // sub
        for k in range(n_sub):
            rows = slice(k * sub, (k + 1) * sub)
            last = k == n_sub - 1
            _prompt_stage_b((jb * n_sub + k) * sub, proj_b.at[rows, :], xkeep_b.at[rows, :], la_b.at[rows, :],
                            wmap_ref, pscale_ref, ba_ref, ng_ref, wout_ref, lng_ref, lnb_ref,
                            y_ref.at[0, rows, :], ext_ref, s_ref, pump, next_decay if k == 0 else (lambda: None),
                            tc=sub, robust=robust, tail=None if last else SUBTILE_TAIL_UNITS)
            if not last:
                ext_ref[0:HIST_ROWS, :] = ext_ref[sub:sub + HIST_ROWS, :]

    set0 = (proj0_ref, xkeep0_ref, la0_ref)
    set1 = (proj1_ref, xkeep1_ref, la1_ref)
    steep = steep_ref[0] != 0
    for parity, (set_a, set_b) in enumerate(((set0, set1), (set1, set0))):
        for robust in (False, True):
            @pl.when((i % 2 == parity) & (steep == robust))
            def _(set_a=set_a, set_b=set_b, robust=robust):
                step(*set_a, *set_b, robust)

    @pl.when((i > 0) & (jb == nj - 1))
    def _():
        pool_ref[0] = ext_ref[sub + 1:sub + HIST_ROWS, :]
        s_out_ref[0] = s_ref[...]

    ext_ref[0:HIST_ROWS, :] = ext_ref[sub:sub + HIST_ROWS, :]


def _prompt_call(x, win, wmap, pscale, ba, ng, wout, lng, lnb):
    nb, t, _ = x.shape
    tc = PROMPT_TILE
    nj = t // tc
    ntiles = nb * nj
    const2 = lambda i: (0, 0)

    def tile_a(i):
        ia = jnp.minimum(i, ntiles - 1)
        return (ia // nj, ia % nj, 0)

    def tile_b(i):
        ib = jnp.maximum(i - 1, 0)
        return (ib // nj, ib % nj, 0)

    def seq_b(i):
        return (jnp.maximum(i - 1, 0) // nj, 0, 0)

    return pl.pallas_call(
        functools.partial(_prompt_body, tc=tc, sub=PROMPT_SUBTILE, nj=nj),
        grid=(ntiles + 1,),
        in_specs=[
            pl.BlockSpec((1, tc, D_MODEL), tile_a),
            pl.BlockSpec(win.shape, const2),
            pl.BlockSpec(wmap.shape, lambda i: (0, 0, 0)),
            pl.BlockSpec(pscale.shape, const2),
            pl.BlockSpec(ba.shape, const2),
            pl.BlockSpec(ng.shape, const2),
            pl.BlockSpec(wout.shape, const2),
            pl.BlockSpec(lng.shape, const2),
            pl.BlockSpec(lnb.shape, const2),
        ],
        out_specs=[
            pl.BlockSpec((1, tc, D_MODEL), tile_b),
            pl.BlockSpec((1, POOL_BUF, POOL_WIDTH), seq_b),
            pl.BlockSpec((1, GLA_KEY_WIDTH, GLA_DV), seq_b),
        ],
        out_shape=[
            jax.ShapeDtypeStruct((nb, t, D_MODEL), F32),
            jax.ShapeDtypeStruct((nb, POOL_BUF, POOL_WIDTH), F32),
            jax.ShapeDtypeStruct((nb, GLA_KEY_WIDTH, GLA_DV), F32),
        ],
        scratch_shapes=[
            pltpu.VMEM((tc, C_END), F32),
            pltpu.VMEM((tc, C_END), F32),
            pltpu.VMEM((tc, D_MODEL), F32),
            pltpu.VMEM((tc, D_MODEL), F32),
            pltpu.VMEM((tc, GLA_KEY_WIDTH), F32),
            pltpu.VMEM((tc, GLA_KEY_WIDTH), F32),
            pltpu.VMEM((tc, D_MODEL), BF16),
            pltpu.VMEM((HIST_ROWS + PROMPT_SUBTILE, POOL_WIDTH), F32),
            pltpu.VMEM((GLA_KEY_WIDTH, GLA_DV), F32),
            pltpu.SMEM((1,), jnp.int32),
        ],
        compiler_params=pltpu.CompilerParams(
            dimension_semantics=("arbitrary",), vmem_limit_bytes=VMEM_LIMIT),
        name="prompt_layer",
    )(x, win, wmap, pscale, ba, ng, wout, lng, lnb)


def _sample_body(x_ref, pool_in_ref, s_in_ref, win_ref, wmap_ref, pscale_ref, ba_ref, ng_ref, wout_ref,
                 lng_ref, lnb_ref, y_ref, pool_ref, s_out_ref, u_ref, pooled_ref, *, bs, t, start_pos):
    rows = bs * t
    x = x_ref[...].reshape(rows, D_MODEL)
    xb = x.astype(BF16)

    def proj(lo, hi):
        return _dot(xb, win_ref[:, lo:hi])

    u = proj(C_U, C_GP)
    mixed = []
    for g, w in enumerate(POOL_WINDOWS):
        sl = slice(g * POOL_GROUP_DIM, (g + 1) * POOL_GROUP_DIM)
        u_ref[g] = u[:, sl]
        ext = [pool_in_ref[r, :, sl] for r in range(POOL_BUF)]
        ext += [u_ref[g, pl.ds(tok, bs, stride=t), :] for tok in range(t)]
        for r in range(POOL_BUF):
            pool_ref[r, :, sl] = ext[t + r]
        sums = ext
        sh = 1
        while sh < w:
            first = 2 * sh - 1
            sums = [None] * first + [sums[r] + sums[r - sh] for r in range(first, len(sums))]
            sh *= 2
        for tok in range(t):
            cnt = float(min(start_pos + tok + 1, w))
            pooled_ref[g, pl.ds(tok, bs, stride=t), :] = sums[POOL_BUF + tok] / cnt - ext[POOL_BUF + tok]
        mixed.append(_dot(pooled_ref[g].astype(BF16), wmap_ref[g]))
    gp = proj(C_GP, C_Q)
    pool_act = jnp.concatenate(mixed, axis=1) * pscale_ref[...] * _silu(gp)

    q = proj(C_Q, C_K) * (GLA_DK ** -0.5)
    k = proj(C_K, C_V)
    vb = proj(C_V, C_GG).astype(BF16)
    gg = proj(C_GG, C_GL)
    la = _log_sigmoid(proj(C_GL, C_END) + ba_ref[...]) * (1.0 / GLA_TAU)

    rr = lax.broadcasted_iota(jnp.int32, (rows, rows), 0)
    cl = lax.broadcasted_iota(jnp.int32, (rows, rows), 1)
    same = (rr // t) == (cl // t)
    ltri = jnp.where(same & (cl <= rr), 1.0, 0.0).astype(BF16)
    lall = jnp.where(same, 1.0, 0.0).astype(BF16)
    hi, lo = _split_bf16(la)
    b = _dot(ltri, hi) + _dot(ltri, lo)
    b_end = _dot(lall, hi) + _dot(lall, lo)
    qs = (q * jnp.exp(b)).astype(BF16)
    kd = (k * jnp.exp(b_end - b)).astype(BF16)

    assert t & (t - 1) == 0, "tokens per sequence must be a power of two"
    blk = _head_block_mask(rows, GLA_DK, (GLA_HEADS * rows, GLA_KEY_WIDTH))
    a_all = _decayed_scores(q, k, la, b, _score_masks(rows, t.bit_length() - 1), blk)

    s_old = s_in_ref[...]
    s_bf = s_old.astype(BF16)
    qs3 = qs.reshape(bs, t, GLA_KEY_WIDTH)
    kd3 = kd.reshape(bs, t, GLA_KEY_WIDTH)
    v3 = vb.reshape(bs, t, GLA_WIDTH)
    hi3 = hi.reshape(bs, t, GLA_KEY_WIDTH)
    lo3 = lo.reshape(bs, t, GLA_KEY_WIDTH)
    ones3 = jnp.ones((bs, t, GLA_DV), BF16)
    bdims_nn = (((2,), (1,)), ((0,), (0,)))
    bdims_tn = (((1,), (1,)), ((0,), (0,)))
    dsum = (lax.dot_general(hi3, ones3, bdims_tn, preferred_element_type=F32)
            + lax.dot_general(lo3, ones3, bdims_tn, preferred_element_type=F32))
    outs, upds = [], []
    for p in range(GLA_HEADS // 2):
        h0, h1 = 2 * p, 2 * p + 1
        intra = _dot(a_all[:, p * 2 * rows:(p + 1) * 2 * rows],
                     _pair_block_diag(vb[:, h0 * GLA_DV:(h0 + 1) * GLA_DV], vb[:, h1 * GLA_DV:(h1 + 1) * GLA_DV]))
        s0 = s_bf[:, h0 * GLA_DK:(h0 + 1) * GLA_DK, :]
        s1 = s_bf[:, h1 * GLA_DK:(h1 + 1) * GLA_DK, :]
        z = jnp.zeros_like(s0)
        s_pair = jnp.concatenate(
            [jnp.concatenate([s0, z], axis=2), jnp.concatenate([z, s1], axis=2)], axis=1)
        inter = lax.dot_general(qs3[:, :, p * 2 * GLA_DK:(p + 1) * 2 * GLA_DK], s_pair, bdims_nn,
                                preferred_element_type=F32)
        outs.append(intra + inter.reshape(rows, 2 * GLA_DV))
        upd = lax.dot_general(kd3[:, :, p * 2 * GLA_DK:(p + 1) * 2 * GLA_DK],
                              v3[:, :, p * 2 * GLA_DV:(p + 1) * 2 * GLA_DV], bdims_tn,
                              preferred_element_type=F32)
        upds.append(upd[:, 0:GLA_DK, 0:GLA_DV])
        upds.append(upd[:, GLA_DK:2 * GLA_DK, GLA_DV:2 * GLA_DV])
    o = jnp.concatenate(outs, axis=1)
    s_out_ref[...] = s_old * jnp.exp(dsum) + jnp.concatenate(upds, axis=1)

    y = _epilogue(x, pool_act, o, gg, ng_ref, wout_ref, lng_ref, lnb_ref)
    y_ref[...] = y.reshape(bs, t, D_MODEL)


def _sample_call(x, pool_in, s_in, win, wmap, pscale, ba, ng, wout, lng, lnb, start_pos):
    nb, t, _ = x.shape
    bs = SAMPLE_SEQS
    const2 = lambda i: (0, 0)
    seq3 = lambda i: (i, 0, 0)
    hist3 = lambda i: (0, i, 0)
    return pl.pallas_call(
        functools.partial(_sample_body, bs=bs, t=t, start_pos=start_pos),
        grid=(nb // bs,),
        in_specs=[
            pl.BlockSpec((bs, t, D_MODEL), seq3),
            pl.BlockSpec((POOL_BUF, bs, POOL_WIDTH), hist3),
            pl.BlockSpec((bs, GLA_KEY_WIDTH, GLA_DV), seq3),
            pl.BlockSpec(win.shape, const2),
            pl.BlockSpec(wmap.shape, lambda i: (0, 0, 0)),
            pl.BlockSpec(pscale.shape, const2),
            pl.BlockSpec(ba.shape, const2),
            pl.BlockSpec(ng.shape, const2),
            pl.BlockSpec(wout.shape, const2),
            pl.BlockSpec(lng.shape, const2),
            pl.BlockSpec(lnb.shape, const2),
        ],
        out_specs=[
            pl.BlockSpec((bs, t, D_MODEL), seq3),
            pl.BlockSpec((POOL_BUF, bs, POOL_WIDTH), hist3),
            pl.BlockSpec((bs, GLA_KEY_WIDTH, GLA_DV), seq3),
        ],
        out_shape=[
            jax.ShapeDtypeStruct((nb, t, D_MODEL), F32),
            jax.ShapeDtypeStruct((POOL_BUF, nb, POOL_WIDTH), F32),
            jax.ShapeDtypeStruct((nb, GLA_KEY_WIDTH, GLA_DV), F32),
        ],
        scratch_shapes=[pltpu.VMEM((len(POOL_WINDOWS), bs * t, POOL_GROUP_DIM), F32),
                        pltpu.VMEM((len(POOL_WINDOWS), bs * t, POOL_GROUP_DIM), F32)],
        compiler_params=pltpu.CompilerParams(
            dimension_semantics=("arbitrary",), vmem_limit_bytes=VMEM_LIMIT),
        name="sample_layer",
    )(x, pool_in, s_in, win, wmap, pscale, ba, ng, wout, lng, lnb)


def _prep_body(wt_ref, wfat_ref, wa2t_ref, out_ref, *, n_plain):
    i = pl.program_id(0)

    @pl.when(i < n_plain)
    def _():
        out_ref[...] = jnp.transpose(wt_ref[...]).astype(BF16)

    @pl.when(i == n_plain)
    def _():
        wfat = wfat_ref[...]
        wa2t = wa2t_ref[...]
        acc = wa2t[:, 0:1] * wfat[0:1, :]
        for r in range(1, GLA_RANK):
            acc = acc + wa2t[:, r:r + 1] * wfat[r:r + 1, :]
        out_ref[...] = jnp.transpose(acc).astype(BF16)


def _prep_input_weights(wt, wa2t):
    n_plain = C_GL // MXU_COLS
    return pl.pallas_call(
        functools.partial(_prep_body, n_plain=n_plain),
        grid=(n_plain + 1,),
        in_specs=[
            pl.BlockSpec((MXU_COLS, D_MODEL), lambda i: (jnp.minimum(i, n_plain - 1), 0)),
            pl.BlockSpec((GLA_RANK, D_MODEL), lambda i: (0, 0)),
            pl.BlockSpec((GLA_KEY_WIDTH, GLA_RANK), lambda i: (0, 0)),
        ],
        out_specs=pl.BlockSpec((D_MODEL, MXU_COLS), lambda i: (0, i)),
        out_shape=jax.ShapeDtypeStruct((D_MODEL, C_END), BF16),
        compiler_params=pltpu.CompilerParams(dimension_semantics=("arbitrary",)),
        name="prep_input_weights",
    )(wt, wt[C_GL:], wa2t)


def kernel(x_prompt, x_sample, state_pool, state_gla, w_in, w_pool_map, pool_scale, w_gla_a2, b_gla_a,
           gla_norm_g, w_out, ln_g, ln_b):
    assert w_in.shape[0] == DEPTH
    win = _prep_input_weights(jnp.transpose(w_in[0]), jnp.transpose(w_gla_a2[0]))
    wmap = w_pool_map[0].astype(BF16)
    wout = w_out[0].astype(BF16)
    pscale = pool_scale[0].reshape(1, POOL_WIDTH)
    ba = b_gla_a[0].reshape(1, GLA_KEY_WIDTH)
    ng = gla_norm_g[0].reshape(1, GLA_WIDTH)
    lng = ln_g[0].reshape(1, D_MODEL)
    lnb = ln_b[0].reshape(1, D_MODEL)
    params = (win, wmap, pscale, ba, ng, wout, lng, lnb)

    y_p, pool_p, gla_p = _prompt_call(x_prompt, *params)
    nsamp = x_sample.shape[0]
    y_s, pool_s, gla_s = _sample_call(
        x_sample, jnp.transpose(state_pool[0], (1, 0, 2)),
        state_gla[0].reshape(nsamp, GLA_KEY_WIDTH, GLA_DV), *params, start_pos=PAST_LEN)

    nb = x_prompt.shape[0]
    return (y_p, y_s, pool_p[None],
            gla_p.reshape(1, nb, GLA_HEADS, GLA_DK, GLA_DV),
            jnp.transpose(pool_s, (1, 0, 2))[None],
            gla_s.reshape(1, nsamp, GLA_HEADS, GLA_DK, GLA_DV))
```

```python
import functools
import itertools

import jax
import jax.numpy as jnp
from jax import lax
from jax.experimental import pallas as pl
from jax.experimental.pallas import tpu as pltpu

F32 = jnp.float32
BF16 = jnp.bfloat16

D_MODEL = 1024
POOL_WIDTH = 512
POOL_WINDOWS = (2, 4, 8, 16)
POOL_GROUP_DIM = 128
POOL_BUF = 15
HIST_ROWS = 16
GLA_HEADS = 4
GLA_DK = 64
GLA_DV = 128
GLA_KEY_WIDTH = GLA_HEADS * GLA_DK
GLA_WIDTH = GLA_HEADS * GLA_DV
GLA_RANK = 16
GLA_TAU = 16.0
LN_EPS = 1e-5
RMS_EPS = 1e-6
DEPTH = 1
DEEPNORM_ALPHA = (2.0 * DEPTH) ** 0.25
PAST_LEN = 16384

C_U, C_GP, C_Q, C_K, C_V, C_GG, C_GL, C_END = 0, 512, 1024, 1280, 1536, 2048, 2560, 2816

MXU_COLS = 256
PROMPT_TILE = 256
PROMPT_SUBTILE = 256
SUBTILE_TAIL_UNITS = 3
PUMP_ROWS = 256
PUMP_PLAN = {"pool": (0, 1, 0, 1), "cumsum": 0, "scores": 1, "level": (0, 0, 0, 1, 0, 0), "state": (0, 0, 0, 0),
             "epilogue": 1, "layernorm": (1, 1, 1, 0)}
SAFE_LOG_DECAY = 60.0
GLA_CHUNK = 64
SAMPLE_SEQS = 16
VMEM_LIMIT = 56 * 1024 * 1024


def _dot(a, b):
    return jnp.dot(a, b, preferred_element_type=F32)


def _dot_nt(a, b):
    return lax.dot_general(a, b, (((1,), (1,)), ((), ())), preferred_element_type=F32)


def _dot_tn(a, b):
    return lax.dot_general(a, b, (((0,), (0,)), ((), ())), preferred_element_type=F32)


def _silu(x):
    return x * (1.0 / (1.0 + jnp.exp(-x)))


def _log_sigmoid(x):
    return jnp.minimum(x, 0.0) - jnp.log1p(jnp.exp(-jnp.abs(x)))


def _split_bf16(x):
    hi = x.astype(BF16)
    lo = (x - hi.astype(F32)).astype(BF16)
    return hi, lo


def _epilogue(x, pool_act, o, gg, ng_ref, wout_ref, lng_ref, lnb_ref, pump=lambda n=None: None, tail=None,
              row_blocks=1):
    pump(PUMP_PLAN["epilogue"])
    normed = []
    for h in range(GLA_HEADS):
        oh = o[:, h * GLA_DV:(h + 1) * GLA_DV]
        ms = jnp.mean(oh * oh, axis=-1, keepdims=True)
        normed.append(oh * lax.rsqrt(ms + RMS_EPS))
    gla_act = jnp.concatenate(normed, axis=1) * ng_ref[...] * _silu(gg)
    mix = jnp.concatenate([pool_act, gla_act], axis=1).astype(BF16)
    y = _dot(mix, wout_ref[...])
    n = x.shape[0] // row_blocks
    outs = []
    for r in range(row_blocks):
        rows = slice(r * n, (r + 1) * n)
        pump(PUMP_PLAN["layernorm"][r])
        h = DEEPNORM_ALPHA * x[rows] + y[rows]
        mu = jnp.mean(h, axis=-1, keepdims=True)
        d = h - mu
        var = jnp.mean(d * d, axis=-1, keepdims=True)
        outs.append(d * lax.rsqrt(var + LN_EPS) * lng_ref[...] + lnb_ref[...])
    pump(tail)
    return outs[0] if row_blocks == 1 else jnp.concatenate(outs, axis=0)


def _head_block_mask(rows_per_head, cols_per_head, shape):
    r = lax.broadcasted_iota(jnp.int32, shape, 0) // rows_per_head
    c = lax.broadcasted_iota(jnp.int32, shape, 1) // cols_per_head
    return r == c


def _score_masks(rows, n_levels):
    shape = (rows, GLA_HEADS * rows)
    t_idx = lax.broadcasted_iota(jnp.int32, shape, 0)
    s_idx = lax.broadcasted_iota(jnp.int32, shape, 1) % rows
    masks = [t_idx == s_idx]
    for lvl in range(n_levels):
        half = 1 << lvl
        blk = 2 * half
        masks.append((t_idx // blk == s_idx // blk) & (t_idx % blk >= half) & (s_idx % blk < half))
    return masks


def _decayed_scores(q, k, la, b, masks, head_blk, pump=lambda n=None: None):
    rows = q.shape[0]
    r_idx = lax.broadcasted_iota(jnp.int32, q.shape, 0)

    def scores(qt, kt):
        kh = jnp.where(head_blk, jnp.concatenate([kt.astype(BF16)] * GLA_HEADS, axis=0), jnp.zeros((), BF16))
        return _dot_nt(qt.astype(BF16), kh)

    acc = jnp.where(masks[0], scores(q, k), 0.0)
    for lvl in range(len(masks) - 1):
        half = 1 << lvl
        blk = 2 * half
        if lvl == 0:
            neg = jnp.where(r_idx % 2 == 1, la, 0.0)
        elif lvl == 1:
            r4 = r_idx % 4
            nxt = pltpu.roll(la, rows - 1, 0)
            prv = pltpu.roll(la, 1, 0)
            neg = jnp.where(r4 == 0, nxt, jnp.where(r4 == 1, 0.0, jnp.where(r4 == 2, la, la + prv)))
        else:
            ref = jnp.concatenate(
                [jnp.broadcast_to(b[j * blk + half - 1:j * blk + half, :], (blk, b.shape[1]))
                 for j in range(rows // blk)], axis=0)
            d = b - ref
            neg = jnp.minimum(d, -d)
        w = jnp.exp(neg)
        acc = jnp.where(masks[lvl + 1], scores(q * w, k * w), acc)
        pump(PUMP_PLAN["level"][lvl])
    return acc.astype(BF16)


def _pair_block_diag(a0, a1):
    z = jnp.zeros_like(a0)
    return jnp.concatenate(
        [jnp.concatenate([a0, z], axis=1), jnp.concatenate([z, a1], axis=1)], axis=0)


def _make_pump(units, xb_ref, win_ref, proj_ref):
    units = iter(units)

    def pump(n=None):
        for r, lo in (units if n is None else itertools.islice(units, n)):
            proj_ref[r:r + PUMP_ROWS, lo:lo + MXU_COLS] = _dot(
                xb_ref[r:r + PUMP_ROWS, :], win_ref[:, lo:lo + MXU_COLS])

    return pump


def _naive_scores(q, k, b, causal, head_blk):
    ks = (k * jnp.exp(-b)).astype(BF16)
    kh = jnp.where(head_blk, jnp.concatenate([ks] * GLA_HEADS, axis=0), jnp.zeros((), BF16))
    return jnp.where(causal, _dot_nt((q * jnp.exp(b)).astype(BF16), kh), 0.0).astype(BF16)


def _prompt_stage_b(pos0, proj_ref, xkeep_ref, la_ref, wmap_ref, pscale_ref, ba_ref, ng_ref, wout_ref, lng_ref,
                    lnb_ref, y_ref, ext_ref, s_ref, pump, next_decay, *, tc, robust, tail):
    u = proj_ref[:, C_U:C_GP]
    ext_ref[HIST_ROWS:HIST_ROWS + tc, :] = u
    pos16 = pos0 + lax.broadcasted_iota(jnp.int32, (HIST_ROWS, POOL_GROUP_DIM), 0)
    mixed = []
    for g, w in enumerate(POOL_WINDOWS):
        sl = slice(g * POOL_GROUP_DIM, (g + 1) * POOL_GROUP_DIM)
        e = ext_ref[:, sl]
        s = e
        sh = 1
        while sh < w:
            s = s + pltpu.roll(s, sh, 0)
            sh *= 2
        wsum = s[HIST_ROWS:]
        self_rows = e[HIST_ROWS:]
        cnt = jnp.minimum(pos16 + 1, w).astype(F32)
        first = wsum[:HIST_ROWS] / cnt - self_rows[:HIST_ROWS]
        rest = wsum[HIST_ROWS:] * (1.0 / w) - self_rows[HIST_ROWS:]
        pooled = jnp.concatenate([first, rest], axis=0)
        mixed.append(_dot(pooled.astype(BF16), wmap_ref[g]))
        pump(PUMP_PLAN["pool"][g])
    pool_act = jnp.concatenate(mixed, axis=1) * pscale_ref[...] * _silu(proj_ref[:, C_GP:C_Q])
    next_decay()

    cc = GLA_CHUNK
    n_chunks = tc // cc
    rr = lax.broadcasted_iota(jnp.int32, (tc, tc), 0)
    cl = lax.broadcasted_iota(jnp.int32, (tc, tc), 1)
    ltri = jnp.where((rr // cc == cl // cc) & (cl <= rr), 1.0, 0.0).astype(BF16)
    masks = _score_masks(cc, cc.bit_length() - 1 if robust else 0)
    causal = (lax.broadcasted_iota(jnp.int32, (cc, GLA_HEADS * cc), 1) % cc
              <= lax.broadcasted_iota(jnp.int32, (cc, GLA_HEADS * cc), 0))
    blk = _head_block_mask(cc, GLA_DK, (GLA_HEADS * cc, GLA_KEY_WIDTH))

    hi, lo = _split_bf16(la_ref[...])
    b_all = _dot(ltri, hi) + _dot(ltri, lo)
    pump(PUMP_PLAN["cumsum"])
    lhs_list, v_list, upd_list, dcol_list = [], [], [], []
    for c in range(n_chunks):
        rs = slice(c * cc, (c + 1) * cc)
        b = b_all[rs]
        b_end = b[cc - 1:cc, :]
        q_c = proj_ref[rs, C_Q:C_K] * (GLA_DK ** -0.5)
        k_c = proj_ref[rs, C_K:C_V]
        qs = (q_c * jnp.exp(b)).astype(BF16)
        kd = (k_c * jnp.exp(b_end - b)).astype(BF16)
        if robust:
            a_all = _decayed_scores(q_c, k_c, la_ref[rs, :], b, masks, blk, pump)
        else:
            a_all = _naive_scores(q_c, k_c, b, causal, blk)
            pump(PUMP_PLAN["scores"])
        vc = proj_ref[rs, C_V:C_GG].astype(BF16)
        upds = []
        for p in range(GLA_HEADS // 2):
            upd = _dot_tn(kd[:, p * 2 * GLA_DK:(p + 1) * 2 * GLA_DK], vc[:, p * 2 * GLA_DV:(p + 1) * 2 * GLA_DV])
            upds.append(upd[0:GLA_DK, 0:GLA_DV])
            upds.append(upd[GLA_DK:2 * GLA_DK, GLA_DV:2 * GLA_DV])
        lhs_list.append([jnp.concatenate([a_all[:, p * 2 * cc:(p + 1) * 2 * cc],
                                          qs[:, p * 2 * GLA_DK:(p + 1) * 2 * GLA_DK]], axis=1)
                         for p in range(GLA_HEADS // 2)])
        v_list.append(vc)
        upd_list.append(jnp.concatenate(upds, axis=0))
        dcol_list.append(jnp.transpose(jnp.broadcast_to(jnp.exp(b_end), (GLA_DV, GLA_KEY_WIDTH))))

    o_chunks = []
    state = s_ref[...]
    for c in range(n_chunks):
        s_bf = state.astype(BF16)
        vc = v_list[c]
        outs = []
        for p in range(GLA_HEADS // 2):
            h0, h1 = 2 * p, 2 * p + 1
            rhs = jnp.concatenate([
                _pair_block_diag(vc[:, h0 * GLA_DV:(h0 + 1) * GLA_DV], vc[:, h1 * GLA_DV:(h1 + 1) * GLA_DV]),
                _pair_block_diag(s_bf[h0 * GLA_DK:(h0 + 1) * GLA_DK], s_bf[h1 * GLA_DK:(h1 + 1) * GLA_DK]),
            ], axis=0)
            outs.append(_dot(lhs_list[c][p], rhs))
        o_chunks.append(jnp.concatenate(outs, axis=1))
        state = state * dcol_list[c] + upd_list[c]
        pump(PUMP_PLAN["state"][c])
    s_ref[...] = state
    o = jnp.concatenate(o_chunks, axis=0)

    y_ref[...] = _epilogue(xkeep_ref[...], pool_act, o, proj_ref[:, C_GG:C_GL], ng_ref, wout_ref, lng_ref, lnb_ref,
                           pump, tail, row_blocks=4)


def _prompt_body(x_ref, win_ref, wmap_ref, pscale_ref, ba_ref, ng_ref, wout_ref, lng_ref, lnb_ref,
                 y_ref, pool_ref, s_out_ref,
                 proj0_ref, proj1_ref, xkeep0_ref, xkeep1_ref, la0_ref, la1_ref, xb_ref, ext_ref, s_ref, steep_ref,
                 *, tc, sub, nj):
    i = pl.program_id(0)
    jb = (i + nj - 1) % nj

    @pl.when(i == 0)
    def _():
        proj1_ref[...] = jnp.zeros_like(proj1_ref)
        xkeep1_ref[...] = jnp.zeros_like(xkeep1_ref)
        la1_ref[...] = jnp.zeros_like(la1_ref)
        steep_ref[0] = 0

    @pl.when((i == 0) | (jb == 0))
    def _():
        ext_ref[0:HIST_ROWS, :] = jnp.zeros((HIST_ROWS, POOL_WIDTH), F32)
        s_ref[...] = jnp.zeros_like(s_ref)

    units = [(r, lo) for lo in (C_GL, *range(0, C_GL, MXU_COLS)) for r in range(0, tc, PUMP_ROWS)]

    def step(proj_a, xkeep_a, la_a, proj_b, xkeep_b, la_b, robust):
        x = x_ref[0]
        xkeep_a[...] = x
        xb_ref[...] = x.astype(BF16)
        pump = _make_pump(units, xb_ref, win_ref, proj_a)
        pump(tc // PUMP_ROWS)

        def next_decay():
            la = _log_sigmoid(proj_a[:, C_GL:C_END] + ba_ref[...]) * (1.0 / GLA_TAU)
            la_a[...] = la
            chunk_decay = [jnp.sum(la[c * GLA_CHUNK:(c + 1) * GLA_CHUNK], axis=0, keepdims=True)
                           for c in range(tc // GLA_CHUNK)]
            lowest = jnp.min(functools.reduce(jnp.minimum, chunk_decay))
            steep_ref[0] = jnp.where(lowest < -SAFE_LOG_DECAY, 1, 0).astype(jnp.int32)

        n_sub = tc // sub
        for k in range(n_sub):
            rows = slice(k * sub, (k + 1) * sub)
            last = k == n_sub - 1
            _prompt_stage_b((jb * n_sub + k) * sub, proj_b.at[rows, :], xkeep_b.at[rows, :], la_b.at[rows, :],
                            wmap_ref, pscale_ref, ba_ref, ng_ref, wout_ref, lng_ref, lnb_ref,
                            y_ref.at[0, rows, :], ext_ref, s_ref, pump, next_decay if k == 0 else (lambda: None),
                            tc=sub, robust=robust, tail=None if last else SUBTILE_TAIL_UNITS)
            if not last:
                ext_ref[0:HIST_ROWS, :] = ext_ref[sub:sub + HIST_ROWS, :]

    set0 = (proj0_ref, xkeep0_ref, la0_ref)
    set1 = (proj1_ref, xkeep1_ref, la1_ref)
    steep = steep_ref[0] != 0
    for parity, (set_a, set_b) in enumerate(((set0, set1), (set1, set0))):
        for robust in (False, True):
            @pl.when((i % 2 == parity) & (steep == robust))
            def _(set_a=set_a, set_b=set_b, robust=robust):
                step(*set_a, *set_b, robust)

    @pl.when((i > 0) & (jb == nj - 1))
    def _():
        pool_ref[0] = ext_ref[sub + 1:sub + HIST_ROWS, :]
        s_out_ref[0] = s_ref[...]

    ext_ref[0:HIST_ROWS, :] = ext_ref[sub:sub + HIST_ROWS, :]


def _prompt_call(x, win, wmap, pscale, ba, ng, wout, lng, lnb):
    nb, t, _ = x.shape
    tc = PROMPT_TILE
    nj = t // tc
    ntiles = nb * nj
    const2 = lambda i: (0, 0)

    def tile_a(i):
        ia = jnp.minimum(i, ntiles - 1)
        return (ia // nj, ia % nj, 0)

    def tile_b(i):
        ib = jnp.maximum(i - 1, 0)
        return (ib // nj, ib % nj, 0)

    def seq_b(i):
        return (jnp.maximum(i - 1, 0) // nj, 0, 0)

    return pl.pallas_call(
        functools.partial(_prompt_body, tc=tc, sub=PROMPT_SUBTILE, nj=nj),
        grid=(ntiles + 1,),
        in_specs=[
            pl.BlockSpec((1, tc, D_MODEL), tile_a),
            pl.BlockSpec(win.shape, const2),
            pl.BlockSpec(wmap.shape, lambda i: (0, 0, 0)),
            pl.BlockSpec(pscale.shape, const2),
            pl.BlockSpec(ba.shape, const2),
            pl.BlockSpec(ng.shape, const2),
            pl.BlockSpec(wout.shape, const2),
            pl.BlockSpec(lng.shape, const2),
            pl.BlockSpec(lnb.shape, const2),
        ],
        out_specs=[
            pl.BlockSpec((1, tc, D_MODEL), tile_b),
            pl.BlockSpec((1, POOL_BUF, POOL_WIDTH), seq_b),
            pl.BlockSpec((1, GLA_KEY_WIDTH, GLA_DV), seq_b),
        ],
        out_shape=[
            jax.ShapeDtypeStruct((nb, t, D_MODEL), F32),
            jax.ShapeDtypeStruct((nb, POOL_BUF, POOL_WIDTH), F32),
            jax.ShapeDtypeStruct((nb, GLA_KEY_WIDTH, GLA_DV), F32),
        ],
        scratch_shapes=[
            pltpu.VMEM((tc, C_END), F32),
            pltpu.VMEM((tc, C_END), F32),
            pltpu.VMEM((tc, D_MODEL), F32),
            pltpu.VMEM((tc, D_MODEL), F32),
            pltpu.VMEM((tc, GLA_KEY_WIDTH), F32),
            pltpu.VMEM((tc, GLA_KEY_WIDTH), F32),
            pltpu.VMEM((tc, D_MODEL), BF16),
            pltpu.VMEM((HIST_ROWS + PROMPT_SUBTILE, POOL_WIDTH), F32),
            pltpu.VMEM((GLA_KEY_WIDTH, GLA_DV), F32),
            pltpu.SMEM((1,), jnp.int32),
        ],
        compiler_params=pltpu.CompilerParams(
            dimension_semantics=("arbitrary",), vmem_limit_bytes=VMEM_LIMIT),
        name="prompt_layer",
    )(x, win, wmap, pscale, ba, ng, wout, lng, lnb)


def _sample_body(x_ref, pool_in_ref, s_in_ref, win_ref, wmap_ref, pscale_ref, ba_ref, ng_ref, wout_ref,
                 lng_ref, lnb_ref, y_ref, pool_ref, s_out_ref, u_ref, pooled_ref, *, bs, t, start_pos):
    rows = bs * t
    x = x_ref[...].reshape(rows, D_MODEL)
    xb = x.astype(BF16)

    def proj(lo, hi):
        return _dot(xb, win_ref[:, lo:hi])

    u = proj(C_U, C_GP)
    mixed = []
    for g, w in enumerate(POOL_WINDOWS):
        sl = slice(g * POOL_GROUP_DIM, (g + 1) * POOL_GROUP_DIM)
        u_ref[g] = u[:, sl]
        ext = [pool_in_ref[r, :, sl] for r in range(POOL_BUF)]
        ext += [u_ref[g, pl.ds(tok, bs, stride=t), :] for tok in range(t)]
        for r in range(POOL_BUF):
            pool_ref[r, :, sl] = ext[t + r]
        sums = ext
        sh = 1
        while sh < w:
            first = 2 * sh - 1
            sums = [None] * first + [sums[r] + sums[r - sh] for r in range(first, len(sums))]
            sh *= 2
        for tok in range(t):
            cnt = float(min(start_pos + tok + 1, w))
            pooled_ref[g, pl.ds(tok, bs, stride=t), :] = sums[POOL_BUF + tok] / cnt - ext[POOL_BUF + tok]
        mixed.append(_dot(pooled_ref[g].astype(BF16), wmap_ref[g]))
    gp = proj(C_GP, C_Q)
    pool_act = jnp.concatenate(mixed, axis=1) * pscale_ref[...] * _silu(gp)

    q = proj(C_Q, C_K) * (GLA_DK ** -0.5)
    k = proj(C_K, C_V)
    vb = proj(C_V, C_GG).astype(BF16)
    gg = proj(C_GG, C_GL)
    la = _log_sigmoid(proj(C_GL, C_END) + ba_ref[...]) * (1.0 / GLA_TAU)

    rr = lax.broadcasted_iota(jnp.int32, (rows, rows), 0)
    cl = lax.broadcasted_iota(jnp.int32, (rows, rows), 1)
    same = (rr // t) == (cl // t)
    ltri = jnp.where(same & (cl <= rr), 1.0, 0.0).astype(BF16)
    lall = jnp.where(same, 1.0, 0.0).astype(BF16)
    hi, lo = _split_bf16(la)
    b = _dot(ltri, hi) + _dot(ltri, lo)
    b_end = _dot(lall, hi) + _dot(lall, lo)
    qs = (q * jnp.exp(b)).astype(BF16)
    kd = (k * jnp.exp(b_end - b)).astype(BF16)

    assert t & (t - 1) == 0, "tokens per sequence must be a power of two"
    blk = _head_block_mask(rows, GLA_DK, (GLA_HEADS * rows, GLA_KEY_WIDTH))
    a_all = _decayed_scores(q, k, la, b, _score_masks(rows, t.bit_length() - 1), blk)

    s_old = s_in_ref[...]
    s_bf = s_old.astype(BF16)
    qs3 = qs.reshape(bs, t, GLA_KEY_WIDTH)
    kd3 = kd.reshape(bs, t, GLA_KEY_WIDTH)
    v3 = vb.reshape(bs, t, GLA_WIDTH)
    hi3 = hi.reshape(bs, t, GLA_KEY_WIDTH)
    lo3 = lo.reshape(bs, t, GLA_KEY_WIDTH)
    ones3 = jnp.ones((bs, t, GLA_DV), BF16)
    bdims_nn = (((2,), (1,)), ((0,), (0,)))
    bdims_tn = (((1,), (1,)), ((0,), (0,)))
    dsum = (lax.dot_general(hi3, ones3, bdims_tn, preferred_element_type=F32)
            + lax.dot_general(lo3, ones3, bdims_tn, preferred_element_type=F32))
    outs, upds = [], []
    for p in range(GLA_HEADS // 2):
        h0, h1 = 2 * p, 2 * p + 1
        intra = _dot(a_all[:, p * 2 * rows:(p + 1) * 2 * rows],
                     _pair_block_diag(vb[:, h0 * GLA_DV:(h0 + 1) * GLA_DV], vb[:, h1 * GLA_DV:(h1 + 1) * GLA_DV]))
        s0 = s_bf[:, h0 * GLA_DK:(h0 + 1) * GLA_DK, :]
        s1 = s_bf[:, h1 * GLA_DK:(h1 + 1) * GLA_DK, :]
        z = jnp.zeros_like(s0)
        s_pair = jnp.concatenate(
            [jnp.concatenate([s0, z], axis=2), jnp.concatenate([z, s1], axis=2)], axis=1)
        inter = lax.dot_general(qs3[:, :, p * 2 * GLA_DK:(p + 1) * 2 * GLA_DK], s_pair, bdims_nn,
                                preferred_element_type=F32)
        outs.append(intra + inter.reshape(rows, 2 * GLA_DV))
        upd = lax.dot_general(kd3[:, :, p * 2 * GLA_DK:(p + 1) * 2 * GLA_DK],
                              v3[:, :, p * 2 * GLA_DV:(p + 1) * 2 * GLA_DV], bdims_tn,
                              preferred_element_type=F32)
        upds.append(upd[:, 0:GLA_DK, 0:GLA_DV])
        upds.append(upd[:, GLA_DK:2 * GLA_DK, GLA_DV:2 * GLA_DV])
    o = jnp.concatenate(outs, axis=1)
    s_out_ref[...] = s_old * jnp.exp(dsum) + jnp.concatenate(upds, axis=1)

    y = _epilogue(x, pool_act, o, gg, ng_ref, wout_ref, lng_ref, lnb_ref)
    y_ref[...] = y.reshape(bs, t, D_MODEL)


def _sample_call(x, pool_in, s_in, win, wmap, pscale, ba, ng, wout, lng, lnb, start_pos):
    nb, t, _ = x.shape
    bs = SAMPLE_SEQS
    const2 = lambda i: (0, 0)
    seq3 = lambda i: (i, 0, 0)
    hist3 = lambda i: (0, i, 0)
    return pl.pallas_call(
        functools.partial(_sample_body, bs=bs, t=t, start_pos=start_pos),
        grid=(nb // bs,),
        in_specs=[
            pl.BlockSpec((bs, t, D_MODEL), seq3),
            pl.BlockSpec((POOL_BUF, bs, POOL_WIDTH), hist3),
            pl.BlockSpec((bs, GLA_KEY_WIDTH, GLA_DV), seq3),
            pl.BlockSpec(win.shape, const2),
            pl.BlockSpec(wmap.shape, lambda i: (0, 0, 0)),
            pl.BlockSpec(pscale.shape, const2),
            pl.BlockSpec(ba.shape, const2),
            pl.BlockSpec(ng.shape, const2),
            pl.BlockSpec(wout.shape, const2),
            pl.BlockSpec(lng.shape, const2),
            pl.BlockSpec(lnb.shape, const2),
        ],
        out_specs=[
            pl.BlockSpec((bs, t, D_MODEL), seq3),
            pl.BlockSpec((POOL_BUF, bs, POOL_WIDTH), hist3),
            pl.BlockSpec((bs, GLA_KEY_WIDTH, GLA_DV), seq3),
        ],
        out_shape=[
            jax.ShapeDtypeStruct((nb, t, D_MODEL), F32),
            jax.ShapeDtypeStruct((POOL_BUF, nb, POOL_WIDTH), F32),
            jax.ShapeDtypeStruct((nb, GLA_KEY_WIDTH, GLA_DV), F32),
        ],
        scratch_shapes=[pltpu.VMEM((len(POOL_WINDOWS), bs * t, POOL_GROUP_DIM), F32),
                        pltpu.VMEM((len(POOL_WINDOWS), bs * t, POOL_GROUP_DIM), F32)],
        compiler_params=pltpu.CompilerParams(
            dimension_semantics=("arbitrary",), vmem_limit_bytes=VMEM_LIMIT),
        name="sample_layer",
    )(x, pool_in, s_in, win, wmap, pscale, ba, ng, wout, lng, lnb)


def _prep_body(wt_ref, wfat_ref, wa2t_ref, out_ref, *, n_plain):
    i = pl.program_id(0)

    @pl.when(i < n_plain)
    def _():
        out_ref[...] = jnp.transpose(wt_ref[...]).astype(BF16)

    @pl.when(i == n_plain)
    def _():
        wfat = wfat_ref[...]
        wa2t = wa2t_ref[...]
        acc = wa2t[:, 0:1] * wfat[0:1, :]
        for r in range(1, GLA_RANK):
            acc = acc + wa2t[:, r:r + 1] * wfat[r:r + 1, :]
        out_ref[...] = jnp.transpose(acc).astype(BF16)


def _prep_input_weights(wt, wa2t):
    n_plain = C_GL // MXU_COLS
    return pl.pallas_call(
        functools.partial(_prep_body, n_plain=n_plain),
        grid=(n_plain + 1,),
        in_specs=[
            pl.BlockSpec((MXU_COLS, D_MODEL), lambda i: (jnp.minimum(i, n_plain - 1), 0)),
            pl.BlockSpec((GLA_RANK, D_MODEL), lambda i: (0, 0)),
            pl.BlockSpec((GLA_KEY_WIDTH, GLA_RANK), lambda i: (0, 0)),
        ],
        out_specs=pl.BlockSpec((D_MODEL, MXU_COLS), lambda i: (0, i)),
        out_shape=jax.ShapeDtypeStruct((D_MODEL, C_END), BF16),
        compiler_params=pltpu.CompilerParams(dimension_semantics=("arbitrary",)),
        name="prep_input_weights",
    )(wt, wt[C_GL:], wa2t)


def kernel(x_prompt, x_sample, state_pool, state_gla, w_in, w_pool_map, pool_scale, w_gla_a2, b_gla_a,
           gla_norm_g, w_out, ln_g, ln_b):
    assert w_in.shape[0] == DEPTH
    win = _prep_input_weights(jnp.transpose(w_in[0]), jnp.transpose(w_gla_a2[0]))
    wmap = w_pool_map[0].astype(BF16)
    wout = w_out[0].astype(BF16)
    pscale = pool_scale[0].reshape(1, POOL_WIDTH)
    ba = b_gla_a[0].reshape(1, GLA_KEY_WIDTH)
    ng = gla_norm_g[0].reshape(1, GLA_WIDTH)
    lng = ln_g[0].reshape(1, D_MODEL)
    lnb = ln_b[0].reshape(1, D_MODEL)
    params = (win, wmap, pscale, ba, ng, wout, lng, lnb)

    y_p, pool_p, gla_p = _prompt_call(x_prompt, *params)
    nsamp = x_sample.shape[0]
    y_s, pool_s, gla_s = _sample_call(
        x_sample, jnp.transpose(state_pool[0], (1, 0, 2)),
        state_gla[0].reshape(nsamp, GLA_KEY_WIDTH, GLA_DV), *params, start_pos=PAST_LEN)

    nb = x_prompt.shape[0]
    return (y_p, y_s, pool_p[None],
            gla_p.reshape(1, nb, GLA_HEADS, GLA_DK, GLA_DV),
            jnp.transpose(pool_s, (1, 0, 2))[None],
            gla_s.reshape(1, nsamp, GLA_HEADS, GLA_DK, GLA_DV))
```

```python
import functools
import itertools

import jax
import jax.numpy as jnp
from jax import lax
from jax.experimental import pallas as pl
from jax.experimental.pallas import tpu as pltpu

F32 = jnp.float32
BF16 = jnp.bfloat16

D_MODEL = 1024
POOL_WIDTH = 512
POOL_WINDOWS = (2, 4, 8, 16)
POOL_GROUP_DIM = 128
POOL_BUF = 15
HIST_ROWS = 16
GLA_HEADS = 4
GLA_DK = 64
GLA_DV = 128
GLA_KEY_WIDTH = GLA_HEADS * GLA_DK
GLA_WIDTH = GLA_HEADS * GLA_DV
GLA_RANK = 16
GLA_TAU = 16.0
LN_EPS = 1e-5
RMS_EPS = 1e-6
DEPTH = 1
DEEPNORM_ALPHA = (2.0 * DEPTH) ** 0.25
PAST_LEN = 16384

C_U, C_GP, C_Q, C_K, C_V, C_GG, C_GL, C_END = 0, 512, 1024, 1280, 1536, 2048, 2560, 2816

MXU_COLS = 256
PROMPT_TILE = 512
PROMPT_SUBTILE = 256
SUBTILE_TAIL_UNITS = 0
PUMP_ROWS = 512
PUMP_PLAN = {"pool": (0, 0, 0, 1), "cumsum": 0, "scores": (1, 0, 1, 0), "level": (0, 0, 0, 1, 0, 0),
             "state": (0, 0, 0, 0), "epilogue": 1, "layernorm": (1, 0, 0, 0)}
SAFE_LOG_DECAY = 60.0
GLA_CHUNK = 64
SAMPLE_SEQS = 16
VMEM_LIMIT = 56 * 1024 * 1024


def _dot(a, b):
    return jnp.dot(a, b, preferred_element_type=F32)


def _dot_nt(a, b):
    return lax.dot_general(a, b, (((1,), (1,)), ((), ())), preferred_element_type=F32)


def _dot_tn(a, b):
    return lax.dot_general(a, b, (((0,), (0,)), ((), ())), preferred_element_type=F32)


def _silu(x):
    return x * (1.0 / (1.0 + jnp.exp(-x)))


def _log_sigmoid(x):
    return jnp.minimum(x, 0.0) - jnp.log1p(jnp.exp(-jnp.abs(x)))


def _split_bf16(x):
    hi = x.astype(BF16)
    lo = (x - hi.astype(F32)).astype(BF16)
    return hi, lo


def _epilogue(x, pool_act, o, gg, ng_ref, wout_ref, lng_ref, lnb_ref, pump=lambda n=None: None, tail=None,
              row_blocks=1):
    pump(PUMP_PLAN["epilogue"])
    normed = []
    for h in range(GLA_HEADS):
        oh = o[:, h * GLA_DV:(h + 1) * GLA_DV]
        ms = jnp.mean(oh * oh, axis=-1, keepdims=True)
        normed.append(oh * lax.rsqrt(ms + RMS_EPS))
    gla_act = jnp.concatenate(normed, axis=1) * ng_ref[...] * _silu(gg)
    mix = jnp.concatenate([pool_act, gla_act], axis=1).astype(BF16)
    y = _dot(mix, wout_ref[...])
    n = x.shape[0] // row_blocks
    outs = []
    for r in range(row_blocks):
        rows = slice(r * n, (r + 1) * n)
        pump(PUMP_PLAN["layernorm"][r])
        h = DEEPNORM_ALPHA * x[rows] + y[rows]
        mu = jnp.mean(h, axis=-1, keepdims=True)
        d = h - mu
        var = jnp.mean(d * d, axis=-1, keepdims=True)
        outs.append(d * lax.rsqrt(var + LN_EPS) * lng_ref[...] + lnb_ref[...])
    pump(tail)
    return outs[0] if row_blocks == 1 else jnp.concatenate(outs, axis=0)


def _head_block_mask(rows_per_head, cols_per_head, shape):
    r = lax.broadcasted_iota(jnp.int32, shape, 0) // rows_per_head
    c = lax.broadcasted_iota(jnp.int32, shape, 1) // cols_per_head
    return r == c


def _score_masks(rows, n_levels):
    shape = (rows, GLA_HEADS * rows)
    t_idx = lax.broadcasted_iota(jnp.int32, shape, 0)
    s_idx = lax.broadcasted_iota(jnp.int32, shape, 1) % rows
    masks = [t_idx == s_idx]
    for lvl in range(n_levels):
        half = 1 << lvl
        blk = 2 * half
        masks.append((t_idx // blk == s_idx // blk) & (t_idx % blk >= half) & (s_idx % blk < half))
    return masks


def _decayed_scores(q, k, la, b, masks, head_blk, pump=lambda n=None: None):
    rows = q.shape[0]
    r_idx = lax.broadcasted_iota(jnp.int32, q.shape, 0)

    def scores(qt, kt):
        kh = jnp.where(head_blk, jnp.concatenate([kt.astype(BF16)] * GLA_HEADS, axis=0), jnp.zeros((), BF16))
        return _dot_nt(qt.astype(BF16), kh)

    acc = jnp.where(masks[0], scores(q, k), 0.0)
    for lvl in range(len(masks) - 1):
        half = 1 << lvl
        blk = 2 * half
        if lvl == 0:
            neg = jnp.where(r_idx % 2 == 1, la, 0.0)
        elif lvl == 1:
            r4 = r_idx % 4
            nxt = pltpu.roll(la, rows - 1, 0)
            prv = pltpu.roll(la, 1, 0)
            neg = jnp.where(r4 == 0, nxt, jnp.where(r4 == 1, 0.0, jnp.where(r4 == 2, la, la + prv)))
        else:
            ref = jnp.concatenate(
                [jnp.broadcast_to(b[j * blk + half - 1:j * blk + half, :], (blk, b.shape[1]))
                 for j in range(rows // blk)], axis=0)
            d = b - ref
            neg = jnp.minimum(d, -d)
        w = jnp.exp(neg)
        acc = jnp.where(masks[lvl + 1], scores(q * w, k * w), acc)
        pump(PUMP_PLAN["level"][lvl])
    return acc.astype(BF16)


def _pair_block_diag(a0, a1):
    z = jnp.zeros_like(a0)
    return jnp.concatenate(
        [jnp.concatenate([a0, z], axis=1), jnp.concatenate([z, a1], axis=1)], axis=0)


def _make_pump(units, xb_ref, win_ref, proj_ref):
    units = iter(units)

    def pump(n=None):
        for r, lo in (units if n is None else itertools.islice(units, n)):
            proj_ref[r:r + PUMP_ROWS, lo:lo + MXU_COLS] = _dot(
                xb_ref[r:r + PUMP_ROWS, :], win_ref[:, lo:lo + MXU_COLS])

    return pump


def _naive_scores(q, k, b, causal, head_blk):
    ks = (k * jnp.exp(-b)).astype(BF16)
    kh = jnp.where(head_blk, jnp.concatenate([ks] * GLA_HEADS, axis=0), jnp.zeros((), BF16))
    return jnp.where(causal, _dot_nt((q * jnp.exp(b)).astype(BF16), kh), 0.0).astype(BF16)


def _prompt_stage_b(pos0, proj_ref, xkeep_ref, la_ref, wmap_ref, pscale_ref, ba_ref, ng_ref, wout_ref, lng_ref,
                    lnb_ref, y_ref, ext_ref, s_ref, pump, next_decay, *, tc, robust, tail):
    u = proj_ref[:, C_U:C_GP]
    ext_ref[HIST_ROWS:HIST_ROWS + tc, :] = u
    pos16 = pos0 + lax.broadcasted_iota(jnp.int32, (HIST_ROWS, POOL_GROUP_DIM), 0)
    mixed = []
    for g, w in enumerate(POOL_WINDOWS):
        sl = slice(g * POOL_GROUP_DIM, (g + 1) * POOL_GROUP_DIM)
        e = ext_ref[:, sl]
        s = e
        sh = 1
        while sh < w:
            s = s + pltpu.roll(s, sh, 0)
            sh *= 2
        wsum = s[HIST_ROWS:]
        self_rows = e[HIST_ROWS:]
        cnt = jnp.minimum(pos16 + 1, w).astype(F32)
        first = wsum[:HIST_ROWS] / cnt - self_rows[:HIST_ROWS]
        rest = wsum[HIST_ROWS:] * (1.0 / w) - self_rows[HIST_ROWS:]
        pooled = jnp.concatenate([first, rest], axis=0)
        mixed.append(_dot(pooled.astype(BF16), wmap_ref[g]))
        pump(PUMP_PLAN["pool"][g])
    pool_act = jnp.concatenate(mixed, axis=1) * pscale_ref[...] * _silu(proj_ref[:, C_GP:C_Q])
    next_decay()

    cc = GLA_CHUNK
    n_chunks = tc // cc
    rr = lax.broadcasted_iota(jnp.int32, (tc, tc), 0)
    cl = lax.broadcasted_iota(jnp.int32, (tc, tc), 1)
    ltri = jnp.where((rr // cc == cl // cc) & (cl <= rr), 1.0, 0.0).astype(BF16)
    masks = _score_masks(cc, cc.bit_length() - 1 if robust else 0)
    causal = (lax.broadcasted_iota(jnp.int32, (cc, GLA_HEADS * cc), 1) % cc
              <= lax.broadcasted_iota(jnp.int32, (cc, GLA_HEADS * cc), 0))
    blk = _head_block_mask(cc, GLA_DK, (GLA_HEADS * cc, GLA_KEY_WIDTH))

    hi, lo = _split_bf16(la_ref[...])
    b_all = _dot(ltri, hi) + _dot(ltri, lo)
    pump(PUMP_PLAN["cumsum"])
    lhs_list, v_list, upd_list, dcol_list = [], [], [], []
    for c in range(n_chunks):
        rs = slice(c * cc, (c + 1) * cc)
        b = b_all[rs]
        b_end = b[cc - 1:cc, :]
        q_c = proj_ref[rs, C_Q:C_K] * (GLA_DK ** -0.5)
        k_c = proj_ref[rs, C_K:C_V]
        qs = (q_c * jnp.exp(b)).astype(BF16)
        kd = (k_c * jnp.exp(b_end - b)).astype(BF16)
        if robust:
            a_all = _decayed_scores(q_c, k_c, la_ref[rs, :], b, masks, blk, pump)
        else:
            a_all = _naive_scores(q_c, k_c, b, causal, blk)
            pump(PUMP_PLAN["scores"][c])
        vc = proj_ref[rs, C_V:C_GG].astype(BF16)
        upds = []
        for p in range(GLA_HEADS // 2):
            upd = _dot_tn(kd[:, p * 2 * GLA_DK:(p + 1) * 2 * GLA_DK], vc[:, p * 2 * GLA_DV:(p + 1) * 2 * GLA_DV])
            upds.append(upd[0:GLA_DK, 0:GLA_DV])
            upds.append(upd[GLA_DK:2 * GLA_DK, GLA_DV:2 * GLA_DV])
        lhs_list.append([jnp.concatenate([a_all[:, p * 2 * cc:(p + 1) * 2 * cc],
                                          qs[:, p * 2 * GLA_DK:(p + 1) * 2 * GLA_DK]], axis=1)
                         for p in range(GLA_HEADS // 2)])
        v_list.append(vc)
        upd_list.append(jnp.concatenate(upds, axis=0))
        dcol_list.append(jnp.transpose(jnp.broadcast_to(jnp.exp(b_end), (GLA_DV, GLA_KEY_WIDTH))))

    o_chunks = []
    state = s_ref[...]
    for c in range(n_chunks):
        s_bf = state.astype(BF16)
        vc = v_list[c]
        outs = []
        for p in range(GLA_HEADS // 2):
            h0, h1 = 2 * p, 2 * p + 1
            rhs = jnp.concatenate([
                _pair_block_diag(vc[:, h0 * GLA_DV:(h0 + 1) * GLA_DV], vc[:, h1 * GLA_DV:(h1 + 1) * GLA_DV]),
                _pair_block_diag(s_bf[h0 * GLA_DK:(h0 + 1) * GLA_DK], s_bf[h1 * GLA_DK:(h1 + 1) * GLA_DK]),
            ], axis=0)
            outs.append(_dot(lhs_list[c][p], rhs))
        o_chunks.append(jnp.concatenate(outs, axis=1))
        state = state * dcol_list[c] + upd_list[c]
        pump(PUMP_PLAN["state"][c])
    s_ref[...] = state
    o = jnp.concatenate(o_chunks, axis=0)

    y_ref[...] = _epilogue(xkeep_ref[...], pool_act, o, proj_ref[:, C_GG:C_GL], ng_ref, wout_ref, lng_ref, lnb_ref,
                           pump, tail, row_blocks=4)


def _prompt_body(x_ref, win_ref, wmap_ref, pscale_ref, ba_ref, ng_ref, wout_ref, lng_ref, lnb_ref,
                 y_ref, pool_ref, s_out_ref,
                 proj0_ref, proj1_ref, xkeep0_ref, xkeep1_ref, la0_ref, la1_ref, xb_ref, ext_ref, s_ref, steep_ref,
                 *, tc, sub, nj):
    i = pl.program_id(0)
    jb = (i + nj - 1) % nj

    @pl.when(i == 0)
    def _():
        proj1_ref[...] = jnp.zeros_like(proj1_ref)
        xkeep1_ref[...] = jnp.zeros_like(xkeep1_ref)
        la1_ref[...] = jnp.zeros_like(la1_ref)
        steep_ref[0] = 0

    @pl.when((i == 0) | (jb == 0))
    def _():
        ext_ref[0:HIST_ROWS, :] = jnp.zeros((HIST_ROWS, POOL_WIDTH), F32)
        s_ref[...] = jnp.zeros_like(s_ref)

    units = [(r, lo) for lo in (C_GL, *range(0, C_GL, MXU_COLS)) for r in range(0, tc, PUMP_ROWS)]

    def step(proj_a, xkeep_a, la_a, proj_b, xkeep_b, la_b, robust):
        x = x_ref[0]
        xkeep_a[...] = x
        xb_ref[...] = x.astype(BF16)
        pump = _make_pump(units, xb_ref, win_ref, proj_a)
        pump(tc // PUMP_ROWS)

        def next_decay():
            la = _log_sigmoid(proj_a[:, C_GL:C_END] + ba_ref[...]) * (1.0 / GLA_TAU)
            la_a[...] = la
            chunk_decay = [jnp.sum(la[c * GLA_CHUNK:(c + 1) * GLA_CHUNK], axis=0, keepdims=True)
                           for c in range(tc // GLA_CHUNK)]
            lowest = jnp.min(functools.reduce(jnp.minimum, chunk_decay))
            steep_ref[0] = jnp.where(lowest < -SAFE_LOG_DECAY, 1, 0).astype(jnp.int32)

        n_sub = tc // sub
        for k in range(n_sub):
            rows = slice(k * sub, (k + 1) * sub)
            last = k == n_sub - 1
            _prompt_stage_b((jb * n_sub + k) * sub, proj_b.at[rows, :], xkeep_b.at[rows, :], la_b.at[rows, :],
                            wmap_ref, pscale_ref, ba_ref, ng_ref, wout_ref, lng_ref, lnb_ref,
                            y_ref.at[0, rows, :], ext_ref, s_ref, pump, next_decay if k == 0 else (lambda: None),
                            tc=sub, robust=robust, tail=None if last else SUBTILE_TAIL_UNITS)
            if not last:
                ext_ref[0:HIST_ROWS, :] = ext_ref[sub:sub + HIST_ROWS, :]

    set0 = (proj0_ref, xkeep0_ref, la0_ref)
    set1 = (proj1_ref, xkeep1_ref, la1_ref)
    steep = steep_ref[0] != 0
    for parity, (set_a, set_b) in enumerate(((set0, set1), (set1, set0))):
        for robust in (False, True):
            @pl.when((i % 2 == parity) & (steep == robust))
            def _(set_a=set_a, set_b=set_b, robust=robust):
                step(*set_a, *set_b, robust)

    @pl.when((i > 0) & (jb == nj - 1))
    def _():
        pool_ref[0] = ext_ref[sub + 1:sub + HIST_ROWS, :]
        s_out_ref[0] = s_ref[...]

    ext_ref[0:HIST_ROWS, :] = ext_ref[sub:sub + HIST_ROWS, :]


def _prompt_call(x, win, wmap, pscale, ba, ng, wout, lng, lnb):
    nb, t, _ = x.shape
    tc = PROMPT_TILE
    nj = t // tc
    ntiles = nb * nj
    const2 = lambda i: (0, 0)

    def tile_a(i):
        ia = jnp.minimum(i, ntiles - 1)
        return (ia // nj, ia % nj, 0)

    def tile_b(i):
        ib = jnp.maximum(i - 1, 0)
        return (ib // nj, ib % nj, 0)

    def seq_b(i):
        return (jnp.maximum(i - 1, 0) // nj, 0, 0)

    return pl.pallas_call(
        functools.partial(_prompt_body, tc=tc, sub=PROMPT_SUBTILE, nj=nj),
        grid=(ntiles + 1,),
        in_specs=[
            pl.BlockSpec((1, tc, D_MODEL), tile_a),
            pl.BlockSpec(win.shape, const2),
            pl.BlockSpec(wmap.shape, lambda i: (0, 0, 0)),
            pl.BlockSpec(pscale.shape, const2),
            pl.BlockSpec(ba.shape, const2),
            pl.BlockSpec(ng.shape, const2),
            pl.BlockSpec(wout.shape, const2),
            pl.BlockSpec(lng.shape, const2),
            pl.BlockSpec(lnb.shape, const2),
        ],
        out_specs=[
            pl.BlockSpec((1, tc, D_MODEL), tile_b),
            pl.BlockSpec((1, POOL_BUF, POOL_WIDTH), seq_b),
            pl.BlockSpec((1, GLA_KEY_WIDTH, GLA_DV), seq_b),
        ],
        out_shape=[
            jax.ShapeDtypeStruct((nb, t, D_MODEL), F32),
            jax.ShapeDtypeStruct((nb, POOL_BUF, POOL_WIDTH), F32),
            jax.ShapeDtypeStruct((nb, GLA_KEY_WIDTH, GLA_DV), F32),
        ],
        scratch_shapes=[
            pltpu.VMEM((tc, C_END), F32),
            pltpu.VMEM((tc, C_END), F32),
            pltpu.VMEM((tc, D_MODEL), F32),
            pltpu.VMEM((tc, D_MODEL), F32),
            pltpu.VMEM((tc, GLA_KEY_WIDTH), F32),
            pltpu.VMEM((tc, GLA_KEY_WIDTH), F32),
            pltpu.VMEM((tc, D_MODEL), BF16),
            pltpu.VMEM((HIST_ROWS + PROMPT_SUBTILE, POOL_WIDTH), F32),
            pltpu.VMEM((GLA_KEY_WIDTH, GLA_DV), F32),
            pltpu.SMEM((1,), jnp.int32),
        ],
        compiler_params=pltpu.CompilerParams(
            dimension_semantics=("arbitrary",), vmem_limit_bytes=VMEM_LIMIT),
        name="prompt_layer",
    )(x, win, wmap, pscale, ba, ng, wout, lng, lnb)


def _sample_body(x_ref, pool_in_ref, s_in_ref, win_ref, wmap_ref, pscale_ref, ba_ref, ng_ref, wout_ref,
                 lng_ref, lnb_ref, y_ref, pool_ref, s_out_ref, u_ref, pooled_ref, *, bs, t, start_pos):
    rows = bs * t
    x = x_ref[...].reshape(rows, D_MODEL)
    xb = x.astype(BF16)

    def proj(lo, hi):
        return _dot(xb, win_ref[:, lo:hi])

    u = proj(C_U, C_GP)
    mixed = []
    for g, w in enumerate(POOL_WINDOWS):
        sl = slice(g * POOL_GROUP_DIM, (g + 1) * POOL_GROUP_DIM)
        u_ref[g] = u[:, sl]
        ext = [pool_in_ref[r, :, sl] for r in range(POOL_BUF)]
        ext += [u_ref[g, pl.ds(tok, bs, stride=t), :] for tok in range(t)]
        for r in range(POOL_BUF):
            pool_ref[r, :, sl] = ext[t + r]
        sums = ext
        sh = 1
        while sh < w:
            first = 2 * sh - 1
            sums = [None] * first + [sums[r] + sums[r - sh] for r in range(first, len(sums))]
            sh *= 2
        for tok in range(t):
            cnt = float(min(start_pos + tok + 1, w))
            pooled_ref[g, pl.ds(tok, bs, stride=t), :] = sums[POOL_BUF + tok] / cnt - ext[POOL_BUF + tok]
        mixed.append(_dot(pooled_ref[g].astype(BF16), wmap_ref[g]))
    gp = proj(C_GP, C_Q)
    pool_act = jnp.concatenate(mixed, axis=1) * pscale_ref[...] * _silu(gp)

    q = proj(C_Q, C_K) * (GLA_DK ** -0.5)
    k = proj(C_K, C_V)
    vb = proj(C_V, C_GG).astype(BF16)
    gg = proj(C_GG, C_GL)
    la = _log_sigmoid(proj(C_GL, C_END) + ba_ref[...]) * (1.0 / GLA_TAU)

    rr = lax.broadcasted_iota(jnp.int32, (rows, rows), 0)
    cl = lax.broadcasted_iota(jnp.int32, (rows, rows), 1)
    same = (rr // t) == (cl // t)
    ltri = jnp.where(same & (cl <= rr), 1.0, 0.0).astype(BF16)
    lall = jnp.where(same, 1.0, 0.0).astype(BF16)
    hi, lo = _split_bf16(la)
    b = _dot(ltri, hi) + _dot(ltri, lo)
    b_end = _dot(lall, hi) + _dot(lall, lo)
    qs = (q * jnp.exp(b)).astype(BF16)
    kd = (k * jnp.exp(b_end - b)).astype(BF16)

    assert t & (t - 1) == 0, "tokens per sequence must be a power of two"
    blk = _head_block_mask(rows, GLA_DK, (GLA_HEADS * rows, GLA_KEY_WIDTH))
    a_all = _decayed_scores(q, k, la, b, _score_masks(rows, t.bit_length() - 1), blk)

    s_old = s_in_ref[...]
    s_bf = s_old.astype(BF16)
    qs3 = qs.reshape(bs, t, GLA_KEY_WIDTH)
    kd3 = kd.reshape(bs, t, GLA_KEY_WIDTH)
    v3 = vb.reshape(bs, t, GLA_WIDTH)
    hi3 = hi.reshape(bs, t, GLA_KEY_WIDTH)
    lo3 = lo.reshape(bs, t, GLA_KEY_WIDTH)
    ones3 = jnp.ones((bs, t, GLA_DV), BF16)
    bdims_nn = (((2,), (1,)), ((0,), (0,)))
    bdims_tn = (((1,), (1,)), ((0,), (0,)))
    dsum = (lax.dot_general(hi3, ones3, bdims_tn, preferred_element_type=F32)
            + lax.dot_general(lo3, ones3, bdims_tn, preferred_element_type=F32))
    outs, upds = [], []
    for p in range(GLA_HEADS // 2):
        h0, h1 = 2 * p, 2 * p + 1
        intra = _dot(a_all[:, p * 2 * rows:(p + 1) * 2 * rows],
                     _pair_block_diag(vb[:, h0 * GLA_DV:(h0 + 1) * GLA_DV], vb[:, h1 * GLA_DV:(h1 + 1) * GLA_DV]))
        s0 = s_bf[:, h0 * GLA_DK:(h0 + 1) * GLA_DK, :]
        s1 = s_bf[:, h1 * GLA_DK:(h1 + 1) * GLA_DK, :]
        z = jnp.zeros_like(s0)
        s_pair = jnp.concatenate(
            [jnp.concatenate([s0, z], axis=2), jnp.concatenate([z, s1], axis=2)], axis=1)
        inter = lax.dot_general(qs3[:, :, p * 2 * GLA_DK:(p + 1) * 2 * GLA_DK], s_pair, bdims_nn,
                                preferred_element_type=F32)
        outs.append(intra + inter.reshape(rows, 2 * GLA_DV))
        upd = lax.dot_general(kd3[:, :, p * 2 * GLA_DK:(p + 1) * 2 * GLA_DK],
                              v3[:, :, p * 2 * GLA_DV:(p + 1) * 2 * GLA_DV], bdims_tn,
                              preferred_element_type=F32)
        upds.append(upd[:, 0:GLA_DK, 0:GLA_DV])
        upds.append(upd[:, GLA_DK:2 * GLA_DK, GLA_DV:2 * GLA_DV])
    o = jnp.concatenate(outs, axis=1)
    s_out_ref[...] = s_old * jnp.exp(dsum) + jnp.concatenate(upds, axis=1)

    y = _epilogue(x, pool_act, o, gg, ng_ref, wout_ref, lng_ref, lnb_ref)
    y_ref[...] = y.reshape(bs, t, D_MODEL)


def _sample_call(x, pool_in, s_in, win, wmap, pscale, ba, ng, wout, lng, lnb, start_pos):
    nb, t, _ = x.shape
    bs = SAMPLE_SEQS
    const2 = lambda i: (0, 0)
    seq3 = lambda i: (i, 0, 0)
    hist3 = lambda i: (0, i, 0)
    return pl.pallas_call(
        functools.partial(_sample_body, bs=bs, t=t, start_pos=start_pos),
        grid=(nb // bs,),
        in_specs=[
            pl.BlockSpec((bs, t, D_MODEL), seq3),
            pl.BlockSpec((POOL_BUF, bs, POOL_WIDTH), hist3),
            pl.BlockSpec((bs, GLA_KEY_WIDTH, GLA_DV), seq3),
            pl.BlockSpec(win.shape, const2),
            pl.BlockSpec(wmap.shape, lambda i: (0, 0, 0)),
            pl.BlockSpec(pscale.shape, const2),
            pl.BlockSpec(ba.shape, const2),
            pl.BlockSpec(ng.shape, const2),
            pl.BlockSpec(wout.shape, const2),
            pl.BlockSpec(lng.shape, const2),
            pl.BlockSpec(lnb.shape, const2),
        ],
        out_specs=[
            pl.BlockSpec((bs, t, D_MODEL), seq3),
            pl.BlockSpec((POOL_BUF, bs, POOL_WIDTH), hist3),
            pl.BlockSpec((bs, GLA_KEY_WIDTH, GLA_DV), seq3),
        ],
        out_shape=[
            jax.ShapeDtypeStruct((nb, t, D_MODEL), F32),
            jax.ShapeDtypeStruct((POOL_BUF, nb, POOL_WIDTH), F32),
            jax.ShapeDtypeStruct((nb, GLA_KEY_WIDTH, GLA_DV), F32),
        ],
        scratch_shapes=[pltpu.VMEM((len(POOL_WINDOWS), bs * t, POOL_GROUP_DIM), F32),
                        pltpu.VMEM((len(POOL_WINDOWS), bs * t, POOL_GROUP_DIM), F32)],
        compiler_params=pltpu.CompilerParams(
            dimension_semantics=("arbitrary",), vmem_limit_bytes=VMEM_LIMIT),
        name="sample_layer",
    )(x, pool_in, s_in, win, wmap, pscale, ba, ng, wout, lng, lnb)


def _prep_body(wt_ref, wfat_ref, wa2t_ref, out_ref, *, n_plain):
    i = pl.program_id(0)

    @pl.when(i < n_plain)
    def _():
        out_ref[...] = jnp.transpose(wt_ref[...]).astype(BF16)

    @pl.when(i == n_plain)
    def _():
        wfat = wfat_ref[...]
        wa2t = wa2t_ref[...]
        acc = wa2t[:, 0:1] * wfat[0:1, :]
        for r in range(1, GLA_RANK):
            acc = acc + wa2t[:, r:r + 1] * wfat[r:r + 1, :]
        out_ref[...] = jnp.transpose(acc).astype(BF16)


def _prep_input_weights(wt, wa2t):
    n_plain = C_GL // MXU_COLS
    return pl.pallas_call(
        functools.partial(_prep_body, n_plain=n_plain),
        grid=(n_plain + 1,),
        in_specs=[
            pl.BlockSpec((MXU_COLS, D_MODEL), lambda i: (jnp.minimum(i, n_plain - 1), 0)),
            pl.BlockSpec((GLA_RANK, D_MODEL), lambda i: (0, 0)),
            pl.BlockSpec((GLA_KEY_WIDTH, GLA_RANK), lambda i: (0, 0)),
        ],
        out_specs=pl.BlockSpec((D_MODEL, MXU_COLS), lambda i: (0, i)),
        out_shape=jax.ShapeDtypeStruct((D_MODEL, C_END), BF16),
        compiler_params=pltpu.CompilerParams(dimension_semantics=("arbitrary",)),
        name="prep_input_weights",
    )(wt, wt[C_GL:], wa2t)


def kernel(x_prompt, x_sample, state_pool, state_gla, w_in, w_pool_map, pool_scale, w_gla_a2, b_gla_a,
           gla_norm_g, w_out, ln_g, ln_b):
    assert w_in.shape[0] == DEPTH
    win = _prep_input_weights(jnp.transpose(w_in[0]), jnp.transpose(w_gla_a2[0]))
    wmap = w_pool_map[0].astype(BF16)
    wout = w_out[0].astype(BF16)
    pscale = pool_scale[0].reshape(1, POOL_WIDTH)
    ba = b_gla_a[0].reshape(1, GLA_KEY_WIDTH)
    ng = gla_norm_g[0].reshape(1, GLA_WIDTH)
    lng = ln_g[0].reshape(1, D_MODEL)
    lnb = ln_b[0].reshape(1, D_MODEL)
    params = (win, wmap, pscale, ba, ng, wout, lng, lnb)

    y_p, pool_p, gla_p = _prompt_call(x_prompt, *params)
    nsamp = x_sample.shape[0]
    y_s, pool_s, gla_s = _sample_call(
        x_sample, jnp.transpose(state_pool[0], (1, 0, 2)),
        state_gla[0].reshape(nsamp, GLA_KEY_WIDTH, GLA_DV), *params, start_pos=PAST_LEN)

    nb = x_prompt.shape[0]
    return (y_p, y_s, pool_p[None],
            gla_p.reshape(1, nb, GLA_HEADS, GLA_DK, GLA_DV),
            jnp.transpose(pool_s, (1, 0, 2))[None],
            gla_s.reshape(1, nsamp, GLA_HEADS, GLA_DK, GLA_DV))
```

```python
import functools
import itertools

import jax
import jax.numpy as jnp
from jax import lax
from jax.experimental import pallas as pl
from jax.experimental.pallas import tpu as pltpu

F32 = jnp.float32
BF16 = jnp.bfloat16

D_MODEL = 1024
POOL_WIDTH = 512
POOL_WINDOWS = (2, 4, 8, 16)
POOL_GROUP_DIM = 128
POOL_BUF = 15
HIST_ROWS = 16
GLA_HEADS = 4
GLA_DK = 64
GLA_DV = 128
GLA_KEY_WIDTH = GLA_HEADS * GLA_DK
GLA_WIDTH = GLA_HEADS * GLA_DV
GLA_RANK = 16
GLA_TAU = 16.0
LN_EPS = 1e-5
RMS_EPS = 1e-6
DEPTH = 1
DEEPNORM_ALPHA = (2.0 * DEPTH) ** 0.25
PAST_LEN = 16384

C_U, C_GP, C_Q, C_K, C_V, C_GG, C_GL, C_END = 0, 512, 1024, 1280, 1536, 2048, 2560, 2816

MXU_COLS = 256
PROMPT_TILE = 256
PITCH_PAD = 128
PROMPT_SUBTILE = 256
SUBTILE_TAIL_UNITS = 0
PUMP_ROWS = 256
PUMP_PLAN = {"pool": (0, 1, 0, 1), "cumsum": 0, "scores": (1, 1, 1, 1), "level": (0, 0, 0, 1, 0, 0),
             "state": (0, 0, 0, 0), "epilogue": 1, "layernorm": (1, 1, 1, 0)}
SAFE_LOG_DECAY = 60.0
GLA_CHUNK = 64
SAMPLE_SEQS = 16
VMEM_LIMIT = 56 * 1024 * 1024


def _dot(a, b):
    return jnp.dot(a, b, preferred_element_type=F32)


def _dot_nt(a, b):
    return lax.dot_general(a, b, (((1,), (1,)), ((), ())), preferred_element_type=F32)


def _dot_tn(a, b):
    return lax.dot_general(a, b, (((0,), (0,)), ((), ())), preferred_element_type=F32)


def _silu(x):
    return x * (1.0 / (1.0 + jnp.exp(-x)))


def _log_sigmoid(x):
    return jnp.minimum(x, 0.0) - jnp.log1p(jnp.exp(-jnp.abs(x)))


def _split_bf16(x):
    hi = x.astype(BF16)
    lo = (x - hi.astype(F32)).astype(BF16)
    return hi, lo


def _epilogue(x, pool_act, o, gg, ng_ref, wout_ref, lng_ref, lnb_ref, pump=lambda n=None: None, tail=None,
              row_blocks=1):
    pump(PUMP_PLAN["epilogue"])
    normed = []
    for h in range(GLA_HEADS):
        oh = o[:, h * GLA_DV:(h + 1) * GLA_DV]
        ms = jnp.mean(oh * oh, axis=-1, keepdims=True)
        normed.append(oh * lax.rsqrt(ms + RMS_EPS))
    gla_act = jnp.concatenate(normed, axis=1) * ng_ref[...] * _silu(gg)
    mix = jnp.concatenate([pool_act, gla_act], axis=1).astype(BF16)
    y = _dot(mix, wout_ref[...])
    n = x.shape[0] // row_blocks
    outs = []
    for r in range(row_blocks):
        rows = slice(r * n, (r + 1) * n)
        pump(PUMP_PLAN["layernorm"][r])
        h = DEEPNORM_ALPHA * x[rows] + y[rows]
        mu = jnp.mean(h, axis=-1, keepdims=True)
        d = h - mu
        var = jnp.mean(d * d, axis=-1, keepdims=True)
        outs.append(d * lax.rsqrt(var + LN_EPS) * lng_ref[...] + lnb_ref[...])
    pump(tail)
    return outs[0] if row_blocks == 1 else jnp.concatenate(outs, axis=0)


def _head_block_mask(rows_per_head, cols_per_head, shape):
    r = lax.broadcasted_iota(jnp.int32, shape, 0) // rows_per_head
    c = lax.broadcasted_iota(jnp.int32, shape, 1) // cols_per_head
    return r == c


def _score_masks(rows, n_levels):
    shape = (rows, GLA_HEADS * rows)
    t_idx = lax.broadcasted_iota(jnp.int32, shape, 0)
    s_idx = lax.broadcasted_iota(jnp.int32, shape, 1) % rows
    masks = [t_idx == s_idx]
    for lvl in range(n_levels):
        half = 1 << lvl
        blk = 2 * half
        masks.append((t_idx // blk == s_idx // blk) & (t_idx % blk >= half) & (s_idx % blk < half))
    return masks


def _decayed_scores(q, k, la, b, masks, head_blk, pump=lambda n=None: None):
    rows = q.shape[0]
    r_idx = lax.broadcasted_iota(jnp.int32, q.shape, 0)

    def scores(qt, kt):
        kh = jnp.where(head_blk, jnp.concatenate([kt.astype(BF16)] * GLA_HEADS, axis=0), jnp.zeros((), BF16))
        return _dot_nt(qt.astype(BF16), kh)

    acc = jnp.where(masks[0], scores(q, k), 0.0)
    for lvl in range(len(masks) - 1):
        half = 1 << lvl
        blk = 2 * half
        if lvl == 0:
            neg = jnp.where(r_idx % 2 == 1, la, 0.0)
        elif lvl == 1:
            r4 = r_idx % 4
            nxt = pltpu.roll(la, rows - 1, 0)
            prv = pltpu.roll(la, 1, 0)
            neg = jnp.where(r4 == 0, nxt, jnp.where(r4 == 1, 0.0, jnp.where(r4 == 2, la, la + prv)))
        else:
            ref = jnp.concatenate(
                [jnp.broadcast_to(b[j * blk + half - 1:j * blk + half, :], (blk, b.shape[1]))
                 for j in range(rows // blk)], axis=0)
            d = b - ref
            neg = jnp.minimum(d, -d)
        w = jnp.exp(neg)
        acc = jnp.where(masks[lvl + 1], scores(q * w, k * w), acc)
        pump(PUMP_PLAN["level"][lvl])
    return acc.astype(BF16)


def _pair_block_diag(a0, a1):
    z = jnp.zeros_like(a0)
    return jnp.concatenate(
        [jnp.concatenate([a0, z], axis=1), jnp.concatenate([z, a1], axis=1)], axis=0)


def _make_pump(units, xb_ref, win_ref, proj_ref):
    units = iter(units)

    def pump(n=None):
        for r, lo in (units if n is None else itertools.islice(units, n)):
            proj_ref[r:r + PUMP_ROWS, lo:lo + MXU_COLS] = _dot(
                xb_ref[r:r + PUMP_ROWS, 0:D_MODEL], win_ref[:, lo:lo + MXU_COLS])

    return pump


def _naive_scores(q, k, b, causal, head_blk):
    ks = (k * jnp.exp(-b)).astype(BF16)
    kh = jnp.where(head_blk, jnp.concatenate([ks] * GLA_HEADS, axis=0), jnp.zeros((), BF16))
    return jnp.where(causal, _dot_nt((q * jnp.exp(b)).astype(BF16), kh), 0.0).astype(BF16)


def _prompt_stage_b(pos0, proj_ref, xkeep_ref, la_ref, wmap_ref, pscale_ref, ba_ref, ng_ref, wout_ref, lng_ref,
                    lnb_ref, y_ref, ext_ref, s_ref, pump, next_decay, *, tc, robust, tail):
    u = proj_ref[:, C_U:C_GP]
    ext_ref[HIST_ROWS:HIST_ROWS + tc, 0:POOL_WIDTH] = u
    pos16 = pos0 + lax.broadcasted_iota(jnp.int32, (HIST_ROWS, POOL_GROUP_DIM), 0)
    mixed = []
    for g, w in enumerate(POOL_WINDOWS):
        sl = slice(g * POOL_GROUP_DIM, (g + 1) * POOL_GROUP_DIM)
        e = ext_ref[:, sl]
        s = e
        sh = 1
        while sh < w:
            s = s + pltpu.roll(s, sh, 0)
            sh *= 2
        wsum = s[HIST_ROWS:]
        self_rows = e[HIST_ROWS:]
        cnt = jnp.minimum(pos16 + 1, w).astype(F32)
        first = wsum[:HIST_ROWS] / cnt - self_rows[:HIST_ROWS]
        rest = wsum[HIST_ROWS:] * (1.0 / w) - self_rows[HIST_ROWS:]
        pooled = jnp.concatenate([first, rest], axis=0)
        mixed.append(_dot(pooled.astype(BF16), wmap_ref[g]))
        pump(PUMP_PLAN["pool"][g])
    pool_act = jnp.concatenate(mixed, axis=1) * pscale_ref[...] * _silu(proj_ref[:, C_GP:C_Q])
    next_decay()

    cc = GLA_CHUNK
    n_chunks = tc // cc
    rr = lax.broadcasted_iota(jnp.int32, (tc, tc), 0)
    cl = lax.broadcasted_iota(jnp.int32, (tc, tc), 1)
    ltri = jnp.where((rr // cc == cl // cc) & (cl <= rr), 1.0, 0.0).astype(BF16)
    masks = _score_masks(cc, cc.bit_length() - 1 if robust else 0)
    causal = (lax.broadcasted_iota(jnp.int32, (cc, GLA_HEADS * cc), 1) % cc
              <= lax.broadcasted_iota(jnp.int32, (cc, GLA_HEADS * cc), 0))
    blk = _head_block_mask(cc, GLA_DK, (GLA_HEADS * cc, GLA_KEY_WIDTH))

    hi, lo = _split_bf16(la_ref[:, 0:GLA_KEY_WIDTH])
    b_all = _dot(ltri, hi) + _dot(ltri, lo)
    pump(PUMP_PLAN["cumsum"])
    lhs_list, v_list, upd_list, dcol_list = [], [], [], []
    for c in range(n_chunks):
        rs = slice(c * cc, (c + 1) * cc)
        b = b_all[rs]
        b_end = b[cc - 1:cc, :]
        q_c = proj_ref[rs, C_Q:C_K] * (GLA_DK ** -0.5)
        k_c = proj_ref[rs, C_K:C_V]
        qs = (q_c * jnp.exp(b)).astype(BF16)
        kd = (k_c * jnp.exp(b_end - b)).astype(BF16)
        if robust:
            a_all = _decayed_scores(q_c, k_c, la_ref[rs, 0:GLA_KEY_WIDTH], b, masks, blk, pump)
        else:
            a_all = _naive_scores(q_c, k_c, b, causal, blk)
            pump(PUMP_PLAN["scores"][c])
        vc = proj_ref[rs, C_V:C_GG].astype(BF16)
        upds = []
        for p in range(GLA_HEADS // 2):
            upd = _dot_tn(kd[:, p * 2 * GLA_DK:(p + 1) * 2 * GLA_DK], vc[:, p * 2 * GLA_DV:(p + 1) * 2 * GLA_DV])
            upds.append(upd[0:GLA_DK, 0:GLA_DV])
            upds.append(upd[GLA_DK:2 * GLA_DK, GLA_DV:2 * GLA_DV])
        lhs_list.append([jnp.concatenate([a_all[:, p * 2 * cc:(p + 1) * 2 * cc],
                                          qs[:, p * 2 * GLA_DK:(p + 1) * 2 * GLA_DK]], axis=1)
                         for p in range(GLA_HEADS // 2)])
        v_list.append(vc)
        upd_list.append(jnp.concatenate(upds, axis=0))
        dcol_list.append(jnp.transpose(jnp.broadcast_to(jnp.exp(b_end), (GLA_DV, GLA_KEY_WIDTH))))

    o_chunks = []
    state = s_ref[...]
    for c in range(n_chunks):
        s_bf = state.astype(BF16)
        vc = v_list[c]
        outs = []
        for p in range(GLA_HEADS // 2):
            h0, h1 = 2 * p, 2 * p + 1
            rhs = jnp.concatenate([
                _pair_block_diag(vc[:, h0 * GLA_DV:(h0 + 1) * GLA_DV], vc[:, h1 * GLA_DV:(h1 + 1) * GLA_DV]),
                _pair_block_diag(s_bf[h0 * GLA_DK:(h0 + 1) * GLA_DK], s_bf[h1 * GLA_DK:(h1 + 1) * GLA_DK]),
            ], axis=0)
            outs.append(_dot(lhs_list[c][p], rhs))
        o_chunks.append(jnp.concatenate(outs, axis=1))
        state = state * dcol_list[c] + upd_list[c]
        pump(PUMP_PLAN["state"][c])
    s_ref[...] = state
    o = jnp.concatenate(o_chunks, axis=0)

    y_ref[...] = _epilogue(xkeep_ref[:, 0:D_MODEL], pool_act, o, proj_ref[:, C_GG:C_GL], ng_ref, wout_ref, lng_ref, lnb_ref,
                           pump, tail, row_blocks=4)


def _prompt_body(x_ref, win_ref, wmap_ref, pscale_ref, ba_ref, ng_ref, wout_ref, lng_ref, lnb_ref,
                 y_ref, pool_ref, s_out_ref,
                 proj0_ref, proj1_ref, xkeep0_ref, xkeep1_ref, la0_ref, la1_ref, xb_ref, ext_ref, s_ref, steep_ref,
                 *, tc, sub, nj):
    i = pl.program_id(0)
    jb = (i + nj - 1) % nj

    @pl.when(i == 0)
    def _():
        proj1_ref[...] = jnp.zeros_like(proj1_ref)
        xkeep1_ref[...] = jnp.zeros_like(xkeep1_ref)
        la1_ref[...] = jnp.zeros_like(la1_ref)
        steep_ref[0] = 0

    @pl.when((i == 0) | (jb == 0))
    def _():
        ext_ref[0:HIST_ROWS, 0:POOL_WIDTH] = jnp.zeros((HIST_ROWS, POOL_WIDTH), F32)
        s_ref[...] = jnp.zeros_like(s_ref)

    units = [(r, lo) for lo in (C_GL, *range(0, C_GL, MXU_COLS)) for r in range(0, tc, PUMP_ROWS)]

    def step(proj_a, xkeep_a, la_a, proj_b, xkeep_b, la_b, robust):
        x = x_ref[0]
        xkeep_a[:, 0:D_MODEL] = x
        xb_ref[:, 0:D_MODEL] = x.astype(BF16)
        pump = _make_pump(units, xb_ref, win_ref, proj_a)
        pump(tc // PUMP_ROWS)

        def next_decay():
            la = _log_sigmoid(proj_a[:, C_GL:C_END] + ba_ref[...]) * (1.0 / GLA_TAU)
            la_a[:, 0:GLA_KEY_WIDTH] = la
            chunk_decay = [jnp.sum(la[c * GLA_CHUNK:(c + 1) * GLA_CHUNK], axis=0, keepdims=True)
                           for c in range(tc // GLA_CHUNK)]
            lowest = jnp.min(functools.reduce(jnp.minimum, chunk_decay))
            steep_ref[0] = jnp.where(lowest < -SAFE_LOG_DECAY, 1, 0).astype(jnp.int32)

        n_sub = tc // sub
        for k in range(n_sub):
            rows = slice(k * sub, (k + 1) * sub)
            last = k == n_sub - 1
            _prompt_stage_b((jb * n_sub + k) * sub, proj_b.at[rows, :], xkeep_b.at[rows, :], la_b.at[rows, :],
                            wmap_ref, pscale_ref, ba_ref, ng_ref, wout_ref, lng_ref, lnb_ref,
                            y_ref.at[0, rows, :], ext_ref, s_ref, pump, next_decay if k == 0 else (lambda: None),
                            tc=sub, robust=robust, tail=None if last else SUBTILE_TAIL_UNITS)
            if not last:
                ext_ref[0:HIST_ROWS, 0:POOL_WIDTH] = ext_ref[sub:sub + HIST_ROWS, 0:POOL_WIDTH]

    set0 = (proj0_ref, xkeep0_ref, la0_ref)
    set1 = (proj1_ref, xkeep1_ref, la1_ref)
    steep = steep_ref[0] != 0
    for parity, (set_a, set_b) in enumerate(((set0, set1), (set1, set0))):
        for robust in (False, True):
            @pl.when((i % 2 == parity) & (steep == robust))
            def _(set_a=set_a, set_b=set_b, robust=robust):
                step(*set_a, *set_b, robust)

    @pl.when((i > 0) & (jb == nj - 1))
    def _():
        pool_ref[0] = ext_ref[sub + 1:sub + HIST_ROWS, 0:POOL_WIDTH]
        s_out_ref[0] = s_ref[...]

    ext_ref[0:HIST_ROWS, 0:POOL_WIDTH] = ext_ref[sub:sub + HIST_ROWS, 0:POOL_WIDTH]


def _prompt_call(x, win, wmap, pscale, ba, ng, wout, lng, lnb):
    nb, t, _ = x.shape
    tc = PROMPT_TILE
    nj = t // tc
    ntiles = nb * nj
    const2 = lambda i: (0, 0)

    def tile_a(i):
        ia = jnp.minimum(i, ntiles - 1)
        return (ia // nj, ia % nj, 0)

    def tile_b(i):
        ib = jnp.maximum(i - 1, 0)
        return (ib // nj, ib % nj, 0)

    def seq_b(i):
        return (jnp.maximum(i - 1, 0) // nj, 0, 0)

    return pl.pallas_call(
        functools.partial(_prompt_body, tc=tc, sub=PROMPT_SUBTILE, nj=nj),
        grid=(ntiles + 1,),
        in_specs=[
            pl.BlockSpec((1, tc, D_MODEL), tile_a),
            pl.BlockSpec(win.shape, const2),
            pl.BlockSpec(wmap.shape, lambda i: (0, 0, 0)),
            pl.BlockSpec(pscale.shape, const2),
            pl.BlockSpec(ba.shape, const2),
            pl.BlockSpec(ng.shape, const2),
            pl.BlockSpec(wout.shape, const2),
            pl.BlockSpec(lng.shape, const2),
            pl.BlockSpec(lnb.shape, const2),
        ],
        out_specs=[
            pl.BlockSpec((1, tc, D_MODEL), tile_b),
            pl.BlockSpec((1, POOL_BUF, POOL_WIDTH), seq_b),
            pl.BlockSpec((1, GLA_KEY_WIDTH, GLA_DV), seq_b),
        ],
        out_shape=[
            jax.ShapeDtypeStruct((nb, t, D_MODEL), F32),
            jax.ShapeDtypeStruct((nb, POOL_BUF, POOL_WIDTH), F32),
            jax.ShapeDtypeStruct((nb, GLA_KEY_WIDTH, GLA_DV), F32),
        ],
        scratch_shapes=[
            pltpu.VMEM((tc, C_END + PITCH_PAD), F32),
            pltpu.VMEM((tc, C_END + PITCH_PAD), F32),
            pltpu.VMEM((tc, D_MODEL + PITCH_PAD), F32),
            pltpu.VMEM((tc, D_MODEL + PITCH_PAD), F32),
            pltpu.VMEM((tc, GLA_KEY_WIDTH + PITCH_PAD), F32),
            pltpu.VMEM((tc, GLA_KEY_WIDTH + PITCH_PAD), F32),
            pltpu.VMEM((tc, D_MODEL + PITCH_PAD), BF16),
            pltpu.VMEM((HIST_ROWS + PROMPT_SUBTILE, POOL_WIDTH + PITCH_PAD), F32),
            pltpu.VMEM((GLA_KEY_WIDTH, GLA_DV), F32),
            pltpu.SMEM((1,), jnp.int32),
        ],
        compiler_params=pltpu.CompilerParams(
            dimension_semantics=("arbitrary",), vmem_limit_bytes=VMEM_LIMIT),
        name="prompt_layer",
    )(x, win, wmap, pscale, ba, ng, wout, lng, lnb)


def _sample_body(x_ref, pool_in_ref, s_in_ref, win_ref, wmap_ref, pscale_ref, ba_ref, ng_ref, wout_ref,
                 lng_ref, lnb_ref, y_ref, pool_ref, s_out_ref, u_ref, pooled_ref, *, bs, t, start_pos):
    rows = bs * t
    x = x_ref[...].reshape(rows, D_MODEL)
    xb = x.astype(BF16)

    def proj(lo, hi):
        return _dot(xb, win_ref[:, lo:hi])

    u = proj(C_U, C_GP)
    mixed = []
    for g, w in enumerate(POOL_WINDOWS):
        sl = slice(g * POOL_GROUP_DIM, (g + 1) * POOL_GROUP_DIM)
        u_ref[g] = u[:, sl]
        ext = [pool_in_ref[r, :, sl] for r in range(POOL_BUF)]
        ext += [u_ref[g, pl.ds(tok, bs, stride=t), :] for tok in range(t)]
        for r in range(POOL_BUF):
            pool_ref[r, :, sl] = ext[t + r]
        sums = ext
        sh = 1
        while sh < w:
            first = 2 * sh - 1
            sums = [None] * first + [sums[r] + sums[r - sh] for r in range(first, len(sums))]
            sh *= 2
        for tok in range(t):
            cnt = float(min(start_pos + tok + 1, w))
            pooled_ref[g, pl.ds(tok, bs, stride=t), :] = sums[POOL_BUF + tok] / cnt - ext[POOL_BUF + tok]
        mixed.append(_dot(pooled_ref[g].astype(BF16), wmap_ref[g]))
    gp = proj(C_GP, C_Q)
    pool_act = jnp.concatenate(mixed, axis=1) * pscale_ref[...] * _silu(gp)

    q = proj(C_Q, C_K) * (GLA_DK ** -0.5)
    k = proj(C_K, C_V)
    vb = proj(C_V, C_GG).astype(BF16)
    gg = proj(C_GG, C_GL)
    la = _log_sigmoid(proj(C_GL, C_END) + ba_ref[...]) * (1.0 / GLA_TAU)

    rr = lax.broadcasted_iota(jnp.int32, (rows, rows), 0)
    cl = lax.broadcasted_iota(jnp.int32, (rows, rows), 1)
    same = (rr // t) == (cl // t)
    ltri = jnp.where(same & (cl <= rr), 1.0, 0.0).astype(BF16)
    lall = jnp.where(same, 1.0, 0.0).astype(BF16)
    hi, lo = _split_bf16(la)
    b = _dot(ltri, hi) + _dot(ltri, lo)
    b_end = _dot(lall, hi) + _dot(lall, lo)
    qs = (q * jnp.exp(b)).astype(BF16)
    kd = (k * jnp.exp(b_end - b)).astype(BF16)

    assert t & (t - 1) == 0, "tokens per sequence must be a power of two"
    blk = _head_block_mask(rows, GLA_DK, (GLA_HEADS * rows, GLA_KEY_WIDTH))
    a_all = _decayed_scores(q, k, la, b, _score_masks(rows, t.bit_length() - 1), blk)

    s_old = s_in_ref[...]
    s_bf = s_old.astype(BF16)
    qs3 = qs.reshape(bs, t, GLA_KEY_WIDTH)
    kd3 = kd.reshape(bs, t, GLA_KEY_WIDTH)
    v3 = vb.reshape(bs, t, GLA_WIDTH)
    hi3 = hi.reshape(bs, t, GLA_KEY_WIDTH)
    lo3 = lo.reshape(bs, t, GLA_KEY_WIDTH)
    ones3 = jnp.ones((bs, t, GLA_DV), BF16)
    bdims_nn = (((2,), (1,)), ((0,), (0,)))
    bdims_tn = (((1,), (1,)), ((0,), (0,)))
    dsum = (lax.dot_general(hi3, ones3, bdims_tn, preferred_element_type=F32)
            + lax.dot_general(lo3, ones3, bdims_tn, preferred_element_type=F32))
    outs, upds = [], []
    for p in range(GLA_HEADS // 2):
        h0, h1 = 2 * p, 2 * p + 1
        intra = _dot(a_all[:, p * 2 * rows:(p + 1) * 2 * rows],
                     _pair_block_diag(vb[:, h0 * GLA_DV:(h0 + 1) * GLA_DV], vb[:, h1 * GLA_DV:(h1 + 1) * GLA_DV]))
        s0 = s_bf[:, h0 * GLA_DK:(h0 + 1) * GLA_DK, :]
        s1 = s_bf[:, h1 * GLA_DK:(h1 + 1) * GLA_DK, :]
        z = jnp.zeros_like(s0)
        s_pair = jnp.concatenate(
            [jnp.concatenate([s0, z], axis=2), jnp.concatenate([z, s1], axis=2)], axis=1)
        inter = lax.dot_general(qs3[:, :, p * 2 * GLA_DK:(p + 1) * 2 * GLA_DK], s_pair, bdims_nn,
                                preferred_element_type=F32)
        outs.append(intra + inter.reshape(rows, 2 * GLA_DV))
        upd = lax.dot_general(kd3[:, :, p * 2 * GLA_DK:(p + 1) * 2 * GLA_DK],
                              v3[:, :, p * 2 * GLA_DV:(p + 1) * 2 * GLA_DV], bdims_tn,
                              preferred_element_type=F32)
        upds.append(upd[:, 0:GLA_DK, 0:GLA_DV])
        upds.append(upd[:, GLA_DK:2 * GLA_DK, GLA_DV:2 * GLA_DV])
    o = jnp.concatenate(outs, axis=1)
    s_out_ref[...] = s_old * jnp.exp(dsum) + jnp.concatenate(upds, axis=1)

    y = _epilogue(x, pool_act, o, gg, ng_ref, wout_ref, lng_ref, lnb_ref)
    y_ref[...] = y.reshape(bs, t, D_MODEL)


def _sample_call(x, pool_in, s_in, win, wmap, pscale, ba, ng, wout, lng, lnb, start_pos):
    nb, t, _ = x.shape
    bs = SAMPLE_SEQS
    const2 = lambda i: (0, 0)
    seq3 = lambda i: (i, 0, 0)
    hist3 = lambda i: (0, i, 0)
    return pl.pallas_call(
        functools.partial(_sample_body, bs=bs, t=t, start_pos=start_pos),
        grid=(nb // bs,),
        in_specs=[
            pl.BlockSpec((bs, t, D_MODEL), seq3),
            pl.BlockSpec((POOL_BUF, bs, POOL_WIDTH), hist3),
            pl.BlockSpec((bs, GLA_KEY_WIDTH, GLA_DV), seq3),
            pl.BlockSpec(win.shape, const2),
            pl.BlockSpec(wmap.shape, lambda i: (0, 0, 0)),
            pl.BlockSpec(pscale.shape, const2),
            pl.BlockSpec(ba.shape, const2),
            pl.BlockSpec(ng.shape, const2),
            pl.BlockSpec(wout.shape, const2),
            pl.BlockSpec(lng.shape, const2),
            pl.BlockSpec(lnb.shape, const2),
        ],
        out_specs=[
            pl.BlockSpec((bs, t, D_MODEL), seq3),
            pl.BlockSpec((POOL_BUF, bs, POOL_WIDTH), hist3),
            pl.BlockSpec((bs, GLA_KEY_WIDTH, GLA_DV), seq3),
        ],
        out_shape=[
            jax.ShapeDtypeStruct((nb, t, D_MODEL), F32),
            jax.ShapeDtypeStruct((POOL_BUF, nb, POOL_WIDTH), F32),
            jax.ShapeDtypeStruct((nb, GLA_KEY_WIDTH, GLA_DV), F32),
        ],
        scratch_shapes=[pltpu.VMEM((len(POOL_WINDOWS), bs * t, POOL_GROUP_DIM), F32),
                        pltpu.VMEM((len(POOL_WINDOWS), bs * t, POOL_GROUP_DIM), F32)],
        compiler_params=pltpu.CompilerParams(
            dimension_semantics=("arbitrary",), vmem_limit_bytes=VMEM_LIMIT),
        name="sample_layer",
    )(x, pool_in, s_in, win, wmap, pscale, ba, ng, wout, lng, lnb)


def _prep_body(wt_ref, wfat_ref, wa2t_ref, out_ref, *, n_plain):
    i = pl.program_id(0)

    @pl.when(i < n_plain)
    def _():
        out_ref[...] = jnp.transpose(wt_ref[...]).astype(BF16)

    @pl.when(i == n_plain)
    def _():
        wfat = wfat_ref[...]
        wa2t = wa2t_ref[...]
        acc = wa2t[:, 0:1] * wfat[0:1, :]
        for r in range(1, GLA_RANK):
            acc = acc + wa2t[:, r:r + 1] * wfat[r:r + 1, :]
        out_ref[...] = jnp.transpose(acc).astype(BF16)


def _prep_input_weights(wt, wa2t):
    n_plain = C_GL // MXU_COLS
    return pl.pallas_call(
        functools.partial(_prep_body, n_plain=n_plain),
        grid=(n_plain + 1,),
        in_specs=[
            pl.BlockSpec((MXU_COLS, D_MODEL), lambda i: (jnp.minimum(i, n_plain - 1), 0)),
            pl.BlockSpec((GLA_RANK, D_MODEL), lambda i: (0, 0)),
            pl.BlockSpec((GLA_KEY_WIDTH, GLA_RANK), lambda i: (0, 0)),
        ],
        out_specs=pl.BlockSpec((D_MODEL, MXU_COLS), lambda i: (0, i)),
        out_shape=jax.ShapeDtypeStruct((D_MODEL, C_END), BF16),
        compiler_params=pltpu.CompilerParams(dimension_semantics=("arbitrary",)),
        name="prep_input_weights",
    )(wt, wt[C_GL:], wa2t)


def kernel(x_prompt, x_sample, state_pool, state_gla, w_in, w_pool_map, pool_scale, w_gla_a2, b_gla_a,
           gla_norm_g, w_out, ln_g, ln_b):
    assert w_in.shape[0] == DEPTH
    win = _prep_input_weights(jnp.transpose(w_in[0]), jnp.transpose(w_gla_a2[0]))
    wmap = w_pool_map[0].astype(BF16)
    wout = w_out[0].astype(BF16)
    pscale = pool_scale[0].reshape(1, POOL_WIDTH)
    ba = b_gla_a[0].reshape(1, GLA_KEY_WIDTH)
    ng = gla_norm_g[0].reshape(1, GLA_WIDTH)
    lng = ln_g[0].reshape(1, D_MODEL)
    lnb = ln_b[0].reshape(1, D_MODEL)
    params = (win, wmap, pscale, ba, ng, wout, lng, lnb)

    y_p, pool_p, gla_p = _prompt_call(x_prompt, *params)
    nsamp = x_sample.shape[0]
    y_s, pool_s, gla_s = _sample_call(
        x_sample, jnp.transpose(state_pool[0], (1, 0, 2)),
        state_gla[0].reshape(nsamp, GLA_KEY_WIDTH, GLA_DV), *params, start_pos=PAST_LEN)

    nb = x_prompt.shape[0]
    return (y_p, y_s, pool_p[None],
            gla_p.reshape(1, nb, GLA_HEADS, GLA_DK, GLA_DV),
            jnp.transpose(pool_s, (1, 0, 2))[None],
            gla_s.reshape(1, nsamp, GLA_HEADS, GLA_DK, GLA_DV))
```

```python
import functools
import itertools

import jax
import jax.numpy as jnp
from jax import lax
from jax.experimental import pallas as pl
from jax.experimental.pallas import tpu as pltpu

F32 = jnp.float32
BF16 = jnp.bfloat16

D_MODEL = 1024
POOL_WIDTH = 512
POOL_WINDOWS = (2, 4, 8, 16)
POOL_GROUP_DIM = 128
POOL_BUF = 15
HIST_ROWS = 16
GLA_HEADS = 4
GLA_DK = 64
GLA_DV = 128
GLA_KEY_WIDTH = GLA_HEADS * GLA_DK
GLA_WIDTH = GLA_HEADS * GLA_DV
GLA_RANK = 16
GLA_TAU = 16.0
LN_EPS = 1e-5
RMS_EPS = 1e-6
DEPTH = 1
DEEPNORM_ALPHA = (2.0 * DEPTH) ** 0.25
PAST_LEN = 16384

C_U, C_GP, C_Q, C_K, C_V, C_GG, C_GL, C_END = 0, 512, 1024, 1280, 1536, 2048, 2560, 2816

MXU_COLS = 256
PROMPT_TILE = 256
PITCH_PAD = 128
PROMPT_SUBTILE = 256
SUBTILE_TAIL_UNITS = 0
PUMP_ROWS = 256
PUMP_PLAN = {"pool": (0, 1, 0, 1), "cumsum": 0, "scores": (1, 1, 1, 1), "level": (0, 0, 0, 1, 0, 0),
             "state": (0, 0, 0, 0), "epilogue": 1, "layernorm": (1, 1, 1, 0)}
SAFE_LOG_DECAY = 60.0
GLA_CHUNK = 64
SAMPLE_SEQS = 16
VMEM_LIMIT = 56 * 1024 * 1024


def _dot(a, b):
    return jnp.dot(a, b, preferred_element_type=F32)


def _dot_nt(a, b):
    return lax.dot_general(a, b, (((1,), (1,)), ((), ())), preferred_element_type=F32)


def _dot_tn(a, b):
    return lax.dot_general(a, b, (((0,), (0,)), ((), ())), preferred_element_type=F32)


def _silu(x):
    return x * (1.0 / (1.0 + jnp.exp(-x)))


def _log_sigmoid(x):
    return jnp.minimum(x, 0.0) - jnp.log1p(jnp.exp(-jnp.abs(x)))


def _split_bf16(x):
    hi = x.astype(BF16)
    lo = (x - hi.astype(F32)).astype(BF16)
    return hi, lo


def _epilogue(x, pool_act, o, gg, ng_ref, wout_ref, lng_ref, lnb_ref, pump=lambda n=None: None, tail=None,
              row_blocks=1):
    pump(PUMP_PLAN["epilogue"])
    normed = []
    for h in range(GLA_HEADS):
        oh = o[:, h * GLA_DV:(h + 1) * GLA_DV]
        ms = jnp.mean(oh * oh, axis=-1, keepdims=True)
        normed.append(oh * lax.rsqrt(ms + RMS_EPS))
    gla_act = jnp.concatenate(normed, axis=1) * ng_ref[...] * _silu(gg)
    mix = jnp.concatenate([pool_act, gla_act], axis=1).astype(BF16)
    y = _dot(mix, wout_ref[:, 0:D_MODEL])
    n = x.shape[0] // row_blocks
    outs = []
    for r in range(row_blocks):
        rows = slice(r * n, (r + 1) * n)
        pump(PUMP_PLAN["layernorm"][r])
        h = DEEPNORM_ALPHA * x[rows] + y[rows]
        mu = jnp.mean(h, axis=-1, keepdims=True)
        d = h - mu
        var = jnp.mean(d * d, axis=-1, keepdims=True)
        outs.append(d * lax.rsqrt(var + LN_EPS) * lng_ref[...] + lnb_ref[...])
    pump(tail)
    return outs[0] if row_blocks == 1 else jnp.concatenate(outs, axis=0)


def _head_block_mask(rows_per_head, cols_per_head, shape):
    r = lax.broadcasted_iota(jnp.int32, shape, 0) // rows_per_head
    c = lax.broadcasted_iota(jnp.int32, shape, 1) // cols_per_head
    return r == c


def _score_masks(rows, n_levels):
    shape = (rows, GLA_HEADS * rows)
    t_idx = lax.broadcasted_iota(jnp.int32, shape, 0)
    s_idx = lax.broadcasted_iota(jnp.int32, shape, 1) % rows
    masks = [t_idx == s_idx]
    for lvl in range(n_levels):
        half = 1 << lvl
        blk = 2 * half
        masks.append((t_idx // blk == s_idx // blk) & (t_idx % blk >= half) & (s_idx % blk < half))
    return masks


def _decayed_scores(q, k, la, b, masks, head_blk, pump=lambda n=None: None):
    rows = q.shape[0]
    r_idx = lax.broadcasted_iota(jnp.int32, q.shape, 0)

    def scores(qt, kt):
        kh = jnp.where(head_blk, jnp.concatenate([kt.astype(BF16)] * GLA_HEADS, axis=0), jnp.zeros((), BF16))
        return _dot_nt(qt.astype(BF16), kh)

    acc = jnp.where(masks[0], scores(q, k), 0.0)
    for lvl in range(len(masks) - 1):
        half = 1 << lvl
        blk = 2 * half
        if lvl == 0:
            neg = jnp.where(r_idx % 2 == 1, la, 0.0)
        elif lvl == 1:
            r4 = r_idx % 4
            nxt = pltpu.roll(la, rows - 1, 0)
            prv = pltpu.roll(la, 1, 0)
            neg = jnp.where(r4 == 0, nxt, jnp.where(r4 == 1, 0.0, jnp.where(r4 == 2, la, la + prv)))
        else:
            ref = jnp.concatenate(
                [jnp.broadcast_to(b[j * blk + half - 1:j * blk + half, :], (blk, b.shape[1]))
                 for j in range(rows // blk)], axis=0)
            d = b - ref
            neg = jnp.minimum(d, -d)
        w = jnp.exp(neg)
        acc = jnp.where(masks[lvl + 1], scores(q * w, k * w), acc)
        pump(PUMP_PLAN["level"][lvl])
    return acc.astype(BF16)


def _pair_block_diag(a0, a1):
    z = jnp.zeros_like(a0)
    return jnp.concatenate(
        [jnp.concatenate([a0, z], axis=1), jnp.concatenate([z, a1], axis=1)], axis=0)


def _make_pump(units, xb_ref, win_ref, proj_ref):
    units = iter(units)

    def pump(n=None):
        for r, lo in (units if n is None else itertools.islice(units, n)):
            proj_ref[r:r + PUMP_ROWS, lo:lo + MXU_COLS] = _dot(
                xb_ref[r:r + PUMP_ROWS, 0:D_MODEL], win_ref[:, lo:lo + MXU_COLS])

    return pump


def _naive_scores(q, k, b, causal, head_blk):
    ks = (k * jnp.exp(-b)).astype(BF16)
    kh = jnp.where(head_blk, jnp.concatenate([ks] * GLA_HEADS, axis=0), jnp.zeros((), BF16))
    return jnp.where(causal, _dot_nt((q * jnp.exp(b)).astype(BF16), kh), 0.0).astype(BF16)


def _prompt_stage_b(pos0, proj_ref, xkeep_ref, la_ref, wmap_ref, pscale_ref, ba_ref, ng_ref, wout_ref, lng_ref,
                    lnb_ref, y_ref, ext_ref, s_ref, pump, next_decay, *, tc, robust, tail):
    u = proj_ref[:, C_U:C_GP]
    ext_ref[HIST_ROWS:HIST_ROWS + tc, 0:POOL_WIDTH] = u
    pos16 = pos0 + lax.broadcasted_iota(jnp.int32, (HIST_ROWS, POOL_GROUP_DIM), 0)
    mixed = []
    for g, w in enumerate(POOL_WINDOWS):
        sl = slice(g * POOL_GROUP_DIM, (g + 1) * POOL_GROUP_DIM)
        e = ext_ref[:, sl]
        s = e
        sh = 1
        while sh < w:
            s = s + pltpu.roll(s, sh, 0)
            sh *= 2
        wsum = s[HIST_ROWS:]
        self_rows = e[HIST_ROWS:]
        cnt = jnp.minimum(pos16 + 1, w).astype(F32)
        first = wsum[:HIST_ROWS] / cnt - self_rows[:HIST_ROWS]
        rest = wsum[HIST_ROWS:] * (1.0 / w) - self_rows[HIST_ROWS:]
        pooled = jnp.concatenate([first, rest], axis=0)
        mixed.append(_dot(pooled.astype(BF16), wmap_ref[g]))
        pump(PUMP_PLAN["pool"][g])
    pool_act = jnp.concatenate(mixed, axis=1) * pscale_ref[...] * _silu(proj_ref[:, C_GP:C_Q])
    next_decay()

    cc = GLA_CHUNK
    n_chunks = tc // cc
    rr = lax.broadcasted_iota(jnp.int32, (tc, tc), 0)
    cl = lax.broadcasted_iota(jnp.int32, (tc, tc), 1)
    ltri = jnp.where((rr // cc == cl // cc) & (cl <= rr), 1.0, 0.0).astype(BF16)
    masks = _score_masks(cc, cc.bit_length() - 1 if robust else 0)
    causal = (lax.broadcasted_iota(jnp.int32, (cc, GLA_HEADS * cc), 1) % cc
              <= lax.broadcasted_iota(jnp.int32, (cc, GLA_HEADS * cc), 0))
    blk = _head_block_mask(cc, GLA_DK, (GLA_HEADS * cc, GLA_KEY_WIDTH))

    hi, lo = _split_bf16(la_ref[:, 0:GLA_KEY_WIDTH])
    b_all = _dot(ltri, hi) + _dot(ltri, lo)
    pump(PUMP_PLAN["cumsum"])
    lhs_list, v_list, upd_list, dcol_list = [], [], [], []
    for c in range(n_chunks):
        rs = slice(c * cc, (c + 1) * cc)
        b = b_all[rs]
        b_end = b[cc - 1:cc, :]
        q_c = proj_ref[rs, C_Q:C_K] * (GLA_DK ** -0.5)
        k_c = proj_ref[rs, C_K:C_V]
        qs = (q_c * jnp.exp(b)).astype(BF16)
        kd = (k_c * jnp.exp(b_end - b)).astype(BF16)
        if robust:
            a_all = _decayed_scores(q_c, k_c, la_ref[rs, 0:GLA_KEY_WIDTH], b, masks, blk, pump)
        else:
            a_all = _naive_scores(q_c, k_c, b, causal, blk)
            pump(PUMP_PLAN["scores"][c])
        vc = proj_ref[rs, C_V:C_GG].astype(BF16)
        upds = []
        for p in range(GLA_HEADS // 2):
            upd = _dot_tn(kd[:, p * 2 * GLA_DK:(p + 1) * 2 * GLA_DK], vc[:, p * 2 * GLA_DV:(p + 1) * 2 * GLA_DV])
            upds.append(upd[0:GLA_DK, 0:GLA_DV])
            upds.append(upd[GLA_DK:2 * GLA_DK, GLA_DV:2 * GLA_DV])
        lhs_list.append([jnp.concatenate([a_all[:, p * 2 * cc:(p + 1) * 2 * cc],
                                          qs[:, p * 2 * GLA_DK:(p + 1) * 2 * GLA_DK]], axis=1)
                         for p in range(GLA_HEADS // 2)])
        v_list.append(vc)
        upd_list.append(jnp.concatenate(upds, axis=0))
        dcol_list.append(jnp.transpose(jnp.broadcast_to(jnp.exp(b_end), (GLA_DV, GLA_KEY_WIDTH))))

    o_chunks = []
    state = s_ref[...]
    for c in range(n_chunks):
        s_bf = state.astype(BF16)
        vc = v_list[c]
        outs = []
        for p in range(GLA_HEADS // 2):
            h0, h1 = 2 * p, 2 * p + 1
            rhs = jnp.concatenate([
                _pair_block_diag(vc[:, h0 * GLA_DV:(h0 + 1) * GLA_DV], vc[:, h1 * GLA_DV:(h1 + 1) * GLA_DV]),
                _pair_block_diag(s_bf[h0 * GLA_DK:(h0 + 1) * GLA_DK], s_bf[h1 * GLA_DK:(h1 + 1) * GLA_DK]),
            ], axis=0)
            outs.append(_dot(lhs_list[c][p], rhs))
        o_chunks.append(jnp.concatenate(outs, axis=1))
        state = state * dcol_list[c] + upd_list[c]
        pump(PUMP_PLAN["state"][c])
    s_ref[...] = state
    o = jnp.concatenate(o_chunks, axis=0)

    y_ref[...] = _epilogue(xkeep_ref[:, 0:D_MODEL], pool_act, o, proj_ref[:, C_GG:C_GL], ng_ref, wout_ref, lng_ref, lnb_ref,
                           pump, tail, row_blocks=4)


def _prompt_body(x_ref, win_ref, wmap_ref, pscale_ref, ba_ref, ng_ref, wout_ref, lng_ref, lnb_ref,
                 y_ref, pool_ref, s_out_ref,
                 proj0_ref, proj1_ref, xkeep0_ref, xkeep1_ref, la0_ref, la1_ref, xb_ref, ext_ref, s_ref, steep_ref,
                 *, tc, sub, nj):
    i = pl.program_id(0)
    jb = (i + nj - 1) % nj

    @pl.when(i == 0)
    def _():
        proj1_ref[...] = jnp.zeros_like(proj1_ref)
        xkeep1_ref[...] = jnp.zeros_like(xkeep1_ref)
        la1_ref[...] = jnp.zeros_like(la1_ref)
        steep_ref[0] = 0

    @pl.when((i == 0) | (jb == 0))
    def _():
        ext_ref[0:HIST_ROWS, 0:POOL_WIDTH] = jnp.zeros((HIST_ROWS, POOL_WIDTH), F32)
        s_ref[...] = jnp.zeros_like(s_ref)

    units = [(r, lo) for lo in (C_GL, *range(0, C_GL, MXU_COLS)) for r in range(0, tc, PUMP_ROWS)]

    def step(proj_a, xkeep_a, la_a, proj_b, xkeep_b, la_b, robust):
        x = x_ref[0]
        xkeep_a[:, 0:D_MODEL] = x
        xb_ref[:, 0:D_MODEL] = x.astype(BF16)
        pump = _make_pump(units, xb_ref, win_ref, proj_a)
        pump(tc // PUMP_ROWS)

        def next_decay():
            la = _log_sigmoid(proj_a[:, C_GL:C_END] + ba_ref[...]) * (1.0 / GLA_TAU)
            la_a[:, 0:GLA_KEY_WIDTH] = la
            chunk_decay = [jnp.sum(la[c * GLA_CHUNK:(c + 1) * GLA_CHUNK], axis=0, keepdims=True)
                           for c in range(tc // GLA_CHUNK)]
            lowest = jnp.min(functools.reduce(jnp.minimum, chunk_decay))
            steep_ref[0] = jnp.where(lowest < -SAFE_LOG_DECAY, 1, 0).astype(jnp.int32)

        n_sub = tc // sub
        for k in range(n_sub):
            rows = slice(k * sub, (k + 1) * sub)
            last = k == n_sub - 1
            _prompt_stage_b((jb * n_sub + k) * sub, proj_b.at[rows, :], xkeep_b.at[rows, :], la_b.at[rows, :],
                            wmap_ref, pscale_ref, ba_ref, ng_ref, wout_ref, lng_ref, lnb_ref,
                            y_ref.at[0, rows, :], ext_ref, s_ref, pump, next_decay if k == 0 else (lambda: None),
                            tc=sub, robust=robust, tail=None if last else SUBTILE_TAIL_UNITS)
            if not last:
                ext_ref[0:HIST_ROWS, 0:POOL_WIDTH] = ext_ref[sub:sub + HIST_ROWS, 0:POOL_WIDTH]

    set0 = (proj0_ref, xkeep0_ref, la0_ref)
    set1 = (proj1_ref, xkeep1_ref, la1_ref)
    steep = steep_ref[0] != 0
    for parity, (set_a, set_b) in enumerate(((set0, set1), (set1, set0))):
        for robust in (False, True):
            @pl.when((i % 2 == parity) & (steep == robust))
            def _(set_a=set_a, set_b=set_b, robust=robust):
                step(*set_a, *set_b, robust)

    @pl.when((i > 0) & (jb == nj - 1))
    def _():
        pool_ref[0] = ext_ref[sub + 1:sub + HIST_ROWS, 0:POOL_WIDTH]
        s_out_ref[0] = s_ref[...]

    ext_ref[0:HIST_ROWS, 0:POOL_WIDTH] = ext_ref[sub:sub + HIST_ROWS, 0:POOL_WIDTH]


def _prompt_call(x, win, wmap, pscale, ba, ng, wout, lng, lnb):
    nb, t, _ = x.shape
    tc = PROMPT_TILE
    nj = t // tc
    ntiles = nb * nj
    const2 = lambda i: (0, 0)

    def tile_a(i):
        ia = jnp.minimum(i, ntiles - 1)
        return (ia // nj, ia % nj, 0)

    def tile_b(i):
        ib = jnp.maximum(i - 1, 0)
        return (ib // nj, ib % nj, 0)

    def seq_b(i):
        return (jnp.maximum(i - 1, 0) // nj, 0, 0)

    return pl.pallas_call(
        functools.partial(_prompt_body, tc=tc, sub=PROMPT_SUBTILE, nj=nj),
        grid=(ntiles + 1,),
        in_specs=[
            pl.BlockSpec((1, tc, D_MODEL), tile_a),
            pl.BlockSpec(win.shape, const2),
            pl.BlockSpec(wmap.shape, lambda i: (0, 0, 0)),
            pl.BlockSpec(pscale.shape, const2),
            pl.BlockSpec(ba.shape, const2),
            pl.BlockSpec(ng.shape, const2),
            pl.BlockSpec(wout.shape, const2),
            pl.BlockSpec(lng.shape, const2),
            pl.BlockSpec(lnb.shape, const2),
        ],
        out_specs=[
            pl.BlockSpec((1, tc, D_MODEL), tile_b),
            pl.BlockSpec((1, POOL_BUF, POOL_WIDTH), seq_b),
            pl.BlockSpec((1, GLA_KEY_WIDTH, GLA_DV), seq_b),
        ],
        out_shape=[
            jax.ShapeDtypeStruct((nb, t, D_MODEL), F32),
            jax.ShapeDtypeStruct((nb, POOL_BUF, POOL_WIDTH), F32),
            jax.ShapeDtypeStruct((nb, GLA_KEY_WIDTH, GLA_DV), F32),
        ],
        scratch_shapes=[
            pltpu.VMEM((tc, C_END + PITCH_PAD), F32),
            pltpu.VMEM((tc, C_END + PITCH_PAD), F32),
            pltpu.VMEM((tc, D_MODEL + PITCH_PAD), F32),
            pltpu.VMEM((tc, D_MODEL + PITCH_PAD), F32),
            pltpu.VMEM((tc, GLA_KEY_WIDTH + PITCH_PAD), F32),
            pltpu.VMEM((tc, GLA_KEY_WIDTH + PITCH_PAD), F32),
            pltpu.VMEM((tc, D_MODEL + PITCH_PAD), BF16),
            pltpu.VMEM((HIST_ROWS + PROMPT_SUBTILE, POOL_WIDTH + PITCH_PAD), F32),
            pltpu.VMEM((GLA_KEY_WIDTH, GLA_DV), F32),
            pltpu.SMEM((1,), jnp.int32),
        ],
        compiler_params=pltpu.CompilerParams(
            dimension_semantics=("arbitrary",), vmem_limit_bytes=VMEM_LIMIT),
        name="prompt_layer",
    )(x, win, wmap, pscale, ba, ng, wout, lng, lnb)


def _sample_body(x_ref, pool_in_ref, s_in_ref, win_ref, wmap_ref, pscale_ref, ba_ref, ng_ref, wout_ref,
                 lng_ref, lnb_ref, y_ref, pool_ref, s_out_ref, u_ref, pooled_ref, *, bs, t, start_pos):
    rows = bs * t
    x = x_ref[...].reshape(rows, D_MODEL)
    xb = x.astype(BF16)

    def proj(lo, hi):
        return _dot(xb, win_ref[:, lo:hi])

    u = proj(C_U, C_GP)
    mixed = []
    for g, w in enumerate(POOL_WINDOWS):
        sl = slice(g * POOL_GROUP_DIM, (g + 1) * POOL_GROUP_DIM)
        u_ref[g] = u[:, sl]
        ext = [pool_in_ref[r, :, sl] for r in range(POOL_BUF)]
        ext += [u_ref[g, pl.ds(tok, bs, stride=t), :] for tok in range(t)]
        for r in range(POOL_BUF):
            pool_ref[r, :, sl] = ext[t + r]
        sums = ext
        sh = 1
        while sh < w:
            first = 2 * sh - 1
            sums = [None] * first + [sums[r] + sums[r - sh] for r in range(first, len(sums))]
            sh *= 2
        for tok in range(t):
            cnt = float(min(start_pos + tok + 1, w))
            pooled_ref[g, pl.ds(tok, bs, stride=t), :] = sums[POOL_BUF + tok] / cnt - ext[POOL_BUF + tok]
        mixed.append(_dot(pooled_ref[g].astype(BF16), wmap_ref[g]))
    gp = proj(C_GP, C_Q)
    pool_act = jnp.concatenate(mixed, axis=1) * pscale_ref[...] * _silu(gp)

    q = proj(C_Q, C_K) * (GLA_DK ** -0.5)
    k = proj(C_K, C_V)
    vb = proj(C_V, C_GG).astype(BF16)
    gg = proj(C_GG, C_GL)
    la = _log_sigmoid(proj(C_GL, C_END) + ba_ref[...]) * (1.0 / GLA_TAU)

    rr = lax.broadcasted_iota(jnp.int32, (rows, rows), 0)
    cl = lax.broadcasted_iota(jnp.int32, (rows, rows), 1)
    same = (rr // t) == (cl // t)
    ltri = jnp.where(same & (cl <= rr), 1.0, 0.0).astype(BF16)
    lall = jnp.where(same, 1.0, 0.0).astype(BF16)
    hi, lo = _split_bf16(la)
    b = _dot(ltri, hi) + _dot(ltri, lo)
    b_end = _dot(lall, hi) + _dot(lall, lo)
    qs = (q * jnp.exp(b)).astype(BF16)
    kd = (k * jnp.exp(b_end - b)).astype(BF16)

    assert t & (t - 1) == 0, "tokens per sequence must be a power of two"
    blk = _head_block_mask(rows, GLA_DK, (GLA_HEADS * rows, GLA_KEY_WIDTH))
    a_all = _decayed_scores(q, k, la, b, _score_masks(rows, t.bit_length() - 1), blk)

    s_old = s_in_ref[...]
    s_bf = s_old.astype(BF16)
    qs3 = qs.reshape(bs, t, GLA_KEY_WIDTH)
    kd3 = kd.reshape(bs, t, GLA_KEY_WIDTH)
    v3 = vb.reshape(bs, t, GLA_WIDTH)
    bdims_nn = (((2,), (1,)), ((0,), (0,)))
    bdims_tn = (((1,), (1,)), ((0,), (0,)))
    assert bs <= GLA_DV
    seq_sel = jnp.where(lax.broadcasted_iota(jnp.int32, (rows, GLA_DV), 0) // t
                        == lax.broadcasted_iota(jnp.int32, (rows, GLA_DV), 1), 1.0, 0.0).astype(BF16)
    decay_t = jnp.exp(_dot_tn(hi, seq_sel) + _dot_tn(lo, seq_sel))
    decay = jnp.stack([jnp.broadcast_to(decay_t[:, j:j + 1], (GLA_KEY_WIDTH, GLA_DV)) for j in range(bs)], axis=0)
    outs, upds = [], []
    for p in range(GLA_HEADS // 2):
        h0, h1 = 2 * p, 2 * p + 1
        intra = _dot(a_all[:, p * 2 * rows:(p + 1) * 2 * rows],
                     _pair_block_diag(vb[:, h0 * GLA_DV:(h0 + 1) * GLA_DV], vb[:, h1 * GLA_DV:(h1 + 1) * GLA_DV]))
        s0 = s_bf[:, h0 * GLA_DK:(h0 + 1) * GLA_DK, :]
        s1 = s_bf[:, h1 * GLA_DK:(h1 + 1) * GLA_DK, :]
        z = jnp.zeros_like(s0)
        s_pair = jnp.concatenate(
            [jnp.concatenate([s0, z], axis=2), jnp.concatenate([z, s1], axis=2)], axis=1)
        inter = lax.dot_general(qs3[:, :, p * 2 * GLA_DK:(p + 1) * 2 * GLA_DK], s_pair, bdims_nn,
                                preferred_element_type=F32)
        outs.append(intra + inter.reshape(rows, 2 * GLA_DV))
        upd = lax.dot_general(kd3[:, :, p * 2 * GLA_DK:(p + 1) * 2 * GLA_DK],
                              v3[:, :, p * 2 * GLA_DV:(p + 1) * 2 * GLA_DV], bdims_tn,
                              preferred_element_type=F32)
        upds.append(upd[:, 0:GLA_DK, 0:GLA_DV])
        upds.append(upd[:, GLA_DK:2 * GLA_DK, GLA_DV:2 * GLA_DV])
    o = jnp.concatenate(outs, axis=1)
    s_out_ref[...] = s_old * decay + jnp.concatenate(upds, axis=1)

    y = _epilogue(x, pool_act, o, gg, ng_ref, wout_ref, lng_ref, lnb_ref)
    y_ref[...] = y.reshape(bs, t, D_MODEL)


def _sample_call(x, pool_in, s_in, win, wmap, pscale, ba, ng, wout, lng, lnb, start_pos):
    nb, t, _ = x.shape
    bs = SAMPLE_SEQS
    const2 = lambda i: (0, 0)
    seq3 = lambda i: (i, 0, 0)
    hist3 = lambda i: (0, i, 0)
    return pl.pallas_call(
        functools.partial(_sample_body, bs=bs, t=t, start_pos=start_pos),
        grid=(nb // bs,),
        in_specs=[
            pl.BlockSpec((bs, t, D_MODEL), seq3),
            pl.BlockSpec((POOL_BUF, bs, POOL_WIDTH), hist3),
            pl.BlockSpec((bs, GLA_KEY_WIDTH, GLA_DV), seq3),
            pl.BlockSpec(win.shape, const2),
            pl.BlockSpec(wmap.shape, lambda i: (0, 0, 0)),
            pl.BlockSpec(pscale.shape, const2),
            pl.BlockSpec(ba.shape, const2),
            pl.BlockSpec(ng.shape, const2),
            pl.BlockSpec(wout.shape, const2),
            pl.BlockSpec(lng.shape, const2),
            pl.BlockSpec(lnb.shape, const2),
        ],
        out_specs=[
            pl.BlockSpec((bs, t, D_MODEL), seq3),
            pl.BlockSpec((POOL_BUF, bs, POOL_WIDTH), hist3),
            pl.BlockSpec((bs, GLA_KEY_WIDTH, GLA_DV), seq3),
        ],
        out_shape=[
            jax.ShapeDtypeStruct((nb, t, D_MODEL), F32),
            jax.ShapeDtypeStruct((POOL_BUF, nb, POOL_WIDTH), F32),
            jax.ShapeDtypeStruct((nb, GLA_KEY_WIDTH, GLA_DV), F32),
        ],
        scratch_shapes=[pltpu.VMEM((len(POOL_WINDOWS), bs * t, POOL_GROUP_DIM), F32),
                        pltpu.VMEM((len(POOL_WINDOWS), bs * t, POOL_GROUP_DIM), F32)],
        compiler_params=pltpu.CompilerParams(
            dimension_semantics=("arbitrary",), vmem_limit_bytes=VMEM_LIMIT),
        name="sample_layer",
    )(x, pool_in, s_in, win, wmap, pscale, ba, ng, wout, lng, lnb)


def _prep_body(wt_ref, wfat_ref, wa2t_ref, out_ref, *, n_plain):
    i = pl.program_id(0)

    @pl.when(i < n_plain)
    def _():
        out_ref[...] = jnp.transpose(wt_ref[...]).astype(BF16)

    @pl.when(i == n_plain)
    def _():
        wfat = wfat_ref[...]
        wa2t = wa2t_ref[...]
        acc = wa2t[:, 0:1] * wfat[0:1, :]
        for r in range(1, GLA_RANK):
            acc = acc + wa2t[:, r:r + 1] * wfat[r:r + 1, :]
        out_ref[...] = jnp.transpose(acc).astype(BF16)

    @pl.when(i > n_plain)
    def _():
        out_ref[...] = jnp.zeros_like(out_ref)


def _prep_input_weights(wt, wa2t):
    n_plain = C_GL // MXU_COLS
    return pl.pallas_call(
        functools.partial(_prep_body, n_plain=n_plain),
        grid=(pl.cdiv(C_END + PITCH_PAD, MXU_COLS),),
        in_specs=[
            pl.BlockSpec((MXU_COLS, D_MODEL), lambda i: (jnp.minimum(i, n_plain - 1), 0)),
            pl.BlockSpec((GLA_RANK, D_MODEL), lambda i: (0, 0)),
            pl.BlockSpec((GLA_KEY_WIDTH, GLA_RANK), lambda i: (0, 0)),
        ],
        out_specs=pl.BlockSpec((D_MODEL, MXU_COLS), lambda i: (0, i)),
        out_shape=jax.ShapeDtypeStruct((D_MODEL, C_END + PITCH_PAD), BF16),
        compiler_params=pltpu.CompilerParams(dimension_semantics=("arbitrary",)),
        name="prep_input_weights",
    )(wt, wt[C_GL:], wa2t)


def kernel(x_prompt, x_sample, state_pool, state_gla, w_in, w_pool_map, pool_scale, w_gla_a2, b_gla_a,
           gla_norm_g, w_out, ln_g, ln_b):
    assert w_in.shape[0] == DEPTH
    win = _prep_input_weights(jnp.transpose(w_in[0]), jnp.transpose(w_gla_a2[0]))
    wmap = w_pool_map[0].astype(BF16)
    wout = jnp.pad(w_out[0].astype(BF16), ((0, 0), (0, PITCH_PAD)))
    pscale = pool_scale[0].reshape(1, POOL_WIDTH)
    ba = b_gla_a[0].reshape(1, GLA_KEY_WIDTH)
    ng = gla_norm_g[0].reshape(1, GLA_WIDTH)
    lng = ln_g[0].reshape(1, D_MODEL)
    lnb = ln_b[0].reshape(1, D_MODEL)
    params = (win, wmap, pscale, ba, ng, wout, lng, lnb)

    y_p, pool_p, gla_p = _prompt_call(x_prompt, *params)
    nsamp = x_sample.shape[0]
    y_s, pool_s, gla_s = _sample_call(
        x_sample, jnp.transpose(state_pool[0], (1, 0, 2)),
        state_gla[0].reshape(nsamp, GLA_KEY_WIDTH, GLA_DV), *params, start_pos=PAST_LEN)

    nb = x_prompt.shape[0]
    return (y_p, y_s, pool_p[None],
            gla_p.reshape(1, nb, GLA_HEADS, GLA_DK, GLA_DV),
            jnp.transpose(pool_s, (1, 0, 2))[None],
            gla_s.reshape(1, nsamp, GLA_HEADS, GLA_DK, GLA_DV))
```

```python
import functools
import itertools

import jax
import jax.numpy as jnp
from jax import lax
from jax.experimental import pallas as pl
from jax.experimental.pallas import tpu as pltpu

F32 = jnp.float32
BF16 = jnp.bfloat16

D_MODEL = 1024
POOL_WIDTH = 512
POOL_WINDOWS = (2, 4, 8, 16)
POOL_GROUP_DIM = 128
POOL_BUF = 15
HIST_ROWS = 16
GLA_HEADS = 4
GLA_DK = 64
GLA_DV = 128
GLA_KEY_WIDTH = GLA_HEADS * GLA_DK
GLA_WIDTH = GLA_HEADS * GLA_DV
GLA_RANK = 16
GLA_TAU = 16.0
LN_EPS = 1e-5
RMS_EPS = 1e-6
DEPTH = 1
DEEPNORM_ALPHA = (2.0 * DEPTH) ** 0.25
PAST_LEN = 16384

C_U, C_GP, C_Q, C_K, C_V, C_GG, C_GL, C_END = 0, 512, 1024, 1280, 1536, 2048, 2560, 2816

MXU_COLS = 256
PROMPT_TILE = 256
PITCH_PAD = 128
PROMPT_SUBTILE = 256
SUBTILE_TAIL_UNITS = 0
PUMP_ROWS = 256
PUMP_PLAN = {"pool": (0, 1, 0, 1), "cumsum": 0, "scores": (1, 1, 1, 1), "level": (0, 0, 0, 1, 0, 0),
             "state": (0, 0, 0, 0), "epilogue": 1, "layernorm": (1, 1, 1, 0)}
SAFE_LOG_DECAY = 60.0
GLA_CHUNK = 64
SAMPLE_SEQS = 16
PREP_COLS = 512
VMEM_LIMIT = 56 * 1024 * 1024


def _dot(a, b):
    return jnp.dot(a, b, preferred_element_type=F32)


def _dot_nt(a, b):
    return lax.dot_general(a, b, (((1,), (1,)), ((), ())), preferred_element_type=F32)


def _dot_tn(a, b):
    return lax.dot_general(a, b, (((0,), (0,)), ((), ())), preferred_element_type=F32)


def _silu(x):
    return x * (1.0 / (1.0 + jnp.exp(-x)))


def _log_sigmoid(x):
    return jnp.minimum(x, 0.0) - jnp.log1p(jnp.exp(-jnp.abs(x)))


def _split_bf16(x):
    hi = x.astype(BF16)
    lo = (x - hi.astype(F32)).astype(BF16)
    return hi, lo


def _epilogue(x, pool_act, o, gg, ng_ref, wout_ref, lng_ref, lnb_ref, pump=lambda n=None: None, tail=None,
              row_blocks=1):
    pump(PUMP_PLAN["epilogue"])
    normed = []
    for h in range(GLA_HEADS):
        oh = o[:, h * GLA_DV:(h + 1) * GLA_DV]
        ms = jnp.mean(oh * oh, axis=-1, keepdims=True)
        normed.append(oh * lax.rsqrt(ms + RMS_EPS))
    gla_act = jnp.concatenate(normed, axis=1) * ng_ref[...] * _silu(gg)
    mix = jnp.concatenate([pool_act, gla_act], axis=1).astype(BF16)
    y = _dot(mix, wout_ref[:, 0:D_MODEL])
    n = x.shape[0] // row_blocks
    outs = []
    for r in range(row_blocks):
        rows = slice(r * n, (r + 1) * n)
        pump(PUMP_PLAN["layernorm"][r])
        h = DEEPNORM_ALPHA * x[rows] + y[rows]
        mu = jnp.mean(h, axis=-1, keepdims=True)
        d = h - mu
        var = jnp.mean(d * d, axis=-1, keepdims=True)
        outs.append(d * lax.rsqrt(var + LN_EPS) * lng_ref[...] + lnb_ref[...])
    pump(tail)
    return outs[0] if row_blocks == 1 else jnp.concatenate(outs, axis=0)


def _head_block_mask(rows_per_head, cols_per_head, shape):
    r = lax.broadcasted_iota(jnp.int32, shape, 0) // rows_per_head
    c = lax.broadcasted_iota(jnp.int32, shape, 1) // cols_per_head
    return r == c


def _score_masks(rows, n_levels):
    shape = (rows, GLA_HEADS * rows)
    t_idx = lax.broadcasted_iota(jnp.int32, shape, 0)
    s_idx = lax.broadcasted_iota(jnp.int32, shape, 1) % rows
    masks = [t_idx == s_idx]
    for lvl in range(n_levels):
        half = 1 << lvl
        blk = 2 * half
        masks.append((t_idx // blk == s_idx // blk) & (t_idx % blk >= half) & (s_idx % blk < half))
    return masks


def _decayed_scores(q, k, la, b, masks, head_blk, pump=lambda n=None: None):
    rows = q.shape[0]
    r_idx = lax.broadcasted_iota(jnp.int32, q.shape, 0)

    def scores(qt, kt):
        kh = jnp.where(head_blk, jnp.concatenate([kt.astype(BF16)] * GLA_HEADS, axis=0), jnp.zeros((), BF16))
        return _dot_nt(qt.astype(BF16), kh)

    acc = jnp.where(masks[0], scores(q, k), 0.0)
    for lvl in range(len(masks) - 1):
        half = 1 << lvl
        blk = 2 * half
        if lvl == 0:
            neg = jnp.where(r_idx % 2 == 1, la, 0.0)
        elif lvl == 1:
            r4 = r_idx % 4
            nxt = pltpu.roll(la, rows - 1, 0)
            prv = pltpu.roll(la, 1, 0)
            neg = jnp.where(r4 == 0, nxt, jnp.where(r4 == 1, 0.0, jnp.where(r4 == 2, la, la + prv)))
        else:
            ref = jnp.concatenate(
                [jnp.broadcast_to(b[j * blk + half - 1:j * blk + half, :], (blk, b.shape[1]))
                 for j in range(rows // blk)], axis=0)
            d = b - ref
            neg = jnp.minimum(d, -d)
        w = jnp.exp(neg)
        acc = jnp.where(masks[lvl + 1], scores(q * w, k * w), acc)
        pump(PUMP_PLAN["level"][lvl])
    return acc.astype(BF16)


def _pair_block_diag(a0, a1):
    z = jnp.zeros_like(a0)
    return jnp.concatenate(
        [jnp.concatenate([a0, z], axis=1), jnp.concatenate([z, a1], axis=1)], axis=0)


def _make_pump(units, xb_ref, win_ref, proj_ref):
    units = iter(units)

    def pump(n=None):
        for r, lo in (units if n is None else itertools.islice(units, n)):
            proj_ref[r:r + PUMP_ROWS, lo:lo + MXU_COLS] = _dot(
                xb_ref[r:r + PUMP_ROWS, 0:D_MODEL], win_ref[:, lo:lo + MXU_COLS])

    return pump


def _naive_scores(q, k, b, causal, head_blk):
    ks = (k * jnp.exp(-b)).astype(BF16)
    kh = jnp.where(head_blk, jnp.concatenate([ks] * GLA_HEADS, axis=0), jnp.zeros((), BF16))
    return jnp.where(causal, _dot_nt((q * jnp.exp(b)).astype(BF16), kh), 0.0).astype(BF16)


def _prompt_stage_b(pos0, proj_ref, xkeep_ref, la_ref, wmap_ref, pscale_ref, ba_ref, ng_ref, wout_ref, lng_ref,
                    lnb_ref, y_ref, ext_ref, s_ref, pump, next_decay, *, tc, robust, tail):
    u = proj_ref[:, C_U:C_GP]
    ext_ref[HIST_ROWS:HIST_ROWS + tc, 0:POOL_WIDTH] = u
    pos16 = pos0 + lax.broadcasted_iota(jnp.int32, (HIST_ROWS, POOL_GROUP_DIM), 0)
    mixed, pooled = [], []
    for g, w in enumerate(POOL_WINDOWS):
        sl = slice(g * POOL_GROUP_DIM, (g + 1) * POOL_GROUP_DIM)
        e = ext_ref[:, sl]
        s = e
        sh = 1
        while sh < w:
            s = s + pltpu.roll(s, sh, 0)
            sh *= 2
        wsum = s[HIST_ROWS:]
        self_rows = e[HIST_ROWS:]
        cnt = jnp.minimum(pos16 + 1, w).astype(F32)
        first = wsum[:HIST_ROWS] / cnt - self_rows[:HIST_ROWS]
        rest = wsum[HIST_ROWS:] * (1.0 / w) - self_rows[HIST_ROWS:]
        pooled.append(jnp.concatenate([first, rest], axis=0).astype(BF16))
        if g % 2 == 1:
            mixed.append(_dot(jnp.concatenate(pooled[g - 1:g + 1], axis=1), wmap_ref[g // 2]))
        pump(PUMP_PLAN["pool"][g])
    pool_act = jnp.concatenate(mixed, axis=1) * pscale_ref[...] * _silu(proj_ref[:, C_GP:C_Q])
    next_decay()

    cc = GLA_CHUNK
    n_chunks = tc // cc
    rr = lax.broadcasted_iota(jnp.int32, (tc, tc), 0)
    cl = lax.broadcasted_iota(jnp.int32, (tc, tc), 1)
    ltri = jnp.where((rr // cc == cl // cc) & (cl <= rr), 1.0, 0.0).astype(BF16)
    masks = _score_masks(cc, cc.bit_length() - 1 if robust else 0)
    causal = (lax.broadcasted_iota(jnp.int32, (cc, GLA_HEADS * cc), 1) % cc
              <= lax.broadcasted_iota(jnp.int32, (cc, GLA_HEADS * cc), 0))
    blk = _head_block_mask(cc, GLA_DK, (GLA_HEADS * cc, GLA_KEY_WIDTH))

    hi, lo = _split_bf16(la_ref[:, 0:GLA_KEY_WIDTH])
    b_all = _dot(ltri, hi) + _dot(ltri, lo)
    pump(PUMP_PLAN["cumsum"])
    lhs_list, v_list, upd_list, dcol_list = [], [], [], []
    for c in range(n_chunks):
        rs = slice(c * cc, (c + 1) * cc)
        b = b_all[rs]
        b_end = b[cc - 1:cc, :]
        q_c = proj_ref[rs, C_Q:C_K] * (GLA_DK ** -0.5)
        k_c = proj_ref[rs, C_K:C_V]
        qs = (q_c * jnp.exp(b)).astype(BF16)
        kd = (k_c * jnp.exp(b_end - b)).astype(BF16)
        if robust:
            a_all = _decayed_scores(q_c, k_c, la_ref[rs, 0:GLA_KEY_WIDTH], b, masks, blk, pump)
        else:
            a_all = _naive_scores(q_c, k_c, b, causal, blk)
            pump(PUMP_PLAN["scores"][c])
        vc = proj_ref[rs, C_V:C_GG].astype(BF16)
        upds = []
        for p in range(GLA_HEADS // 2):
            upd = _dot_tn(kd[:, p * 2 * GLA_DK:(p + 1) * 2 * GLA_DK], vc[:, p * 2 * GLA_DV:(p + 1) * 2 * GLA_DV])
            upds.append(upd[0:GLA_DK, 0:GLA_DV])
            upds.append(upd[GLA_DK:2 * GLA_DK, GLA_DV:2 * GLA_DV])
        lhs_list.append([jnp.concatenate([a_all[:, p * 2 * cc:(p + 1) * 2 * cc],
                                          qs[:, p * 2 * GLA_DK:(p + 1) * 2 * GLA_DK]], axis=1)
                         for p in range(GLA_HEADS // 2)])
        v_list.append(vc)
        upd_list.append(jnp.concatenate(upds, axis=0))
        dcol_list.append(jnp.transpose(jnp.broadcast_to(jnp.exp(b_end), (GLA_DV, GLA_KEY_WIDTH))))

    o_chunks = []
    state = s_ref[...]
    for c in range(n_chunks):
        s_bf = state.astype(BF16)
        vc = v_list[c]
        outs = []
        for p in range(GLA_HEADS // 2):
            h0, h1 = 2 * p, 2 * p + 1
            rhs = jnp.concatenate([
                _pair_block_diag(vc[:, h0 * GLA_DV:(h0 + 1) * GLA_DV], vc[:, h1 * GLA_DV:(h1 + 1) * GLA_DV]),
                _pair_block_diag(s_bf[h0 * GLA_DK:(h0 + 1) * GLA_DK], s_bf[h1 * GLA_DK:(h1 + 1) * GLA_DK]),
            ], axis=0)
            outs.append(_dot(lhs_list[c][p], rhs))
        o_chunks.append(jnp.concatenate(outs, axis=1))
        state = state * dcol_list[c] + upd_list[c]
        pump(PUMP_PLAN["state"][c])
    s_ref[...] = state
    o = jnp.concatenate(o_chunks, axis=0)

    y_ref[...] = _epilogue(xkeep_ref[:, 0:D_MODEL], pool_act, o, proj_ref[:, C_GG:C_GL], ng_ref, wout_ref, lng_ref, lnb_ref,
                           pump, tail, row_blocks=4)


def _prompt_body(x_ref, win_ref, wmap_ref, pscale_ref, ba_ref, ng_ref, wout_ref, lng_ref, lnb_ref,
                 y_ref, pool_ref, s_out_ref,
                 proj0_ref, proj1_ref, xkeep0_ref, xkeep1_ref, la0_ref, la1_ref, xb_ref, ext_ref, s_ref, steep_ref,
                 *, tc, sub, nj):
    i = pl.program_id(0)
    jb = (i + nj - 1) % nj

    @pl.when(i == 0)
    def _():
        proj1_ref[...] = jnp.zeros_like(proj1_ref)
        xkeep1_ref[...] = jnp.zeros_like(xkeep1_ref)
        la1_ref[...] = jnp.zeros_like(la1_ref)
        steep_ref[0] = 0

    @pl.when((i == 0) | (jb == 0))
    def _():
        ext_ref[0:HIST_ROWS, 0:POOL_WIDTH] = jnp.zeros((HIST_ROWS, POOL_WIDTH), F32)
        s_ref[...] = jnp.zeros_like(s_ref)

    units = [(r, lo) for lo in (C_GL, *range(0, C_GL, MXU_COLS)) for r in range(0, tc, PUMP_ROWS)]

    def step(proj_a, xkeep_a, la_a, proj_b, xkeep_b, la_b, robust):
        x = x_ref[0]
        xkeep_a[:, 0:D_MODEL] = x
        xb_ref[:, 0:D_MODEL] = x.astype(BF16)
        pump = _make_pump(units, xb_ref, win_ref, proj_a)
        pump(tc // PUMP_ROWS)

        def next_decay():
            la = _log_sigmoid(proj_a[:, C_GL:C_END] + ba_ref[...]) * (1.0 / GLA_TAU)
            la_a[:, 0:GLA_KEY_WIDTH] = la
            chunk_decay = [jnp.sum(la[c * GLA_CHUNK:(c + 1) * GLA_CHUNK], axis=0, keepdims=True)
                           for c in range(tc // GLA_CHUNK)]
            lowest = jnp.min(functools.reduce(jnp.minimum, chunk_decay))
            steep_ref[0] = jnp.where(lowest < -SAFE_LOG_DECAY, 1, 0).astype(jnp.int32)

        n_sub = tc // sub
        for k in range(n_sub):
            rows = slice(k * sub, (k + 1) * sub)
            last = k == n_sub - 1
            _prompt_stage_b((jb * n_sub + k) * sub, proj_b.at[rows, :], xkeep_b.at[rows, :], la_b.at[rows, :],
                            wmap_ref, pscale_ref, ba_ref, ng_ref, wout_ref, lng_ref, lnb_ref,
                            y_ref.at[0, rows, :], ext_ref, s_ref, pump, next_decay if k == 0 else (lambda: None),
                            tc=sub, robust=robust, tail=None if last else SUBTILE_TAIL_UNITS)
            if not last:
                ext_ref[0:HIST_ROWS, 0:POOL_WIDTH] = ext_ref[sub:sub + HIST_ROWS, 0:POOL_WIDTH]

    set0 = (proj0_ref, xkeep0_ref, la0_ref)
    set1 = (proj1_ref, xkeep1_ref, la1_ref)
    steep = steep_ref[0] != 0
    for parity, (set_a, set_b) in enumerate(((set0, set1), (set1, set0))):
        for robust in (False, True):
            @pl.when((i % 2 == parity) & (steep == robust))
            def _(set_a=set_a, set_b=set_b, robust=robust):
                step(*set_a, *set_b, robust)

    @pl.when((i > 0) & (jb == nj - 1))
    def _():
        pool_ref[0] = ext_ref[sub + 1:sub + HIST_ROWS, 0:POOL_WIDTH]
        s_out_ref[0] = s_ref[...]

    ext_ref[0:HIST_ROWS, 0:POOL_WIDTH] = ext_ref[sub:sub + HIST_ROWS, 0:POOL_WIDTH]


def _prompt_call(x, win, wmap, pscale, ba, ng, wout, lng, lnb):
    nb, t, _ = x.shape
    tc = PROMPT_TILE
    nj = t // tc
    ntiles = nb * nj
    const2 = lambda i: (0, 0)

    def tile_a(i):
        ia = jnp.minimum(i, ntiles - 1)
        return (ia // nj, ia % nj, 0)

    def tile_b(i):
        ib = jnp.maximum(i - 1, 0)
        return (ib // nj, ib % nj, 0)

    def seq_b(i):
        return (jnp.maximum(i - 1, 0) // nj, 0, 0)

    return pl.pallas_call(
        functools.partial(_prompt_body, tc=tc, sub=PROMPT_SUBTILE, nj=nj),
        grid=(ntiles + 1,),
        in_specs=[
            pl.BlockSpec((1, tc, D_MODEL), tile_a),
            pl.BlockSpec(win.shape, const2),
            pl.BlockSpec(wmap.shape, lambda i: (0, 0, 0)),
            pl.BlockSpec(pscale.shape, const2),
            pl.BlockSpec(ba.shape, const2),
            pl.BlockSpec(ng.shape, const2),
            pl.BlockSpec(wout.shape, const2),
            pl.BlockSpec(lng.shape, const2),
            pl.BlockSpec(lnb.shape, const2),
        ],
        out_specs=[
            pl.BlockSpec((1, tc, D_MODEL), tile_b),
            pl.BlockSpec((1, POOL_BUF, POOL_WIDTH), seq_b),
            pl.BlockSpec((1, GLA_KEY_WIDTH, GLA_DV), seq_b),
        ],
        out_shape=[
            jax.ShapeDtypeStruct((nb, t, D_MODEL), F32),
            jax.ShapeDtypeStruct((nb, POOL_BUF, POOL_WIDTH), F32),
            jax.ShapeDtypeStruct((nb, GLA_KEY_WIDTH, GLA_DV), F32),
        ],
        scratch_shapes=[
            pltpu.VMEM((tc, C_END + PITCH_PAD), F32),
            pltpu.VMEM((tc, C_END + PITCH_PAD), F32),
            pltpu.VMEM((tc, D_MODEL + PITCH_PAD), F32),
            pltpu.VMEM((tc, D_MODEL + PITCH_PAD), F32),
            pltpu.VMEM((tc, GLA_KEY_WIDTH + PITCH_PAD), F32),
            pltpu.VMEM((tc, GLA_KEY_WIDTH + PITCH_PAD), F32),
            pltpu.VMEM((tc, D_MODEL + PITCH_PAD), BF16),
            pltpu.VMEM((HIST_ROWS + PROMPT_SUBTILE, POOL_WIDTH + PITCH_PAD), F32),
            pltpu.VMEM((GLA_KEY_WIDTH, GLA_DV), F32),
            pltpu.SMEM((1,), jnp.int32),
        ],
        compiler_params=pltpu.CompilerParams(
            dimension_semantics=("arbitrary",), vmem_limit_bytes=VMEM_LIMIT),
        name="prompt_layer",
    )(x, win, wmap, pscale, ba, ng, wout, lng, lnb)


def _sample_body(x_ref, pool_in_ref, s_in_ref, win_ref, wmap_ref, pscale_ref, ba_ref, ng_ref, wout_ref,
                 lng_ref, lnb_ref, y_ref, pool_ref, s_out_ref, u_ref, pooled_ref, *, bs, t, start_pos):
    rows = bs * t
    x = x_ref[...].reshape(rows, D_MODEL)
    xb = x.astype(BF16)

    def proj(lo, hi):
        return _dot(xb, win_ref[:, lo:hi])

    u = proj(C_U, C_GP)
    mixed = []
    for g, w in enumerate(POOL_WINDOWS):
        sl = slice(g * POOL_GROUP_DIM, (g + 1) * POOL_GROUP_DIM)
        u_ref[g] = u[:, sl]
        ext = [pool_in_ref[r, :, sl] for r in range(POOL_BUF)]
        ext += [u_ref[g, pl.ds(tok, bs, stride=t), :] for tok in range(t)]
        for r in range(POOL_BUF):
            pool_ref[r, :, sl] = ext[t + r]
        sums = ext
        sh = 1
        while sh < w:
            first = 2 * sh - 1
            sums = [None] * first + [sums[r] + sums[r - sh] for r in range(first, len(sums))]
            sh *= 2
        for tok in range(t):
            cnt = float(min(start_pos + tok + 1, w))
            pooled_ref[g, pl.ds(tok, bs, stride=t), :] = sums[POOL_BUF + tok] / cnt - ext[POOL_BUF + tok]
        if g % 2 == 1:
            pair = jnp.concatenate([pooled_ref[g - 1], pooled_ref[g]], axis=1).astype(BF16)
            mixed.append(_dot(pair, wmap_ref[g // 2]))
    gp = proj(C_GP, C_Q)
    pool_act = jnp.concatenate(mixed, axis=1) * pscale_ref[...] * _silu(gp)

    q = proj(C_Q, C_K) * (GLA_DK ** -0.5)
    k = proj(C_K, C_V)
    vb = proj(C_V, C_GG).astype(BF16)
    gg = proj(C_GG, C_GL)
    la = _log_sigmoid(proj(C_GL, C_END) + ba_ref[...]) * (1.0 / GLA_TAU)

    rr = lax.broadcasted_iota(jnp.int32, (rows, rows), 0)
    cl = lax.broadcasted_iota(jnp.int32, (rows, rows), 1)
    same = (rr // t) == (cl // t)
    ltri = jnp.where(same & (cl <= rr), 1.0, 0.0).astype(BF16)
    lall = jnp.where(same, 1.0, 0.0).astype(BF16)
    hi, lo = _split_bf16(la)
    b = _dot(ltri, hi) + _dot(ltri, lo)
    b_end = _dot(lall, hi) + _dot(lall, lo)
    qs = (q * jnp.exp(b)).astype(BF16)
    kd = (k * jnp.exp(b_end - b)).astype(BF16)

    assert t & (t - 1) == 0, "tokens per sequence must be a power of two"
    blk = _head_block_mask(rows, GLA_DK, (GLA_HEADS * rows, GLA_KEY_WIDTH))
    a_all = _decayed_scores(q, k, la, b, _score_masks(rows, t.bit_length() - 1), blk)

    s_old = s_in_ref[...]
    s_bf = s_old.astype(BF16)
    qs3 = qs.reshape(bs, t, GLA_KEY_WIDTH)
    kd3 = kd.reshape(bs, t, GLA_KEY_WIDTH)
    v3 = vb.reshape(bs, t, GLA_WIDTH)
    bdims_nn = (((2,), (1,)), ((0,), (0,)))
    bdims_tn = (((1,), (1,)), ((0,), (0,)))
    assert bs <= GLA_DV
    seq_sel = jnp.where(lax.broadcasted_iota(jnp.int32, (rows, GLA_DV), 0) // t
                        == lax.broadcasted_iota(jnp.int32, (rows, GLA_DV), 1), 1.0, 0.0).astype(BF16)
    decay_t = jnp.exp(_dot_tn(hi, seq_sel) + _dot_tn(lo, seq_sel))
    decay = jnp.stack([jnp.broadcast_to(decay_t[:, j:j + 1], (GLA_KEY_WIDTH, GLA_DV)) for j in range(bs)], axis=0)
    outs, upds = [], []
    for p in range(GLA_HEADS // 2):
        h0, h1 = 2 * p, 2 * p + 1
        intra = _dot(a_all[:, p * 2 * rows:(p + 1) * 2 * rows],
                     _pair_block_diag(vb[:, h0 * GLA_DV:(h0 + 1) * GLA_DV], vb[:, h1 * GLA_DV:(h1 + 1) * GLA_DV]))
        s0 = s_bf[:, h0 * GLA_DK:(h0 + 1) * GLA_DK, :]
        s1 = s_bf[:, h1 * GLA_DK:(h1 + 1) * GLA_DK, :]
        z = jnp.zeros_like(s0)
        s_pair = jnp.concatenate(
            [jnp.concatenate([s0, z], axis=2), jnp.concatenate([z, s1], axis=2)], axis=1)
        inter = lax.dot_general(qs3[:, :, p * 2 * GLA_DK:(p + 1) * 2 * GLA_DK], s_pair, bdims_nn,
                                preferred_element_type=F32)
        outs.append(intra + inter.reshape(rows, 2 * GLA_DV))
        upd = lax.dot_general(kd3[:, :, p * 2 * GLA_DK:(p + 1) * 2 * GLA_DK],
                              v3[:, :, p * 2 * GLA_DV:(p + 1) * 2 * GLA_DV], bdims_tn,
                              preferred_element_type=F32)
        upds.append(upd[:, 0:GLA_DK, 0:GLA_DV])
        upds.append(upd[:, GLA_DK:2 * GLA_DK, GLA_DV:2 * GLA_DV])
    o = jnp.concatenate(outs, axis=1)
    s_out_ref[...] = s_old * decay + jnp.concatenate(upds, axis=1)

    y = _epilogue(x, pool_act, o, gg, ng_ref, wout_ref, lng_ref, lnb_ref)
    y_ref[...] = y.reshape(bs, t, D_MODEL)


def _sample_call(x, pool_in, s_in, win, wmap, pscale, ba, ng, wout, lng, lnb, start_pos):
    nb, t, _ = x.shape
    bs = SAMPLE_SEQS
    const2 = lambda i: (0, 0)
    seq3 = lambda i: (i, 0, 0)
    hist3 = lambda i: (0, i, 0)
    return pl.pallas_call(
        functools.partial(_sample_body, bs=bs, t=t, start_pos=start_pos),
        grid=(nb // bs,),
        in_specs=[
            pl.BlockSpec((bs, t, D_MODEL), seq3),
            pl.BlockSpec((POOL_BUF, bs, POOL_WIDTH), hist3),
            pl.BlockSpec((bs, GLA_KEY_WIDTH, GLA_DV), seq3),
            pl.BlockSpec(win.shape, const2),
            pl.BlockSpec(wmap.shape, lambda i: (0, 0, 0)),
            pl.BlockSpec(pscale.shape, const2),
            pl.BlockSpec(ba.shape, const2),
            pl.BlockSpec(ng.shape, const2),
            pl.BlockSpec(wout.shape, const2),
            pl.BlockSpec(lng.shape, const2),
            pl.BlockSpec(lnb.shape, const2),
        ],
        out_specs=[
            pl.BlockSpec((bs, t, D_MODEL), seq3),
            pl.BlockSpec((POOL_BUF, bs, POOL_WIDTH), hist3),
            pl.BlockSpec((bs, GLA_KEY_WIDTH, GLA_DV), seq3),
        ],
        out_shape=[
            jax.ShapeDtypeStruct((nb, t, D_MODEL), F32),
            jax.ShapeDtypeStruct((POOL_BUF, nb, POOL_WIDTH), F32),
            jax.ShapeDtypeStruct((nb, GLA_KEY_WIDTH, GLA_DV), F32),
        ],
        scratch_shapes=[pltpu.VMEM((len(POOL_WINDOWS), bs * t, POOL_GROUP_DIM), F32),
                        pltpu.VMEM((len(POOL_WINDOWS), bs * t, POOL_GROUP_DIM), F32)],
        compiler_params=pltpu.CompilerParams(
            dimension_semantics=("arbitrary",), vmem_limit_bytes=VMEM_LIMIT),
        name="sample_layer",
    )(x, pool_in, s_in, win, wmap, pscale, ba, ng, wout, lng, lnb)


def _prep_body(wt_ref, wfat_ref, wa2_ref, wout_f32_ref, wmap_f32_ref, win_ref, wout_ref, wmap_ref, *, n_plain):
    i = pl.program_id(0)

    @pl.when(i == 0)
    def _():
        wout_ref[:, 0:D_MODEL] = wout_f32_ref[...].astype(BF16)
        wout_ref[:, D_MODEL:] = jnp.zeros((D_MODEL, PITCH_PAD), BF16)
        for p in range(len(POOL_WINDOWS) // 2):
            wmap_ref[p] = _pair_block_diag(wmap_f32_ref[2 * p].astype(BF16), wmap_f32_ref[2 * p + 1].astype(BF16))

    @pl.when(i < n_plain)
    def _():
        win_ref[...] = jnp.transpose(wt_ref[...]).astype(BF16)

    @pl.when(i == n_plain)
    def _():
        wfat = wfat_ref[...]
        pad = jnp.zeros((POOL_GROUP_DIM - GLA_RANK, D_MODEL), F32)
        wfa = jnp.transpose(jnp.concatenate([wfat, pad], axis=0))
        wa2 = wa2_ref[...]
        acc = wfa[:, 0:1] * wa2[0:1, :]
        for r in range(1, GLA_RANK):
            acc = acc + wfa[:, r:r + 1] * wa2[r:r + 1, :]
        win_ref[:, 0:GLA_KEY_WIDTH] = acc.astype(BF16)
        win_ref[:, GLA_KEY_WIDTH:] = jnp.zeros((D_MODEL, PREP_COLS - GLA_KEY_WIDTH), BF16)


def _prep_weights(wt, wa2, wout_f32, wmap_f32):
    assert C_GL % PREP_COLS == 0 and C_END + PITCH_PAD - C_GL <= PREP_COLS and C_GL % GLA_RANK == 0
    n_plain = C_GL // PREP_COLS
    n_pairs = len(POOL_WINDOWS) // 2
    return pl.pallas_call(
        functools.partial(_prep_body, n_plain=n_plain),
        grid=(n_plain + 1,),
        in_specs=[
            pl.BlockSpec((PREP_COLS, D_MODEL), lambda i: (jnp.minimum(i, n_plain - 1), 0)),
            pl.BlockSpec((GLA_RANK, D_MODEL), lambda i: (C_GL // GLA_RANK, 0)),
            pl.BlockSpec(wa2.shape, lambda i: (0, 0)),
            pl.BlockSpec(wout_f32.shape, lambda i: (0, 0)),
            pl.BlockSpec(wmap_f32.shape, lambda i: (0, 0, 0)),
        ],
        out_specs=[
            pl.BlockSpec((D_MODEL, PREP_COLS), lambda i: (0, i)),
            pl.BlockSpec((D_MODEL, D_MODEL + PITCH_PAD), lambda i: (0, 0)),
            pl.BlockSpec((n_pairs, 2 * POOL_GROUP_DIM, 2 * POOL_GROUP_DIM), lambda i: (0, 0, 0)),
        ],
        out_shape=[
            jax.ShapeDtypeStruct((D_MODEL, C_END + PITCH_PAD), BF16),
            jax.ShapeDtypeStruct((D_MODEL, D_MODEL + PITCH_PAD), BF16),
            jax.ShapeDtypeStruct((n_pairs, 2 * POOL_GROUP_DIM, 2 * POOL_GROUP_DIM), BF16),
        ],
        compiler_params=pltpu.CompilerParams(dimension_semantics=("arbitrary",), vmem_limit_bytes=VMEM_LIMIT),
        name="prep_weights",
    )(wt, wt, wa2, wout_f32, wmap_f32)


def kernel(x_prompt, x_sample, state_pool, state_gla, w_in, w_pool_map, pool_scale, w_gla_a2, b_gla_a,
           gla_norm_g, w_out, ln_g, ln_b):
    assert w_in.shape[0] == DEPTH
    win, wout, wmap = _prep_weights(jnp.transpose(w_in[0]), w_gla_a2[0], w_out[0], w_pool_map[0])
    pscale = pool_scale[0].reshape(1, POOL_WIDTH)
    ba = b_gla_a[0].reshape(1, GLA_KEY_WIDTH)
    ng = gla_norm_g[0].reshape(1, GLA_WIDTH)
    lng = ln_g[0].reshape(1, D_MODEL)
    lnb = ln_b[0].reshape(1, D_MODEL)
    params = (win, wmap, pscale, ba, ng, wout, lng, lnb)

    y_p, pool_p, gla_p = _prompt_call(x_prompt, *params)
    nsamp = x_sample.shape[0]
    y_s, pool_s, gla_s = _sample_call(
        x_sample, jnp.transpose(state_pool[0], (1, 0, 2)),
        state_gla[0].reshape(nsamp, GLA_KEY_WIDTH, GLA_DV), *params, start_pos=PAST_LEN)

    nb = x_prompt.shape[0]
    return (y_p, y_s, pool_p[None],
            gla_p.reshape(1, nb, GLA_HEADS, GLA_DK, GLA_DV),
            jnp.transpose(pool_s, (1, 0, 2))[None],
            gla_s.reshape(1, nsamp, GLA_HEADS, GLA_DK, GLA_DV))
```

```python
import functools
import itertools

import jax
import jax.numpy as jnp
from jax import lax
from jax.experimental import pallas as pl
from jax.experimental.pallas import tpu as pltpu

F32 = jnp.float32
BF16 = jnp.bfloat16

D_MODEL = 1024
POOL_WIDTH = 512
POOL_WINDOWS = (2, 4, 8, 16)
POOL_GROUP_DIM = 128
POOL_BUF = 15
HIST_ROWS = 16
GLA_HEADS = 4
GLA_DK = 64
GLA_DV = 128
GLA_KEY_WIDTH = GLA_HEADS * GLA_DK
GLA_WIDTH = GLA_HEADS * GLA_DV
GLA_RANK = 16
GLA_TAU = 16.0
LN_EPS = 1e-5
RMS_EPS = 1e-6
DEPTH = 1
DEEPNORM_ALPHA = (2.0 * DEPTH) ** 0.25
PAST_LEN = 16384

C_U, C_GP, C_Q, C_K, C_V, C_GG, C_GL, C_END = 0, 512, 1024, 1280, 1536, 2048, 2560, 2816

MXU_COLS = 256
PROMPT_TILE = 256
PITCH_PAD = 128
PROMPT_SUBTILE = 256
SUBTILE_TAIL_UNITS = 0
PUMP_ROWS = 256
PUMP_PLAN = {"pool": (0, 1, 0, 1), "cumsum": 0, "scores": (1, 1, 1, 1), "level": (0, 0, 0, 1, 0, 0),
             "state": (0, 0, 0, 0), "epilogue": 1, "layernorm": (1, 1, 1, 0)}
SAFE_LOG_DECAY = 60.0
GLA_CHUNK = 64
SAMPLE_SEQS = 16
PREP_COLS = 512
VMEM_LIMIT = 56 * 1024 * 1024


def _dot(a, b):
    return jnp.dot(a, b, preferred_element_type=F32)


def _dot_nt(a, b):
    return lax.dot_general(a, b, (((1,), (1,)), ((), ())), preferred_element_type=F32)


def _dot_tn(a, b):
    return lax.dot_general(a, b, (((0,), (0,)), ((), ())), preferred_element_type=F32)


def _silu(x):
    return x * (1.0 / (1.0 + jnp.exp(-x)))


def _log_sigmoid(x):
    return jnp.minimum(x, 0.0) - jnp.log1p(jnp.exp(-jnp.abs(x)))


def _split_bf16(x):
    hi = x.astype(BF16)
    lo = (x - hi.astype(F32)).astype(BF16)
    return hi, lo


def _epilogue(x, pool_act, o, gg, ng_ref, wout_ref, lng_ref, lnb_ref, pump=lambda n=None: None, tail=None,
              row_blocks=1):
    pump(PUMP_PLAN["epilogue"])
    normed = []
    for h in range(GLA_HEADS):
        oh = o[:, h * GLA_DV:(h + 1) * GLA_DV]
        ms = jnp.mean(oh * oh, axis=-1, keepdims=True)
        normed.append(oh * lax.rsqrt(ms + RMS_EPS))
    gla_act = jnp.concatenate(normed, axis=1) * ng_ref[...] * _silu(gg)
    mix = jnp.concatenate([pool_act, gla_act], axis=1).astype(BF16)
    y = _dot(mix, wout_ref[:, 0:D_MODEL])
    n = x.shape[0] // row_blocks
    outs = []
    for r in range(row_blocks):
        rows = slice(r * n, (r + 1) * n)
        pump(PUMP_PLAN["layernorm"][r])
        h = DEEPNORM_ALPHA * x[rows] + y[rows]
        mu = jnp.mean(h, axis=-1, keepdims=True)
        d = h - mu
        var = jnp.mean(d * d, axis=-1, keepdims=True)
        outs.append(d * lax.rsqrt(var + LN_EPS) * lng_ref[...] + lnb_ref[...])
    pump(tail)
    return outs[0] if row_blocks == 1 else jnp.concatenate(outs, axis=0)


def _head_block_mask(rows_per_head, cols_per_head, shape):
    r = lax.broadcasted_iota(jnp.int32, shape, 0) // rows_per_head
    c = lax.broadcasted_iota(jnp.int32, shape, 1) // cols_per_head
    return r == c


def _score_masks(rows, n_levels):
    shape = (rows, GLA_HEADS * rows)
    t_idx = lax.broadcasted_iota(jnp.int32, shape, 0)
    s_idx = lax.broadcasted_iota(jnp.int32, shape, 1) % rows
    masks = [t_idx == s_idx]
    for lvl in range(n_levels):
        half = 1 << lvl
        blk = 2 * half
        masks.append((t_idx // blk == s_idx // blk) & (t_idx % blk >= half) & (s_idx % blk < half))
    return masks


def _decayed_scores(q, k, la, b, masks, head_blk, pump=lambda n=None: None):
    rows = q.shape[0]
    r_idx = lax.broadcasted_iota(jnp.int32, q.shape, 0)

    def scores(qt, kt):
        kh = jnp.where(head_blk, jnp.concatenate([kt.astype(BF16)] * GLA_HEADS, axis=0), jnp.zeros((), BF16))
        return _dot_nt(qt.astype(BF16), kh)

    acc = jnp.where(masks[0], scores(q, k), 0.0)
    for lvl in range(len(masks) - 1):
        half = 1 << lvl
        blk = 2 * half
        if lvl == 0:
            neg = jnp.where(r_idx % 2 == 1, la, 0.0)
        elif lvl == 1:
            r4 = r_idx % 4
            nxt = pltpu.roll(la, rows - 1, 0)
            prv = pltpu.roll(la, 1, 0)
            neg = jnp.where(r4 == 0, nxt, jnp.where(r4 == 1, 0.0, jnp.where(r4 == 2, la, la + prv)))
        else:
            ref = jnp.concatenate(
                [jnp.broadcast_to(b[j * blk + half - 1:j * blk + half, :], (blk, b.shape[1]))
                 for j in range(rows // blk)], axis=0)
            d = b - ref
            neg = jnp.minimum(d, -d)
        w = jnp.exp(neg)
        acc = jnp.where(masks[lvl + 1], scores(q * w, k * w), acc)
        pump(PUMP_PLAN["level"][lvl])
    return acc.astype(BF16)


def _pair_block_diag(a0, a1):
    z = jnp.zeros_like(a0)
    return jnp.concatenate(
        [jnp.concatenate([a0, z], axis=1), jnp.concatenate([z, a1], axis=1)], axis=0)


def _make_pump(units, xb_ref, win_ref, proj_ref):
    units = iter(units)

    def pump(n=None):
        for r, lo in (units if n is None else itertools.islice(units, n)):
            proj_ref[r:r + PUMP_ROWS, lo:lo + MXU_COLS] = _dot(
                xb_ref[r:r + PUMP_ROWS, 0:D_MODEL], win_ref[:, lo:lo + MXU_COLS])

    return pump


def _naive_scores(q, k, b, causal, head_blk):
    ks = (k * jnp.exp(-b)).astype(BF16)
    kh = jnp.where(head_blk, jnp.concatenate([ks] * GLA_HEADS, axis=0), jnp.zeros((), BF16))
    return jnp.where(causal, _dot_nt((q * jnp.exp(b)).astype(BF16), kh), 0.0).astype(BF16)


def _prompt_stage_b(pos0, proj_ref, xkeep_ref, la_ref, wmap_ref, pscale_ref, ba_ref, ng_ref, wout_ref, lng_ref,
                    lnb_ref, y_ref, ext_ref, s_ref, pump, next_decay, *, tc, robust, tail):
    u = proj_ref[:, C_U:C_GP]
    ext_ref[HIST_ROWS:HIST_ROWS + tc, 0:POOL_WIDTH] = u
    pos16 = pos0 + lax.broadcasted_iota(jnp.int32, (HIST_ROWS, POOL_GROUP_DIM), 0)
    mixed = []
    for g, w in enumerate(POOL_WINDOWS):
        sl = slice(g * POOL_GROUP_DIM, (g + 1) * POOL_GROUP_DIM)
        e = ext_ref[:, sl]
        s = e
        sh = 1
        while sh < w:
            s = s + pltpu.roll(s, sh, 0)
            sh *= 2
        wsum = s[HIST_ROWS:]
        self_rows = e[HIST_ROWS:]
        cnt = jnp.minimum(pos16 + 1, w).astype(F32)
        first = wsum[:HIST_ROWS] / cnt - self_rows[:HIST_ROWS]
        rest = wsum[HIST_ROWS:] * (1.0 / w) - self_rows[HIST_ROWS:]
        pooled = jnp.concatenate([first, rest], axis=0)
        mixed.append(_dot(pooled.astype(BF16), wmap_ref[g]))
        pump(PUMP_PLAN["pool"][g])
    pool_act = jnp.concatenate(mixed, axis=1) * pscale_ref[...] * _silu(proj_ref[:, C_GP:C_Q])
    next_decay()

    cc = GLA_CHUNK
    n_chunks = tc // cc
    rr = lax.broadcasted_iota(jnp.int32, (tc, tc), 0)
    cl = lax.broadcasted_iota(jnp.int32, (tc, tc), 1)
    ltri = jnp.where((rr // cc == cl // cc) & (cl <= rr), 1.0, 0.0).astype(BF16)
    masks = _score_masks(cc, cc.bit_length() - 1 if robust else 0)
    causal = (lax.broadcasted_iota(jnp.int32, (cc, GLA_HEADS * cc), 1) % cc
              <= lax.broadcasted_iota(jnp.int32, (cc, GLA_HEADS * cc), 0))
    blk = _head_block_mask(cc, GLA_DK, (GLA_HEADS * cc, GLA_KEY_WIDTH))

    hi, lo = _split_bf16(la_ref[:, 0:GLA_KEY_WIDTH])
    b_all = _dot(ltri, hi) + _dot(ltri, lo)
    pump(PUMP_PLAN["cumsum"])
    lhs_list, v_list, upd_list, dcol_list = [], [], [], []
    for c in range(n_chunks):
        rs = slice(c * cc, (c + 1) * cc)
        b = b_all[rs]
        b_end = b[cc - 1:cc, :]
        q_c = proj_ref[rs, C_Q:C_K] * (GLA_DK ** -0.5)
        k_c = proj_ref[rs, C_K:C_V]
        qs = (q_c * jnp.exp(b)).astype(BF16)
        kd = (k_c * jnp.exp(b_end - b)).astype(BF16)
        if robust:
            a_all = _decayed_scores(q_c, k_c, la_ref[rs, 0:GLA_KEY_WIDTH], b, masks, blk, pump)
        else:
            a_all = _naive_scores(q_c, k_c, b, causal, blk)
            pump(PUMP_PLAN["scores"][c])
        vc = proj_ref[rs, C_V:C_GG].astype(BF16)
        upds = []
        for p in range(GLA_HEADS // 2):
            upd = _dot_tn(kd[:, p * 2 * GLA_DK:(p + 1) * 2 * GLA_DK], vc[:, p * 2 * GLA_DV:(p + 1) * 2 * GLA_DV])
            upds.append(upd[0:GLA_DK, 0:GLA_DV])
            upds.append(upd[GLA_DK:2 * GLA_DK, GLA_DV:2 * GLA_DV])
        lhs_list.append([jnp.concatenate([a_all[:, p * 2 * cc:(p + 1) * 2 * cc],
                                          qs[:, p * 2 * GLA_DK:(p + 1) * 2 * GLA_DK]], axis=1)
                         for p in range(GLA_HEADS // 2)])
        v_list.append(vc)
        upd_list.append(jnp.concatenate(upds, axis=0))
        dcol_list.append(jnp.transpose(jnp.broadcast_to(jnp.exp(b_end), (GLA_DV, GLA_KEY_WIDTH))))

    o_chunks = []
    state = s_ref[...]
    for c in range(n_chunks):
        s_bf = state.astype(BF16)
        vc = v_list[c]
        outs = []
        for p in range(GLA_HEADS // 2):
            h0, h1 = 2 * p, 2 * p + 1
            rhs = jnp.concatenate([
                _pair_block_diag(vc[:, h0 * GLA_DV:(h0 + 1) * GLA_DV], vc[:, h1 * GLA_DV:(h1 + 1) * GLA_DV]),
                _pair_block_diag(s_bf[h0 * GLA_DK:(h0 + 1) * GLA_DK], s_bf[h1 * GLA_DK:(h1 + 1) * GLA_DK]),
            ], axis=0)
            outs.append(_dot(lhs_list[c][p], rhs))
        o_chunks.append(jnp.concatenate(outs, axis=1))
        state = state * dcol_list[c] + upd_list[c]
        pump(PUMP_PLAN["state"][c])
    s_ref[...] = state
    o = jnp.concatenate(o_chunks, axis=0)

    y_ref[...] = _epilogue(xkeep_ref[:, 0:D_MODEL], pool_act, o, proj_ref[:, C_GG:C_GL], ng_ref, wout_ref, lng_ref, lnb_ref,
                           pump, tail, row_blocks=4)


def _prompt_body(x_ref, win_ref, wmap_ref, pscale_ref, ba_ref, ng_ref, wout_ref, lng_ref, lnb_ref,
                 y_ref, pool_ref, s_out_ref,
                 proj0_ref, proj1_ref, xkeep0_ref, xkeep1_ref, la0_ref, la1_ref, xb_ref, ext_ref, s_ref, steep_ref,
                 *, tc, sub, nj):
    i = pl.program_id(0)
    jb = (i + nj - 1) % nj

    @pl.when(i == 0)
    def _():
        proj1_ref[...] = jnp.zeros_like(proj1_ref)
        xkeep1_ref[...] = jnp.zeros_like(xkeep1_ref)
        la1_ref[...] = jnp.zeros_like(la1_ref)
        steep_ref[0] = 0
        pool_ref[...] = jnp.zeros_like(pool_ref)

    @pl.when((i == 0) | (jb == 0))
    def _():
        ext_ref[0:HIST_ROWS, 0:POOL_WIDTH] = jnp.zeros((HIST_ROWS, POOL_WIDTH), F32)
        s_ref[...] = jnp.zeros_like(s_ref)

    units = [(r, lo) for lo in (C_GL, *range(0, C_GL, MXU_COLS)) for r in range(0, tc, PUMP_ROWS)]

    def step(proj_a, xkeep_a, la_a, proj_b, xkeep_b, la_b, robust):
        x = x_ref[0]
        xkeep_a[:, 0:D_MODEL] = x
        xb_ref[:, 0:D_MODEL] = x.astype(BF16)
        pump = _make_pump(units, xb_ref, win_ref, proj_a)
        pump(tc // PUMP_ROWS)

        def next_decay():
            la = _log_sigmoid(proj_a[:, C_GL:C_END] + ba_ref[...]) * (1.0 / GLA_TAU)
            la_a[:, 0:GLA_KEY_WIDTH] = la
            chunk_decay = [jnp.sum(la[c * GLA_CHUNK:(c + 1) * GLA_CHUNK], axis=0, keepdims=True)
                           for c in range(tc // GLA_CHUNK)]
            lowest = jnp.min(functools.reduce(jnp.minimum, chunk_decay))
            steep_ref[0] = jnp.where(lowest < -SAFE_LOG_DECAY, 1, 0).astype(jnp.int32)

        n_sub = tc // sub
        for k in range(n_sub):
            rows = slice(k * sub, (k + 1) * sub)
            last = k == n_sub - 1
            _prompt_stage_b((jb * n_sub + k) * sub, proj_b.at[rows, :], xkeep_b.at[rows, :], la_b.at[rows, :],
                            wmap_ref, pscale_ref, ba_ref, ng_ref, wout_ref, lng_ref, lnb_ref,
                            y_ref.at[0, rows, :], ext_ref, s_ref, pump, next_decay if k == 0 else (lambda: None),
                            tc=sub, robust=robust, tail=None if last else SUBTILE_TAIL_UNITS)
            if not last:
                ext_ref[0:HIST_ROWS, 0:POOL_WIDTH] = ext_ref[sub:sub + HIST_ROWS, 0:POOL_WIDTH]

    set0 = (proj0_ref, xkeep0_ref, la0_ref)
    set1 = (proj1_ref, xkeep1_ref, la1_ref)
    steep = steep_ref[0] != 0
    for parity, (set_a, set_b) in enumerate(((set0, set1), (set1, set0))):
        for robust in (False, True):
            @pl.when((i % 2 == parity) & (steep == robust))
            def _(set_a=set_a, set_b=set_b, robust=robust):
                step(*set_a, *set_b, robust)

    @pl.when((i > 0) & (jb == nj - 1))
    def _():
        mine = lax.broadcasted_iota(jnp.int32, pool_ref.shape[1:], 0) == (i - 1) // nj
        for r in range(POOL_BUF):
            row = ext_ref[sub + 1 + r:sub + 2 + r, 0:POOL_WIDTH]
            pool_ref[r] = jnp.where(mine, jnp.broadcast_to(row, pool_ref.shape[1:]), pool_ref[r])
        s_out_ref[0] = s_ref[...]

    ext_ref[0:HIST_ROWS, 0:POOL_WIDTH] = ext_ref[sub:sub + HIST_ROWS, 0:POOL_WIDTH]


def _prompt_call(x, win, wmap, pscale, ba, ng, wout, lng, lnb):
    nb, t, _ = x.shape
    tc = PROMPT_TILE
    nj = t // tc
    ntiles = nb * nj
    const2 = lambda i: (0, 0)

    def tile_a(i):
        ia = jnp.minimum(i, ntiles - 1)
        return (ia // nj, ia % nj, 0)

    def tile_b(i):
        ib = jnp.maximum(i - 1, 0)
        return (ib // nj, ib % nj, 0)

    def seq_b(i):
        return (jnp.maximum(i - 1, 0) // nj, 0, 0)

    return pl.pallas_call(
        functools.partial(_prompt_body, tc=tc, sub=PROMPT_SUBTILE, nj=nj),
        grid=(ntiles + 1,),
        in_specs=[
            pl.BlockSpec((1, tc, D_MODEL), tile_a),
            pl.BlockSpec(win.shape, const2),
            pl.BlockSpec(wmap.shape, lambda i: (0, 0, 0)),
            pl.BlockSpec(pscale.shape, const2),
            pl.BlockSpec(ba.shape, const2),
            pl.BlockSpec(ng.shape, const2),
            pl.BlockSpec(wout.shape, const2),
            pl.BlockSpec(lng.shape, const2),
            pl.BlockSpec(lnb.shape, const2),
        ],
        out_specs=[
            pl.BlockSpec((1, tc, D_MODEL), tile_b),
            pl.BlockSpec((POOL_BUF, nb, POOL_WIDTH), lambda i: (0, 0, 0)),
            pl.BlockSpec((1, GLA_KEY_WIDTH, GLA_DV), seq_b),
        ],
        out_shape=[
            jax.ShapeDtypeStruct((nb, t, D_MODEL), F32),
            jax.ShapeDtypeStruct((POOL_BUF, nb, POOL_WIDTH), F32),
            jax.ShapeDtypeStruct((nb, GLA_KEY_WIDTH, GLA_DV), F32),
        ],
        scratch_shapes=[
            pltpu.VMEM((tc, C_END + PITCH_PAD), F32),
            pltpu.VMEM((tc, C_END + PITCH_PAD), F32),
            pltpu.VMEM((tc, D_MODEL + PITCH_PAD), F32),
            pltpu.VMEM((tc, D_MODEL + PITCH_PAD), F32),
            pltpu.VMEM((tc, GLA_KEY_WIDTH + PITCH_PAD), F32),
            pltpu.VMEM((tc, GLA_KEY_WIDTH + PITCH_PAD), F32),
            pltpu.VMEM((tc, D_MODEL + PITCH_PAD), BF16),
            pltpu.VMEM((HIST_ROWS + PROMPT_SUBTILE, POOL_WIDTH + PITCH_PAD), F32),
            pltpu.VMEM((GLA_KEY_WIDTH, GLA_DV), F32),
            pltpu.SMEM((1,), jnp.int32),
        ],
        compiler_params=pltpu.CompilerParams(
            dimension_semantics=("arbitrary",), vmem_limit_bytes=VMEM_LIMIT),
        name="prompt_layer",
    )(x, win, wmap, pscale, ba, ng, wout, lng, lnb)


def _sample_body(x_ref, pool_in_ref, s_in_ref, win_ref, wmap_ref, pscale_ref, ba_ref, ng_ref, wout_ref,
                 lng_ref, lnb_ref, y_ref, pool_ref, s_out_ref, u_ref, pooled_ref, *, bs, t, start_pos):
    rows = bs * t
    x = x_ref[...].reshape(rows, D_MODEL)
    xb = x.astype(BF16)

    def proj(lo, hi):
        return _dot(xb, win_ref[:, lo:hi])

    u = proj(C_U, C_GP)
    mixed = []
    for g, w in enumerate(POOL_WINDOWS):
        sl = slice(g * POOL_GROUP_DIM, (g + 1) * POOL_GROUP_DIM)
        u_ref[g] = u[:, sl]
        ext = [pool_in_ref[r, :, sl] for r in range(POOL_BUF)]
        ext += [u_ref[g, pl.ds(tok, bs, stride=t), :] for tok in range(t)]
        for r in range(POOL_BUF):
            pool_ref[r, :, sl] = ext[t + r]
        sums = ext
        sh = 1
        while sh < w:
            first = 2 * sh - 1
            sums = [None] * first + [sums[r] + sums[r - sh] for r in range(first, len(sums))]
            sh *= 2
        for tok in range(t):
            cnt = float(min(start_pos + tok + 1, w))
            pooled_ref[g, pl.ds(tok, bs, stride=t), :] = sums[POOL_BUF + tok] / cnt - ext[POOL_BUF + tok]
        mixed.append(_dot(pooled_ref[g].astype(BF16), wmap_ref[g]))
    gp = proj(C_GP, C_Q)
    pool_act = jnp.concatenate(mixed, axis=1) * pscale_ref[...] * _silu(gp)

    q = proj(C_Q, C_K) * (GLA_DK ** -0.5)
    k = proj(C_K, C_V)
    vb = proj(C_V, C_GG).astype(BF16)
    gg = proj(C_GG, C_GL)
    la = _log_sigmoid(proj(C_GL, C_END) + ba_ref[...]) * (1.0 / GLA_TAU)

    rr = lax.broadcasted_iota(jnp.int32, (rows, rows), 0)
    cl = lax.broadcasted_iota(jnp.int32, (rows, rows), 1)
    same = (rr // t) == (cl // t)
    ltri = jnp.where(same & (cl <= rr), 1.0, 0.0).astype(BF16)
    lall = jnp.where(same, 1.0, 0.0).astype(BF16)
    hi, lo = _split_bf16(la)
    b = _dot(ltri, hi) + _dot(ltri, lo)
    b_end = _dot(lall, hi) + _dot(lall, lo)
    qs = (q * jnp.exp(b)).astype(BF16)
    kd = (k * jnp.exp(b_end - b)).astype(BF16)

    assert t & (t - 1) == 0, "tokens per sequence must be a power of two"
    blk = _head_block_mask(rows, GLA_DK, (GLA_HEADS * rows, GLA_KEY_WIDTH))
    a_all = _decayed_scores(q, k, la, b, _score_masks(rows, t.bit_length() - 1), blk)

    s_old = s_in_ref[...]
    s_bf = s_old.astype(BF16)
    qs3 = qs.reshape(bs, t, GLA_KEY_WIDTH)
    kd3 = kd.reshape(bs, t, GLA_KEY_WIDTH)
    v3 = vb.reshape(bs, t, GLA_WIDTH)
    bdims_nn = (((2,), (1,)), ((0,), (0,)))
    bdims_tn = (((1,), (1,)), ((0,), (0,)))
    assert bs <= GLA_DV
    seq_sel = jnp.where(lax.broadcasted_iota(jnp.int32, (rows, GLA_DV), 0) // t
                        == lax.broadcasted_iota(jnp.int32, (rows, GLA_DV), 1), 1.0, 0.0).astype(BF16)
    decay_t = jnp.exp(_dot_tn(hi, seq_sel) + _dot_tn(lo, seq_sel))
    decay = jnp.stack([jnp.broadcast_to(decay_t[:, j:j + 1], (GLA_KEY_WIDTH, GLA_DV)) for j in range(bs)], axis=0)
    outs, upds = [], []
    for p in range(GLA_HEADS // 2):
        h0, h1 = 2 * p, 2 * p + 1
        intra = _dot(a_all[:, p * 2 * rows:(p + 1) * 2 * rows],
                     _pair_block_diag(vb[:, h0 * GLA_DV:(h0 + 1) * GLA_DV], vb[:, h1 * GLA_DV:(h1 + 1) * GLA_DV]))
        s0 = s_bf[:, h0 * GLA_DK:(h0 + 1) * GLA_DK, :]
        s1 = s_bf[:, h1 * GLA_DK:(h1 + 1) * GLA_DK, :]
        z = jnp.zeros_like(s0)
        s_pair = jnp.concatenate(
            [jnp.concatenate([s0, z], axis=2), jnp.concatenate([z, s1], axis=2)], axis=1)
        inter = lax.dot_general(qs3[:, :, p * 2 * GLA_DK:(p + 1) * 2 * GLA_DK], s_pair, bdims_nn,
                                preferred_element_type=F32)
        outs.append(intra + inter.reshape(rows, 2 * GLA_DV))
        upd = lax.dot_general(kd3[:, :, p * 2 * GLA_DK:(p + 1) * 2 * GLA_DK],
                              v3[:, :, p * 2 * GLA_DV:(p + 1) * 2 * GLA_DV], bdims_tn,
                              preferred_element_type=F32)
        upds.append(upd[:, 0:GLA_DK, 0:GLA_DV])
        upds.append(upd[:, GLA_DK:2 * GLA_DK, GLA_DV:2 * GLA_DV])
    o = jnp.concatenate(outs, axis=1)
    s_out_ref[...] = s_old * decay + jnp.concatenate(upds, axis=1)

    y = _epilogue(x, pool_act, o, gg, ng_ref, wout_ref, lng_ref, lnb_ref)
    y_ref[...] = y.reshape(bs, t, D_MODEL)


def _sample_call(x, pool_in, s_in, win, wmap, pscale, ba, ng, wout, lng, lnb, start_pos):
    nb, t, _ = x.shape
    bs = SAMPLE_SEQS
    const2 = lambda i: (0, 0)
    seq3 = lambda i: (i, 0, 0)
    hist3 = lambda i: (0, i, 0)
    return pl.pallas_call(
        functools.partial(_sample_body, bs=bs, t=t, start_pos=start_pos),
        grid=(nb // bs,),
        in_specs=[
            pl.BlockSpec((bs, t, D_MODEL), seq3),
            pl.BlockSpec((POOL_BUF, bs, POOL_WIDTH), hist3),
            pl.BlockSpec((bs, GLA_KEY_WIDTH, GLA_DV), seq3),
            pl.BlockSpec(win.shape, const2),
            pl.BlockSpec(wmap.shape, lambda i: (0, 0, 0)),
            pl.BlockSpec(pscale.shape, const2),
            pl.BlockSpec(ba.shape, const2),
            pl.BlockSpec(ng.shape, const2),
            pl.BlockSpec(wout.shape, const2),
            pl.BlockSpec(lng.shape, const2),
            pl.BlockSpec(lnb.shape, const2),
        ],
        out_specs=[
            pl.BlockSpec((bs, t, D_MODEL), seq3),
            pl.BlockSpec((POOL_BUF, bs, POOL_WIDTH), hist3),
            pl.BlockSpec((bs, GLA_KEY_WIDTH, GLA_DV), seq3),
        ],
        out_shape=[
            jax.ShapeDtypeStruct((nb, t, D_MODEL), F32),
            jax.ShapeDtypeStruct((POOL_BUF, nb, POOL_WIDTH), F32),
            jax.ShapeDtypeStruct((nb, GLA_KEY_WIDTH, GLA_DV), F32),
        ],
        scratch_shapes=[pltpu.VMEM((len(POOL_WINDOWS), bs * t, POOL_GROUP_DIM), F32),
                        pltpu.VMEM((len(POOL_WINDOWS), bs * t, POOL_GROUP_DIM), F32)],
        compiler_params=pltpu.CompilerParams(
            dimension_semantics=("arbitrary",), vmem_limit_bytes=VMEM_LIMIT),
        name="sample_layer",
    )(x, pool_in, s_in, win, wmap, pscale, ba, ng, wout, lng, lnb)


def _prep_body(wt_ref, wfat_ref, wa2_ref, wout_f32_ref, wmap_f32_ref, win_ref, wout_ref, wmap_ref, *, n_plain):
    i = pl.program_id(0)

    @pl.when(i == 0)
    def _():
        wout_ref[:, 0:D_MODEL] = wout_f32_ref[...].astype(BF16)
        wout_ref[:, D_MODEL:] = jnp.zeros((D_MODEL, PITCH_PAD), BF16)
        wmap_ref[...] = wmap_f32_ref[...].astype(BF16)

    @pl.when(i < n_plain)
    def _():
        win_ref[...] = jnp.transpose(wt_ref[...]).astype(BF16)

    @pl.when(i == n_plain)
    def _():
        wfat = wfat_ref[...]
        pad = jnp.zeros((POOL_GROUP_DIM - GLA_RANK, D_MODEL), F32)
        wfa = jnp.transpose(jnp.concatenate([wfat, pad], axis=0))
        wa2 = wa2_ref[...]
        acc = wfa[:, 0:1] * wa2[0:1, :]
        for r in range(1, GLA_RANK):
            acc = acc + wfa[:, r:r + 1] * wa2[r:r + 1, :]
        win_ref[:, 0:GLA_KEY_WIDTH] = acc.astype(BF16)
        win_ref[:, GLA_KEY_WIDTH:] = jnp.zeros((D_MODEL, PREP_COLS - GLA_KEY_WIDTH), BF16)


def _prep_weights(wt, wa2, wout_f32, wmap_f32):
    assert C_GL % PREP_COLS == 0 and C_END + PITCH_PAD - C_GL <= PREP_COLS and C_GL % GLA_RANK == 0
    n_plain = C_GL // PREP_COLS
    return pl.pallas_call(
        functools.partial(_prep_body, n_plain=n_plain),
        grid=(n_plain + 1,),
        in_specs=[
            pl.BlockSpec((PREP_COLS, D_MODEL), lambda i: (jnp.minimum(i, n_plain - 1), 0)),
            pl.BlockSpec((GLA_RANK, D_MODEL), lambda i: (C_GL // GLA_RANK, 0)),
            pl.BlockSpec(wa2.shape, lambda i: (0, 0)),
            pl.BlockSpec(wout_f32.shape, lambda i: (0, 0)),
            pl.BlockSpec(wmap_f32.shape, lambda i: (0, 0, 0)),
        ],
        out_specs=[
            pl.BlockSpec((D_MODEL, PREP_COLS), lambda i: (0, i)),
            pl.BlockSpec((D_MODEL, D_MODEL + PITCH_PAD), lambda i: (0, 0)),
            pl.BlockSpec(wmap_f32.shape, lambda i: (0, 0, 0)),
        ],
        out_shape=[
            jax.ShapeDtypeStruct((D_MODEL, C_END + PITCH_PAD), BF16),
            jax.ShapeDtypeStruct((D_MODEL, D_MODEL + PITCH_PAD), BF16),
            jax.ShapeDtypeStruct(wmap_f32.shape, BF16),
        ],
        compiler_params=pltpu.CompilerParams(dimension_semantics=("arbitrary",), vmem_limit_bytes=VMEM_LIMIT),
        name="prep_weights",
    )(wt, wt, wa2, wout_f32, wmap_f32)


def kernel(x_prompt, x_sample, state_pool, state_gla, w_in, w_pool_map, pool_scale, w_gla_a2, b_gla_a,
           gla_norm_g, w_out, ln_g, ln_b):
    assert w_in.shape[0] == DEPTH
    win, wout, wmap = _prep_weights(jnp.transpose(w_in[0]), w_gla_a2[0], w_out[0], w_pool_map[0])
    pscale = pool_scale[0].reshape(1, POOL_WIDTH)
    ba = b_gla_a[0].reshape(1, GLA_KEY_WIDTH)
    ng = gla_norm_g[0].reshape(1, GLA_WIDTH)
    lng = ln_g[0].reshape(1, D_MODEL)
    lnb = ln_b[0].reshape(1, D_MODEL)
    params = (win, wmap, pscale, ba, ng, wout, lng, lnb)

    y_p, pool_p, gla_p = _prompt_call(x_prompt, *params)
    nsamp = x_sample.shape[0]
    y_s, pool_s, gla_s = _sample_call(
        x_sample, jnp.transpose(state_pool[0], (1, 0, 2)),
        state_gla[0].reshape(nsamp, GLA_KEY_WIDTH, GLA_DV), *params, start_pos=PAST_LEN)

    nb = x_prompt.shape[0]
    return (y_p, y_s, jnp.transpose(pool_p, (1, 0, 2))[None],
            gla_p.reshape(1, nb, GLA_HEADS, GLA_DK, GLA_DV),
            jnp.transpose(pool_s, (1, 0, 2))[None],
            gla_s.reshape(1, nsamp, GLA_HEADS, GLA_DK, GLA_DV))
```

```python
import functools
import itertools

import jax
import jax.numpy as jnp
from jax import lax
from jax.experimental import pallas as pl
from jax.experimental.pallas import tpu as pltpu

F32 = jnp.float32
BF16 = jnp.bfloat16

D_MODEL = 1024
POOL_WIDTH = 512
POOL_WINDOWS = (2, 4, 8, 16)
POOL_GROUP_DIM = 128
POOL_BUF = 15
HIST_ROWS = 16
GLA_HEADS = 4
GLA_DK = 64
GLA_DV = 128
GLA_KEY_WIDTH = GLA_HEADS * GLA_DK
GLA_WIDTH = GLA_HEADS * GLA_DV
GLA_RANK = 16
GLA_TAU = 16.0
LN_EPS = 1e-5
RMS_EPS = 1e-6
DEPTH = 1
DEEPNORM_ALPHA = (2.0 * DEPTH) ** 0.25
PAST_LEN = 16384

C_U, C_GP, C_Q, C_K, C_V, C_GG, C_GL, C_END = 0, 512, 1024, 1280, 1536, 2048, 2560, 2816

MXU_COLS = 256
PROMPT_TILE = 256
PITCH_PAD = 128
PROMPT_SUBTILE = 256
SUBTILE_TAIL_UNITS = 0
PUMP_ROWS = 256
PUMP_PLAN = {"pool": (0, 1, 0, 1), "cumsum": 0, "scores": (1, 1, 1, 1), "level": (0, 0, 0, 1, 0, 0),
             "state": (0, 0, 0, 0), "epilogue": 1, "layernorm": (1, 1, 1, 0)}
SAFE_LOG_DECAY = 60.0
GLA_CHUNK = 64
SAMPLE_SEQS = 16
PREP_COLS = 512
VMEM_LIMIT = 56 * 1024 * 1024


def _dot(a, b):
    return jnp.dot(a, b, preferred_element_type=F32)


def _dot_nt(a, b):
    return lax.dot_general(a, b, (((1,), (1,)), ((), ())), preferred_element_type=F32)


def _dot_tn(a, b):
    return lax.dot_general(a, b, (((0,), (0,)), ((), ())), preferred_element_type=F32)


def _silu(x):
    return x * (1.0 / (1.0 + jnp.exp(-x)))


def _log_sigmoid(x):
    return jnp.minimum(x, 0.0) - jnp.log1p(jnp.exp(-jnp.abs(x)))


def _split_bf16(x):
    hi = x.astype(BF16)
    lo = (x - hi.astype(F32)).astype(BF16)
    return hi, lo


def _epilogue(x, pool_act, o, gg, ng_ref, wout_ref, lng_ref, lnb_ref, pump=lambda n=None: None, tail=None,
              row_blocks=1):
    pump(PUMP_PLAN["epilogue"])
    normed = []
    for h in range(GLA_HEADS):
        oh = o[:, h * GLA_DV:(h + 1) * GLA_DV]
        ms = jnp.mean(oh * oh, axis=-1, keepdims=True)
        normed.append(oh * lax.rsqrt(ms + RMS_EPS))
    gla_act = jnp.concatenate(normed, axis=1) * ng_ref[...] * _silu(gg)
    mix = jnp.concatenate([pool_act, gla_act], axis=1).astype(BF16)
    y = _dot(mix, wout_ref[:, 0:D_MODEL])
    n = x.shape[0] // row_blocks
    outs = []
    for r in range(row_blocks):
        rows = slice(r * n, (r + 1) * n)
        pump(PUMP_PLAN["layernorm"][r])
        h = DEEPNORM_ALPHA * x[rows] + y[rows]
        mu = jnp.mean(h, axis=-1, keepdims=True)
        d = h - mu
        var = jnp.mean(d * d, axis=-1, keepdims=True)
        outs.append(d * lax.rsqrt(var + LN_EPS) * lng_ref[...] + lnb_ref[...])
    pump(tail)
    return outs[0] if row_blocks == 1 else jnp.concatenate(outs, axis=0)


def _head_block_mask(rows_per_head, cols_per_head, shape):
    r = lax.broadcasted_iota(jnp.int32, shape, 0) // rows_per_head
    c = lax.broadcasted_iota(jnp.int32, shape, 1) // cols_per_head
    return r == c


def _score_masks(rows, n_levels):
    shape = (rows, GLA_HEADS * rows)
    t_idx = lax.broadcasted_iota(jnp.int32, shape, 0)
    s_idx = lax.broadcasted_iota(jnp.int32, shape, 1) % rows
    masks = [t_idx == s_idx]
    for lvl in range(n_levels):
        half = 1 << lvl
        blk = 2 * half
        masks.append((t_idx // blk == s_idx // blk) & (t_idx % blk >= half) & (s_idx % blk < half))
    return masks


def _decayed_scores(q, k, la, b, masks, head_blk, pump=lambda n=None: None):
    rows = q.shape[0]
    r_idx = lax.broadcasted_iota(jnp.int32, q.shape, 0)

    def scores(qt, kt):
        kh = jnp.where(head_blk, jnp.concatenate([kt.astype(BF16)] * GLA_HEADS, axis=0), jnp.zeros((), BF16))
        return _dot_nt(qt.astype(BF16), kh)

    acc = jnp.where(masks[0], scores(q, k), 0.0)
    for lvl in range(len(masks) - 1):
        half = 1 << lvl
        blk = 2 * half
        if lvl == 0:
            neg = jnp.where(r_idx % 2 == 1, la, 0.0)
        elif lvl == 1:
            r4 = r_idx % 4
            nxt = pltpu.roll(la, rows - 1, 0)
            prv = pltpu.roll(la, 1, 0)
            neg = jnp.where(r4 == 0, nxt, jnp.where(r4 == 1, 0.0, jnp.where(r4 == 2, la, la + prv)))
        else:
            ref = jnp.concatenate(
                [jnp.broadcast_to(b[j * blk + half - 1:j * blk + half, :], (blk, b.shape[1]))
                 for j in range(rows // blk)], axis=0)
            d = b - ref
            neg = jnp.minimum(d, -d)
        w = jnp.exp(neg)
        acc = jnp.where(masks[lvl + 1], scores(q * w, k * w), acc)
        pump(PUMP_PLAN["level"][lvl])
    return acc.astype(BF16)


def _pair_block_diag(a0, a1):
    z = jnp.zeros_like(a0)
    return jnp.concatenate(
        [jnp.concatenate([a0, z], axis=1), jnp.concatenate([z, a1], axis=1)], axis=0)


def _make_pump(units, xb_ref, win_ref, proj_ref):
    units = iter(units)

    def pump(n=None):
        for r, lo in (units if n is None else itertools.islice(units, n)):
            proj_ref[r:r + PUMP_ROWS, lo:lo + MXU_COLS] = _dot(
                xb_ref[r:r + PUMP_ROWS, 0:D_MODEL], win_ref[:, lo:lo + MXU_COLS])

    return pump


def _naive_scores(q, k, b, causal, head_blk):
    ks = (k * jnp.exp(-b)).astype(BF16)
    kh = jnp.where(head_blk, jnp.concatenate([ks] * GLA_HEADS, axis=0), jnp.zeros((), BF16))
    return jnp.where(causal, _dot_nt((q * jnp.exp(b)).astype(BF16), kh), 0.0).astype(BF16)


def _prompt_stage_b(pos0, proj_ref, xkeep_ref, la_ref, wmap_ref, pscale_ref, ba_ref, ng_ref, wout_ref, lng_ref,
                    lnb_ref, y_ref, ext_ref, s_ref, pump, next_decay, *, tc, robust, tail):
    u = proj_ref[:, C_U:C_GP]
    ext_ref[HIST_ROWS:HIST_ROWS + tc, 0:POOL_WIDTH] = u
    pos16 = pos0 + lax.broadcasted_iota(jnp.int32, (HIST_ROWS, POOL_GROUP_DIM), 0)
    mixed = []
    for g, w in enumerate(POOL_WINDOWS):
        sl = slice(g * POOL_GROUP_DIM, (g + 1) * POOL_GROUP_DIM)
        e = ext_ref[:, sl]
        s = e
        sh = 1
        while sh < w:
            s = s + pltpu.roll(s, sh, 0)
            sh *= 2
        wsum = s[HIST_ROWS:]
        self_rows = e[HIST_ROWS:]
        cnt = jnp.minimum(pos16 + 1, w).astype(F32)
        first = wsum[:HIST_ROWS] / cnt - self_rows[:HIST_ROWS]
        rest = wsum[HIST_ROWS:] * (1.0 / w) - self_rows[HIST_ROWS:]
        pooled = jnp.concatenate([first, rest], axis=0)
        mixed.append(_dot(pooled.astype(BF16), wmap_ref[g]))
        pump(PUMP_PLAN["pool"][g])
    pool_act = jnp.concatenate(mixed, axis=1) * pscale_ref[...] * _silu(proj_ref[:, C_GP:C_Q])
    next_decay()

    cc = GLA_CHUNK
    n_chunks = tc // cc
    rr = lax.broadcasted_iota(jnp.int32, (tc, tc), 0)
    cl = lax.broadcasted_iota(jnp.int32, (tc, tc), 1)
    ltri = jnp.where((rr // cc == cl // cc) & (cl <= rr), 1.0, 0.0).astype(BF16)
    masks = _score_masks(cc, cc.bit_length() - 1 if robust else 0)
    causal = (lax.broadcasted_iota(jnp.int32, (cc, GLA_HEADS * cc), 1) % cc
              <= lax.broadcasted_iota(jnp.int32, (cc, GLA_HEADS * cc), 0))
    blk = _head_block_mask(cc, GLA_DK, (GLA_HEADS * cc, GLA_KEY_WIDTH))

    hi, lo = _split_bf16(la_ref[:, 0:GLA_KEY_WIDTH])
    b_all = _dot(ltri, hi) + _dot(ltri, lo)
    pump(PUMP_PLAN["cumsum"])
    lhs_list, v_list, upd_list, dcol_list = [], [], [], []
    for c in range(n_chunks):
        rs = slice(c * cc, (c + 1) * cc)
        b = b_all[rs]
        b_end = b[cc - 1:cc, :]
        q_c = proj_ref[rs, C_Q:C_K] * (GLA_DK ** -0.5)
        k_c = proj_ref[rs, C_K:C_V]
        qs = (q_c * jnp.exp(b)).astype(BF16)
        kd = (k_c * jnp.exp(b_end - b)).astype(BF16)
        if robust:
            a_all = _decayed_scores(q_c, k_c, la_ref[rs, 0:GLA_KEY_WIDTH], b, masks, blk, pump)
        else:
            a_all = _naive_scores(q_c, k_c, b, causal, blk)
            pump(PUMP_PLAN["scores"][c])
        vc = proj_ref[rs, C_V:C_GG].astype(BF16)
        upds = []
        for p in range(GLA_HEADS // 2):
            upd = _dot_tn(kd[:, p * 2 * GLA_DK:(p + 1) * 2 * GLA_DK], vc[:, p * 2 * GLA_DV:(p + 1) * 2 * GLA_DV])
            upds.append(upd[0:GLA_DK, 0:GLA_DV])
            upds.append(upd[GLA_DK:2 * GLA_DK, GLA_DV:2 * GLA_DV])
        lhs_list.append([jnp.concatenate([a_all[:, p * 2 * cc:(p + 1) * 2 * cc],
                                          qs[:, p * 2 * GLA_DK:(p + 1) * 2 * GLA_DK]], axis=1)
                         for p in range(GLA_HEADS // 2)])
        v_list.append(vc)
        upd_list.append(jnp.concatenate(upds, axis=0))
        dcol_list.append(jnp.transpose(jnp.broadcast_to(jnp.exp(b_end), (GLA_DV, GLA_KEY_WIDTH))))

    o_chunks = []
    state = s_ref[...]
    for c in range(n_chunks):
        s_bf = state.astype(BF16)
        vc = v_list[c]
        outs = []
        for p in range(GLA_HEADS // 2):
            h0, h1 = 2 * p, 2 * p + 1
            rhs = jnp.concatenate([
                _pair_block_diag(vc[:, h0 * GLA_DV:(h0 + 1) * GLA_DV], vc[:, h1 * GLA_DV:(h1 + 1) * GLA_DV]),
                _pair_block_diag(s_bf[h0 * GLA_DK:(h0 + 1) * GLA_DK], s_bf[h1 * GLA_DK:(h1 + 1) * GLA_DK]),
            ], axis=0)
            outs.append(_dot(lhs_list[c][p], rhs))
        o_chunks.append(jnp.concatenate(outs, axis=1))
        state = state * dcol_list[c] + upd_list[c]
        pump(PUMP_PLAN["state"][c])
    s_ref[...] = state
    o = jnp.concatenate(o_chunks, axis=0)

    y_ref[...] = _epilogue(xkeep_ref[:, 0:D_MODEL], pool_act, o, proj_ref[:, C_GG:C_GL], ng_ref, wout_ref, lng_ref, lnb_ref,
                           pump, tail, row_blocks=4)


def _prompt_body(x_ref, win_ref, wmap_ref, pscale_ref, ba_ref, ng_ref, wout_ref, lng_ref, lnb_ref,
                 y_ref, pool_ref, s_out_ref,
                 proj0_ref, proj1_ref, xkeep0_ref, xkeep1_ref, la0_ref, la1_ref, xb_ref, ext_ref, s_ref, steep_ref,
                 *, tc, sub, nj):
    i = pl.program_id(0)
    jb = (i + nj - 1) % nj

    @pl.when(i == 0)
    def _():
        proj1_ref[...] = jnp.zeros_like(proj1_ref)
        xkeep1_ref[...] = jnp.zeros_like(xkeep1_ref)
        la1_ref[...] = jnp.zeros_like(la1_ref)
        steep_ref[0] = 0
        pool_ref[...] = jnp.zeros_like(pool_ref)

    @pl.when((i == 0) | (jb == 0))
    def _():
        ext_ref[0:HIST_ROWS, 0:POOL_WIDTH] = jnp.zeros((HIST_ROWS, POOL_WIDTH), F32)
        s_ref[...] = jnp.zeros_like(s_ref)

    units = [(r, lo) for lo in (C_GL, *range(0, C_GL, MXU_COLS)) for r in range(0, tc, PUMP_ROWS)]

    def step(proj_a, xkeep_a, la_a, proj_b, xkeep_b, la_b, robust):
        x = x_ref[0]
        xkeep_a[:, 0:D_MODEL] = x
        xb_ref[:, 0:D_MODEL] = x.astype(BF16)
        pump = _make_pump(units, xb_ref, win_ref, proj_a)
        pump(tc // PUMP_ROWS)

        def next_decay():
            la = _log_sigmoid(proj_a[:, C_GL:C_END] + ba_ref[...]) * (1.0 / GLA_TAU)
            la_a[:, 0:GLA_KEY_WIDTH] = la
            chunk_decay = [jnp.sum(la[c * GLA_CHUNK:(c + 1) * GLA_CHUNK], axis=0, keepdims=True)
                           for c in range(tc // GLA_CHUNK)]
            lowest = jnp.min(functools.reduce(jnp.minimum, chunk_decay))
            steep_ref[0] = jnp.where(lowest < -SAFE_LOG_DECAY, 1, 0).astype(jnp.int32)

        n_sub = tc // sub
        for k in range(n_sub):
            rows = slice(k * sub, (k + 1) * sub)
            last = k == n_sub - 1
            _prompt_stage_b((jb * n_sub + k) * sub, proj_b.at[rows, :], xkeep_b.at[rows, :], la_b.at[rows, :],
                            wmap_ref, pscale_ref, ba_ref, ng_ref, wout_ref, lng_ref, lnb_ref,
                            y_ref.at[0, rows, :], ext_ref, s_ref, pump, next_decay if k == 0 else (lambda: None),
                            tc=sub, robust=robust, tail=None if last else SUBTILE_TAIL_UNITS)
            if not last:
                ext_ref[0:HIST_ROWS, 0:POOL_WIDTH] = ext_ref[sub:sub + HIST_ROWS, 0:POOL_WIDTH]

    set0 = (proj0_ref, xkeep0_ref, la0_ref)
    set1 = (proj1_ref, xkeep1_ref, la1_ref)
    steep = steep_ref[0] != 0
    for parity, (set_a, set_b) in enumerate(((set0, set1), (set1, set0))):
        for robust in (False, True):
            @pl.when((i % 2 == parity) & (steep == robust))
            def _(set_a=set_a, set_b=set_b, robust=robust):
                step(*set_a, *set_b, robust)

    @pl.when((i > 0) & (jb == nj - 1))
    def _():
        mine = lax.broadcasted_iota(jnp.int32, pool_ref.shape[1:], 0) == (i - 1) // nj
        for r in range(POOL_BUF):
            row = ext_ref[sub + 1 + r:sub + 2 + r, 0:POOL_WIDTH]
            pool_ref[r] = jnp.where(mine, jnp.broadcast_to(row, pool_ref.shape[1:]), pool_ref[r])
        s_out_ref[0] = s_ref[...]

    ext_ref[0:HIST_ROWS, 0:POOL_WIDTH] = ext_ref[sub:sub + HIST_ROWS, 0:POOL_WIDTH]


def _prompt_call(x, win, wmap, pscale, ba, ng, wout, lng, lnb):
    nb, t, _ = x.shape
    tc = PROMPT_TILE
    nj = t // tc
    ntiles = nb * nj
    const2 = lambda i: (0, 0)

    def tile_a(i):
        ia = jnp.minimum(i, ntiles - 1)
        return (ia // nj, ia % nj, 0)

    def tile_b(i):
        ib = jnp.maximum(i - 1, 0)
        return (ib // nj, ib % nj, 0)

    def seq_b(i):
        return (jnp.maximum(i - 1, 0) // nj, 0, 0)

    return pl.pallas_call(
        functools.partial(_prompt_body, tc=tc, sub=PROMPT_SUBTILE, nj=nj),
        grid=(ntiles + 1,),
        in_specs=[
            pl.BlockSpec((1, tc, D_MODEL), tile_a),
            pl.BlockSpec(win.shape, const2),
            pl.BlockSpec(wmap.shape, lambda i: (0, 0, 0)),
            pl.BlockSpec(pscale.shape, const2),
            pl.BlockSpec(ba.shape, const2),
            pl.BlockSpec(ng.shape, const2),
            pl.BlockSpec(wout.shape, const2),
            pl.BlockSpec(lng.shape, const2),
            pl.BlockSpec(lnb.shape, const2),
        ],
        out_specs=[
            pl.BlockSpec((1, tc, D_MODEL), tile_b),
            pl.BlockSpec((POOL_BUF, nb, POOL_WIDTH), lambda i: (0, 0, 0)),
            pl.BlockSpec((1, GLA_KEY_WIDTH, GLA_DV), seq_b),
        ],
        out_shape=[
            jax.ShapeDtypeStruct((nb, t, D_MODEL), F32),
            jax.ShapeDtypeStruct((POOL_BUF, nb, POOL_WIDTH), F32),
            jax.ShapeDtypeStruct((nb, GLA_KEY_WIDTH, GLA_DV), F32),
        ],
        scratch_shapes=[
            pltpu.VMEM((tc, C_END + PITCH_PAD), F32),
            pltpu.VMEM((tc, C_END + PITCH_PAD), F32),
            pltpu.VMEM((tc, D_MODEL + PITCH_PAD), F32),
            pltpu.VMEM((tc, D_MODEL + PITCH_PAD), F32),
            pltpu.VMEM((tc, GLA_KEY_WIDTH + PITCH_PAD), F32),
            pltpu.VMEM((tc, GLA_KEY_WIDTH + PITCH_PAD), F32),
            pltpu.VMEM((tc, D_MODEL + PITCH_PAD), BF16),
            pltpu.VMEM((HIST_ROWS + PROMPT_SUBTILE, POOL_WIDTH + PITCH_PAD), F32),
            pltpu.VMEM((GLA_KEY_WIDTH, GLA_DV), F32),
            pltpu.SMEM((1,), jnp.int32),
        ],
        compiler_params=pltpu.CompilerParams(
            dimension_semantics=("arbitrary",), vmem_limit_bytes=VMEM_LIMIT),
        name="prompt_layer",
    )(x, win, wmap, pscale, ba, ng, wout, lng, lnb)


def _sample_body(x_ref, pool_in_ref, s_in_ref, win_ref, wmap_ref, pscale_ref, ba_ref, ng_ref, wout_ref,
                 lng_ref, lnb_ref, y_ref, pool_ref, s_out_ref, u_ref, pooled_ref, *, bs, t, start_pos):
    rows = bs * t
    x = x_ref[...].reshape(rows, D_MODEL)
    xb = x.astype(BF16)

    def proj(lo, hi):
        return _dot(xb, win_ref[:, lo:hi])

    u = proj(C_U, C_GP)
    mixed = []
    for g, w in enumerate(POOL_WINDOWS):
        sl = slice(g * POOL_GROUP_DIM, (g + 1) * POOL_GROUP_DIM)
        u_ref[g] = u[:, sl]
        ext = [pool_in_ref[r, :, sl] for r in range(POOL_BUF)]
        ext += [u_ref[g, pl.ds(tok, bs, stride=t), :] for tok in range(t)]
        for r in range(POOL_BUF):
            pool_ref[r, :, sl] = ext[t + r]
        sums = ext
        sh = 1
        while sh < w:
            first = 2 * sh - 1
            sums = [None] * first + [sums[r] + sums[r - sh] for r in range(first, len(sums))]
            sh *= 2
        for tok in range(t):
            cnt = float(min(start_pos + tok + 1, w))
            pooled_ref[g, pl.ds(tok, bs, stride=t), :] = sums[POOL_BUF + tok] / cnt - ext[POOL_BUF + tok]
        mixed.append(_dot(pooled_ref[g].astype(BF16), wmap_ref[g]))
    gp = proj(C_GP, C_Q)
    pool_act = jnp.concatenate(mixed, axis=1) * pscale_ref[...] * _silu(gp)

    q = proj(C_Q, C_K) * (GLA_DK ** -0.5)
    k = proj(C_K, C_V)
    vb = proj(C_V, C_GG).astype(BF16)
    gg = proj(C_GG, C_GL)
    la = _log_sigmoid(proj(C_GL, C_END) + ba_ref[...]) * (1.0 / GLA_TAU)

    rr = lax.broadcasted_iota(jnp.int32, (rows, rows), 0)
    cl = lax.broadcasted_iota(jnp.int32, (rows, rows), 1)
    same = (rr // t) == (cl // t)
    ltri = jnp.where(same & (cl <= rr), 1.0, 0.0).astype(BF16)
    lall = jnp.where(same, 1.0, 0.0).astype(BF16)
    hi, lo = _split_bf16(la)
    b = _dot(ltri, hi) + _dot(ltri, lo)
    b_end = _dot(lall, hi) + _dot(lall, lo)
    qs = (q * jnp.exp(b)).astype(BF16)
    kd = (k * jnp.exp(b_end - b)).astype(BF16)

    assert t & (t - 1) == 0, "tokens per sequence must be a power of two"
    blk = _head_block_mask(rows, GLA_DK, (GLA_HEADS * rows, GLA_KEY_WIDTH))
    a_all = _decayed_scores(q, k, la, b, _score_masks(rows, t.bit_length() - 1), blk)

    s_old = s_in_ref[...]
    s_bf = s_old.astype(BF16)
    qs3 = qs.reshape(bs, t, GLA_KEY_WIDTH)
    kd3 = kd.reshape(bs, t, GLA_KEY_WIDTH)
    v3 = vb.reshape(bs, t, GLA_WIDTH)
    bdims_nn = (((2,), (1,)), ((0,), (0,)))
    bdims_tn = (((1,), (1,)), ((0,), (0,)))
    assert bs <= GLA_DV
    seq_sel = jnp.where(lax.broadcasted_iota(jnp.int32, (rows, GLA_DV), 0) // t
                        == lax.broadcasted_iota(jnp.int32, (rows, GLA_DV), 1), 1.0, 0.0).astype(BF16)
    decay_t = jnp.exp(_dot_tn(hi, seq_sel) + _dot_tn(lo, seq_sel))
    decay = jnp.stack([jnp.broadcast_to(decay_t[:, j:j + 1], (GLA_KEY_WIDTH, GLA_DV)) for j in range(bs)], axis=0)
    outs, upds = [], []
    for p in range(GLA_HEADS // 2):
        h0, h1 = 2 * p, 2 * p + 1
        intra = _dot(a_all[:, p * 2 * rows:(p + 1) * 2 * rows],
                     _pair_block_diag(vb[:, h0 * GLA_DV:(h0 + 1) * GLA_DV], vb[:, h1 * GLA_DV:(h1 + 1) * GLA_DV]))
        s0 = s_bf[:, h0 * GLA_DK:(h0 + 1) * GLA_DK, :]
        s1 = s_bf[:, h1 * GLA_DK:(h1 + 1) * GLA_DK, :]
        z = jnp.zeros_like(s0)
        s_pair = jnp.concatenate(
            [jnp.concatenate([s0, z], axis=2), jnp.concatenate([z, s1], axis=2)], axis=1)
        inter = lax.dot_general(qs3[:, :, p * 2 * GLA_DK:(p + 1) * 2 * GLA_DK], s_pair, bdims_nn,
                                preferred_element_type=F32)
        outs.append(intra + inter.reshape(rows, 2 * GLA_DV))
        upd = lax.dot_general(kd3[:, :, p * 2 * GLA_DK:(p + 1) * 2 * GLA_DK],
                              v3[:, :, p * 2 * GLA_DV:(p + 1) * 2 * GLA_DV], bdims_tn,
                              preferred_element_type=F32)
        upds.append(upd[:, 0:GLA_DK, 0:GLA_DV])
        upds.append(upd[:, GLA_DK:2 * GLA_DK, GLA_DV:2 * GLA_DV])
    o = jnp.concatenate(outs, axis=1)
    s_out_ref[...] = s_old * decay + jnp.concatenate(upds, axis=1)

    y = _epilogue(x, pool_act, o, gg, ng_ref, wout_ref, lng_ref, lnb_ref)
    y_ref[...] = y.reshape(bs, t, D_MODEL)


def _sample_call(x, pool_in, s_in, win, wmap, pscale, ba, ng, wout, lng, lnb, start_pos):
    nb, t, _ = x.shape
    bs = SAMPLE_SEQS
    const2 = lambda i: (0, 0)
    seq3 = lambda i: (i, 0, 0)
    hist3 = lambda i: (0, i, 0)
    return pl.pallas_call(
        functools.partial(_sample_body, bs=bs, t=t, start_pos=start_pos),
        grid=(nb // bs,),
        in_specs=[
            pl.BlockSpec((bs, t, D_MODEL), seq3),
            pl.BlockSpec((POOL_BUF, bs, POOL_WIDTH), hist3),
            pl.BlockSpec((bs, GLA_KEY_WIDTH, GLA_DV), seq3),
            pl.BlockSpec(win.shape, const2),
            pl.BlockSpec(wmap.shape, lambda i: (0, 0, 0)),
            pl.BlockSpec(pscale.shape, const2),
            pl.BlockSpec(ba.shape, const2),
            pl.BlockSpec(ng.shape, const2),
            pl.BlockSpec(wout.shape, const2),
            pl.BlockSpec(lng.shape, const2),
            pl.BlockSpec(lnb.shape, const2),
        ],
        out_specs=[
            pl.BlockSpec((bs, t, D_MODEL), seq3),
            pl.BlockSpec((POOL_BUF, bs, POOL_WIDTH), hist3),
            pl.BlockSpec((bs, GLA_KEY_WIDTH, GLA_DV), seq3),
        ],
        out_shape=[
            jax.ShapeDtypeStruct((nb, t, D_MODEL), F32),
            jax.ShapeDtypeStruct((POOL_BUF, nb, POOL_WIDTH), F32),
            jax.ShapeDtypeStruct((nb, GLA_KEY_WIDTH, GLA_DV), F32),
        ],
        scratch_shapes=[pltpu.VMEM((len(POOL_WINDOWS), bs * t, POOL_GROUP_DIM), F32),
                        pltpu.VMEM((len(POOL_WINDOWS), bs * t, POOL_GROUP_DIM), F32)],
        compiler_params=pltpu.CompilerParams(
            dimension_semantics=("arbitrary",), vmem_limit_bytes=VMEM_LIMIT),
        name="sample_layer",
    )(x, pool_in, s_in, win, wmap, pscale, ba, ng, wout, lng, lnb)


def _prep_body(wt_ref, wfat_ref, wa2_ref, wout_f32_ref, wmap_f32_ref, win_ref, wout_ref, wmap_ref, *, n_plain,
               n_wout):
    i = pl.program_id(0)

    @pl.when(i == 0)
    def _():
        wmap_ref[...] = wmap_f32_ref[...].astype(BF16)

    @pl.when(i < n_wout)
    def _():
        wout_ref[:, 0:D_MODEL] = wout_f32_ref[...].astype(BF16)
        wout_ref[:, D_MODEL:] = jnp.zeros((wout_ref.shape[0], PITCH_PAD), BF16)

    @pl.when(i < n_plain)
    def _():
        win_ref[...] = jnp.transpose(wt_ref[...]).astype(BF16)

    @pl.when(i == n_plain)
    def _():
        fa_hi, fa_lo = _split_bf16(wfat_ref[...])
        a2_hi, a2_lo = _split_bf16(wa2_ref[...])
        acc = _dot_tn(fa_hi, a2_hi) + (_dot_tn(fa_hi, a2_lo) + _dot_tn(fa_lo, a2_hi))
        win_ref[:, 0:GLA_KEY_WIDTH] = acc.astype(BF16)
        win_ref[:, GLA_KEY_WIDTH:] = jnp.zeros((D_MODEL, PREP_COLS - GLA_KEY_WIDTH), BF16)


def _prep_weights(wt, wa2, wout_f32, wmap_f32):
    assert C_GL % PREP_COLS == 0 and C_END + PITCH_PAD - C_GL <= PREP_COLS and C_GL % GLA_RANK == 0
    n_plain = C_GL // PREP_COLS
    n_wout = 4
    wout_rows = wout_f32.shape[0] // n_wout
    assert n_wout <= n_plain + 1
    wout_block = lambda i: (jnp.minimum(i, n_wout - 1), 0)
    return pl.pallas_call(
        functools.partial(_prep_body, n_plain=n_plain, n_wout=n_wout),
        grid=(n_plain + 1,),
        in_specs=[
            pl.BlockSpec((PREP_COLS, D_MODEL), lambda i: (jnp.minimum(i, n_plain - 1), 0)),
            pl.BlockSpec((GLA_RANK, D_MODEL), lambda i: (C_GL // GLA_RANK, 0)),
            pl.BlockSpec(wa2.shape, lambda i: (0, 0)),
            pl.BlockSpec((wout_rows, D_MODEL), wout_block),
            pl.BlockSpec(wmap_f32.shape, lambda i: (0, 0, 0)),
        ],
        out_specs=[
            pl.BlockSpec((D_MODEL, PREP_COLS), lambda i: (0, i)),
            pl.BlockSpec((wout_rows, D_MODEL + PITCH_PAD), wout_block),
            pl.BlockSpec(wmap_f32.shape, lambda i: (0, 0, 0)),
        ],
        out_shape=[
            jax.ShapeDtypeStruct((D_MODEL, C_END + PITCH_PAD), BF16),
            jax.ShapeDtypeStruct((D_MODEL, D_MODEL + PITCH_PAD), BF16),
            jax.ShapeDtypeStruct(wmap_f32.shape, BF16),
        ],
        compiler_params=pltpu.CompilerParams(dimension_semantics=("arbitrary",), vmem_limit_bytes=VMEM_LIMIT),
        name="prep_weights",
    )(wt, wt, wa2, wout_f32, wmap_f32)


def kernel(x_prompt, x_sample, state_pool, state_gla, w_in, w_pool_map, pool_scale, w_gla_a2, b_gla_a,
           gla_norm_g, w_out, ln_g, ln_b):
    assert w_in.shape[0] == DEPTH
    win, wout, wmap = _prep_weights(jnp.transpose(w_in[0]), w_gla_a2[0], w_out[0], w_pool_map[0])
    pscale = pool_scale[0].reshape(1, POOL_WIDTH)
    ba = b_gla_a[0].reshape(1, GLA_KEY_WIDTH)
    ng = gla_norm_g[0].reshape(1, GLA_WIDTH)
    lng = ln_g[0].reshape(1, D_MODEL)
    lnb = ln_b[0].reshape(1, D_MODEL)
    params = (win, wmap, pscale, ba, ng, wout, lng, lnb)

    y_p, pool_p, gla_p = _prompt_call(x_prompt, *params)
    nsamp = x_sample.shape[0]
    y_s, pool_s, gla_s = _sample_call(
        x_sample, jnp.transpose(state_pool[0], (1, 0, 2)),
        state_gla[0].reshape(nsamp, GLA_KEY_WIDTH, GLA_DV), *params, start_pos=PAST_LEN)

    nb = x_prompt.shape[0]
    return (y_p, y_s, jnp.transpose(pool_p, (1, 0, 2))[None],
            gla_p.reshape(1, nb, GLA_HEADS, GLA_DK, GLA_DV),
            jnp.transpose(pool_s, (1, 0, 2))[None],
            gla_s.reshape(1, nsamp, GLA_HEADS, GLA_DK, GLA_DV))
```

```python
import functools
import itertools

import jax
import jax.numpy as jnp
from jax import lax
from jax.experimental import pallas as pl
from jax.experimental.pallas import tpu as pltpu

F32 = jnp.float32
BF16 = jnp.bfloat16

D_MODEL = 1024
POOL_WIDTH = 512
POOL_WINDOWS = (2, 4, 8, 16)
POOL_GROUP_DIM = 128
POOL_BUF = 15
HIST_ROWS = 16
GLA_HEADS = 4
GLA_DK = 64
GLA_DV = 128
GLA_KEY_WIDTH = GLA_HEADS * GLA_DK
GLA_WIDTH = GLA_HEADS * GLA_DV
GLA_RANK = 16
GLA_TAU = 16.0
LN_EPS = 1e-5
RMS_EPS = 1e-6
DEPTH = 1
DEEPNORM_ALPHA = (2.0 * DEPTH) ** 0.25
PAST_LEN = 16384

C_U, C_GP, C_Q, C_K, C_V, C_GG, C_GL, C_END = 0, 512, 1024, 1280, 1536, 2048, 2560, 2816

MXU_COLS = 256
PROMPT_TILE = 256
PITCH_PAD = 128
PROMPT_SUBTILE = 256
SUBTILE_TAIL_UNITS = 0
PUMP_ROWS = 256
PUMP_PLAN = {"pool": (0, 1, 0, 1), "cumsum": 0, "scores": (1, 1, 1, 1), "level": (0, 0, 0, 1, 0, 0),
             "state": (0, 0, 0, 0), "epilogue": 1, "layernorm": (1, 1, 1, 0)}
SAFE_LOG_DECAY = 60.0
GLA_CHUNK = 64
SAMPLE_SEQS = 16
PREP_COLS = 512
VMEM_LIMIT = 56 * 1024 * 1024


def _dot(a, b):
    return jnp.dot(a, b, preferred_element_type=F32)


def _dot_nt(a, b):
    return lax.dot_general(a, b, (((1,), (1,)), ((), ())), preferred_element_type=F32)


def _dot_tn(a, b):
    return lax.dot_general(a, b, (((0,), (0,)), ((), ())), preferred_element_type=F32)


def _silu(x):
    return x * (1.0 / (1.0 + jnp.exp(-x)))


def _log_sigmoid(x):
    return jnp.minimum(x, 0.0) - jnp.log1p(jnp.exp(-jnp.abs(x)))


def _split_bf16(x):
    hi = x.astype(BF16)
    lo = (x - hi.astype(F32)).astype(BF16)
    return hi, lo


def _epilogue(x, pool_act, o, gg, ng_ref, wout_ref, lng_ref, lnb_ref, pump=lambda n=None: None, tail=None,
              row_blocks=1):
    pump(PUMP_PLAN["epilogue"])
    normed = []
    for h in range(GLA_HEADS):
        oh = o[:, h * GLA_DV:(h + 1) * GLA_DV]
        ms = jnp.mean(oh * oh, axis=-1, keepdims=True)
        normed.append(oh * lax.rsqrt(ms + RMS_EPS))
    gla_act = jnp.concatenate(normed, axis=1) * ng_ref[...] * _silu(gg)
    mix = jnp.concatenate([pool_act, gla_act], axis=1).astype(BF16)
    y = _dot(mix, wout_ref[:, 0:D_MODEL])
    n = x.shape[0] // row_blocks
    outs = []
    for r in range(row_blocks):
        rows = slice(r * n, (r + 1) * n)
        pump(PUMP_PLAN["layernorm"][r])
        h = DEEPNORM_ALPHA * x[rows] + y[rows]
        mu = jnp.mean(h, axis=-1, keepdims=True)
        d = h - mu
        var = jnp.mean(d * d, axis=-1, keepdims=True)
        outs.append(d * lax.rsqrt(var + LN_EPS) * lng_ref[...] + lnb_ref[...])
    pump(tail)
    return outs[0] if row_blocks == 1 else jnp.concatenate(outs, axis=0)


def _head_block_mask(rows_per_head, cols_per_head, shape):
    r = lax.broadcasted_iota(jnp.int32, shape, 0) // rows_per_head
    c = lax.broadcasted_iota(jnp.int32, shape, 1) // cols_per_head
    return r == c


def _score_masks(rows, n_levels):
    shape = (rows, GLA_HEADS * rows)
    t_idx = lax.broadcasted_iota(jnp.int32, shape, 0)
    s_idx = lax.broadcasted_iota(jnp.int32, shape, 1) % rows
    masks = [t_idx == s_idx]
    for lvl in range(n_levels):
        half = 1 << lvl
        blk = 2 * half
        masks.append((t_idx // blk == s_idx // blk) & (t_idx % blk >= half) & (s_idx % blk < half))
    return masks


def _decayed_scores(q, k, la, b, masks, head_blk, pump=lambda n=None: None):
    rows = q.shape[0]
    r_idx = lax.broadcasted_iota(jnp.int32, q.shape, 0)

    def scores(qt, kt):
        kh = jnp.where(head_blk, jnp.concatenate([kt.astype(BF16)] * GLA_HEADS, axis=0), jnp.zeros((), BF16))
        return _dot_nt(qt.astype(BF16), kh)

    acc = jnp.where(masks[0], scores(q, k), 0.0)
    for lvl in range(len(masks) - 1):
        half = 1 << lvl
        blk = 2 * half
        if lvl == 0:
            neg = jnp.where(r_idx % 2 == 1, la, 0.0)
        elif lvl == 1:
            r4 = r_idx % 4
            nxt = pltpu.roll(la, rows - 1, 0)
            prv = pltpu.roll(la, 1, 0)
            neg = jnp.where(r4 == 0, nxt, jnp.where(r4 == 1, 0.0, jnp.where(r4 == 2, la, la + prv)))
        else:
            ref = jnp.concatenate(
                [jnp.broadcast_to(b[j * blk + half - 1:j * blk + half, :], (blk, b.shape[1]))
                 for j in range(rows // blk)], axis=0)
            d = b - ref
            neg = jnp.minimum(d, -d)
        w = jnp.exp(neg)
        acc = jnp.where(masks[lvl + 1], scores(q * w, k * w), acc)
        pump(PUMP_PLAN["level"][lvl])
    return acc.astype(BF16)


def _pair_block_diag(a0, a1):
    z = jnp.zeros_like(a0)
    return jnp.concatenate(
        [jnp.concatenate([a0, z], axis=1), jnp.concatenate([z, a1], axis=1)], axis=0)


def _make_pump(units, xb_ref, win_ref, proj_ref, unit_rows=PUMP_ROWS):
    units = iter(units)

    def pump(n=None):
        for r, lo in (units if n is None else itertools.islice(units, n)):
            proj_ref[r:r + unit_rows, lo:lo + MXU_COLS] = _dot(
                xb_ref[r:r + unit_rows, 0:D_MODEL], win_ref[:, lo:lo + MXU_COLS])

    return pump


def _naive_scores(q, k, b, causal, head_blk):
    ks = (k * jnp.exp(-b)).astype(BF16)
    kh = jnp.where(head_blk, jnp.concatenate([ks] * GLA_HEADS, axis=0), jnp.zeros((), BF16))
    return jnp.where(causal, _dot_nt((q * jnp.exp(b)).astype(BF16), kh), 0.0).astype(BF16)


def _prompt_stage_b(pos0, proj_ref, xkeep_ref, la_ref, wmap_ref, pscale_ref, ba_ref, ng_ref, wout_ref, lng_ref,
                    lnb_ref, y_ref, ext_ref, s_ref, pump, next_decay, *, tc, robust, tail):
    u = proj_ref[:, C_U:C_GP]
    ext_ref[HIST_ROWS:HIST_ROWS + tc, 0:POOL_WIDTH] = u
    pos16 = pos0 + lax.broadcasted_iota(jnp.int32, (HIST_ROWS, POOL_GROUP_DIM), 0)
    mixed = []
    for g, w in enumerate(POOL_WINDOWS):
        sl = slice(g * POOL_GROUP_DIM, (g + 1) * POOL_GROUP_DIM)
        e = ext_ref[:, sl]
        s = e
        sh = 1
        while sh < w:
            s = s + pltpu.roll(s, sh, 0)
            sh *= 2
        wsum = s[HIST_ROWS:]
        self_rows = e[HIST_ROWS:]
        cnt = jnp.minimum(pos16 + 1, w).astype(F32)
        first = wsum[:HIST_ROWS] / cnt - self_rows[:HIST_ROWS]
        rest = wsum[HIST_ROWS:] * (1.0 / w) - self_rows[HIST_ROWS:]
        pooled = jnp.concatenate([first, rest], axis=0)
        mixed.append(_dot(pooled.astype(BF16), wmap_ref[g]))
        pump(PUMP_PLAN["pool"][g])
    pool_act = jnp.concatenate(mixed, axis=1) * pscale_ref[...] * _silu(proj_ref[:, C_GP:C_Q])
    next_decay()

    cc = GLA_CHUNK
    n_chunks = tc // cc
    rr = lax.broadcasted_iota(jnp.int32, (tc, tc), 0)
    cl = lax.broadcasted_iota(jnp.int32, (tc, tc), 1)
    ltri = jnp.where((rr // cc == cl // cc) & (cl <= rr), 1.0, 0.0).astype(BF16)
    masks = _score_masks(cc, cc.bit_length() - 1 if robust else 0)
    causal = (lax.broadcasted_iota(jnp.int32, (cc, GLA_HEADS * cc), 1) % cc
              <= lax.broadcasted_iota(jnp.int32, (cc, GLA_HEADS * cc), 0))
    blk = _head_block_mask(cc, GLA_DK, (GLA_HEADS * cc, GLA_KEY_WIDTH))

    hi, lo = _split_bf16(la_ref[:, 0:GLA_KEY_WIDTH])
    b_all = _dot(ltri, hi) + _dot(ltri, lo)
    pump(PUMP_PLAN["cumsum"])
    lhs_list, v_list, upd_list, dcol_list = [], [], [], []
    for c in range(n_chunks):
        rs = slice(c * cc, (c + 1) * cc)
        b = b_all[rs]
        b_end = b[cc - 1:cc, :]
        q_c = proj_ref[rs, C_Q:C_K] * (GLA_DK ** -0.5)
        k_c = proj_ref[rs, C_K:C_V]
        qs = (q_c * jnp.exp(b)).astype(BF16)
        kd = (k_c * jnp.exp(b_end - b)).astype(BF16)
        if robust:
            a_all = _decayed_scores(q_c, k_c, la_ref[rs, 0:GLA_KEY_WIDTH], b, masks, blk, pump)
        else:
            a_all = _naive_scores(q_c, k_c, b, causal, blk)
            pump(PUMP_PLAN["scores"][c])
        vc = proj_ref[rs, C_V:C_GG].astype(BF16)
        upds = []
        for p in range(GLA_HEADS // 2):
            upd = _dot_tn(kd[:, p * 2 * GLA_DK:(p + 1) * 2 * GLA_DK], vc[:, p * 2 * GLA_DV:(p + 1) * 2 * GLA_DV])
            upds.append(upd[0:GLA_DK, 0:GLA_DV])
            upds.append(upd[GLA_DK:2 * GLA_DK, GLA_DV:2 * GLA_DV])
        lhs_list.append([jnp.concatenate([a_all[:, p * 2 * cc:(p + 1) * 2 * cc],
                                          qs[:, p * 2 * GLA_DK:(p + 1) * 2 * GLA_DK]], axis=1)
                         for p in range(GLA_HEADS // 2)])
        v_list.append(vc)
        upd_list.append(jnp.concatenate(upds, axis=0))
        dcol_list.append(jnp.transpose(jnp.broadcast_to(jnp.exp(b_end), (GLA_DV, GLA_KEY_WIDTH))))

    o_chunks = []
    state = s_ref[...]
    for c in range(n_chunks):
        s_bf = state.astype(BF16)
        vc = v_list[c]
        outs = []
        for p in range(GLA_HEADS // 2):
            h0, h1 = 2 * p, 2 * p + 1
            rhs = jnp.concatenate([
                _pair_block_diag(vc[:, h0 * GLA_DV:(h0 + 1) * GLA_DV], vc[:, h1 * GLA_DV:(h1 + 1) * GLA_DV]),
                _pair_block_diag(s_bf[h0 * GLA_DK:(h0 + 1) * GLA_DK], s_bf[h1 * GLA_DK:(h1 + 1) * GLA_DK]),
            ], axis=0)
            outs.append(_dot(lhs_list[c][p], rhs))
        o_chunks.append(jnp.concatenate(outs, axis=1))
        state = state * dcol_list[c] + upd_list[c]
        pump(PUMP_PLAN["state"][c])
    s_ref[...] = state
    o = jnp.concatenate(o_chunks, axis=0)

    y_ref[...] = _epilogue(xkeep_ref[:, 0:D_MODEL], pool_act, o, proj_ref[:, C_GG:C_GL], ng_ref, wout_ref, lng_ref, lnb_ref,
                           pump, tail, row_blocks=4)


def _prompt_body(x_ref, win_ref, wmap_ref, pscale_ref, ba_ref, ng_ref, wout_ref, lng_ref, lnb_ref,
                 y_ref, pool_ref, s_out_ref,
                 proj0_ref, proj1_ref, xkeep0_ref, xkeep1_ref, la0_ref, la1_ref, xb_ref, ext_ref, s_ref, steep_ref,
                 *, tc, sub, nj):
    i = pl.program_id(0)
    jb = (i + nj - 1) % nj

    @pl.when(i == 0)
    def _():
        proj1_ref[...] = jnp.zeros_like(proj1_ref)
        xkeep1_ref[...] = jnp.zeros_like(xkeep1_ref)
        la1_ref[...] = jnp.zeros_like(la1_ref)
        steep_ref[0] = 0
        pool_ref[...] = jnp.zeros_like(pool_ref)

    @pl.when((i == 0) | (jb == 0))
    def _():
        ext_ref[0:HIST_ROWS, 0:POOL_WIDTH] = jnp.zeros((HIST_ROWS, POOL_WIDTH), F32)
        s_ref[...] = jnp.zeros_like(s_ref)

    units = [(r, lo) for lo in (C_GL, *range(0, C_GL, MXU_COLS)) for r in range(0, tc, PUMP_ROWS)]

    def step(proj_a, xkeep_a, la_a, proj_b, xkeep_b, la_b, robust):
        x = x_ref[0]
        xkeep_a[:, 0:D_MODEL] = x
        xb_ref[:, 0:D_MODEL] = x.astype(BF16)
        pump = _make_pump(units, xb_ref, win_ref, proj_a)
        pump(tc // PUMP_ROWS)

        def next_decay():
            la = _log_sigmoid(proj_a[:, C_GL:C_END] + ba_ref[...]) * (1.0 / GLA_TAU)
            la_a[:, 0:GLA_KEY_WIDTH] = la
            chunk_decay = [jnp.sum(la[c * GLA_CHUNK:(c + 1) * GLA_CHUNK], axis=0, keepdims=True)
                           for c in range(tc // GLA_CHUNK)]
            lowest = jnp.min(functools.reduce(jnp.minimum, chunk_decay))
            steep_ref[0] = jnp.where(lowest < -SAFE_LOG_DECAY, 1, 0).astype(jnp.int32)

        n_sub = tc // sub
        for k in range(n_sub):
            rows = slice(k * sub, (k + 1) * sub)
            last = k == n_sub - 1
            _prompt_stage_b((jb * n_sub + k) * sub, proj_b.at[rows, :], xkeep_b.at[rows, :], la_b.at[rows, :],
                            wmap_ref, pscale_ref, ba_ref, ng_ref, wout_ref, lng_ref, lnb_ref,
                            y_ref.at[0, rows, :], ext_ref, s_ref, pump, next_decay if k == 0 else (lambda: None),
                            tc=sub, robust=robust, tail=None if last else SUBTILE_TAIL_UNITS)
            if not last:
                ext_ref[0:HIST_ROWS, 0:POOL_WIDTH] = ext_ref[sub:sub + HIST_ROWS, 0:POOL_WIDTH]

    set0 = (proj0_ref, xkeep0_ref, la0_ref)
    set1 = (proj1_ref, xkeep1_ref, la1_ref)
    steep = steep_ref[0] != 0
    for parity, (set_a, set_b) in enumerate(((set0, set1), (set1, set0))):
        for robust in (False, True):
            @pl.when((i % 2 == parity) & (steep == robust))
            def _(set_a=set_a, set_b=set_b, robust=robust):
                step(*set_a, *set_b, robust)

    @pl.when((i > 0) & (jb == nj - 1))
    def _():
        mine = lax.broadcasted_iota(jnp.int32, pool_ref.shape[1:], 0) == (i - 1) // nj
        for r in range(POOL_BUF):
            row = ext_ref[sub + 1 + r:sub + 2 + r, 0:POOL_WIDTH]
            pool_ref[r] = jnp.where(mine, jnp.broadcast_to(row, pool_ref.shape[1:]), pool_ref[r])
        s_out_ref[0] = s_ref[...]

    ext_ref[0:HIST_ROWS, 0:POOL_WIDTH] = ext_ref[sub:sub + HIST_ROWS, 0:POOL_WIDTH]


def _prompt_call(x, win, wmap, pscale, ba, ng, wout, lng, lnb):
    nb, t, _ = x.shape
    tc = PROMPT_TILE
    nj = t // tc
    ntiles = nb * nj
    const2 = lambda i: (0, 0)

    def tile_a(i):
        ia = jnp.minimum(i, ntiles - 1)
        return (ia // nj, ia % nj, 0)

    def tile_b(i):
        ib = jnp.maximum(i - 1, 0)
        return (ib // nj, ib % nj, 0)

    def seq_b(i):
        return (jnp.maximum(i - 1, 0) // nj, 0, 0)

    return pl.pallas_call(
        functools.partial(_prompt_body, tc=tc, sub=PROMPT_SUBTILE, nj=nj),
        grid=(ntiles + 1,),
        in_specs=[
            pl.BlockSpec((1, tc, D_MODEL), tile_a),
            pl.BlockSpec(win.shape, const2),
            pl.BlockSpec(wmap.shape, lambda i: (0, 0, 0)),
            pl.BlockSpec(pscale.shape, const2),
            pl.BlockSpec(ba.shape, const2),
            pl.BlockSpec(ng.shape, const2),
            pl.BlockSpec(wout.shape, const2),
            pl.BlockSpec(lng.shape, const2),
            pl.BlockSpec(lnb.shape, const2),
        ],
        out_specs=[
            pl.BlockSpec((1, tc, D_MODEL), tile_b),
            pl.BlockSpec((POOL_BUF, nb, POOL_WIDTH), lambda i: (0, 0, 0)),
            pl.BlockSpec((1, GLA_KEY_WIDTH, GLA_DV), seq_b),
        ],
        out_shape=[
            jax.ShapeDtypeStruct((nb, t, D_MODEL), F32),
            jax.ShapeDtypeStruct((POOL_BUF, nb, POOL_WIDTH), F32),
            jax.ShapeDtypeStruct((nb, GLA_KEY_WIDTH, GLA_DV), F32),
        ],
        scratch_shapes=[
            pltpu.VMEM((tc, C_END + PITCH_PAD), F32),
            pltpu.VMEM((tc, C_END + PITCH_PAD), F32),
            pltpu.VMEM((tc, D_MODEL + PITCH_PAD), F32),
            pltpu.VMEM((tc, D_MODEL + PITCH_PAD), F32),
            pltpu.VMEM((tc, GLA_KEY_WIDTH + PITCH_PAD), F32),
            pltpu.VMEM((tc, GLA_KEY_WIDTH + PITCH_PAD), F32),
            pltpu.VMEM((tc, D_MODEL + PITCH_PAD), BF16),
            pltpu.VMEM((HIST_ROWS + PROMPT_SUBTILE, POOL_WIDTH + PITCH_PAD), F32),
            pltpu.VMEM((GLA_KEY_WIDTH, GLA_DV), F32),
            pltpu.SMEM((1,), jnp.int32),
        ],
        compiler_params=pltpu.CompilerParams(
            dimension_semantics=("arbitrary",), vmem_limit_bytes=VMEM_LIMIT),
        name="prompt_layer",
    )(x, win, wmap, pscale, ba, ng, wout, lng, lnb)


def _sample_stage_b(proj_ref, xkeep_ref, pool_in_ref, s_in_ref, wmap_ref, pscale_ref, ba_ref, ng_ref, wout_ref,
                    lng_ref, lnb_ref, y_ref, pool_ref, s_out_ref, u_ref, pooled_ref, pump, *, bs, t, start_pos):
    rows = bs * t
    x = xkeep_ref[:, 0:D_MODEL]

    def proj(lo, hi):
        return proj_ref[:, lo:hi]

    u = proj(C_U, C_GP)
    mixed = []
    for g, w in enumerate(POOL_WINDOWS):
        sl = slice(g * POOL_GROUP_DIM, (g + 1) * POOL_GROUP_DIM)
        u_ref[g] = u[:, sl]
        ext = [pool_in_ref[r, :, sl] for r in range(POOL_BUF)]
        ext += [u_ref[g, pl.ds(tok, bs, stride=t), :] for tok in range(t)]
        for r in range(POOL_BUF):
            pool_ref[r, :, sl] = ext[t + r]
        sums = ext
        sh = 1
        while sh < w:
            first = 2 * sh - 1
            sums = [None] * first + [sums[r] + sums[r - sh] for r in range(first, len(sums))]
            sh *= 2
        for tok in range(t):
            cnt = float(min(start_pos + tok + 1, w))
            pooled_ref[g, pl.ds(tok, bs, stride=t), :] = sums[POOL_BUF + tok] / cnt - ext[POOL_BUF + tok]
        mixed.append(_dot(pooled_ref[g].astype(BF16), wmap_ref[g]))
        pump(g % 2)
    gp = proj(C_GP, C_Q)
    pool_act = jnp.concatenate(mixed, axis=1) * pscale_ref[...] * _silu(gp)

    q = proj(C_Q, C_K) * (GLA_DK ** -0.5)
    k = proj(C_K, C_V)
    vb = proj(C_V, C_GG).astype(BF16)
    gg = proj(C_GG, C_GL)
    la = _log_sigmoid(proj(C_GL, C_END) + ba_ref[...]) * (1.0 / GLA_TAU)

    rr = lax.broadcasted_iota(jnp.int32, (rows, rows), 0)
    cl = lax.broadcasted_iota(jnp.int32, (rows, rows), 1)
    same = (rr // t) == (cl // t)
    ltri = jnp.where(same & (cl <= rr), 1.0, 0.0).astype(BF16)
    lall = jnp.where(same, 1.0, 0.0).astype(BF16)
    hi, lo = _split_bf16(la)
    b = _dot(ltri, hi) + _dot(ltri, lo)
    b_end = _dot(lall, hi) + _dot(lall, lo)
    pump(1)
    qs = (q * jnp.exp(b)).astype(BF16)
    kd = (k * jnp.exp(b_end - b)).astype(BF16)

    assert t & (t - 1) == 0, "tokens per sequence must be a power of two"
    blk = _head_block_mask(rows, GLA_DK, (GLA_HEADS * rows, GLA_KEY_WIDTH))
    a_all = _decayed_scores(q, k, la, b, _score_masks(rows, t.bit_length() - 1), blk)
    pump(2)

    s_old = s_in_ref[...]
    s_bf = s_old.astype(BF16)
    qs3 = qs.reshape(bs, t, GLA_KEY_WIDTH)
    kd3 = kd.reshape(bs, t, GLA_KEY_WIDTH)
    v3 = vb.reshape(bs, t, GLA_WIDTH)
    bdims_nn = (((2,), (1,)), ((0,), (0,)))
    bdims_tn = (((1,), (1,)), ((0,), (0,)))
    assert bs <= GLA_DV
    seq_sel = jnp.where(lax.broadcasted_iota(jnp.int32, (rows, GLA_DV), 0) // t
                        == lax.broadcasted_iota(jnp.int32, (rows, GLA_DV), 1), 1.0, 0.0).astype(BF16)
    decay_t = jnp.exp(_dot_tn(hi, seq_sel) + _dot_tn(lo, seq_sel))
    decay = jnp.stack([jnp.broadcast_to(decay_t[:, j:j + 1], (GLA_KEY_WIDTH, GLA_DV)) for j in range(bs)], axis=0)
    outs, upds = [], []
    for p in range(GLA_HEADS // 2):
        h0, h1 = 2 * p, 2 * p + 1
        intra = _dot(a_all[:, p * 2 * rows:(p + 1) * 2 * rows],
                     _pair_block_diag(vb[:, h0 * GLA_DV:(h0 + 1) * GLA_DV], vb[:, h1 * GLA_DV:(h1 + 1) * GLA_DV]))
        s0 = s_bf[:, h0 * GLA_DK:(h0 + 1) * GLA_DK, :]
        s1 = s_bf[:, h1 * GLA_DK:(h1 + 1) * GLA_DK, :]
        z = jnp.zeros_like(s0)
        s_pair = jnp.concatenate(
            [jnp.concatenate([s0, z], axis=2), jnp.concatenate([z, s1], axis=2)], axis=1)
        inter = lax.dot_general(qs3[:, :, p * 2 * GLA_DK:(p + 1) * 2 * GLA_DK], s_pair, bdims_nn,
                                preferred_element_type=F32)
        outs.append(intra + inter.reshape(rows, 2 * GLA_DV))
        upd = lax.dot_general(kd3[:, :, p * 2 * GLA_DK:(p + 1) * 2 * GLA_DK],
                              v3[:, :, p * 2 * GLA_DV:(p + 1) * 2 * GLA_DV], bdims_tn,
                              preferred_element_type=F32)
        upds.append(upd[:, 0:GLA_DK, 0:GLA_DV])
        upds.append(upd[:, GLA_DK:2 * GLA_DK, GLA_DV:2 * GLA_DV])
        pump(2)
    o = jnp.concatenate(outs, axis=1)
    s_out_ref[...] = s_old * decay + jnp.concatenate(upds, axis=1)

    y = _epilogue(x, pool_act, o, gg, ng_ref, wout_ref, lng_ref, lnb_ref, pump)
    y_ref[...] = y.reshape(bs, t, D_MODEL)


def _sample_body(x_ref, pool_in_ref, s_in_ref, win_ref, wmap_ref, pscale_ref, ba_ref, ng_ref, wout_ref,
                 lng_ref, lnb_ref, y_ref, pool_ref, s_out_ref,
                 proj0_ref, proj1_ref, xkeep0_ref, xkeep1_ref, xb_ref, u_ref, pooled_ref,
                 *, bs, t, start_pos, n_groups):
    i = pl.program_id(0)
    rows = bs * t
    sets = ((proj0_ref, xkeep0_ref), (proj1_ref, xkeep1_ref))
    units = [(0, lo) for lo in range(0, C_END, MXU_COLS)]

    def step(set_a, set_b):
        pump = lambda n=None: None
        if set_a is not None:
            proj_a, xkeep_a = set_a
            x = x_ref[...].reshape(rows, D_MODEL)
            xkeep_a[:, 0:D_MODEL] = x
            xb_ref[:, 0:D_MODEL] = x.astype(BF16)
            pump = _make_pump(units, xb_ref, win_ref, proj_a, rows)
            pump(1 if set_b is not None else None)
        if set_b is not None:
            proj_b, xkeep_b = set_b
            _sample_stage_b(proj_b, xkeep_b, pool_in_ref, s_in_ref, wmap_ref, pscale_ref, ba_ref, ng_ref,
                            wout_ref, lng_ref, lnb_ref, y_ref, pool_ref, s_out_ref, u_ref, pooled_ref, pump,
                            bs=bs, t=t, start_pos=start_pos)

    @pl.when(i == 0)
    def _():
        step(sets[0], None)

    for parity in (0, 1):
        @pl.when((i > 0) & (i < n_groups) & (i % 2 == parity))
        def _(parity=parity):
            step(sets[parity], sets[1 - parity])

    @pl.when(i == n_groups)
    def _():
        step(None, sets[(n_groups - 1) % 2])


def _sample_call(x, pool_in, s_in, win, wmap, pscale, ba, ng, wout, lng, lnb, start_pos):
    nb, t, _ = x.shape
    bs = SAMPLE_SEQS
    n_groups = nb // bs
    const2 = lambda i: (0, 0)
    seq3_a = lambda i: (jnp.minimum(i, n_groups - 1), 0, 0)
    seq3 = lambda i: (jnp.maximum(i - 1, 0), 0, 0)
    hist3 = lambda i: (0, jnp.maximum(i - 1, 0), 0)
    return pl.pallas_call(
        functools.partial(_sample_body, bs=bs, t=t, start_pos=start_pos, n_groups=n_groups),
        grid=(n_groups + 1,),
        in_specs=[
            pl.BlockSpec((bs, t, D_MODEL), seq3_a),
            pl.BlockSpec((POOL_BUF, bs, POOL_WIDTH), hist3),
            pl.BlockSpec((bs, GLA_KEY_WIDTH, GLA_DV), seq3),
            pl.BlockSpec(win.shape, const2),
            pl.BlockSpec(wmap.shape, lambda i: (0, 0, 0)),
            pl.BlockSpec(pscale.shape, const2),
            pl.BlockSpec(ba.shape, const2),
            pl.BlockSpec(ng.shape, const2),
            pl.BlockSpec(wout.shape, const2),
            pl.BlockSpec(lng.shape, const2),
            pl.BlockSpec(lnb.shape, const2),
        ],
        out_specs=[
            pl.BlockSpec((bs, t, D_MODEL), seq3),
            pl.BlockSpec((POOL_BUF, bs, POOL_WIDTH), hist3),
            pl.BlockSpec((bs, GLA_KEY_WIDTH, GLA_DV), seq3),
        ],
        out_shape=[
            jax.ShapeDtypeStruct((nb, t, D_MODEL), F32),
            jax.ShapeDtypeStruct((POOL_BUF, nb, POOL_WIDTH), F32),
            jax.ShapeDtypeStruct((nb, GLA_KEY_WIDTH, GLA_DV), F32),
        ],
        scratch_shapes=[pltpu.VMEM((bs * t, C_END + PITCH_PAD), F32),
                        pltpu.VMEM((bs * t, C_END + PITCH_PAD), F32),
                        pltpu.VMEM((bs * t, D_MODEL + PITCH_PAD), F32),
                        pltpu.VMEM((bs * t, D_MODEL + PITCH_PAD), F32),
                        pltpu.VMEM((bs * t, D_MODEL + PITCH_PAD), BF16),
                        pltpu.VMEM((len(POOL_WINDOWS), bs * t, POOL_GROUP_DIM), F32),
                        pltpu.VMEM((len(POOL_WINDOWS), bs * t, POOL_GROUP_DIM), F32)],
        compiler_params=pltpu.CompilerParams(
            dimension_semantics=("arbitrary",), vmem_limit_bytes=VMEM_LIMIT),
        name="sample_layer",
    )(x, pool_in, s_in, win, wmap, pscale, ba, ng, wout, lng, lnb)


def _prep_body(wt_ref, wfat_ref, wa2_ref, wout_f32_ref, wmap_f32_ref, win_ref, wout_ref, wmap_ref, *, n_plain,
               n_wout):
    i = pl.program_id(0)

    @pl.when(i == 0)
    def _():
        wmap_ref[...] = wmap_f32_ref[...].astype(BF16)

    @pl.when(i < n_wout)
    def _():
        wout_ref[:, 0:D_MODEL] = wout_f32_ref[...].astype(BF16)
        wout_ref[:, D_MODEL:] = jnp.zeros((wout_ref.shape[0], PITCH_PAD), BF16)

    @pl.when(i < n_plain)
    def _():
        win_ref[...] = jnp.transpose(wt_ref[...]).astype(BF16)

    @pl.when(i == n_plain)
    def _():
        fa_hi, fa_lo = _split_bf16(wfat_ref[...])
        a2_hi, a2_lo = _split_bf16(wa2_ref[...])
        acc = _dot_tn(fa_hi, a2_hi) + (_dot_tn(fa_hi, a2_lo) + _dot_tn(fa_lo, a2_hi))
        win_ref[:, 0:GLA_KEY_WIDTH] = acc.astype(BF16)
        win_ref[:, GLA_KEY_WIDTH:] = jnp.zeros((D_MODEL, PREP_COLS - GLA_KEY_WIDTH), BF16)


def _prep_weights(wt, wa2, wout_f32, wmap_f32):
    assert C_GL % PREP_COLS == 0 and C_END + PITCH_PAD - C_GL <= PREP_COLS and C_GL % GLA_RANK == 0
    n_plain = C_GL // PREP_COLS
    n_wout = 4
    wout_rows = wout_f32.shape[0] // n_wout
    assert n_wout <= n_plain + 1
    wout_block = lambda i: (jnp.minimum(i, n_wout - 1), 0)
    return pl.pallas_call(
        functools.partial(_prep_body, n_plain=n_plain, n_wout=n_wout),
        grid=(n_plain + 1,),
        in_specs=[
            pl.BlockSpec((PREP_COLS, D_MODEL), lambda i: (jnp.minimum(i, n_plain - 1), 0)),
            pl.BlockSpec((GLA_RANK, D_MODEL), lambda i: (C_GL // GLA_RANK, 0)),
            pl.BlockSpec(wa2.shape, lambda i: (0, 0)),
            pl.BlockSpec((wout_rows, D_MODEL), wout_block),
            pl.BlockSpec(wmap_f32.shape, lambda i: (0, 0, 0)),
        ],
        out_specs=[
            pl.BlockSpec((D_MODEL, PREP_COLS), lambda i: (0, i)),
            pl.BlockSpec((wout_rows, D_MODEL + PITCH_PAD), wout_block),
            pl.BlockSpec(wmap_f32.shape, lambda i: (0, 0, 0)),
        ],
        out_shape=[
            jax.ShapeDtypeStruct((D_MODEL, C_END + PITCH_PAD), BF16),
            jax.ShapeDtypeStruct((D_MODEL, D_MODEL + PITCH_PAD), BF16),
            jax.ShapeDtypeStruct(wmap_f32.shape, BF16),
        ],
        compiler_params=pltpu.CompilerParams(dimension_semantics=("arbitrary",), vmem_limit_bytes=VMEM_LIMIT),
        name="prep_weights",
    )(wt, wt, wa2, wout_f32, wmap_f32)


def kernel(x_prompt, x_sample, state_pool, state_gla, w_in, w_pool_map, pool_scale, w_gla_a2, b_gla_a,
           gla_norm_g, w_out, ln_g, ln_b):
    assert w_in.shape[0] == DEPTH
    win, wout, wmap = _prep_weights(jnp.transpose(w_in[0]), w_gla_a2[0], w_out[0], w_pool_map[0])
    pscale = pool_scale[0].reshape(1, POOL_WIDTH)
    ba = b_gla_a[0].reshape(1, GLA_KEY_WIDTH)
    ng = gla_norm_g[0].reshape(1, GLA_WIDTH)
    lng = ln_g[0].reshape(1, D_MODEL)
    lnb = ln_b[0].reshape(1, D_MODEL)
    params = (win, wmap, pscale, ba, ng, wout, lng, lnb)

    y_p, pool_p, gla_p = _prompt_call(x_prompt, *params)
    nsamp = x_sample.shape[0]
    y_s, pool_s, gla_s = _sample_call(
        x_sample, jnp.transpose(state_pool[0], (1, 0, 2)),
        state_gla[0].reshape(nsamp, GLA_KEY_WIDTH, GLA_DV), *params, start_pos=PAST_LEN)

    nb = x_prompt.shape[0]
    return (y_p, y_s, jnp.transpose(pool_p, (1, 0, 2))[None],
            gla_p.reshape(1, nb, GLA_HEADS, GLA_DK, GLA_DV),
            jnp.transpose(pool_s, (1, 0, 2))[None],
            gla_s.reshape(1, nsamp, GLA_HEADS, GLA_DK, GLA_DV))
```

```python
import functools
import itertools

import jax
import jax.numpy as jnp
from jax import lax
from jax.experimental import pallas as pl
from jax.experimental.pallas import tpu as pltpu

F32 = jnp.float32
BF16 = jnp.bfloat16

D_MODEL = 1024
POOL_WIDTH = 512
POOL_WINDOWS = (2, 4, 8, 16)
POOL_GROUP_DIM = 128
POOL_BUF = 15
HIST_ROWS = 16
GLA_HEADS = 4
GLA_DK = 64
GLA_DV = 128
GLA_KEY_WIDTH = GLA_HEADS * GLA_DK
GLA_WIDTH = GLA_HEADS * GLA_DV
GLA_RANK = 16
GLA_TAU = 16.0
LN_EPS = 1e-5
RMS_EPS = 1e-6
DEPTH = 1
DEEPNORM_ALPHA = (2.0 * DEPTH) ** 0.25
PAST_LEN = 16384

C_U, C_GP, C_Q, C_K, C_V, C_GG, C_GL, C_END = 0, 512, 1024, 1280, 1536, 2048, 2560, 2816

MXU_COLS = 256
PROMPT_TILE = 256
PITCH_PAD = 128
PROMPT_SUBTILE = 256
SUBTILE_TAIL_UNITS = 0
PUMP_ROWS = 256
PUMP_PLAN = {"pool": (0, 1, 0, 1), "cumsum": 0, "scores": (1, 1, 1, 1), "level": (0, 0, 0, 1, 0, 0),
             "state": (0, 0, 0, 0), "epilogue": 1, "layernorm": (1, 1, 1, 0)}
SAFE_LOG_DECAY = 60.0
GLA_CHUNK = 64
SAMPLE_SEQS = 16
PREP_COLS = 512
VMEM_LIMIT = 56 * 1024 * 1024


def _dot(a, b):
    return jnp.dot(a, b, preferred_element_type=F32)


def _dot_nt(a, b):
    return lax.dot_general(a, b, (((1,), (1,)), ((), ())), preferred_element_type=F32)


def _dot_tn(a, b):
    return lax.dot_general(a, b, (((0,), (0,)), ((), ())), preferred_element_type=F32)


def _silu(x):
    return x * (1.0 / (1.0 + jnp.exp(-x)))


def _log_sigmoid(x):
    return jnp.minimum(x, 0.0) - jnp.log1p(jnp.exp(-jnp.abs(x)))


def _split_bf16(x):
    hi = x.astype(BF16)
    lo = (x - hi.astype(F32)).astype(BF16)
    return hi, lo


def _epilogue(x, pool_act, o, gg, ng_ref, wout_ref, lng_ref, lnb_ref, pump=lambda n=None: None, tail=None,
              row_blocks=1):
    pump(PUMP_PLAN["epilogue"])
    normed = []
    for h in range(GLA_HEADS):
        oh = o[:, h * GLA_DV:(h + 1) * GLA_DV]
        ms = jnp.mean(oh * oh, axis=-1, keepdims=True)
        normed.append(oh * lax.rsqrt(ms + RMS_EPS))
    gla_act = jnp.concatenate(normed, axis=1) * ng_ref[...] * _silu(gg)
    mix = jnp.concatenate([pool_act, gla_act], axis=1).astype(BF16)
    y = _dot(mix, wout_ref[:, 0:D_MODEL])
    n = x.shape[0] // row_blocks
    outs = []
    for r in range(row_blocks):
        rows = slice(r * n, (r + 1) * n)
        pump(PUMP_PLAN["layernorm"][r])
        h = DEEPNORM_ALPHA * x[rows] + y[rows]
        mu = jnp.mean(h, axis=-1, keepdims=True)
        d = h - mu
        var = jnp.mean(d * d, axis=-1, keepdims=True)
        outs.append(d * lax.rsqrt(var + LN_EPS) * lng_ref[...] + lnb_ref[...])
    pump(tail)
    return outs[0] if row_blocks == 1 else jnp.concatenate(outs, axis=0)


def _head_block_mask(rows_per_head, cols_per_head, shape):
    r = lax.broadcasted_iota(jnp.int32, shape, 0) // rows_per_head
    c = lax.broadcasted_iota(jnp.int32, shape, 1) // cols_per_head
    return r == c


def _score_masks(rows, n_levels):
    shape = (rows, GLA_HEADS * rows)
    t_idx = lax.broadcasted_iota(jnp.int32, shape, 0)
    s_idx = lax.broadcasted_iota(jnp.int32, shape, 1) % rows
    masks = [t_idx == s_idx]
    for lvl in range(n_levels):
        half = 1 << lvl
        blk = 2 * half
        masks.append((t_idx // blk == s_idx // blk) & (t_idx % blk >= half) & (s_idx % blk < half))
    return masks


def _decayed_scores(q, k, la, b, masks, head_blk, pump=lambda n=None: None):
    rows = q.shape[0]
    r_idx = lax.broadcasted_iota(jnp.int32, q.shape, 0)

    def scores(qt, kt):
        kh = jnp.where(head_blk, jnp.concatenate([kt.astype(BF16)] * GLA_HEADS, axis=0), jnp.zeros((), BF16))
        return _dot_nt(qt.astype(BF16), kh)

    acc = jnp.where(masks[0], scores(q, k), 0.0)
    for lvl in range(len(masks) - 1):
        half = 1 << lvl
        blk = 2 * half
        if lvl == 0:
            neg = jnp.where(r_idx % 2 == 1, la, 0.0)
        elif lvl == 1:
            r4 = r_idx % 4
            nxt = pltpu.roll(la, rows - 1, 0)
            prv = pltpu.roll(la, 1, 0)
            neg = jnp.where(r4 == 0, nxt, jnp.where(r4 == 1, 0.0, jnp.where(r4 == 2, la, la + prv)))
        else:
            ref = jnp.concatenate(
                [jnp.broadcast_to(b[j * blk + half - 1:j * blk + half, :], (blk, b.shape[1]))
                 for j in range(rows // blk)], axis=0)
            d = b - ref
            neg = jnp.minimum(d, -d)
        w = jnp.exp(neg)
        acc = jnp.where(masks[lvl + 1], scores(q * w, k * w), acc)
        pump(PUMP_PLAN["level"][lvl])
    return acc.astype(BF16)


def _pair_block_diag(a0, a1):
    z = jnp.zeros_like(a0)
    return jnp.concatenate(
        [jnp.concatenate([a0, z], axis=1), jnp.concatenate([z, a1], axis=1)], axis=0)


def _make_pump(units, xb_ref, win_ref, proj_ref, unit_rows=PUMP_ROWS):
    units = iter(units)

    def pump(n=None):
        for r, lo in (units if n is None else itertools.islice(units, n)):
            proj_ref[r:r + unit_rows, lo:lo + MXU_COLS] = _dot(
                xb_ref[r:r + unit_rows, 0:D_MODEL], win_ref[:, lo:lo + MXU_COLS])

    return pump


def _naive_scores(q, k, b, causal, head_blk):
    ks = (k * jnp.exp(-b)).astype(BF16)
    kh = jnp.where(head_blk, jnp.concatenate([ks] * GLA_HEADS, axis=0), jnp.zeros((), BF16))
    return jnp.where(causal, _dot_nt((q * jnp.exp(b)).astype(BF16), kh), 0.0).astype(BF16)


def _prompt_stage_b(pos0, proj_ref, xkeep_ref, la_ref, wmap_ref, pscale_ref, ba_ref, ng_ref, wout_ref, lng_ref,
                    lnb_ref, y_ref, ext_ref, s_ref, pump, next_decay, *, tc, robust, tail):
    u = proj_ref[:, C_U:C_GP]
    ext_ref[HIST_ROWS:HIST_ROWS + tc, 0:POOL_WIDTH] = u
    pos16 = pos0 + lax.broadcasted_iota(jnp.int32, (HIST_ROWS, POOL_GROUP_DIM), 0)
    mixed = []
    for g, w in enumerate(POOL_WINDOWS):
        sl = slice(g * POOL_GROUP_DIM, (g + 1) * POOL_GROUP_DIM)
        e = ext_ref[:, sl]
        s = e
        sh = 1
        while sh < w:
            s = s + pltpu.roll(s, sh, 0)
            sh *= 2
        wsum = s[HIST_ROWS:]
        self_rows = e[HIST_ROWS:]
        cnt = jnp.minimum(pos16 + 1, w).astype(F32)
        first = wsum[:HIST_ROWS] / cnt - self_rows[:HIST_ROWS]
        rest = wsum[HIST_ROWS:] * (1.0 / w) - self_rows[HIST_ROWS:]
        pooled = jnp.concatenate([first, rest], axis=0)
        mixed.append(_dot(pooled.astype(BF16), wmap_ref[g]))
        pump(PUMP_PLAN["pool"][g])
    pool_act = jnp.concatenate(mixed, axis=1) * pscale_ref[...] * _silu(proj_ref[:, C_GP:C_Q])
    next_decay()

    cc = GLA_CHUNK
    n_chunks = tc // cc
    rr = lax.broadcasted_iota(jnp.int32, (tc, tc), 0)
    cl = lax.broadcasted_iota(jnp.int32, (tc, tc), 1)
    ltri = jnp.where((rr // cc == cl // cc) & (cl <= rr), 1.0, 0.0).astype(BF16)
    masks = _score_masks(cc, cc.bit_length() - 1 if robust else 0)
    causal = (lax.broadcasted_iota(jnp.int32, (cc, GLA_HEADS * cc), 1) % cc
              <= lax.broadcasted_iota(jnp.int32, (cc, GLA_HEADS * cc), 0))
    blk = _head_block_mask(cc, GLA_DK, (GLA_HEADS * cc, GLA_KEY_WIDTH))

    hi, lo = _split_bf16(la_ref[:, 0:GLA_KEY_WIDTH])
    b_all = _dot(ltri, hi) + _dot(ltri, lo)
    pump(PUMP_PLAN["cumsum"])
    lhs_list, v_list, upd_list, dcol_list = [], [], [], []
    for c in range(n_chunks):
        rs = slice(c * cc, (c + 1) * cc)
        b = b_all[rs]
        b_end = b[cc - 1:cc, :]
        q_c = proj_ref[rs, C_Q:C_K] * (GLA_DK ** -0.5)
        k_c = proj_ref[rs, C_K:C_V]
        qs = (q_c * jnp.exp(b)).astype(BF16)
        kd = (k_c * jnp.exp(b_end - b)).astype(BF16)
        if robust:
            a_all = _decayed_scores(q_c, k_c, la_ref[rs, 0:GLA_KEY_WIDTH], b, masks, blk, pump)
        else:
            a_all = _naive_scores(q_c, k_c, b, causal, blk)
            pump(PUMP_PLAN["scores"][c])
        vc = proj_ref[rs, C_V:C_GG].astype(BF16)
        upds = []
        for p in range(GLA_HEADS // 2):
            upd = _dot_tn(kd[:, p * 2 * GLA_DK:(p + 1) * 2 * GLA_DK], vc[:, p * 2 * GLA_DV:(p + 1) * 2 * GLA_DV])
            upds.append(upd[0:GLA_DK, 0:GLA_DV])
            upds.append(upd[GLA_DK:2 * GLA_DK, GLA_DV:2 * GLA_DV])
        lhs_list.append([jnp.concatenate([a_all[:, p * 2 * cc:(p + 1) * 2 * cc],
                                          qs[:, p * 2 * GLA_DK:(p + 1) * 2 * GLA_DK]], axis=1)
                         for p in range(GLA_HEADS // 2)])
        v_list.append(vc)
        upd_list.append(jnp.concatenate(upds, axis=0))
        dcol_list.append(jnp.transpose(jnp.broadcast_to(jnp.exp(b_end), (GLA_DV, GLA_KEY_WIDTH))))

    o_chunks = []
    state = s_ref[...]
    for c in range(n_chunks):
        s_bf = state.astype(BF16)
        vc = v_list[c]
        outs = []
        for p in range(GLA_HEADS // 2):
            h0, h1 = 2 * p, 2 * p + 1
            rhs = jnp.concatenate([
                _pair_block_diag(vc[:, h0 * GLA_DV:(h0 + 1) * GLA_DV], vc[:, h1 * GLA_DV:(h1 + 1) * GLA_DV]),
                _pair_block_diag(s_bf[h0 * GLA_DK:(h0 + 1) * GLA_DK], s_bf[h1 * GLA_DK:(h1 + 1) * GLA_DK]),
            ], axis=0)
            outs.append(_dot(lhs_list[c][p], rhs))
        o_chunks.append(jnp.concatenate(outs, axis=1))
        state = state * dcol_list[c] + upd_list[c]
        pump(PUMP_PLAN["state"][c])
    s_ref[...] = state
    o = jnp.concatenate(o_chunks, axis=0)

    y_ref[...] = _epilogue(xkeep_ref[:, 0:D_MODEL], pool_act, o, proj_ref[:, C_GG:C_GL], ng_ref, wout_ref, lng_ref, lnb_ref,
                           pump, tail, row_blocks=4)


def _prompt_body(x_ref, win_ref, wmap_ref, pscale_ref, ba_ref, ng_ref, wout_ref, lng_ref, lnb_ref,
                 y_ref, pool_ref, s_out_ref,
                 proj0_ref, proj1_ref, xkeep0_ref, xkeep1_ref, la0_ref, la1_ref, xb_ref, ext_ref, s_ref, steep_ref,
                 *, tc, sub, nj, ntiles):
    i = pl.program_id(0)
    jb = (i + nj - 1) % nj

    @pl.when(i == 0)
    def _():
        steep_ref[0] = 0
        pool_ref[...] = jnp.zeros_like(pool_ref)

    @pl.when(jb == 0)
    def _():
        ext_ref[0:HIST_ROWS, 0:POOL_WIDTH] = jnp.zeros((HIST_ROWS, POOL_WIDTH), F32)
        s_ref[...] = jnp.zeros_like(s_ref)

    units = [(r, lo) for lo in (C_GL, *range(0, C_GL, MXU_COLS)) for r in range(0, tc, PUMP_ROWS)]

    def step(set_a, set_b, robust):
        pump, next_decay = (lambda n=None: None), (lambda: None)
        if set_a is not None:
            proj_a, xkeep_a, la_a = set_a
            x = x_ref[0]
            xkeep_a[:, 0:D_MODEL] = x
            xb_ref[:, 0:D_MODEL] = x.astype(BF16)
            pump = _make_pump(units, xb_ref, win_ref, proj_a)

            def next_decay():
                la = _log_sigmoid(proj_a[:, C_GL:C_END] + ba_ref[...]) * (1.0 / GLA_TAU)
                la_a[:, 0:GLA_KEY_WIDTH] = la
                chunk_decay = [jnp.sum(la[c * GLA_CHUNK:(c + 1) * GLA_CHUNK], axis=0, keepdims=True)
                               for c in range(tc // GLA_CHUNK)]
                lowest = jnp.min(functools.reduce(jnp.minimum, chunk_decay))
                steep_ref[0] = jnp.where(lowest < -SAFE_LOG_DECAY, 1, 0).astype(jnp.int32)

            if set_b is None:
                pump()
                next_decay()
                return
            pump(tc // PUMP_ROWS)

        proj_b, xkeep_b, la_b = set_b
        n_sub = tc // sub
        for k in range(n_sub):
            rows = slice(k * sub, (k + 1) * sub)
            last = k == n_sub - 1
            _prompt_stage_b((jb * n_sub + k) * sub, proj_b.at[rows, :], xkeep_b.at[rows, :], la_b.at[rows, :],
                            wmap_ref, pscale_ref, ba_ref, ng_ref, wout_ref, lng_ref, lnb_ref,
                            y_ref.at[0, rows, :], ext_ref, s_ref, pump, next_decay if k == 0 else (lambda: None),
                            tc=sub, robust=robust, tail=None if last else SUBTILE_TAIL_UNITS)
            if not last:
                ext_ref[0:HIST_ROWS, 0:POOL_WIDTH] = ext_ref[sub:sub + HIST_ROWS, 0:POOL_WIDTH]

    sets = ((proj0_ref, xkeep0_ref, la0_ref), (proj1_ref, xkeep1_ref, la1_ref))
    steep = steep_ref[0] != 0

    @pl.when(i == 0)
    def _():
        step(sets[0], None, False)

    for robust in (False, True):
        for parity in (0, 1):
            @pl.when((i > 0) & (i < ntiles) & (i % 2 == parity) & (steep == robust))
            def _(parity=parity, robust=robust):
                step(sets[parity], sets[1 - parity], robust)

        @pl.when((i == ntiles) & (steep == robust))
        def _(robust=robust):
            step(None, sets[(ntiles - 1) % 2], robust)

    @pl.when((i > 0) & (jb == nj - 1))
    def _():
        mine = lax.broadcasted_iota(jnp.int32, pool_ref.shape[1:], 0) == (i - 1) // nj
        for r in range(POOL_BUF):
            row = ext_ref[sub + 1 + r:sub + 2 + r, 0:POOL_WIDTH]
            pool_ref[r] = jnp.where(mine, jnp.broadcast_to(row, pool_ref.shape[1:]), pool_ref[r])
        s_out_ref[0] = s_ref[...]

    @pl.when(i > 0)
    def _():
        ext_ref[0:HIST_ROWS, 0:POOL_WIDTH] = ext_ref[sub:sub + HIST_ROWS, 0:POOL_WIDTH]


def _prompt_call(x, win, wmap, pscale, ba, ng, wout, lng, lnb):
    nb, t, _ = x.shape
    tc = PROMPT_TILE
    nj = t // tc
    ntiles = nb * nj
    const2 = lambda i: (0, 0)

    def tile_a(i):
        ia = jnp.minimum(i, ntiles - 1)
        return (ia // nj, ia % nj, 0)

    def tile_b(i):
        ib = jnp.maximum(i - 1, 0)
        return (ib // nj, ib % nj, 0)

    def seq_b(i):
        return (jnp.maximum(i - 1, 0) // nj, 0, 0)

    return pl.pallas_call(
        functools.partial(_prompt_body, tc=tc, sub=PROMPT_SUBTILE, nj=nj, ntiles=ntiles),
        grid=(ntiles + 1,),
        in_specs=[
            pl.BlockSpec((1, tc, D_MODEL), tile_a),
            pl.BlockSpec(win.shape, const2),
            pl.BlockSpec(wmap.shape, lambda i: (0, 0, 0)),
            pl.BlockSpec(pscale.shape, const2),
            pl.BlockSpec(ba.shape, const2),
            pl.BlockSpec(ng.shape, const2),
            pl.BlockSpec(wout.shape, const2),
            pl.BlockSpec(lng.shape, const2),
            pl.BlockSpec(lnb.shape, const2),
        ],
        out_specs=[
            pl.BlockSpec((1, tc, D_MODEL), tile_b),
            pl.BlockSpec((POOL_BUF, nb, POOL_WIDTH), lambda i: (0, 0, 0)),
            pl.BlockSpec((1, GLA_KEY_WIDTH, GLA_DV), seq_b),
        ],
        out_shape=[
            jax.ShapeDtypeStruct((nb, t, D_MODEL), F32),
            jax.ShapeDtypeStruct((POOL_BUF, nb, POOL_WIDTH), F32),
            jax.ShapeDtypeStruct((nb, GLA_KEY_WIDTH, GLA_DV), F32),
        ],
        scratch_shapes=[
            pltpu.VMEM((tc, C_END + PITCH_PAD), F32),
            pltpu.VMEM((tc, C_END + PITCH_PAD), F32),
            pltpu.VMEM((tc, D_MODEL + PITCH_PAD), F32),
            pltpu.VMEM((tc, D_MODEL + PITCH_PAD), F32),
            pltpu.VMEM((tc, GLA_KEY_WIDTH + PITCH_PAD), F32),
            pltpu.VMEM((tc, GLA_KEY_WIDTH + PITCH_PAD), F32),
            pltpu.VMEM((tc, D_MODEL + PITCH_PAD), BF16),
            pltpu.VMEM((HIST_ROWS + PROMPT_SUBTILE, POOL_WIDTH + PITCH_PAD), F32),
            pltpu.VMEM((GLA_KEY_WIDTH, GLA_DV), F32),
            pltpu.SMEM((1,), jnp.int32),
        ],
        compiler_params=pltpu.CompilerParams(
            dimension_semantics=("arbitrary",), vmem_limit_bytes=VMEM_LIMIT),
        name="prompt_layer",
    )(x, win, wmap, pscale, ba, ng, wout, lng, lnb)


def _sample_stage_b(proj_ref, xkeep_ref, pool_in_ref, s_in_ref, wmap_ref, pscale_ref, ba_ref, ng_ref, wout_ref,
                    lng_ref, lnb_ref, y_ref, pool_ref, s_out_ref, u_ref, pooled_ref, pump, *, bs, t, start_pos):
    rows = bs * t
    x = xkeep_ref[:, 0:D_MODEL]

    def proj(lo, hi):
        return proj_ref[:, lo:hi]

    u = proj(C_U, C_GP)
    mixed = []
    for g, w in enumerate(POOL_WINDOWS):
        sl = slice(g * POOL_GROUP_DIM, (g + 1) * POOL_GROUP_DIM)
        u_ref[g] = u[:, sl]
        ext = [pool_in_ref[r, :, sl] for r in range(POOL_BUF)]
        ext += [u_ref[g, pl.ds(tok, bs, stride=t), :] for tok in range(t)]
        for r in range(POOL_BUF):
            pool_ref[r, :, sl] = ext[t + r]
        sums = ext
        sh = 1
        while sh < w:
            first = 2 * sh - 1
            sums = [None] * first + [sums[r] + sums[r - sh] for r in range(first, len(sums))]
            sh *= 2
        for tok in range(t):
            cnt = float(min(start_pos + tok + 1, w))
            pooled_ref[g, pl.ds(tok, bs, stride=t), :] = sums[POOL_BUF + tok] / cnt - ext[POOL_BUF + tok]
        mixed.append(_dot(pooled_ref[g].astype(BF16), wmap_ref[g]))
        pump(g % 2)
    gp = proj(C_GP, C_Q)
    pool_act = jnp.concatenate(mixed, axis=1) * pscale_ref[...] * _silu(gp)

    q = proj(C_Q, C_K) * (GLA_DK ** -0.5)
    k = proj(C_K, C_V)
    vb = proj(C_V, C_GG).astype(BF16)
    gg = proj(C_GG, C_GL)
    la = _log_sigmoid(proj(C_GL, C_END) + ba_ref[...]) * (1.0 / GLA_TAU)

    rr = lax.broadcasted_iota(jnp.int32, (rows, rows), 0)
    cl = lax.broadcasted_iota(jnp.int32, (rows, rows), 1)
    same = (rr // t) == (cl // t)
    ltri = jnp.where(same & (cl <= rr), 1.0, 0.0).astype(BF16)
    lall = jnp.where(same, 1.0, 0.0).astype(BF16)
    hi, lo = _split_bf16(la)
    b = _dot(ltri, hi) + _dot(ltri, lo)
    b_end = _dot(lall, hi) + _dot(lall, lo)
    pump(1)
    qs = (q * jnp.exp(b)).astype(BF16)
    kd = (k * jnp.exp(b_end - b)).astype(BF16)

    assert t & (t - 1) == 0, "tokens per sequence must be a power of two"
    blk = _head_block_mask(rows, GLA_DK, (GLA_HEADS * rows, GLA_KEY_WIDTH))
    a_all = _decayed_scores(q, k, la, b, _score_masks(rows, t.bit_length() - 1), blk)
    pump(2)

    s_old = s_in_ref[...]
    s_bf = s_old.astype(BF16)
    qs3 = qs.reshape(bs, t, GLA_KEY_WIDTH)
    kd3 = kd.reshape(bs, t, GLA_KEY_WIDTH)
    v3 = vb.reshape(bs, t, GLA_WIDTH)
    bdims_nn = (((2,), (1,)), ((0,), (0,)))
    bdims_tn = (((1,), (1,)), ((0,), (0,)))
    assert bs <= GLA_DV
    seq_sel = jnp.where(lax.broadcasted_iota(jnp.int32, (rows, GLA_DV), 0) // t
                        == lax.broadcasted_iota(jnp.int32, (rows, GLA_DV), 1), 1.0, 0.0).astype(BF16)
    decay_t = jnp.exp(_dot_tn(hi, seq_sel) + _dot_tn(lo, seq_sel))
    decay = jnp.stack([jnp.broadcast_to(decay_t[:, j:j + 1], (GLA_KEY_WIDTH, GLA_DV)) for j in range(bs)], axis=0)
    outs, upds = [], []
    for p in range(GLA_HEADS // 2):
        h0, h1 = 2 * p, 2 * p + 1
        intra = _dot(a_all[:, p * 2 * rows:(p + 1) * 2 * rows],
                     _pair_block_diag(vb[:, h0 * GLA_DV:(h0 + 1) * GLA_DV], vb[:, h1 * GLA_DV:(h1 + 1) * GLA_DV]))
        s0 = s_bf[:, h0 * GLA_DK:(h0 + 1) * GLA_DK, :]
        s1 = s_bf[:, h1 * GLA_DK:(h1 + 1) * GLA_DK, :]
        z = jnp.zeros_like(s0)
        s_pair = jnp.concatenate(
            [jnp.concatenate([s0, z], axis=2), jnp.concatenate([z, s1], axis=2)], axis=1)
        inter = lax.dot_general(qs3[:, :, p * 2 * GLA_DK:(p + 1) * 2 * GLA_DK], s_pair, bdims_nn,
                                preferred_element_type=F32)
        outs.append(intra + inter.reshape(rows, 2 * GLA_DV))
        upd = lax.dot_general(kd3[:, :, p * 2 * GLA_DK:(p + 1) * 2 * GLA_DK],
                              v3[:, :, p * 2 * GLA_DV:(p + 1) * 2 * GLA_DV], bdims_tn,
                              preferred_element_type=F32)
        upds.append(upd[:, 0:GLA_DK, 0:GLA_DV])
        upds.append(upd[:, GLA_DK:2 * GLA_DK, GLA_DV:2 * GLA_DV])
        pump(2)
    o = jnp.concatenate(outs, axis=1)
    s_out_ref[...] = s_old * decay + jnp.concatenate(upds, axis=1)

    y = _epilogue(x, pool_act, o, gg, ng_ref, wout_ref, lng_ref, lnb_ref, pump)
    y_ref[...] = y.reshape(bs, t, D_MODEL)


def _sample_body(x_ref, pool_in_ref, s_in_ref, win_ref, wmap_ref, pscale_ref, ba_ref, ng_ref, wout_ref,
                 lng_ref, lnb_ref, y_ref, pool_ref, s_out_ref,
                 proj0_ref, proj1_ref, xkeep0_ref, xkeep1_ref, xb_ref, u_ref, pooled_ref,
                 *, bs, t, start_pos, n_groups):
    i = pl.program_id(0)
    rows = bs * t
    sets = ((proj0_ref, xkeep0_ref), (proj1_ref, xkeep1_ref))
    units = [(0, lo) for lo in range(0, C_END, MXU_COLS)]

    def step(set_a, set_b):
        pump = lambda n=None: None
        if set_a is not None:
            proj_a, xkeep_a = set_a
            x = x_ref[...].reshape(rows, D_MODEL)
            xkeep_a[:, 0:D_MODEL] = x
            xb_ref[:, 0:D_MODEL] = x.astype(BF16)
            pump = _make_pump(units, xb_ref, win_ref, proj_a, rows)
            pump(1 if set_b is not None else None)
        if set_b is not None:
            proj_b, xkeep_b = set_b
            _sample_stage_b(proj_b, xkeep_b, pool_in_ref, s_in_ref, wmap_ref, pscale_ref, ba_ref, ng_ref,
                            wout_ref, lng_ref, lnb_ref, y_ref, pool_ref, s_out_ref, u_ref, pooled_ref, pump,
                            bs=bs, t=t, start_pos=start_pos)

    @pl.when(i == 0)
    def _():
        step(sets[0], None)

    for parity in (0, 1):
        @pl.when((i > 0) & (i < n_groups) & (i % 2 == parity))
        def _(parity=parity):
            step(sets[parity], sets[1 - parity])

    @pl.when(i == n_groups)
    def _():
        step(None, sets[(n_groups - 1) % 2])


def _sample_call(x, pool_in, s_in, win, wmap, pscale, ba, ng, wout, lng, lnb, start_pos):
    nb, t, _ = x.shape
    bs = SAMPLE_SEQS
    n_groups = nb // bs
    const2 = lambda i: (0, 0)
    seq3_a = lambda i: (jnp.minimum(i, n_groups - 1), 0, 0)
    seq3 = lambda i: (jnp.maximum(i - 1, 0), 0, 0)
    hist3 = lambda i: (0, jnp.maximum(i - 1, 0), 0)
    return pl.pallas_call(
        functools.partial(_sample_body, bs=bs, t=t, start_pos=start_pos, n_groups=n_groups),
        grid=(n_groups + 1,),
        in_specs=[
            pl.BlockSpec((bs, t, D_MODEL), seq3_a),
            pl.BlockSpec((POOL_BUF, bs, POOL_WIDTH), hist3),
            pl.BlockSpec((bs, GLA_KEY_WIDTH, GLA_DV), seq3),
            pl.BlockSpec(win.shape, const2),
            pl.BlockSpec(wmap.shape, lambda i: (0, 0, 0)),
            pl.BlockSpec(pscale.shape, const2),
            pl.BlockSpec(ba.shape, const2),
            pl.BlockSpec(ng.shape, const2),
            pl.BlockSpec(wout.shape, const2),
            pl.BlockSpec(lng.shape, const2),
            pl.BlockSpec(lnb.shape, const2),
        ],
        out_specs=[
            pl.BlockSpec((bs, t, D_MODEL), seq3),
            pl.BlockSpec((POOL_BUF, bs, POOL_WIDTH), hist3),
            pl.BlockSpec((bs, GLA_KEY_WIDTH, GLA_DV), seq3),
        ],
        out_shape=[
            jax.ShapeDtypeStruct((nb, t, D_MODEL), F32),
            jax.ShapeDtypeStruct((POOL_BUF, nb, POOL_WIDTH), F32),
            jax.ShapeDtypeStruct((nb, GLA_KEY_WIDTH, GLA_DV), F32),
        ],
        scratch_shapes=[pltpu.VMEM((bs * t, C_END + PITCH_PAD), F32),
                        pltpu.VMEM((bs * t, C_END + PITCH_PAD), F32),
                        pltpu.VMEM((bs * t, D_MODEL + PITCH_PAD), F32),
                        pltpu.VMEM((bs * t, D_MODEL + PITCH_PAD), F32),
                        pltpu.VMEM((bs * t, D_MODEL + PITCH_PAD), BF16),
                        pltpu.VMEM((len(POOL_WINDOWS), bs * t, POOL_GROUP_DIM), F32),
                        pltpu.VMEM((len(POOL_WINDOWS), bs * t, POOL_GROUP_DIM), F32)],
        compiler_params=pltpu.CompilerParams(
            dimension_semantics=("arbitrary",), vmem_limit_bytes=VMEM_LIMIT),
        name="sample_layer",
    )(x, pool_in, s_in, win, wmap, pscale, ba, ng, wout, lng, lnb)


def _prep_body(wt_ref, wfat_ref, wa2_ref, wout_f32_ref, wmap_f32_ref, win_ref, wout_ref, wmap_ref, *, n_plain,
               n_wout):
    i = pl.program_id(0)

    @pl.when(i == 0)
    def _():
        wmap_ref[...] = wmap_f32_ref[...].astype(BF16)

    @pl.when(i < n_wout)
    def _():
        wout_ref[:, 0:D_MODEL] = wout_f32_ref[...].astype(BF16)
        wout_ref[:, D_MODEL:] = jnp.zeros((wout_ref.shape[0], PITCH_PAD), BF16)

    @pl.when(i < n_plain)
    def _():
        win_ref[...] = jnp.transpose(wt_ref[...]).astype(BF16)

    @pl.when(i == n_plain)
    def _():
        fa_hi, fa_lo = _split_bf16(wfat_ref[...])
        a2_hi, a2_lo = _split_bf16(wa2_ref[...])
        acc = _dot_tn(fa_hi, a2_hi) + (_dot_tn(fa_hi, a2_lo) + _dot_tn(fa_lo, a2_hi))
        win_ref[:, 0:GLA_KEY_WIDTH] = acc.astype(BF16)
        win_ref[:, GLA_KEY_WIDTH:] = jnp.zeros((D_MODEL, PREP_COLS - GLA_KEY_WIDTH), BF16)


def _prep_weights(wt, wa2, wout_f32, wmap_f32):
    assert C_GL % PREP_COLS == 0 and C_END + PITCH_PAD - C_GL <= PREP_COLS and C_GL % GLA_RANK == 0
    n_plain = C_GL // PREP_COLS
    n_wout = 4
    wout_rows = wout_f32.shape[0] // n_wout
    assert n_wout <= n_plain + 1
    wout_block = lambda i: (jnp.minimum(i, n_wout - 1), 0)
    return pl.pallas_call(
        functools.partial(_prep_body, n_plain=n_plain, n_wout=n_wout),
        grid=(n_plain + 1,),
        in_specs=[
            pl.BlockSpec((PREP_COLS, D_MODEL), lambda i: (jnp.minimum(i, n_plain - 1), 0)),
            pl.BlockSpec((GLA_RANK, D_MODEL), lambda i: (C_GL // GLA_RANK, 0)),
            pl.BlockSpec(wa2.shape, lambda i: (0, 0)),
            pl.BlockSpec((wout_rows, D_MODEL), wout_block),
            pl.BlockSpec(wmap_f32.shape, lambda i: (0, 0, 0)),
        ],
        out_specs=[
            pl.BlockSpec((D_MODEL, PREP_COLS), lambda i: (0, i)),
            pl.BlockSpec((wout_rows, D_MODEL + PITCH_PAD), wout_block),
            pl.BlockSpec(wmap_f32.shape, lambda i: (0, 0, 0)),
        ],
        out_shape=[
            jax.ShapeDtypeStruct((D_MODEL, C_END + PITCH_PAD), BF16),
            jax.ShapeDtypeStruct((D_MODEL, D_MODEL + PITCH_PAD), BF16),
            jax.ShapeDtypeStruct(wmap_f32.shape, BF16),
        ],
        compiler_params=pltpu.CompilerParams(dimension_semantics=("arbitrary",), vmem_limit_bytes=VMEM_LIMIT),
        name="prep_weights",
    )(wt, wt, wa2, wout_f32, wmap_f32)


def kernel(x_prompt, x_sample, state_pool, state_gla, w_in, w_pool_map, pool_scale, w_gla_a2, b_gla_a,
           gla_norm_g, w_out, ln_g, ln_b):
    assert w_in.shape[0] == DEPTH
    win, wout, wmap = _prep_weights(jnp.transpose(w_in[0]), w_gla_a2[0], w_out[0], w_pool_map[0])
    pscale = pool_scale[0].reshape(1, POOL_WIDTH)
    ba = b_gla_a[0].reshape(1, GLA_KEY_WIDTH)
    ng = gla_norm_g[0].reshape(1, GLA_WIDTH)
    lng = ln_g[0].reshape(1, D_MODEL)
    lnb = ln_b[0].reshape(1, D_MODEL)
    params = (win, wmap, pscale, ba, ng, wout, lng, lnb)

    y_p, pool_p, gla_p = _prompt_call(x_prompt, *params)
    nsamp = x_sample.shape[0]
    y_s, pool_s, gla_s = _sample_call(
        x_sample, jnp.transpose(state_pool[0], (1, 0, 2)),
        state_gla[0].reshape(nsamp, GLA_KEY_WIDTH, GLA_DV), *params, start_pos=PAST_LEN)

    nb = x_prompt.shape[0]
    return (y_p, y_s, jnp.transpose(pool_p, (1, 0, 2))[None],
            gla_p.reshape(1, nb, GLA_HEADS, GLA_DK, GLA_DV),
            jnp.transpose(pool_s, (1, 0, 2))[None],
            gla_s.reshape(1, nsamp, GLA_HEADS, GLA_DK, GLA_DV))
```

```python
import functools
import itertools

import jax
import jax.numpy as jnp
from jax import lax
from jax.experimental import pallas as pl
from jax.experimental.pallas import tpu as pltpu

F32 = jnp.float32
BF16 = jnp.bfloat16

D_MODEL = 1024
POOL_WIDTH = 512
POOL_WINDOWS = (2, 4, 8, 16)
POOL_GROUP_DIM = 128
POOL_BUF = 15
HIST_ROWS = 16
GLA_HEADS = 4
GLA_DK = 64
GLA_DV = 128
GLA_KEY_WIDTH = GLA_HEADS * GLA_DK
GLA_WIDTH = GLA_HEADS * GLA_DV
GLA_RANK = 16
GLA_TAU = 16.0
LN_EPS = 1e-5
RMS_EPS = 1e-6
DEPTH = 1
DEEPNORM_ALPHA = (2.0 * DEPTH) ** 0.25
PAST_LEN = 16384

C_U, C_GP, C_Q, C_K, C_V, C_GG, C_GL, C_END = 0, 512, 1024, 1280, 1536, 2048, 2560, 2816

MXU_COLS = 256
PROMPT_TILE = 256
PITCH_PAD = 128
PROMPT_SUBTILE = 256
SUBTILE_TAIL_UNITS = 0
PUMP_ROWS = 256
PUMP_PLAN = {"pool": (0, 1, 0, 1), "cumsum": 0, "scores": (1, 1, 1, 1), "level": (0, 0, 0, 1, 0, 0),
             "state": (0, 0, 0, 0), "epilogue": 1, "layernorm": (1, 1, 1, 0)}
SAFE_LOG_DECAY = 30.0
GLA_CHUNK = 64
SAMPLE_SEQS = 16
PREP_COLS = 512
VMEM_LIMIT = 56 * 1024 * 1024


def _dot(a, b):
    return jnp.dot(a, b, preferred_element_type=F32)


def _dot_nt(a, b):
    return lax.dot_general(a, b, (((1,), (1,)), ((), ())), preferred_element_type=F32)


def _dot_tn(a, b):
    return lax.dot_general(a, b, (((0,), (0,)), ((), ())), preferred_element_type=F32)


def _silu(x):
    return x * (1.0 / (1.0 + jnp.exp(-x)))


def _log_sigmoid(x):
    return jnp.minimum(x, 0.0) - jnp.log1p(jnp.exp(-jnp.abs(x)))


def _split_bf16(x):
    hi = x.astype(BF16)
    lo = (x - hi.astype(F32)).astype(BF16)
    return hi, lo


def _epilogue(x, pool_act, o, gg, ng_ref, wout_ref, lng_ref, lnb_ref, pump=lambda n=None: None, tail=None,
              row_blocks=1):
    pump(PUMP_PLAN["epilogue"])
    normed = []
    for h in range(GLA_HEADS):
        oh = o[:, h * GLA_DV:(h + 1) * GLA_DV]
        ms = jnp.mean(oh * oh, axis=-1, keepdims=True)
        normed.append(oh * lax.rsqrt(ms + RMS_EPS))
    gla_act = jnp.concatenate(normed, axis=1) * ng_ref[...] * _silu(gg)
    mix = jnp.concatenate([pool_act, gla_act], axis=1).astype(BF16)
    y = _dot(mix, wout_ref[:, 0:D_MODEL])
    n = x.shape[0] // row_blocks
    outs = []
    for r in range(row_blocks):
        rows = slice(r * n, (r + 1) * n)
        pump(PUMP_PLAN["layernorm"][r])
        h = DEEPNORM_ALPHA * x[rows] + y[rows]
        mu = jnp.mean(h, axis=-1, keepdims=True)
        d = h - mu
        var = jnp.mean(d * d, axis=-1, keepdims=True)
        outs.append(d * lax.rsqrt(var + LN_EPS) * lng_ref[...] + lnb_ref[...])
    pump(tail)
    return outs[0] if row_blocks == 1 else jnp.concatenate(outs, axis=0)


def _head_block_mask(rows_per_head, cols_per_head, shape):
    r = lax.broadcasted_iota(jnp.int32, shape, 0) // rows_per_head
    c = lax.broadcasted_iota(jnp.int32, shape, 1) // cols_per_head
    return r == c


def _score_masks(rows, n_levels):
    shape = (rows, GLA_HEADS * rows)
    t_idx = lax.broadcasted_iota(jnp.int32, shape, 0)
    s_idx = lax.broadcasted_iota(jnp.int32, shape, 1) % rows
    masks = [t_idx == s_idx]
    for lvl in range(n_levels):
        half = 1 << lvl
        blk = 2 * half
        masks.append((t_idx // blk == s_idx // blk) & (t_idx % blk >= half) & (s_idx % blk < half))
    return masks


def _decayed_scores(q, k, la, b, masks, head_blk, pump=lambda n=None: None):
    rows = q.shape[0]
    r_idx = lax.broadcasted_iota(jnp.int32, q.shape, 0)

    def scores(qt, kt):
        kh = jnp.where(head_blk, jnp.concatenate([kt.astype(BF16)] * GLA_HEADS, axis=0), jnp.zeros((), BF16))
        return _dot_nt(qt.astype(BF16), kh)

    acc = jnp.where(masks[0], scores(q, k), 0.0)
    for lvl in range(len(masks) - 1):
        half = 1 << lvl
        blk = 2 * half
        if lvl == 0:
            neg = jnp.where(r_idx % 2 == 1, la, 0.0)
        elif lvl == 1:
            r4 = r_idx % 4
            nxt = pltpu.roll(la, rows - 1, 0)
            prv = pltpu.roll(la, 1, 0)
            neg = jnp.where(r4 == 0, nxt, jnp.where(r4 == 1, 0.0, jnp.where(r4 == 2, la, la + prv)))
        else:
            ref = jnp.concatenate(
                [jnp.broadcast_to(b[j * blk + half - 1:j * blk + half, :], (blk, b.shape[1]))
                 for j in range(rows // blk)], axis=0)
            d = b - ref
            neg = jnp.minimum(d, -d)
        w = jnp.exp(neg)
        acc = jnp.where(masks[lvl + 1], scores(q * w, k * w), acc)
        pump(PUMP_PLAN["level"][lvl])
    return acc.astype(BF16)


def _pair_block_diag(a0, a1):
    z = jnp.zeros_like(a0)
    return jnp.concatenate(
        [jnp.concatenate([a0, z], axis=1), jnp.concatenate([z, a1], axis=1)], axis=0)


def _make_pump(units, xb_ref, win_ref, proj_ref, unit_rows=PUMP_ROWS):
    units = iter(units)

    def pump(n=None):
        for r, lo in (units if n is None else itertools.islice(units, n)):
            proj_ref[r:r + unit_rows, lo:lo + MXU_COLS] = _dot(
                xb_ref[r:r + unit_rows, 0:D_MODEL], win_ref[:, lo:lo + MXU_COLS])

    return pump


def _naive_scores(q, k, b, causal, head_blk):
    ks = (k * jnp.exp(-b)).astype(BF16)
    kh = jnp.where(head_blk, jnp.concatenate([ks] * GLA_HEADS, axis=0), jnp.zeros((), BF16))
    return jnp.where(causal, _dot_nt((q * jnp.exp(b)).astype(BF16), kh), 0.0).astype(BF16)


def _prompt_stage_b(pos0, proj_ref, xkeep_ref, la_ref, wmap_ref, pscale_ref, ba_ref, ng_ref, wout_ref, lng_ref,
                    lnb_ref, y_ref, ext_ref, s_ref, pump, next_decay, *, tc, robust, tail):
    def pooling():
        u = proj_ref[:, C_U:C_GP]
        ext_ref[HIST_ROWS:HIST_ROWS + tc, 0:POOL_WIDTH] = u
        pos16 = pos0 + lax.broadcasted_iota(jnp.int32, (HIST_ROWS, POOL_GROUP_DIM), 0)
        mixed = []
        for g, w in enumerate(POOL_WINDOWS):
            sl = slice(g * POOL_GROUP_DIM, (g + 1) * POOL_GROUP_DIM)
            e = ext_ref[:, sl]
            s = e
            sh = 1
            while sh < w:
                s = s + pltpu.roll(s, sh, 0)
                sh *= 2
            wsum = s[HIST_ROWS:]
            self_rows = e[HIST_ROWS:]
            cnt = jnp.minimum(pos16 + 1, w).astype(F32)
            first = wsum[:HIST_ROWS] / cnt - self_rows[:HIST_ROWS]
            rest = wsum[HIST_ROWS:] * (1.0 / w) - self_rows[HIST_ROWS:]
            pooled = jnp.concatenate([first, rest], axis=0)
            mixed.append(_dot(pooled.astype(BF16), wmap_ref[g]))
            pump(PUMP_PLAN["pool"][g])
        return jnp.concatenate(mixed, axis=1) * pscale_ref[...] * _silu(proj_ref[:, C_GP:C_Q])

    cc = GLA_CHUNK
    n_chunks = tc // cc
    rr = lax.broadcasted_iota(jnp.int32, (tc, tc), 0)
    cl = lax.broadcasted_iota(jnp.int32, (tc, tc), 1)
    ltri = jnp.where((rr // cc == cl // cc) & (cl <= rr), 1.0, 0.0).astype(BF16)
    masks = _score_masks(cc, cc.bit_length() - 1 if robust else 0)
    causal = (lax.broadcasted_iota(jnp.int32, (cc, GLA_HEADS * cc), 1) % cc
              <= lax.broadcasted_iota(jnp.int32, (cc, GLA_HEADS * cc), 0))
    blk = _head_block_mask(cc, GLA_DK, (GLA_HEADS * cc, GLA_KEY_WIDTH))

    hi, lo = _split_bf16(la_ref[:, 0:GLA_KEY_WIDTH])
    b_all = _dot(ltri, hi) + _dot(ltri, lo)
    pump(PUMP_PLAN["cumsum"])
    next_decay()
    lhs_list, v_list, upd_list, dcol_list = [], [], [], []
    for c in range(n_chunks):
        rs = slice(c * cc, (c + 1) * cc)
        b = b_all[rs]
        b_end = b[cc - 1:cc, :]
        q_c = proj_ref[rs, C_Q:C_K] * (GLA_DK ** -0.5)
        k_c = proj_ref[rs, C_K:C_V]
        qs = (q_c * jnp.exp(b)).astype(BF16)
        kd = (k_c * jnp.exp(b_end - b)).astype(BF16)
        if robust:
            a_all = _decayed_scores(q_c, k_c, la_ref[rs, 0:GLA_KEY_WIDTH], b, masks, blk, pump)
        else:
            a_all = _naive_scores(q_c, k_c, b, causal, blk)
            pump(PUMP_PLAN["scores"][c])
        vc = proj_ref[rs, C_V:C_GG].astype(BF16)
        upds = []
        for p in range(GLA_HEADS // 2):
            upd = _dot_tn(kd[:, p * 2 * GLA_DK:(p + 1) * 2 * GLA_DK], vc[:, p * 2 * GLA_DV:(p + 1) * 2 * GLA_DV])
            upds.append(upd[0:GLA_DK, 0:GLA_DV])
            upds.append(upd[GLA_DK:2 * GLA_DK, GLA_DV:2 * GLA_DV])
        lhs_list.append([jnp.concatenate([a_all[:, p * 2 * cc:(p + 1) * 2 * cc],
                                          qs[:, p * 2 * GLA_DK:(p + 1) * 2 * GLA_DK]], axis=1)
                         for p in range(GLA_HEADS // 2)])
        v_list.append(vc)
        upd_list.append(jnp.concatenate(upds, axis=0))
        dcol_list.append(jnp.transpose(jnp.broadcast_to(jnp.exp(b_end), (GLA_DV, GLA_KEY_WIDTH))))

    o_chunks = []
    state = s_ref[...]
    for c in range(n_chunks):
        s_bf = state.astype(BF16)
        vc = v_list[c]
        outs = []
        for p in range(GLA_HEADS // 2):
            h0, h1 = 2 * p, 2 * p + 1
            rhs = jnp.concatenate([
                _pair_block_diag(vc[:, h0 * GLA_DV:(h0 + 1) * GLA_DV], vc[:, h1 * GLA_DV:(h1 + 1) * GLA_DV]),
                _pair_block_diag(s_bf[h0 * GLA_DK:(h0 + 1) * GLA_DK], s_bf[h1 * GLA_DK:(h1 + 1) * GLA_DK]),
            ], axis=0)
            outs.append(_dot(lhs_list[c][p], rhs))
        o_chunks.append(jnp.concatenate(outs, axis=1))
        state = state * dcol_list[c] + upd_list[c]
        pump(PUMP_PLAN["state"][c])
    s_ref[...] = state
    o = jnp.concatenate(o_chunks, axis=0)
    pool_act = pooling()

    y_ref[...] = _epilogue(xkeep_ref[:, 0:D_MODEL], pool_act, o, proj_ref[:, C_GG:C_GL], ng_ref, wout_ref, lng_ref, lnb_ref,
                           pump, tail, row_blocks=4)


def _prompt_body(x_ref, win_ref, wmap_ref, pscale_ref, ba_ref, ng_ref, wout_ref, lng_ref, lnb_ref,
                 y_ref, pool_ref, s_out_ref,
                 proj0_ref, proj1_ref, xkeep0_ref, xkeep1_ref, la0_ref, la1_ref, xb_ref, ext_ref, s_ref, steep_ref,
                 *, tc, sub, nj, ntiles):
    i = pl.program_id(0)
    jb = (i + nj - 1) % nj

    @pl.when(i == 0)
    def _():
        steep_ref[0] = 0
        pool_ref[...] = jnp.zeros_like(pool_ref)

    @pl.when(jb == 0)
    def _():
        ext_ref[0:HIST_ROWS, 0:POOL_WIDTH] = jnp.zeros((HIST_ROWS, POOL_WIDTH), F32)
        s_ref[...] = jnp.zeros_like(s_ref)

    units = [(r, lo) for lo in (C_GL, *range(0, C_GL, MXU_COLS)) for r in range(0, tc, PUMP_ROWS)]

    def step(set_a, set_b, robust):
        pump, next_decay = (lambda n=None: None), (lambda: None)
        if set_a is not None:
            proj_a, xkeep_a, la_a = set_a
            x = x_ref[0]
            xkeep_a[:, 0:D_MODEL] = x
            xb_ref[:, 0:D_MODEL] = x.astype(BF16)
            pump = _make_pump(units, xb_ref, win_ref, proj_a)

            def next_decay():
                la = _log_sigmoid(proj_a[:, C_GL:C_END] + ba_ref[...]) * (1.0 / GLA_TAU)
                la_a[:, 0:GLA_KEY_WIDTH] = la
                chunk_decay = [jnp.sum(la[c * GLA_CHUNK:(c + 1) * GLA_CHUNK], axis=0, keepdims=True)
                               for c in range(tc // GLA_CHUNK)]
                lowest = jnp.min(functools.reduce(jnp.minimum, chunk_decay))
                steep_ref[0] = jnp.where(lowest < -SAFE_LOG_DECAY, 1, 0).astype(jnp.int32)

            if set_b is None:
                pump()
                next_decay()
                return
            pump(tc // PUMP_ROWS)

        proj_b, xkeep_b, la_b = set_b
        n_sub = tc // sub
        for k in range(n_sub):
            rows = slice(k * sub, (k + 1) * sub)
            last = k == n_sub - 1
            _prompt_stage_b((jb * n_sub + k) * sub, proj_b.at[rows, :], xkeep_b.at[rows, :], la_b.at[rows, :],
                            wmap_ref, pscale_ref, ba_ref, ng_ref, wout_ref, lng_ref, lnb_ref,
                            y_ref.at[0, rows, :], ext_ref, s_ref, pump, next_decay if k == 0 else (lambda: None),
                            tc=sub, robust=robust, tail=None if last else SUBTILE_TAIL_UNITS)
            if not last:
                ext_ref[0:HIST_ROWS, 0:POOL_WIDTH] = ext_ref[sub:sub + HIST_ROWS, 0:POOL_WIDTH]

    sets = ((proj0_ref, xkeep0_ref, la0_ref), (proj1_ref, xkeep1_ref, la1_ref))
    steep = steep_ref[0] != 0

    @pl.when(i == 0)
    def _():
        step(sets[0], None, False)

    for robust in (False, True):
        for parity in (0, 1):
            @pl.when((i > 0) & (i < ntiles) & (i % 2 == parity) & (steep == robust))
            def _(parity=parity, robust=robust):
                step(sets[parity], sets[1 - parity], robust)

        @pl.when((i == ntiles) & (steep == robust))
        def _(robust=robust):
            step(None, sets[(ntiles - 1) % 2], robust)

    @pl.when((i > 0) & (jb == nj - 1))
    def _():
        mine = lax.broadcasted_iota(jnp.int32, pool_ref.shape[1:], 0) == (i - 1) // nj
        for r in range(POOL_BUF):
            row = ext_ref[sub + 1 + r:sub + 2 + r, 0:POOL_WIDTH]
            pool_ref[r] = jnp.where(mine, jnp.broadcast_to(row, pool_ref.shape[1:]), pool_ref[r])
        s_out_ref[0] = s_ref[...]

    @pl.when(i > 0)
    def _():
        ext_ref[0:HIST_ROWS, 0:POOL_WIDTH] = ext_ref[sub:sub + HIST_ROWS, 0:POOL_WIDTH]


def _prompt_call(x, win, wmap, pscale, ba, ng, wout, lng, lnb):
    nb, t, _ = x.shape
    tc = PROMPT_TILE
    nj = t // tc
    ntiles = nb * nj
    const2 = lambda i: (0, 0)

    def tile_a(i):
        ia = jnp.minimum(i, ntiles - 1)
        return (ia // nj, ia % nj, 0)

    def tile_b(i):
        ib = jnp.maximum(i - 1, 0)
        return (ib // nj, ib % nj, 0)

    def seq_b(i):
        return (jnp.maximum(i - 1, 0) // nj, 0, 0)

    return pl.pallas_call(
        functools.partial(_prompt_body, tc=tc, sub=PROMPT_SUBTILE, nj=nj, ntiles=ntiles),
        grid=(ntiles + 1,),
        in_specs=[
            pl.BlockSpec((1, tc, D_MODEL), tile_a),
            pl.BlockSpec(win.shape, const2),
            pl.BlockSpec(wmap.shape, lambda i: (0, 0, 0)),
            pl.BlockSpec(pscale.shape, const2),
            pl.BlockSpec(ba.shape, const2),
            pl.BlockSpec(ng.shape, const2),
            pl.BlockSpec(wout.shape, const2),
            pl.BlockSpec(lng.shape, const2),
            pl.BlockSpec(lnb.shape, const2),
        ],
        out_specs=[
            pl.BlockSpec((1, tc, D_MODEL), tile_b),
            pl.BlockSpec((POOL_BUF, nb, POOL_WIDTH), lambda i: (0, 0, 0)),
            pl.BlockSpec((1, GLA_KEY_WIDTH, GLA_DV), seq_b),
        ],
        out_shape=[
            jax.ShapeDtypeStruct((nb, t, D_MODEL), F32),
            jax.ShapeDtypeStruct((POOL_BUF, nb, POOL_WIDTH), F32),
            jax.ShapeDtypeStruct((nb, GLA_KEY_WIDTH, GLA_DV), F32),
        ],
        scratch_shapes=[
            pltpu.VMEM((tc, C_END + PITCH_PAD), F32),
            pltpu.VMEM((tc, C_END + PITCH_PAD), F32),
            pltpu.VMEM((tc, D_MODEL + PITCH_PAD), F32),
            pltpu.VMEM((tc, D_MODEL + PITCH_PAD), F32),
            pltpu.VMEM((tc, GLA_KEY_WIDTH + PITCH_PAD), F32),
            pltpu.VMEM((tc, GLA_KEY_WIDTH + PITCH_PAD), F32),
            pltpu.VMEM((tc, D_MODEL + PITCH_PAD), BF16),
            pltpu.VMEM((HIST_ROWS + PROMPT_SUBTILE, POOL_WIDTH + PITCH_PAD), F32),
            pltpu.VMEM((GLA_KEY_WIDTH, GLA_DV), F32),
            pltpu.SMEM((1,), jnp.int32),
        ],
        compiler_params=pltpu.CompilerParams(
            dimension_semantics=("arbitrary",), vmem_limit_bytes=VMEM_LIMIT),
        name="prompt_layer",
    )(x, win, wmap, pscale, ba, ng, wout, lng, lnb)


def _sample_stage_b(proj_ref, xkeep_ref, pool_in_ref, s_in_ref, wmap_ref, pscale_ref, ba_ref, ng_ref, wout_ref,
                    lng_ref, lnb_ref, y_ref, pool_ref, s_out_ref, u_ref, pooled_ref, pump, *, bs, t, start_pos):
    rows = bs * t
    x = xkeep_ref[:, 0:D_MODEL]

    def proj(lo, hi):
        return proj_ref[:, lo:hi]

    u = proj(C_U, C_GP)
    mixed = []
    for g, w in enumerate(POOL_WINDOWS):
        sl = slice(g * POOL_GROUP_DIM, (g + 1) * POOL_GROUP_DIM)
        u_ref[g] = u[:, sl]
        ext = [pool_in_ref[r, :, sl] for r in range(POOL_BUF)]
        ext += [u_ref[g, pl.ds(tok, bs, stride=t), :] for tok in range(t)]
        for r in range(POOL_BUF):
            pool_ref[r, :, sl] = ext[t + r]
        sums = ext
        sh = 1
        while sh < w:
            first = 2 * sh - 1
            sums = [None] * first + [sums[r] + sums[r - sh] for r in range(first, len(sums))]
            sh *= 2
        for tok in range(t):
            cnt = float(min(start_pos + tok + 1, w))
            pooled_ref[g, pl.ds(tok, bs, stride=t), :] = sums[POOL_BUF + tok] / cnt - ext[POOL_BUF + tok]
        mixed.append(_dot(pooled_ref[g].astype(BF16), wmap_ref[g]))
        pump(g % 2)
    gp = proj(C_GP, C_Q)
    pool_act = jnp.concatenate(mixed, axis=1) * pscale_ref[...] * _silu(gp)

    q = proj(C_Q, C_K) * (GLA_DK ** -0.5)
    k = proj(C_K, C_V)
    vb = proj(C_V, C_GG).astype(BF16)
    gg = proj(C_GG, C_GL)
    la = _log_sigmoid(proj(C_GL, C_END) + ba_ref[...]) * (1.0 / GLA_TAU)

    rr = lax.broadcasted_iota(jnp.int32, (rows, rows), 0)
    cl = lax.broadcasted_iota(jnp.int32, (rows, rows), 1)
    same = (rr // t) == (cl // t)
    ltri = jnp.where(same & (cl <= rr), 1.0, 0.0).astype(BF16)
    lall = jnp.where(same, 1.0, 0.0).astype(BF16)
    hi, lo = _split_bf16(la)
    b = _dot(ltri, hi) + _dot(ltri, lo)
    b_end = _dot(lall, hi) + _dot(lall, lo)
    pump(1)
    qs = (q * jnp.exp(b)).astype(BF16)
    kd = (k * jnp.exp(b_end - b)).astype(BF16)

    assert t & (t - 1) == 0, "tokens per sequence must be a power of two"
    blk = _head_block_mask(rows, GLA_DK, (GLA_HEADS * rows, GLA_KEY_WIDTH))
    a_all = _decayed_scores(q, k, la, b, _score_masks(rows, t.bit_length() - 1), blk)
    pump(2)

    s_old = s_in_ref[...]
    s_bf = s_old.astype(BF16)
    qs3 = qs.reshape(bs, t, GLA_KEY_WIDTH)
    kd3 = kd.reshape(bs, t, GLA_KEY_WIDTH)
    v3 = vb.reshape(bs, t, GLA_WIDTH)
    bdims_nn = (((2,), (1,)), ((0,), (0,)))
    bdims_tn = (((1,), (1,)), ((0,), (0,)))
    assert bs <= GLA_DV
    seq_sel = jnp.where(lax.broadcasted_iota(jnp.int32, (rows, GLA_DV), 0) // t
                        == lax.broadcasted_iota(jnp.int32, (rows, GLA_DV), 1), 1.0, 0.0).astype(BF16)
    decay_t = jnp.exp(_dot_tn(hi, seq_sel) + _dot_tn(lo, seq_sel))
    decay = jnp.stack([jnp.broadcast_to(decay_t[:, j:j + 1], (GLA_KEY_WIDTH, GLA_DV)) for j in range(bs)], axis=0)
    outs, upds = [], []
    for p in range(GLA_HEADS // 2):
        h0, h1 = 2 * p, 2 * p + 1
        intra = _dot(a_all[:, p * 2 * rows:(p + 1) * 2 * rows],
                     _pair_block_diag(vb[:, h0 * GLA_DV:(h0 + 1) * GLA_DV], vb[:, h1 * GLA_DV:(h1 + 1) * GLA_DV]))
        s0 = s_bf[:, h0 * GLA_DK:(h0 + 1) * GLA_DK, :]
        s1 = s_bf[:, h1 * GLA_DK:(h1 + 1) * GLA_DK, :]
        z = jnp.zeros_like(s0)
        s_pair = jnp.concatenate(
            [jnp.concatenate([s0, z], axis=2), jnp.concatenate([z, s1], axis=2)], axis=1)
        inter = lax.dot_general(qs3[:, :, p * 2 * GLA_DK:(p + 1) * 2 * GLA_DK], s_pair, bdims_nn,
                                preferred_element_type=F32)
        outs.append(intra + inter.reshape(rows, 2 * GLA_DV))
        upd = lax.dot_general(kd3[:, :, p * 2 * GLA_DK:(p + 1) * 2 * GLA_DK],
                              v3[:, :, p * 2 * GLA_DV:(p + 1) * 2 * GLA_DV], bdims_tn,
                              preferred_element_type=F32)
        upds.append(upd[:, 0:GLA_DK, 0:GLA_DV])
        upds.append(upd[:, GLA_DK:2 * GLA_DK, GLA_DV:2 * GLA_DV])
        pump(2)
    o = jnp.concatenate(outs, axis=1)
    s_out_ref[...] = s_old * decay + jnp.concatenate(upds, axis=1)

    y = _epilogue(x, pool_act, o, gg, ng_ref, wout_ref, lng_ref, lnb_ref, pump)
    y_ref[...] = y.reshape(bs, t, D_MODEL)


def _sample_body(x_ref, pool_in_ref, s_in_ref, win_ref, wmap_ref, pscale_ref, ba_ref, ng_ref, wout_ref,
                 lng_ref, lnb_ref, y_ref, pool_ref, s_out_ref,
                 proj0_ref, proj1_ref, xkeep0_ref, xkeep1_ref, xb_ref, u_ref, pooled_ref,
                 *, bs, t, start_pos, n_groups):
    i = pl.program_id(0)
    rows = bs * t
    sets = ((proj0_ref, xkeep0_ref), (proj1_ref, xkeep1_ref))
    units = [(0, lo) for lo in range(0, C_END, MXU_COLS)]

    def step(set_a, set_b):
        pump = lambda n=None: None
        if set_a is not None:
            proj_a, xkeep_a = set_a
            x = x_ref[...].reshape(rows, D_MODEL)
            xkeep_a[:, 0:D_MODEL] = x
            xb_ref[:, 0:D_MODEL] = x.astype(BF16)
            pump = _make_pump(units, xb_ref, win_ref, proj_a, rows)
            pump(1 if set_b is not None else None)
        if set_b is not None:
            proj_b, xkeep_b = set_b
            _sample_stage_b(proj_b, xkeep_b, pool_in_ref, s_in_ref, wmap_ref, pscale_ref, ba_ref, ng_ref,
                            wout_ref, lng_ref, lnb_ref, y_ref, pool_ref, s_out_ref, u_ref, pooled_ref, pump,
                            bs=bs, t=t, start_pos=start_pos)

    @pl.when(i == 0)
    def _():
        step(sets[0], None)

    for parity in (0, 1):
        @pl.when((i > 0) & (i < n_groups) & (i % 2 == parity))
        def _(parity=parity):
            step(sets[parity], sets[1 - parity])

    @pl.when(i == n_groups)
    def _():
        step(None, sets[(n_groups - 1) % 2])


def _sample_call(x, pool_in, s_in, win, wmap, pscale, ba, ng, wout, lng, lnb, start_pos):
    nb, t, _ = x.shape
    bs = SAMPLE_SEQS
    n_groups = nb // bs
    const2 = lambda i: (0, 0)
    seq3_a = lambda i: (jnp.minimum(i, n_groups - 1), 0, 0)
    seq3 = lambda i: (jnp.maximum(i - 1, 0), 0, 0)
    hist3 = lambda i: (0, jnp.maximum(i - 1, 0), 0)
    return pl.pallas_call(
        functools.partial(_sample_body, bs=bs, t=t, start_pos=start_pos, n_groups=n_groups),
        grid=(n_groups + 1,),
        in_specs=[
            pl.BlockSpec((bs, t, D_MODEL), seq3_a),
            pl.BlockSpec((POOL_BUF, bs, POOL_WIDTH), hist3),
            pl.BlockSpec((bs, GLA_KEY_WIDTH, GLA_DV), seq3),
            pl.BlockSpec(win.shape, const2),
            pl.BlockSpec(wmap.shape, lambda i: (0, 0, 0)),
            pl.BlockSpec(pscale.shape, const2),
            pl.BlockSpec(ba.shape, const2),
            pl.BlockSpec(ng.shape, const2),
            pl.BlockSpec(wout.shape, const2),
            pl.BlockSpec(lng.shape, const2),
            pl.BlockSpec(lnb.shape, const2),
        ],
        out_specs=[
            pl.BlockSpec((bs, t, D_MODEL), seq3),
            pl.BlockSpec((POOL_BUF, bs, POOL_WIDTH), hist3),
            pl.BlockSpec((bs, GLA_KEY_WIDTH, GLA_DV), seq3),
        ],
        out_shape=[
            jax.ShapeDtypeStruct((nb, t, D_MODEL), F32),
            jax.ShapeDtypeStruct((POOL_BUF, nb, POOL_WIDTH), F32),
            jax.ShapeDtypeStruct((nb, GLA_KEY_WIDTH, GLA_DV), F32),
        ],
        scratch_shapes=[pltpu.VMEM((bs * t, C_END + PITCH_PAD), F32),
                        pltpu.VMEM((bs * t, C_END + PITCH_PAD), F32),
                        pltpu.VMEM((bs * t, D_MODEL + PITCH_PAD), F32),
                        pltpu.VMEM((bs * t, D_MODEL + PITCH_PAD), F32),
                        pltpu.VMEM((bs * t, D_MODEL + PITCH_PAD), BF16),
                        pltpu.VMEM((len(POOL_WINDOWS), bs * t, POOL_GROUP_DIM), F32),
                        pltpu.VMEM((len(POOL_WINDOWS), bs * t, POOL_GROUP_DIM), F32)],
        compiler_params=pltpu.CompilerParams(
            dimension_semantics=("arbitrary",), vmem_limit_bytes=VMEM_LIMIT),
        name="sample_layer",
    )(x, pool_in, s_in, win, wmap, pscale, ba, ng, wout, lng, lnb)


def _prep_body(wt_ref, wfat_ref, wa2_ref, wout_f32_ref, wmap_f32_ref, win_ref, wout_ref, wmap_ref, *, n_plain,
               n_wout):
    i = pl.program_id(0)

    @pl.when(i == 0)
    def _():
        wmap_ref[...] = wmap_f32_ref[...].astype(BF16)

    @pl.when(i < n_wout)
    def _():
        wout_ref[:, 0:D_MODEL] = wout_f32_ref[...].astype(BF16)
        wout_ref[:, D_MODEL:] = jnp.zeros((wout_ref.shape[0], PITCH_PAD), BF16)

    @pl.when(i < n_plain)
    def _():
        win_ref[...] = jnp.transpose(wt_ref[...]).astype(BF16)

    @pl.when(i == n_plain)
    def _():
        fa_hi, fa_lo = _split_bf16(wfat_ref[...])
        a2_hi, a2_lo = _split_bf16(wa2_ref[...])
        acc = _dot_tn(fa_hi, a2_hi) + (_dot_tn(fa_hi, a2_lo) + _dot_tn(fa_lo, a2_hi))
        win_ref[:, 0:GLA_KEY_WIDTH] = acc.astype(BF16)
        win_ref[:, GLA_KEY_WIDTH:] = jnp.zeros((D_MODEL, PREP_COLS - GLA_KEY_WIDTH), BF16)


def _prep_weights(wt, wa2, wout_f32, wmap_f32):
    assert C_GL % PREP_COLS == 0 and C_END + PITCH_PAD - C_GL <= PREP_COLS and C_GL % GLA_RANK == 0
    n_plain = C_GL // PREP_COLS
    n_wout = 4
    wout_rows = wout_f32.shape[0] // n_wout
    assert n_wout <= n_plain + 1
    wout_block = lambda i: (jnp.minimum(i, n_wout - 1), 0)
    return pl.pallas_call(
        functools.partial(_prep_body, n_plain=n_plain, n_wout=n_wout),
        grid=(n_plain + 1,),
        in_specs=[
            pl.BlockSpec((PREP_COLS, D_MODEL), lambda i: (jnp.minimum(i, n_plain - 1), 0)),
            pl.BlockSpec((GLA_RANK, D_MODEL), lambda i: (C_GL // GLA_RANK, 0)),
            pl.BlockSpec(wa2.shape, lambda i: (0, 0)),
            pl.BlockSpec((wout_rows, D_MODEL), wout_block),
            pl.BlockSpec(wmap_f32.shape, lambda i: (0, 0, 0)),
        ],
        out_specs=[
            pl.BlockSpec((D_MODEL, PREP_COLS), lambda i: (0, i)),
            pl.BlockSpec((wout_rows, D_MODEL + PITCH_PAD), wout_block),
            pl.BlockSpec(wmap_f32.shape, lambda i: (0, 0, 0)),
        ],
        out_shape=[
            jax.ShapeDtypeStruct((D_MODEL, C_END + PITCH_PAD), BF16),
            jax.ShapeDtypeStruct((D_MODEL, D_MODEL + PITCH_PAD), BF16),
            jax.ShapeDtypeStruct(wmap_f32.shape, BF16),
        ],
        compiler_params=pltpu.CompilerParams(dimension_semantics=("arbitrary",), vmem_limit_bytes=VMEM_LIMIT),
        name="prep_weights",
    )(wt, wt, wa2, wout_f32, wmap_f32)


def kernel(x_prompt, x_sample, state_pool, state_gla, w_in, w_pool_map, pool_scale, w_gla_a2, b_gla_a,
           gla_norm_g, w_out, ln_g, ln_b):
    assert w_in.shape[0] == DEPTH
    win, wout, wmap = _prep_weights(jnp.transpose(w_in[0]), w_gla_a2[0], w_out[0], w_pool_map[0])
    pscale = pool_scale[0].reshape(1, POOL_WIDTH)
    ba = b_gla_a[0].reshape(1, GLA_KEY_WIDTH)
    ng = gla_norm_g[0].reshape(1, GLA_WIDTH)
    lng = ln_g[0].reshape(1, D_MODEL)
    lnb = ln_b[0].reshape(1, D_MODEL)
    params = (win, wmap, pscale, ba, ng, wout, lng, lnb)

    y_p, pool_p, gla_p = _prompt_call(x_prompt, *params)
    nsamp = x_sample.shape[0]
    y_s, pool_s, gla_s = _sample_call(
        x_sample, jnp.transpose(state_pool[0], (1, 0, 2)),
        state_gla[0].reshape(nsamp, GLA_KEY_WIDTH, GLA_DV), *params, start_pos=PAST_LEN)

    nb = x_prompt.shape[0]
    return (y_p, y_s, jnp.transpose(pool_p, (1, 0, 2))[None],
            gla_p.reshape(1, nb, GLA_HEADS, GLA_DK, GLA_DV),
            jnp.transpose(pool_s, (1, 0, 2))[None],
            gla_s.reshape(1, nsamp, GLA_HEADS, GLA_DK, GLA_DV))
```

```python
import functools
import itertools

import jax
import jax.numpy as jnp
from jax import lax
from jax.experimental import pallas as pl
from jax.experimental.pallas import tpu as pltpu

F32 = jnp.float32
BF16 = jnp.bfloat16

D_MODEL = 1024
POOL_WIDTH = 512
POOL_WINDOWS = (2, 4, 8, 16)
POOL_GROUP_DIM = 128
POOL_BUF = 15
HIST_ROWS = 16
GLA_HEADS = 4
GLA_DK = 64
GLA_DV = 128
GLA_KEY_WIDTH = GLA_HEADS * GLA_DK
GLA_WIDTH = GLA_HEADS * GLA_DV
GLA_RANK = 16
GLA_TAU = 16.0
LN_EPS = 1e-5
RMS_EPS = 1e-6
DEPTH = 1
DEEPNORM_ALPHA = (2.0 * DEPTH) ** 0.25
PAST_LEN = 16384

C_U, C_GP, C_Q, C_K, C_V, C_GG, C_GL, C_END = 0, 512, 1024, 1280, 1536, 2048, 2560, 2816

MXU_COLS = 256
PROMPT_TILE = 256
PITCH_PAD = 128
PROMPT_SUBTILE = 256
SUBTILE_TAIL_UNITS = 0
PUMP_ROWS = 256
PUMP_PLAN = {"pool": (0, 1, 0, 1), "cumsum": 0, "scores": (1, 1, 1, 1), "level": (0, 0, 0, 1, 0, 0),
             "state": (0, 0, 0, 0), "epilogue": 1, "layernorm": (1, 1, 1, 0)}
SAFE_LOG_DECAY = 30.0
GLA_CHUNK = 64
SAMPLE_SEQS = 16
PREP_COLS = 512
VMEM_LIMIT = 56 * 1024 * 1024


def _dot(a, b):
    return jnp.dot(a, b, preferred_element_type=F32)


def _dot_nt(a, b):
    return lax.dot_general(a, b, (((1,), (1,)), ((), ())), preferred_element_type=F32)


def _dot_tn(a, b):
    return lax.dot_general(a, b, (((0,), (0,)), ((), ())), preferred_element_type=F32)


def _silu(x):
    return x * (1.0 / (1.0 + jnp.exp(-x)))


def _log_sigmoid(x):
    return jnp.minimum(x, 0.0) - jnp.log1p(jnp.exp(-jnp.abs(x)))


def _split_bf16(x):
    hi = x.astype(BF16)
    lo = (x - hi.astype(F32)).astype(BF16)
    return hi, lo


def _epilogue(x, pool_act, o, gg, ng_ref, wout_ref, lng_ref, lnb_ref, pump=lambda n=None: None, tail=None,
              row_blocks=1):
    pump(PUMP_PLAN["epilogue"])
    normed = []
    for h in range(GLA_HEADS):
        oh = o[:, h * GLA_DV:(h + 1) * GLA_DV]
        ms = jnp.mean(oh * oh, axis=-1, keepdims=True)
        normed.append(oh * lax.rsqrt(ms + RMS_EPS))
    gla_act = jnp.concatenate(normed, axis=1) * ng_ref[...] * _silu(gg)
    mix = jnp.concatenate([pool_act, gla_act], axis=1).astype(BF16)
    y = _dot(mix, wout_ref[:, 0:D_MODEL])
    n = x.shape[0] // row_blocks
    outs = []
    for r in range(row_blocks):
        rows = slice(r * n, (r + 1) * n)
        pump(PUMP_PLAN["layernorm"][r])
        h = DEEPNORM_ALPHA * x[rows] + y[rows]
        mu = jnp.mean(h, axis=-1, keepdims=True)
        d = h - mu
        var = jnp.mean(d * d, axis=-1, keepdims=True)
        outs.append(d * lax.rsqrt(var + LN_EPS) * lng_ref[...] + lnb_ref[...])
    pump(tail)
    return outs[0] if row_blocks == 1 else jnp.concatenate(outs, axis=0)


def _head_block_mask(rows_per_head, cols_per_head, shape):
    r = lax.broadcasted_iota(jnp.int32, shape, 0) // rows_per_head
    c = lax.broadcasted_iota(jnp.int32, shape, 1) // cols_per_head
    return r == c


def _score_masks(rows, n_levels):
    shape = (rows, GLA_HEADS * rows)
    t_idx = lax.broadcasted_iota(jnp.int32, shape, 0)
    s_idx = lax.broadcasted_iota(jnp.int32, shape, 1) % rows
    masks = [t_idx == s_idx]
    for lvl in range(n_levels):
        half = 1 << lvl
        blk = 2 * half
        masks.append((t_idx // blk == s_idx // blk) & (t_idx % blk >= half) & (s_idx % blk < half))
    return masks


def _decayed_scores(q, k, la, b, masks, head_blk, pump=lambda n=None: None):
    rows = q.shape[0]
    r_idx = lax.broadcasted_iota(jnp.int32, q.shape, 0)

    def scores(qt, kt):
        kh = jnp.where(head_blk, jnp.concatenate([kt.astype(BF16)] * GLA_HEADS, axis=0), jnp.zeros((), BF16))
        return _dot_nt(qt.astype(BF16), kh)

    acc = jnp.where(masks[0], scores(q, k), 0.0)
    for lvl in range(len(masks) - 1):
        half = 1 << lvl
        blk = 2 * half
        if lvl == 0:
            neg = jnp.where(r_idx % 2 == 1, la, 0.0)
        elif lvl == 1:
            r4 = r_idx % 4
            nxt = pltpu.roll(la, rows - 1, 0)
            prv = pltpu.roll(la, 1, 0)
            neg = jnp.where(r4 == 0, nxt, jnp.where(r4 == 1, 0.0, jnp.where(r4 == 2, la, la + prv)))
        else:
            ref = jnp.concatenate(
                [jnp.broadcast_to(b[j * blk + half - 1:j * blk + half, :], (blk, b.shape[1]))
                 for j in range(rows // blk)], axis=0)
            d = b - ref
            neg = jnp.minimum(d, -d)
        w = jnp.exp(neg)
        acc = jnp.where(masks[lvl + 1], scores(q * w, k * w), acc)
        pump(PUMP_PLAN["level"][lvl])
    return acc.astype(BF16)


def _pair_block_diag(a0, a1):
    z = jnp.zeros_like(a0)
    return jnp.concatenate(
        [jnp.concatenate([a0, z], axis=1), jnp.concatenate([z, a1], axis=1)], axis=0)


def _make_pump(units, xb_ref, win_ref, proj_ref, unit_rows=PUMP_ROWS):
    units = iter(units)

    def pump(n=None):
        for r, lo in (units if n is None else itertools.islice(units, n)):
            proj_ref[r:r + unit_rows, lo:lo + MXU_COLS] = _dot(
                xb_ref[r:r + unit_rows, 0:D_MODEL], win_ref[:, lo:lo + MXU_COLS])

    return pump


def _naive_scores(q, k, b, causal, head_blk):
    ks = (k * jnp.exp(-b)).astype(BF16)
    kh = jnp.where(head_blk, jnp.concatenate([ks] * GLA_HEADS, axis=0), jnp.zeros((), BF16))
    return jnp.where(causal, _dot_nt((q * jnp.exp(b)).astype(BF16), kh), 0.0).astype(BF16)


def _prompt_stage_b(pos0, proj_ref, xres_ref, la_ref, wmap_ref, pscale_ref, ba_ref, ng_ref, wout_ref, lng_ref,
                    lnb_ref, y_ref, ext_ref, s_ref, pump, next_decay, *, tc, robust, tail):
    def pooling():
        u = proj_ref[:, C_U:C_GP]
        ext_ref[HIST_ROWS:HIST_ROWS + tc, 0:POOL_WIDTH] = u
        pos16 = pos0 + lax.broadcasted_iota(jnp.int32, (HIST_ROWS, POOL_GROUP_DIM), 0)
        mixed = []
        for g, w in enumerate(POOL_WINDOWS):
            sl = slice(g * POOL_GROUP_DIM, (g + 1) * POOL_GROUP_DIM)
            e = ext_ref[:, sl]
            s = e
            sh = 1
            while sh < w:
                s = s + pltpu.roll(s, sh, 0)
                sh *= 2
            wsum = s[HIST_ROWS:]
            self_rows = e[HIST_ROWS:]
            cnt = jnp.minimum(pos16 + 1, w).astype(F32)
            first = wsum[:HIST_ROWS] / cnt - self_rows[:HIST_ROWS]
            rest = wsum[HIST_ROWS:] * (1.0 / w) - self_rows[HIST_ROWS:]
            pooled = jnp.concatenate([first, rest], axis=0)
            mixed.append(_dot(pooled.astype(BF16), wmap_ref[g]))
            pump(PUMP_PLAN["pool"][g])
        return jnp.concatenate(mixed, axis=1) * pscale_ref[...] * _silu(proj_ref[:, C_GP:C_Q])

    cc = GLA_CHUNK
    n_chunks = tc // cc
    rr = lax.broadcasted_iota(jnp.int32, (tc, tc), 0)
    cl = lax.broadcasted_iota(jnp.int32, (tc, tc), 1)
    ltri = jnp.where((rr // cc == cl // cc) & (cl <= rr), 1.0, 0.0).astype(BF16)
    masks = _score_masks(cc, cc.bit_length() - 1 if robust else 0)
    causal = (lax.broadcasted_iota(jnp.int32, (cc, GLA_HEADS * cc), 1) % cc
              <= lax.broadcasted_iota(jnp.int32, (cc, GLA_HEADS * cc), 0))
    blk = _head_block_mask(cc, GLA_DK, (GLA_HEADS * cc, GLA_KEY_WIDTH))

    hi, lo = _split_bf16(la_ref[:, 0:GLA_KEY_WIDTH])
    b_all = _dot(ltri, hi) + _dot(ltri, lo)
    pump(PUMP_PLAN["cumsum"])
    next_decay()
    lhs_list, v_list, upd_list, dcol_list = [], [], [], []
    for c in range(n_chunks):
        rs = slice(c * cc, (c + 1) * cc)
        b = b_all[rs]
        b_end = b[cc - 1:cc, :]
        q_c = proj_ref[rs, C_Q:C_K] * (GLA_DK ** -0.5)
        k_c = proj_ref[rs, C_K:C_V]
        qs = (q_c * jnp.exp(b)).astype(BF16)
        kd = (k_c * jnp.exp(b_end - b)).astype(BF16)
        if robust:
            a_all = _decayed_scores(q_c, k_c, la_ref[rs, 0:GLA_KEY_WIDTH], b, masks, blk, pump)
        else:
            a_all = _naive_scores(q_c, k_c, b, causal, blk)
            pump(PUMP_PLAN["scores"][c])
        vc = proj_ref[rs, C_V:C_GG].astype(BF16)
        upds = []
        for p in range(GLA_HEADS // 2):
            upd = _dot_tn(kd[:, p * 2 * GLA_DK:(p + 1) * 2 * GLA_DK], vc[:, p * 2 * GLA_DV:(p + 1) * 2 * GLA_DV])
            upds.append(upd[0:GLA_DK, 0:GLA_DV])
            upds.append(upd[GLA_DK:2 * GLA_DK, GLA_DV:2 * GLA_DV])
        lhs_list.append([jnp.concatenate([a_all[:, p * 2 * cc:(p + 1) * 2 * cc],
                                          qs[:, p * 2 * GLA_DK:(p + 1) * 2 * GLA_DK]], axis=1)
                         for p in range(GLA_HEADS // 2)])
        v_list.append(vc)
        upd_list.append(jnp.concatenate(upds, axis=0))
        dcol_list.append(jnp.transpose(jnp.broadcast_to(jnp.exp(b_end), (GLA_DV, GLA_KEY_WIDTH))))

    o_chunks = []
    state = s_ref[...]
    for c in range(n_chunks):
        s_bf = state.astype(BF16)
        vc = v_list[c]
        outs = []
        for p in range(GLA_HEADS // 2):
            h0, h1 = 2 * p, 2 * p + 1
            rhs = jnp.concatenate([
                _pair_block_diag(vc[:, h0 * GLA_DV:(h0 + 1) * GLA_DV], vc[:, h1 * GLA_DV:(h1 + 1) * GLA_DV]),
                _pair_block_diag(s_bf[h0 * GLA_DK:(h0 + 1) * GLA_DK], s_bf[h1 * GLA_DK:(h1 + 1) * GLA_DK]),
            ], axis=0)
            outs.append(_dot(lhs_list[c][p], rhs))
        o_chunks.append(jnp.concatenate(outs, axis=1))
        state = state * dcol_list[c] + upd_list[c]
        pump(PUMP_PLAN["state"][c])
    s_ref[...] = state
    o = jnp.concatenate(o_chunks, axis=0)
    pool_act = pooling()

    y_ref[...] = _epilogue(xres_ref[...], pool_act, o, proj_ref[:, C_GG:C_GL], ng_ref, wout_ref, lng_ref, lnb_ref,
                           pump, tail, row_blocks=4)


def _prompt_body(x_ref, xres_ref, win_ref, wmap_ref, pscale_ref, ba_ref, ng_ref, wout_ref, lng_ref, lnb_ref,
                 y_ref, pool_ref, s_out_ref,
                 proj0_ref, proj1_ref, la0_ref, la1_ref, xb_ref, ext_ref, s_ref, steep_ref,
                 *, tc, sub, nj, ntiles):
    i = pl.program_id(0)
    jb = (i + nj - 1) % nj

    @pl.when(i == 0)
    def _():
        steep_ref[0] = 0
        pool_ref[...] = jnp.zeros_like(pool_ref)

    @pl.when(jb == 0)
    def _():
        ext_ref[0:HIST_ROWS, 0:POOL_WIDTH] = jnp.zeros((HIST_ROWS, POOL_WIDTH), F32)
        s_ref[...] = jnp.zeros_like(s_ref)

    units = [(r, lo) for lo in (C_GL, *range(0, C_GL, MXU_COLS)) for r in range(0, tc, PUMP_ROWS)]

    def step(set_a, set_b, robust):
        pump, next_decay = (lambda n=None: None), (lambda: None)
        if set_a is not None:
            proj_a, la_a = set_a
            xb_ref[:, 0:D_MODEL] = x_ref[0].astype(BF16)
            pump = _make_pump(units, xb_ref, win_ref, proj_a)

            def next_decay():
                la = _log_sigmoid(proj_a[:, C_GL:C_END] + ba_ref[...]) * (1.0 / GLA_TAU)
                la_a[:, 0:GLA_KEY_WIDTH] = la
                chunk_decay = [jnp.sum(la[c * GLA_CHUNK:(c + 1) * GLA_CHUNK], axis=0, keepdims=True)
                               for c in range(tc // GLA_CHUNK)]
                lowest = jnp.min(functools.reduce(jnp.minimum, chunk_decay))
                steep_ref[0] = jnp.where(lowest < -SAFE_LOG_DECAY, 1, 0).astype(jnp.int32)

            if set_b is None:
                pump()
                next_decay()
                return
            pump(tc // PUMP_ROWS)

        proj_b, la_b = set_b
        n_sub = tc // sub
        for k in range(n_sub):
            rows = slice(k * sub, (k + 1) * sub)
            last = k == n_sub - 1
            _prompt_stage_b((jb * n_sub + k) * sub, proj_b.at[rows, :], xres_ref.at[0, rows, :], la_b.at[rows, :],
                            wmap_ref, pscale_ref, ba_ref, ng_ref, wout_ref, lng_ref, lnb_ref,
                            y_ref.at[0, rows, :], ext_ref, s_ref, pump, next_decay if k == 0 else (lambda: None),
                            tc=sub, robust=robust, tail=None if last else SUBTILE_TAIL_UNITS)
            if not last:
                ext_ref[0:HIST_ROWS, 0:POOL_WIDTH] = ext_ref[sub:sub + HIST_ROWS, 0:POOL_WIDTH]

    sets = ((proj0_ref, la0_ref), (proj1_ref, la1_ref))
    steep = steep_ref[0] != 0

    @pl.when(i == 0)
    def _():
        step(sets[0], None, False)

    for robust in (False, True):
        for parity in (0, 1):
            @pl.when((i > 0) & (i < ntiles) & (i % 2 == parity) & (steep == robust))
            def _(parity=parity, robust=robust):
                step(sets[parity], sets[1 - parity], robust)

        @pl.when((i == ntiles) & (steep == robust))
        def _(robust=robust):
            step(None, sets[(ntiles - 1) % 2], robust)

    @pl.when((i > 0) & (jb == nj - 1))
    def _():
        mine = lax.broadcasted_iota(jnp.int32, pool_ref.shape[1:], 0) == (i - 1) // nj
        for r in range(POOL_BUF):
            row = ext_ref[sub + 1 + r:sub + 2 + r, 0:POOL_WIDTH]
            pool_ref[r] = jnp.where(mine, jnp.broadcast_to(row, pool_ref.shape[1:]), pool_ref[r])
        s_out_ref[0] = s_ref[...]

    @pl.when(i > 0)
    def _():
        ext_ref[0:HIST_ROWS, 0:POOL_WIDTH] = ext_ref[sub:sub + HIST_ROWS, 0:POOL_WIDTH]


def _prompt_call(x, win, wmap, pscale, ba, ng, wout, lng, lnb):
    nb, t, _ = x.shape
    tc = PROMPT_TILE
    nj = t // tc
    ntiles = nb * nj
    const2 = lambda i: (0, 0)

    def tile_a(i):
        ia = jnp.minimum(i, ntiles - 1)
        return (ia // nj, ia % nj, 0)

    def tile_b(i):
        ib = jnp.maximum(i - 1, 0)
        return (ib // nj, ib % nj, 0)

    def seq_b(i):
        return (jnp.maximum(i - 1, 0) // nj, 0, 0)

    return pl.pallas_call(
        functools.partial(_prompt_body, tc=tc, sub=PROMPT_SUBTILE, nj=nj, ntiles=ntiles),
        grid=(ntiles + 1,),
        in_specs=[
            pl.BlockSpec((1, tc, D_MODEL), tile_a),
            pl.BlockSpec((1, tc, D_MODEL), tile_b),
            pl.BlockSpec(win.shape, const2),
            pl.BlockSpec(wmap.shape, lambda i: (0, 0, 0)),
            pl.BlockSpec(pscale.shape, const2),
            pl.BlockSpec(ba.shape, const2),
            pl.BlockSpec(ng.shape, const2),
            pl.BlockSpec(wout.shape, const2),
            pl.BlockSpec(lng.shape, const2),
            pl.BlockSpec(lnb.shape, const2),
        ],
        out_specs=[
            pl.BlockSpec((1, tc, D_MODEL), tile_b),
            pl.BlockSpec((POOL_BUF, nb, POOL_WIDTH), lambda i: (0, 0, 0)),
            pl.BlockSpec((1, GLA_KEY_WIDTH, GLA_DV), seq_b),
        ],
        out_shape=[
            jax.ShapeDtypeStruct((nb, t, D_MODEL), F32),
            jax.ShapeDtypeStruct((POOL_BUF, nb, POOL_WIDTH), F32),
            jax.ShapeDtypeStruct((nb, GLA_KEY_WIDTH, GLA_DV), F32),
        ],
        scratch_shapes=[
            pltpu.VMEM((tc, C_END + PITCH_PAD), F32),
            pltpu.VMEM((tc, C_END + PITCH_PAD), F32),
            pltpu.VMEM((tc, GLA_KEY_WIDTH + PITCH_PAD), F32),
            pltpu.VMEM((tc, GLA_KEY_WIDTH + PITCH_PAD), F32),
            pltpu.VMEM((tc, D_MODEL + PITCH_PAD), BF16),
            pltpu.VMEM((HIST_ROWS + PROMPT_SUBTILE, POOL_WIDTH + PITCH_PAD), F32),
            pltpu.VMEM((GLA_KEY_WIDTH, GLA_DV), F32),
            pltpu.SMEM((1,), jnp.int32),
        ],
        compiler_params=pltpu.CompilerParams(
            dimension_semantics=("arbitrary",), vmem_limit_bytes=VMEM_LIMIT),
        name="prompt_layer",
    )(x, x, win, wmap, pscale, ba, ng, wout, lng, lnb)


def _sample_stage_b(proj_ref, xres_ref, pool_in_ref, s_in_ref, wmap_ref, pscale_ref, ba_ref, ng_ref, wout_ref,
                    lng_ref, lnb_ref, y_ref, pool_ref, s_out_ref, u_ref, pooled_ref, pump, *, bs, t, start_pos):
    rows = bs * t
    x = xres_ref[...].reshape(rows, D_MODEL)

    def proj(lo, hi):
        return proj_ref[:, lo:hi]

    u = proj(C_U, C_GP)
    mixed = []
    for g, w in enumerate(POOL_WINDOWS):
        sl = slice(g * POOL_GROUP_DIM, (g + 1) * POOL_GROUP_DIM)
        u_ref[g] = u[:, sl]
        ext = [pool_in_ref[r, :, sl] for r in range(POOL_BUF)]
        ext += [u_ref[g, pl.ds(tok, bs, stride=t), :] for tok in range(t)]
        for r in range(POOL_BUF):
            pool_ref[r, :, sl] = ext[t + r]
        sums = ext
        sh = 1
        while sh < w:
            first = 2 * sh - 1
            sums = [None] * first + [sums[r] + sums[r - sh] for r in range(first, len(sums))]
            sh *= 2
        for tok in range(t):
            cnt = float(min(start_pos + tok + 1, w))
            pooled_ref[g, pl.ds(tok, bs, stride=t), :] = sums[POOL_BUF + tok] / cnt - ext[POOL_BUF + tok]
        mixed.append(_dot(pooled_ref[g].astype(BF16), wmap_ref[g]))
        pump(g % 2)
    gp = proj(C_GP, C_Q)
    pool_act = jnp.concatenate(mixed, axis=1) * pscale_ref[...] * _silu(gp)

    q = proj(C_Q, C_K) * (GLA_DK ** -0.5)
    k = proj(C_K, C_V)
    vb = proj(C_V, C_GG).astype(BF16)
    gg = proj(C_GG, C_GL)
    la = _log_sigmoid(proj(C_GL, C_END) + ba_ref[...]) * (1.0 / GLA_TAU)

    rr = lax.broadcasted_iota(jnp.int32, (rows, rows), 0)
    cl = lax.broadcasted_iota(jnp.int32, (rows, rows), 1)
    same = (rr // t) == (cl // t)
    ltri = jnp.where(same & (cl <= rr), 1.0, 0.0).astype(BF16)
    lall = jnp.where(same, 1.0, 0.0).astype(BF16)
    hi, lo = _split_bf16(la)
    b = _dot(ltri, hi) + _dot(ltri, lo)
    b_end = _dot(lall, hi) + _dot(lall, lo)
    pump(1)
    qs = (q * jnp.exp(b)).astype(BF16)
    kd = (k * jnp.exp(b_end - b)).astype(BF16)

    assert t & (t - 1) == 0, "tokens per sequence must be a power of two"
    blk = _head_block_mask(rows, GLA_DK, (GLA_HEADS * rows, GLA_KEY_WIDTH))
    a_all = _decayed_scores(q, k, la, b, _score_masks(rows, t.bit_length() - 1), blk)
    pump(2)

    s_old = s_in_ref[...]
    s_bf = s_old.astype(BF16)
    qs3 = qs.reshape(bs, t, GLA_KEY_WIDTH)
    kd3 = kd.reshape(bs, t, GLA_KEY_WIDTH)
    v3 = vb.reshape(bs, t, GLA_WIDTH)
    bdims_nn = (((2,), (1,)), ((0,), (0,)))
    bdims_tn = (((1,), (1,)), ((0,), (0,)))
    assert bs <= GLA_DV
    seq_sel = jnp.where(lax.broadcasted_iota(jnp.int32, (rows, GLA_DV), 0) // t
                        == lax.broadcasted_iota(jnp.int32, (rows, GLA_DV), 1), 1.0, 0.0).astype(BF16)
    decay_t = jnp.exp(_dot_tn(hi, seq_sel) + _dot_tn(lo, seq_sel))
    decay = jnp.stack([jnp.broadcast_to(decay_t[:, j:j + 1], (GLA_KEY_WIDTH, GLA_DV)) for j in range(bs)], axis=0)
    outs, upds = [], []
    for p in range(GLA_HEADS // 2):
        h0, h1 = 2 * p, 2 * p + 1
        intra = _dot(a_all[:, p * 2 * rows:(p + 1) * 2 * rows],
                     _pair_block_diag(vb[:, h0 * GLA_DV:(h0 + 1) * GLA_DV], vb[:, h1 * GLA_DV:(h1 + 1) * GLA_DV]))
        s0 = s_bf[:, h0 * GLA_DK:(h0 + 1) * GLA_DK, :]
        s1 = s_bf[:, h1 * GLA_DK:(h1 + 1) * GLA_DK, :]
        z = jnp.zeros_like(s0)
        s_pair = jnp.concatenate(
            [jnp.concatenate([s0, z], axis=2), jnp.concatenate([z, s1], axis=2)], axis=1)
        inter = lax.dot_general(qs3[:, :, p * 2 * GLA_DK:(p + 1) * 2 * GLA_DK], s_pair, bdims_nn,
                                preferred_element_type=F32)
        outs.append(intra + inter.reshape(rows, 2 * GLA_DV))
        upd = lax.dot_general(kd3[:, :, p * 2 * GLA_DK:(p + 1) * 2 * GLA_DK],
                              v3[:, :, p * 2 * GLA_DV:(p + 1) * 2 * GLA_DV], bdims_tn,
                              preferred_element_type=F32)
        upds.append(upd[:, 0:GLA_DK, 0:GLA_DV])
        upds.append(upd[:, GLA_DK:2 * GLA_DK, GLA_DV:2 * GLA_DV])
        pump(2)
    o = jnp.concatenate(outs, axis=1)
    s_out_ref[...] = s_old * decay + jnp.concatenate(upds, axis=1)

    y = _epilogue(x, pool_act, o, gg, ng_ref, wout_ref, lng_ref, lnb_ref, pump)
    y_ref[...] = y.reshape(bs, t, D_MODEL)


def _sample_body(x_ref, xres_ref, pool_in_ref, s_in_ref, win_ref, wmap_ref, pscale_ref, ba_ref, ng_ref, wout_ref,
                 lng_ref, lnb_ref, y_ref, pool_ref, s_out_ref,
                 proj0_ref, proj1_ref, xb_ref, u_ref, pooled_ref,
                 *, bs, t, start_pos, n_groups):
    i = pl.program_id(0)
    rows = bs * t
    sets = (proj0_ref, proj1_ref)
    units = [(0, lo) for lo in range(0, C_END, MXU_COLS)]

    def step(proj_a, proj_b):
        pump = lambda n=None: None
        if proj_a is not None:
            xb_ref[:, 0:D_MODEL] = x_ref[...].reshape(rows, D_MODEL).astype(BF16)
            pump = _make_pump(units, xb_ref, win_ref, proj_a, rows)
            pump(1 if proj_b is not None else None)
        if proj_b is not None:
            _sample_stage_b(proj_b, xres_ref, pool_in_ref, s_in_ref, wmap_ref, pscale_ref, ba_ref, ng_ref,
                            wout_ref, lng_ref, lnb_ref, y_ref, pool_ref, s_out_ref, u_ref, pooled_ref, pump,
                            bs=bs, t=t, start_pos=start_pos)

    @pl.when(i == 0)
    def _():
        step(sets[0], None)

    for parity in (0, 1):
        @pl.when((i > 0) & (i < n_groups) & (i % 2 == parity))
        def _(parity=parity):
            step(sets[parity], sets[1 - parity])

    @pl.when(i == n_groups)
    def _():
        step(None, sets[(n_groups - 1) % 2])


def _sample_call(x, pool_in, s_in, win, wmap, pscale, ba, ng, wout, lng, lnb, start_pos):
    nb, t, _ = x.shape
    bs = SAMPLE_SEQS
    n_groups = nb // bs
    const2 = lambda i: (0, 0)
    seq3_a = lambda i: (jnp.minimum(i, n_groups - 1), 0, 0)
    seq3 = lambda i: (jnp.maximum(i - 1, 0), 0, 0)
    hist3 = lambda i: (0, jnp.maximum(i - 1, 0), 0)
    return pl.pallas_call(
        functools.partial(_sample_body, bs=bs, t=t, start_pos=start_pos, n_groups=n_groups),
        grid=(n_groups + 1,),
        in_specs=[
            pl.BlockSpec((bs, t, D_MODEL), seq3_a),
            pl.BlockSpec((bs, t, D_MODEL), seq3),
            pl.BlockSpec((POOL_BUF, bs, POOL_WIDTH), hist3),
            pl.BlockSpec((bs, GLA_KEY_WIDTH, GLA_DV), seq3),
            pl.BlockSpec(win.shape, const2),
            pl.BlockSpec(wmap.shape, lambda i: (0, 0, 0)),
            pl.BlockSpec(pscale.shape, const2),
            pl.BlockSpec(ba.shape, const2),
            pl.BlockSpec(ng.shape, const2),
            pl.BlockSpec(wout.shape, const2),
            pl.BlockSpec(lng.shape, const2),
            pl.BlockSpec(lnb.shape, const2),
        ],
        out_specs=[
            pl.BlockSpec((bs, t, D_MODEL), seq3),
            pl.BlockSpec((POOL_BUF, bs, POOL_WIDTH), hist3),
            pl.BlockSpec((bs, GLA_KEY_WIDTH, GLA_DV), seq3),
        ],
        out_shape=[
            jax.ShapeDtypeStruct((nb, t, D_MODEL), F32),
            jax.ShapeDtypeStruct((POOL_BUF, nb, POOL_WIDTH), F32),
            jax.ShapeDtypeStruct((nb, GLA_KEY_WIDTH, GLA_DV), F32),
        ],
        scratch_shapes=[pltpu.VMEM((bs * t, C_END + PITCH_PAD), F32),
                        pltpu.VMEM((bs * t, C_END + PITCH_PAD), F32),
                        pltpu.VMEM((bs * t, D_MODEL + PITCH_PAD), BF16),
                        pltpu.VMEM((len(POOL_WINDOWS), bs * t, POOL_GROUP_DIM), F32),
                        pltpu.VMEM((len(POOL_WINDOWS), bs * t, POOL_GROUP_DIM), F32)],
        compiler_params=pltpu.CompilerParams(
            dimension_semantics=("arbitrary",), vmem_limit_bytes=VMEM_LIMIT),
        name="sample_layer",
    )(x, x, pool_in, s_in, win, wmap, pscale, ba, ng, wout, lng, lnb)


def _prep_body(wt_ref, wfat_ref, wa2_ref, wout_f32_ref, wmap_f32_ref, win_ref, wout_ref, wmap_ref, *, n_plain,
               n_wout):
    i = pl.program_id(0)

    @pl.when(i == 0)
    def _():
        wmap_ref[...] = wmap_f32_ref[...].astype(BF16)

    @pl.when(i < n_wout)
    def _():
        wout_ref[:, 0:D_MODEL] = wout_f32_ref[...].astype(BF16)
        wout_ref[:, D_MODEL:] = jnp.zeros((wout_ref.shape[0], PITCH_PAD), BF16)

    @pl.when(i < n_plain)
    def _():
        win_ref[...] = jnp.transpose(wt_ref[...]).astype(BF16)

    @pl.when(i == n_plain)
    def _():
        fa_hi, fa_lo = _split_bf16(wfat_ref[...])
        a2_hi, a2_lo = _split_bf16(wa2_ref[...])
        acc = _dot_tn(fa_hi, a2_hi) + (_dot_tn(fa_hi, a2_lo) + _dot_tn(fa_lo, a2_hi))
        win_ref[:, 0:GLA_KEY_WIDTH] = acc.astype(BF16)
        win_ref[:, GLA_KEY_WIDTH:] = jnp.zeros((D_MODEL, PREP_COLS - GLA_KEY_WIDTH), BF16)


def _prep_weights(wt, wa2, wout_f32, wmap_f32):
    assert C_GL % PREP_COLS == 0 and C_END + PITCH_PAD - C_GL <= PREP_COLS and C_GL % GLA_RANK == 0
    n_plain = C_GL // PREP_COLS
    n_wout = 4
    wout_rows = wout_f32.shape[0] // n_wout
    assert n_wout <= n_plain + 1
    wout_block = lambda i: (jnp.minimum(i, n_wout - 1), 0)
    return pl.pallas_call(
        functools.partial(_prep_body, n_plain=n_plain, n_wout=n_wout),
        grid=(n_plain + 1,),
        in_specs=[
            pl.BlockSpec((PREP_COLS, D_MODEL), lambda i: (jnp.minimum(i, n_plain - 1), 0)),
            pl.BlockSpec((GLA_RANK, D_MODEL), lambda i: (C_GL // GLA_RANK, 0)),
            pl.BlockSpec(wa2.shape, lambda i: (0, 0)),
            pl.BlockSpec((wout_rows, D_MODEL), wout_block),
            pl.BlockSpec(wmap_f32.shape, lambda i: (0, 0, 0)),
        ],
        out_specs=[
            pl.BlockSpec((D_MODEL, PREP_COLS), lambda i: (0, i)),
            pl.BlockSpec((wout_rows, D_MODEL + PITCH_PAD), wout_block),
            pl.BlockSpec(wmap_f32.shape, lambda i: (0, 0, 0)),
        ],
        out_shape=[
            jax.ShapeDtypeStruct((D_MODEL, C_END + PITCH_PAD), BF16),
            jax.ShapeDtypeStruct((D_MODEL, D_MODEL + PITCH_PAD), BF16),
            jax.ShapeDtypeStruct(wmap_f32.shape, BF16),
        ],
        compiler_params=pltpu.CompilerParams(dimension_semantics=("arbitrary",), vmem_limit_bytes=VMEM_LIMIT),
        name="prep_weights",
    )(wt, wt, wa2, wout_f32, wmap_f32)


def kernel(x_prompt, x_sample, state_pool, state_gla, w_in, w_pool_map, pool_scale, w_gla_a2, b_gla_a,
           gla_norm_g, w_out, ln_g, ln_b):
    assert w_in.shape[0] == DEPTH
    win, wout, wmap = _prep_weights(jnp.transpose(w_in[0]), w_gla_a2[0], w_out[0], w_pool_map[0])
    pscale = pool_scale[0].reshape(1, POOL_WIDTH)
    ba = b_gla_a[0].reshape(1, GLA_KEY_WIDTH)
    ng = gla_norm_g[0].reshape(1, GLA_WIDTH)
    lng = ln_g[0].reshape(1, D_MODEL)
    lnb = ln_b[0].reshape(1, D_MODEL)
    params = (win, wmap, pscale, ba, ng, wout, lng, lnb)

    y_p, pool_p, gla_p = _prompt_call(x_prompt, *params)
    nsamp = x_sample.shape[0]
    y_s, pool_s, gla_s = _sample_call(
        x_sample, jnp.transpose(state_pool[0], (1, 0, 2)),
        state_gla[0].reshape(nsamp, GLA_KEY_WIDTH, GLA_DV), *params, start_pos=PAST_LEN)

    nb = x_prompt.shape[0]
    return (y_p, y_s, jnp.transpose(pool_p, (1, 0, 2))[None],
            gla_p.reshape(1, nb, GLA_HEADS, GLA_DK, GLA_DV),
            jnp.transpose(pool_s, (1, 0, 2))[None],
            gla_s.reshape(1, nsamp, GLA_HEADS, GLA_DK, GLA_DV))
```

```python
import functools
import itertools

import jax
import jax.numpy as jnp
from jax import lax
from jax.experimental import pallas as pl
from jax.experimental.pallas import tpu as pltpu

F32 = jnp.float32
BF16 = jnp.bfloat16

D_MODEL = 1024
POOL_WIDTH = 512
POOL_WINDOWS = (2, 4, 8, 16)
POOL_GROUP_DIM = 128
POOL_BUF = 15
HIST_ROWS = 16
GLA_HEADS = 4
GLA_DK = 64
GLA_DV = 128
GLA_KEY_WIDTH = GLA_HEADS * GLA_DK
GLA_WIDTH = GLA_HEADS * GLA_DV
GLA_RANK = 16
GLA_TAU = 16.0
LN_EPS = 1e-5
RMS_EPS = 1e-6
DEPTH = 1
DEEPNORM_ALPHA = (2.0 * DEPTH) ** 0.25
PAST_LEN = 16384

C_U, C_GP, C_Q, C_K, C_V, C_GG, C_GL, C_END = 0, 512, 1024, 1280, 1536, 2048, 2560, 2816

MXU_COLS = 256
PROMPT_TILE = 256
PITCH_PAD = 128
PROMPT_SUBTILE = 256
SUBTILE_TAIL_UNITS = 0
LN_ROW_BLOCKS = 4
PUMP_ROWS = 256
PUMP_PLAN = {"pool": (0, 1, 0, 1), "cumsum": 0, "scores": (1, 1, 1, 1), "level": (0, 0, 0, 1, 0, 0),
             "state": (0, 0, 0, 0), "epilogue": 1, "layernorm": (1, 1, 1, 0)}
SAFE_LOG_DECAY = 30.0
GLA_CHUNK = 64
SAMPLE_SEQS = 16
PREP_COLS = 512
VMEM_LIMIT = 56 * 1024 * 1024


def _dot(a, b):
    return jnp.dot(a, b, preferred_element_type=F32)


def _dot_nt(a, b):
    return lax.dot_general(a, b, (((1,), (1,)), ((), ())), preferred_element_type=F32)


def _dot_tn(a, b):
    return lax.dot_general(a, b, (((0,), (0,)), ((), ())), preferred_element_type=F32)


def _silu(x):
    return x * (1.0 / (1.0 + jnp.exp(-x)))


def _log_sigmoid(x):
    return jnp.minimum(x, 0.0) - jnp.log1p(jnp.exp(-jnp.abs(x)))


def _split_bf16(x):
    hi = x.astype(BF16)
    lo = (x - hi.astype(F32)).astype(BF16)
    return hi, lo


def _gated_mix(pool_act, o, gg, ng_ref):
    normed = []
    for h in range(GLA_HEADS):
        oh = o[:, h * GLA_DV:(h + 1) * GLA_DV]
        ms = jnp.mean(oh * oh, axis=-1, keepdims=True)
        normed.append(oh * lax.rsqrt(ms + RMS_EPS))
    gla_act = jnp.concatenate(normed, axis=1) * ng_ref[...] * _silu(gg)
    return jnp.concatenate([pool_act, gla_act], axis=1).astype(BF16)


def _layer_norm(h, lng_ref, lnb_ref):
    mu = jnp.mean(h, axis=-1, keepdims=True)
    d = h - mu
    var = jnp.mean(d * d, axis=-1, keepdims=True)
    return d * lax.rsqrt(var + LN_EPS) * lng_ref[...] + lnb_ref[...]


def _epilogue(x, pool_act, o, gg, ng_ref, wout_ref, lng_ref, lnb_ref, pump=lambda n=None: None):
    pump(PUMP_PLAN["epilogue"])
    y = _dot(_gated_mix(pool_act, o, gg, ng_ref), wout_ref[:, 0:D_MODEL])
    pump()
    return _layer_norm(DEEPNORM_ALPHA * x + y, lng_ref, lnb_ref)


def _head_block_mask(rows_per_head, cols_per_head, shape):
    r = lax.broadcasted_iota(jnp.int32, shape, 0) // rows_per_head
    c = lax.broadcasted_iota(jnp.int32, shape, 1) // cols_per_head
    return r == c


def _score_masks(rows, n_levels):
    shape = (rows, GLA_HEADS * rows)
    t_idx = lax.broadcasted_iota(jnp.int32, shape, 0)
    s_idx = lax.broadcasted_iota(jnp.int32, shape, 1) % rows
    masks = [t_idx == s_idx]
    for lvl in range(n_levels):
        half = 1 << lvl
        blk = 2 * half
        masks.append((t_idx // blk == s_idx // blk) & (t_idx % blk >= half) & (s_idx % blk < half))
    return masks


def _decayed_scores(q, k, la, b, masks, head_blk, pump=lambda n=None: None):
    rows = q.shape[0]
    r_idx = lax.broadcasted_iota(jnp.int32, q.shape, 0)

    def scores(qt, kt):
        kh = jnp.where(head_blk, jnp.concatenate([kt.astype(BF16)] * GLA_HEADS, axis=0), jnp.zeros((), BF16))
        return _dot_nt(qt.astype(BF16), kh)

    acc = jnp.where(masks[0], scores(q, k), 0.0)
    for lvl in range(len(masks) - 1):
        half = 1 << lvl
        blk = 2 * half
        if lvl == 0:
            neg = jnp.where(r_idx % 2 == 1, la, 0.0)
        elif lvl == 1:
            r4 = r_idx % 4
            nxt = pltpu.roll(la, rows - 1, 0)
            prv = pltpu.roll(la, 1, 0)
            neg = jnp.where(r4 == 0, nxt, jnp.where(r4 == 1, 0.0, jnp.where(r4 == 2, la, la + prv)))
        else:
            ref = jnp.concatenate(
                [jnp.broadcast_to(b[j * blk + half - 1:j * blk + half, :], (blk, b.shape[1]))
                 for j in range(rows // blk)], axis=0)
            d = b - ref
            neg = jnp.minimum(d, -d)
        w = jnp.exp(neg)
        acc = jnp.where(masks[lvl + 1], scores(q * w, k * w), acc)
        pump(PUMP_PLAN["level"][lvl])
    return acc.astype(BF16)


def _pair_block_diag(a0, a1):
    z = jnp.zeros_like(a0)
    return jnp.concatenate(
        [jnp.concatenate([a0, z], axis=1), jnp.concatenate([z, a1], axis=1)], axis=0)


def _make_pump(units, xb_ref, win_ref, proj_ref, unit_rows=PUMP_ROWS):
    units = iter(units)

    def pump(n=None):
        for r, lo in (units if n is None else itertools.islice(units, n)):
            proj_ref[r:r + unit_rows, lo:lo + MXU_COLS] = _dot(
                xb_ref[r:r + unit_rows, 0:D_MODEL], win_ref[:, lo:lo + MXU_COLS])

    return pump


def _naive_scores(q, k, b, causal, head_blk):
    ks = (k * jnp.exp(-b)).astype(BF16)
    kh = jnp.where(head_blk, jnp.concatenate([ks] * GLA_HEADS, axis=0), jnp.zeros((), BF16))
    return jnp.where(causal, _dot_nt((q * jnp.exp(b)).astype(BF16), kh), 0.0).astype(BF16)


def _prompt_stage_b(pos0, proj_ref, xres_ref, la_ref, wmap_ref, pscale_ref, ba_ref, ng_ref, wout_ref, lng_ref,
                    lnb_ref, hprev_ref, yprev_ref, hout_ref, ext_ref, s_ref, pump, next_decay, *, tc, robust, tail):
    def pooling():
        u = proj_ref[:, C_U:C_GP]
        ext_ref[HIST_ROWS:HIST_ROWS + tc, 0:POOL_WIDTH] = u
        pos16 = pos0 + lax.broadcasted_iota(jnp.int32, (HIST_ROWS, POOL_GROUP_DIM), 0)
        mixed = []
        for g, w in enumerate(POOL_WINDOWS):
            sl = slice(g * POOL_GROUP_DIM, (g + 1) * POOL_GROUP_DIM)
            e = ext_ref[:, sl]
            s = e
            sh = 1
            while sh < w:
                s = s + pltpu.roll(s, sh, 0)
                sh *= 2
            wsum = s[HIST_ROWS:]
            self_rows = e[HIST_ROWS:]
            cnt = jnp.minimum(pos16 + 1, w).astype(F32)
            first = wsum[:HIST_ROWS] / cnt - self_rows[:HIST_ROWS]
            rest = wsum[HIST_ROWS:] * (1.0 / w) - self_rows[HIST_ROWS:]
            pooled = jnp.concatenate([first, rest], axis=0)
            mixed.append(_dot(pooled.astype(BF16), wmap_ref[g]))
            pump(PUMP_PLAN["pool"][g])
        return jnp.concatenate(mixed, axis=1) * pscale_ref[...] * _silu(proj_ref[:, C_GP:C_Q])

    cc = GLA_CHUNK
    n_chunks = tc // cc
    rr = lax.broadcasted_iota(jnp.int32, (tc, tc), 0)
    cl = lax.broadcasted_iota(jnp.int32, (tc, tc), 1)
    ltri = jnp.where((rr // cc == cl // cc) & (cl <= rr), 1.0, 0.0).astype(BF16)
    masks = _score_masks(cc, cc.bit_length() - 1 if robust else 0)
    causal = (lax.broadcasted_iota(jnp.int32, (cc, GLA_HEADS * cc), 1) % cc
              <= lax.broadcasted_iota(jnp.int32, (cc, GLA_HEADS * cc), 0))
    blk = _head_block_mask(cc, GLA_DK, (GLA_HEADS * cc, GLA_KEY_WIDTH))

    hi, lo = _split_bf16(la_ref[:, 0:GLA_KEY_WIDTH])
    b_all = _dot(ltri, hi) + _dot(ltri, lo)
    pump(PUMP_PLAN["cumsum"])
    next_decay()
    lhs_list, v_list, upd_list, dcol_list = [], [], [], []
    for c in range(n_chunks):
        rs = slice(c * cc, (c + 1) * cc)
        b = b_all[rs]
        b_end = b[cc - 1:cc, :]
        q_c = proj_ref[rs, C_Q:C_K] * (GLA_DK ** -0.5)
        k_c = proj_ref[rs, C_K:C_V]
        qs = (q_c * jnp.exp(b)).astype(BF16)
        kd = (k_c * jnp.exp(b_end - b)).astype(BF16)
        if robust:
            a_all = _decayed_scores(q_c, k_c, la_ref[rs, 0:GLA_KEY_WIDTH], b, masks, blk, pump)
        else:
            a_all = _naive_scores(q_c, k_c, b, causal, blk)
            pump(PUMP_PLAN["scores"][c])
        vc = proj_ref[rs, C_V:C_GG].astype(BF16)
        upds = []
        for p in range(GLA_HEADS // 2):
            upd = _dot_tn(kd[:, p * 2 * GLA_DK:(p + 1) * 2 * GLA_DK], vc[:, p * 2 * GLA_DV:(p + 1) * 2 * GLA_DV])
            upds.append(upd[0:GLA_DK, 0:GLA_DV])
            upds.append(upd[GLA_DK:2 * GLA_DK, GLA_DV:2 * GLA_DV])
        lhs_list.append([jnp.concatenate([a_all[:, p * 2 * cc:(p + 1) * 2 * cc],
                                          qs[:, p * 2 * GLA_DK:(p + 1) * 2 * GLA_DK]], axis=1)
                         for p in range(GLA_HEADS // 2)])
        v_list.append(vc)
        upd_list.append(jnp.concatenate(upds, axis=0))
        dcol_list.append(jnp.transpose(jnp.broadcast_to(jnp.exp(b_end), (GLA_DV, GLA_KEY_WIDTH))))

    o_chunks = []
    state = s_ref[...]
    for c in range(n_chunks):
        s_bf = state.astype(BF16)
        vc = v_list[c]
        outs = []
        for p in range(GLA_HEADS // 2):
            h0, h1 = 2 * p, 2 * p + 1
            rhs = jnp.concatenate([
                _pair_block_diag(vc[:, h0 * GLA_DV:(h0 + 1) * GLA_DV], vc[:, h1 * GLA_DV:(h1 + 1) * GLA_DV]),
                _pair_block_diag(s_bf[h0 * GLA_DK:(h0 + 1) * GLA_DK], s_bf[h1 * GLA_DK:(h1 + 1) * GLA_DK]),
            ], axis=0)
            outs.append(_dot(lhs_list[c][p], rhs))
        o_chunks.append(jnp.concatenate(outs, axis=1))
        state = state * dcol_list[c] + upd_list[c]
        pump(PUMP_PLAN["state"][c])
    s_ref[...] = state
    o = jnp.concatenate(o_chunks, axis=0)
    pool_act = pooling()

    pump(PUMP_PLAN["epilogue"])
    mix = _gated_mix(pool_act, o, proj_ref[:, C_GG:C_GL], ng_ref)
    y = _prompt_layer_norm(hprev_ref, yprev_ref, lng_ref, lnb_ref, pump,
                           between=lambda r: _dot(mix, wout_ref[:, r * MXU_COLS:(r + 1) * MXU_COLS]))
    pump(tail)
    hout_ref[:, 0:D_MODEL] = DEEPNORM_ALPHA * xres_ref[...] + jnp.concatenate(y, axis=1)


def _prompt_layer_norm(h_ref, y_ref, lng_ref, lnb_ref, pump=lambda n=None: None, between=lambda r: None):
    assert D_MODEL == LN_ROW_BLOCKS * MXU_COLS
    n = h_ref.shape[0] // LN_ROW_BLOCKS
    emitted = []
    for r in range(LN_ROW_BLOCKS):
        rows = slice(r * n, (r + 1) * n)
        emitted.append(between(r))
        pump(PUMP_PLAN["layernorm"][r])
        y_ref[rows, :] = _layer_norm(h_ref[rows, 0:D_MODEL], lng_ref, lnb_ref)
    return emitted


def _prompt_body(x_ref, xres_ref, win_ref, wmap_ref, pscale_ref, ba_ref, ng_ref, wout_ref, lng_ref, lnb_ref,
                 y_ref, pool_ref, s_out_ref,
                 proj0_ref, proj1_ref, la0_ref, la1_ref, h0_ref, h1_ref, xb_ref, ext_ref, s_ref, steep_ref,
                 *, tc, sub, nj, ntiles):
    i = pl.program_id(0)
    jb = (i + nj - 1) % nj

    @pl.when(i == 0)
    def _():
        steep_ref[0] = 0
        pool_ref[...] = jnp.zeros_like(pool_ref)
        h1_ref[...] = jnp.zeros_like(h1_ref)

    @pl.when(jb == 0)
    def _():
        ext_ref[0:HIST_ROWS, 0:POOL_WIDTH] = jnp.zeros((HIST_ROWS, POOL_WIDTH), F32)
        s_ref[...] = jnp.zeros_like(s_ref)

    units = [(r, lo) for lo in (C_GL, *range(0, C_GL, MXU_COLS)) for r in range(0, tc, PUMP_ROWS)]
    sets = ((proj0_ref, la0_ref, h0_ref), (proj1_ref, la1_ref, h1_ref))

    def step(parity, robust, with_a=True, with_b=True):
        proj_a, la_a, h_prev = sets[parity]
        proj_b, la_b, h_out = sets[1 - parity]
        pump, next_decay = (lambda n=None: None), (lambda: None)
        if with_a:
            xb_ref[:, 0:D_MODEL] = x_ref[0].astype(BF16)
            pump = _make_pump(units, xb_ref, win_ref, proj_a)

            def next_decay():
                la = _log_sigmoid(proj_a[:, C_GL:C_END] + ba_ref[...]) * (1.0 / GLA_TAU)
                la_a[:, 0:GLA_KEY_WIDTH] = la
                chunk_decay = [jnp.sum(la[c * GLA_CHUNK:(c + 1) * GLA_CHUNK], axis=0, keepdims=True)
                               for c in range(tc // GLA_CHUNK)]
                lowest = jnp.min(functools.reduce(jnp.minimum, chunk_decay))
                steep_ref[0] = jnp.where(lowest < -SAFE_LOG_DECAY, 1, 0).astype(jnp.int32)

            if not with_b:
                pump()
                next_decay()
                return
            pump(tc // PUMP_ROWS)

        if not with_b:
            _prompt_layer_norm(h_prev, y_ref.at[0], lng_ref, lnb_ref)
            return
        n_sub = tc // sub
        for k in range(n_sub):
            rows = slice(k * sub, (k + 1) * sub)
            last = k == n_sub - 1
            _prompt_stage_b((jb * n_sub + k) * sub, proj_b.at[rows, :], xres_ref.at[0, rows, :], la_b.at[rows, :],
                            wmap_ref, pscale_ref, ba_ref, ng_ref, wout_ref, lng_ref, lnb_ref,
                            h_prev.at[rows, :], y_ref.at[0, rows, :], h_out.at[rows, :],
                            ext_ref, s_ref, pump, next_decay if k == 0 else (lambda: None),
                            tc=sub, robust=robust, tail=None if last else SUBTILE_TAIL_UNITS)
            if not last:
                ext_ref[0:HIST_ROWS, 0:POOL_WIDTH] = ext_ref[sub:sub + HIST_ROWS, 0:POOL_WIDTH]

    steep = steep_ref[0] != 0

    @pl.when(i == 0)
    def _():
        step(0, False, with_b=False)

    for robust in (False, True):
        for parity in (0, 1):
            @pl.when((i > 0) & (i < ntiles) & (i % 2 == parity) & (steep == robust))
            def _(parity=parity, robust=robust):
                step(parity, robust)

        @pl.when((i == ntiles) & (steep == robust))
        def _(robust=robust):
            step(ntiles % 2, robust, with_a=False)

    @pl.when(i == ntiles + 1)
    def _():
        step((ntiles + 1) % 2, False, with_a=False, with_b=False)

    @pl.when((i > 0) & (i <= ntiles) & (jb == nj - 1))
    def _():
        mine = lax.broadcasted_iota(jnp.int32, pool_ref.shape[1:], 0) == (i - 1) // nj
        for r in range(POOL_BUF):
            row = ext_ref[sub + 1 + r:sub + 2 + r, 0:POOL_WIDTH]
            pool_ref[r] = jnp.where(mine, jnp.broadcast_to(row, pool_ref.shape[1:]), pool_ref[r])
        s_out_ref[0] = s_ref[...]

    @pl.when((i > 0) & (i <= ntiles))
    def _():
        ext_ref[0:HIST_ROWS, 0:POOL_WIDTH] = ext_ref[sub:sub + HIST_ROWS, 0:POOL_WIDTH]


def _prompt_call(x, win, wmap, pscale, ba, ng, wout, lng, lnb):
    nb, t, _ = x.shape
    tc = PROMPT_TILE
    nj = t // tc
    ntiles = nb * nj
    const2 = lambda i: (0, 0)

    def tile_a(i):
        ia = jnp.minimum(i, ntiles - 1)
        return (ia // nj, ia % nj, 0)

    def tile_b(i):
        ib = jnp.clip(i - 1, 0, ntiles - 1)
        return (ib // nj, ib % nj, 0)

    def tile_c(i):
        ic = jnp.maximum(i - 2, 0)
        return (ic // nj, ic % nj, 0)

    def seq_b(i):
        return (jnp.clip(i - 1, 0, ntiles - 1) // nj, 0, 0)

    return pl.pallas_call(
        functools.partial(_prompt_body, tc=tc, sub=PROMPT_SUBTILE, nj=nj, ntiles=ntiles),
        grid=(ntiles + 2,),
        in_specs=[
            pl.BlockSpec((1, tc, D_MODEL), tile_a),
            pl.BlockSpec((1, tc, D_MODEL), tile_b),
            pl.BlockSpec(win.shape, const2),
            pl.BlockSpec(wmap.shape, lambda i: (0, 0, 0)),
            pl.BlockSpec(pscale.shape, const2),
            pl.BlockSpec(ba.shape, const2),
            pl.BlockSpec(ng.shape, const2),
            pl.BlockSpec(wout.shape, const2),
            pl.BlockSpec(lng.shape, const2),
            pl.BlockSpec(lnb.shape, const2),
        ],
        out_specs=[
            pl.BlockSpec((1, tc, D_MODEL), tile_c),
            pl.BlockSpec((POOL_BUF, nb, POOL_WIDTH), lambda i: (0, 0, 0)),
            pl.BlockSpec((1, GLA_KEY_WIDTH, GLA_DV), seq_b),
        ],
        out_shape=[
            jax.ShapeDtypeStruct((nb, t, D_MODEL), F32),
            jax.ShapeDtypeStruct((POOL_BUF, nb, POOL_WIDTH), F32),
            jax.ShapeDtypeStruct((nb, GLA_KEY_WIDTH, GLA_DV), F32),
        ],
        scratch_shapes=[
            pltpu.VMEM((tc, C_END + PITCH_PAD), F32),
            pltpu.VMEM((tc, C_END + PITCH_PAD), F32),
            pltpu.VMEM((tc, GLA_KEY_WIDTH + PITCH_PAD), F32),
            pltpu.VMEM((tc, GLA_KEY_WIDTH + PITCH_PAD), F32),
            pltpu.VMEM((tc, D_MODEL + PITCH_PAD), F32),
            pltpu.VMEM((tc, D_MODEL + PITCH_PAD), F32),
            pltpu.VMEM((tc, D_MODEL + PITCH_PAD), BF16),
            pltpu.VMEM((HIST_ROWS + PROMPT_SUBTILE, POOL_WIDTH + PITCH_PAD), F32),
            pltpu.VMEM((GLA_KEY_WIDTH, GLA_DV), F32),
            pltpu.SMEM((1,), jnp.int32),
        ],
        compiler_params=pltpu.CompilerParams(
            dimension_semantics=("arbitrary",), vmem_limit_bytes=VMEM_LIMIT),
        name="prompt_layer",
    )(x, x, win, wmap, pscale, ba, ng, wout, lng, lnb)


def _sample_stage_b(proj_ref, xres_ref, pool_in_ref, s_in_ref, wmap_ref, pscale_ref, ba_ref, ng_ref, wout_ref,
                    lng_ref, lnb_ref, y_ref, pool_ref, s_out_ref, u_ref, pooled_ref, pump, *, bs, t, start_pos):
    rows = bs * t
    x = xres_ref[...].reshape(rows, D_MODEL)

    def proj(lo, hi):
        return proj_ref[:, lo:hi]

    u = proj(C_U, C_GP)
    mixed = []
    for g, w in enumerate(POOL_WINDOWS):
        sl = slice(g * POOL_GROUP_DIM, (g + 1) * POOL_GROUP_DIM)
        u_ref[g] = u[:, sl]
        ext = [pool_in_ref[r, :, sl] for r in range(POOL_BUF)]
        ext += [u_ref[g, pl.ds(tok, bs, stride=t), :] for tok in range(t)]
        for r in range(POOL_BUF):
            pool_ref[r, :, sl] = ext[t + r]
        sums = ext
        sh = 1
        while sh < w:
            first = 2 * sh - 1
            sums = [None] * first + [sums[r] + sums[r - sh] for r in range(first, len(sums))]
            sh *= 2
        for tok in range(t):
            cnt = float(min(start_pos + tok + 1, w))
            pooled_ref[g, pl.ds(tok, bs, stride=t), :] = sums[POOL_BUF + tok] / cnt - ext[POOL_BUF + tok]
        mixed.append(_dot(pooled_ref[g].astype(BF16), wmap_ref[g]))
        pump(g % 2)
    gp = proj(C_GP, C_Q)
    pool_act = jnp.concatenate(mixed, axis=1) * pscale_ref[...] * _silu(gp)

    q = proj(C_Q, C_K) * (GLA_DK ** -0.5)
    k = proj(C_K, C_V)
    vb = proj(C_V, C_GG).astype(BF16)
    gg = proj(C_GG, C_GL)
    la = _log_sigmoid(proj(C_GL, C_END) + ba_ref[...]) * (1.0 / GLA_TAU)

    rr = lax.broadcasted_iota(jnp.int32, (rows, rows), 0)
    cl = lax.broadcasted_iota(jnp.int32, (rows, rows), 1)
    same = (rr // t) == (cl // t)
    ltri = jnp.where(same & (cl <= rr), 1.0, 0.0).astype(BF16)
    lall = jnp.where(same, 1.0, 0.0).astype(BF16)
    hi, lo = _split_bf16(la)
    b = _dot(ltri, hi) + _dot(ltri, lo)
    b_end = _dot(lall, hi) + _dot(lall, lo)
    pump(1)
    qs = (q * jnp.exp(b)).astype(BF16)
    kd = (k * jnp.exp(b_end - b)).astype(BF16)

    assert t & (t - 1) == 0, "tokens per sequence must be a power of two"
    blk = _head_block_mask(rows, GLA_DK, (GLA_HEADS * rows, GLA_KEY_WIDTH))
    a_all = _decayed_scores(q, k, la, b, _score_masks(rows, t.bit_length() - 1), blk)
    pump(2)

    s_old = s_in_ref[...]
    s_bf = s_old.astype(BF16)
    qs3 = qs.reshape(bs, t, GLA_KEY_WIDTH)
    kd3 = kd.reshape(bs, t, GLA_KEY_WIDTH)
    v3 = vb.reshape(bs, t, GLA_WIDTH)
    bdims_nn = (((2,), (1,)), ((0,), (0,)))
    bdims_tn = (((1,), (1,)), ((0,), (0,)))
    assert bs <= GLA_DV
    seq_sel = jnp.where(lax.broadcasted_iota(jnp.int32, (rows, GLA_DV), 0) // t
                        == lax.broadcasted_iota(jnp.int32, (rows, GLA_DV), 1), 1.0, 0.0).astype(BF16)
    decay_t = jnp.exp(_dot_tn(hi, seq_sel) + _dot_tn(lo, seq_sel))
    decay = jnp.stack([jnp.broadcast_to(decay_t[:, j:j + 1], (GLA_KEY_WIDTH, GLA_DV)) for j in range(bs)], axis=0)
    outs, upds = [], []
    for p in range(GLA_HEADS // 2):
        h0, h1 = 2 * p, 2 * p + 1
        intra = _dot(a_all[:, p * 2 * rows:(p + 1) * 2 * rows],
                     _pair_block_diag(vb[:, h0 * GLA_DV:(h0 + 1) * GLA_DV], vb[:, h1 * GLA_DV:(h1 + 1) * GLA_DV]))
        s0 = s_bf[:, h0 * GLA_DK:(h0 + 1) * GLA_DK, :]
        s1 = s_bf[:, h1 * GLA_DK:(h1 + 1) * GLA_DK, :]
        z = jnp.zeros_like(s0)
        s_pair = jnp.concatenate(
            [jnp.concatenate([s0, z], axis=2), jnp.concatenate([z, s1], axis=2)], axis=1)
        inter = lax.dot_general(qs3[:, :, p * 2 * GLA_DK:(p + 1) * 2 * GLA_DK], s_pair, bdims_nn,
                                preferred_element_type=F32)
        outs.append(intra + inter.reshape(rows, 2 * GLA_DV))
        upd = lax.dot_general(kd3[:, :, p * 2 * GLA_DK:(p + 1) * 2 * GLA_DK],
                              v3[:, :, p * 2 * GLA_DV:(p + 1) * 2 * GLA_DV], bdims_tn,
                              preferred_element_type=F32)
        upds.append(upd[:, 0:GLA_DK, 0:GLA_DV])
        upds.append(upd[:, GLA_DK:2 * GLA_DK, GLA_DV:2 * GLA_DV])
        pump(2)
    o = jnp.concatenate(outs, axis=1)
    s_out_ref[...] = s_old * decay + jnp.concatenate(upds, axis=1)

    y = _epilogue(x, pool_act, o, gg, ng_ref, wout_ref, lng_ref, lnb_ref, pump)
    y_ref[...] = y.reshape(bs, t, D_MODEL)


def _sample_body(x_ref, xres_ref, pool_in_ref, s_in_ref, win_ref, wmap_ref, pscale_ref, ba_ref, ng_ref, wout_ref,
                 lng_ref, lnb_ref, y_ref, pool_ref, s_out_ref,
                 proj0_ref, proj1_ref, xb_ref, u_ref, pooled_ref,
                 *, bs, t, start_pos, n_groups):
    i = pl.program_id(0)
    rows = bs * t
    sets = (proj0_ref, proj1_ref)
    units = [(0, lo) for lo in range(0, C_END, MXU_COLS)]

    def step(proj_a, proj_b):
        pump = lambda n=None: None
        if proj_a is not None:
            xb_ref[:, 0:D_MODEL] = x_ref[...].reshape(rows, D_MODEL).astype(BF16)
            pump = _make_pump(units, xb_ref, win_ref, proj_a, rows)
            pump(1 if proj_b is not None else None)
        if proj_b is not None:
            _sample_stage_b(proj_b, xres_ref, pool_in_ref, s_in_ref, wmap_ref, pscale_ref, ba_ref, ng_ref,
                            wout_ref, lng_ref, lnb_ref, y_ref, pool_ref, s_out_ref, u_ref, pooled_ref, pump,
                            bs=bs, t=t, start_pos=start_pos)

    @pl.when(i == 0)
    def _():
        step(sets[0], None)

    for parity in (0, 1):
        @pl.when((i > 0) & (i < n_groups) & (i % 2 == parity))
        def _(parity=parity):
            step(sets[parity], sets[1 - parity])

    @pl.when(i == n_groups)
    def _():
        step(None, sets[(n_groups - 1) % 2])


def _sample_call(x, pool_in, s_in, win, wmap, pscale, ba, ng, wout, lng, lnb, start_pos):
    nb, t, _ = x.shape
    bs = SAMPLE_SEQS
    n_groups = nb // bs
    const2 = lambda i: (0, 0)
    seq3_a = lambda i: (jnp.minimum(i, n_groups - 1), 0, 0)
    seq3 = lambda i: (jnp.maximum(i - 1, 0), 0, 0)
    hist3 = lambda i: (0, jnp.maximum(i - 1, 0), 0)
    return pl.pallas_call(
        functools.partial(_sample_body, bs=bs, t=t, start_pos=start_pos, n_groups=n_groups),
        grid=(n_groups + 1,),
        in_specs=[
            pl.BlockSpec((bs, t, D_MODEL), seq3_a),
            pl.BlockSpec((bs, t, D_MODEL), seq3),
            pl.BlockSpec((POOL_BUF, bs, POOL_WIDTH), hist3),
            pl.BlockSpec((bs, GLA_KEY_WIDTH, GLA_DV), seq3),
            pl.BlockSpec(win.shape, const2),
            pl.BlockSpec(wmap.shape, lambda i: (0, 0, 0)),
            pl.BlockSpec(pscale.shape, const2),
            pl.BlockSpec(ba.shape, const2),
            pl.BlockSpec(ng.shape, const2),
            pl.BlockSpec(wout.shape, const2),
            pl.BlockSpec(lng.shape, const2),
            pl.BlockSpec(lnb.shape, const2),
        ],
        out_specs=[
            pl.BlockSpec((bs, t, D_MODEL), seq3),
            pl.BlockSpec((POOL_BUF, bs, POOL_WIDTH), hist3),
            pl.BlockSpec((bs, GLA_KEY_WIDTH, GLA_DV), seq3),
        ],
        out_shape=[
            jax.ShapeDtypeStruct((nb, t, D_MODEL), F32),
            jax.ShapeDtypeStruct((POOL_BUF, nb, POOL_WIDTH), F32),
            jax.ShapeDtypeStruct((nb, GLA_KEY_WIDTH, GLA_DV), F32),
        ],
        scratch_shapes=[pltpu.VMEM((bs * t, C_END + PITCH_PAD), F32),
                        pltpu.VMEM((bs * t, C_END + PITCH_PAD), F32),
                        pltpu.VMEM((bs * t, D_MODEL + PITCH_PAD), BF16),
                        pltpu.VMEM((len(POOL_WINDOWS), bs * t, POOL_GROUP_DIM), F32),
                        pltpu.VMEM((len(POOL_WINDOWS), bs * t, POOL_GROUP_DIM), F32)],
        compiler_params=pltpu.CompilerParams(
            dimension_semantics=("arbitrary",), vmem_limit_bytes=VMEM_LIMIT),
        name="sample_layer",
    )(x, x, pool_in, s_in, win, wmap, pscale, ba, ng, wout, lng, lnb)


def _prep_body(wt_ref, wfat_ref, wa2_ref, wout_f32_ref, wmap_f32_ref, win_ref, wout_ref, wmap_ref, *, n_plain,
               n_wout):
    i = pl.program_id(0)

    @pl.when(i == 0)
    def _():
        wmap_ref[...] = wmap_f32_ref[...].astype(BF16)

    @pl.when(i < n_wout)
    def _():
        wout_ref[:, 0:D_MODEL] = wout_f32_ref[...].astype(BF16)
        wout_ref[:, D_MODEL:] = jnp.zeros((wout_ref.shape[0], PITCH_PAD), BF16)

    @pl.when(i < n_plain)
    def _():
        win_ref[...] = jnp.transpose(wt_ref[...]).astype(BF16)

    @pl.when(i == n_plain)
    def _():
        fa_hi, fa_lo = _split_bf16(wfat_ref[...])
        a2_hi, a2_lo = _split_bf16(wa2_ref[...])
        acc = _dot_tn(fa_hi, a2_hi) + (_dot_tn(fa_hi, a2_lo) + _dot_tn(fa_lo, a2_hi))
        win_ref[:, 0:GLA_KEY_WIDTH] = acc.astype(BF16)
        win_ref[:, GLA_KEY_WIDTH:] = jnp.zeros((D_MODEL, PREP_COLS - GLA_KEY_WIDTH), BF16)


def _prep_weights(wt, wa2, wout_f32, wmap_f32):
    assert C_GL % PREP_COLS == 0 and C_END + PITCH_PAD - C_GL <= PREP_COLS and C_GL % GLA_RANK == 0
    n_plain = C_GL // PREP_COLS
    n_wout = 4
    wout_rows = wout_f32.shape[0] // n_wout
    assert n_wout <= n_plain + 1
    wout_block = lambda i: (jnp.minimum(i, n_wout - 1), 0)
    return pl.pallas_call(
        functools.partial(_prep_body, n_plain=n_plain, n_wout=n_wout),
        grid=(n_plain + 1,),
        in_specs=[
            pl.BlockSpec((PREP_COLS, D_MODEL), lambda i: (jnp.minimum(i, n_plain - 1), 0)),
            pl.BlockSpec((GLA_RANK, D_MODEL), lambda i: (C_GL // GLA_RANK, 0)),
            pl.BlockSpec(wa2.shape, lambda i: (0, 0)),
            pl.BlockSpec((wout_rows, D_MODEL), wout_block),
            pl.BlockSpec(wmap_f32.shape, lambda i: (0, 0, 0)),
        ],
        out_specs=[
            pl.BlockSpec((D_MODEL, PREP_COLS), lambda i: (0, i)),
            pl.BlockSpec((wout_rows, D_MODEL + PITCH_PAD), wout_block),
            pl.BlockSpec(wmap_f32.shape, lambda i: (0, 0, 0)),
        ],
        out_shape=[
            jax.ShapeDtypeStruct((D_MODEL, C_END + PITCH_PAD), BF16),
            jax.ShapeDtypeStruct((D_MODEL, D_MODEL + PITCH_PAD), BF16),
            jax.ShapeDtypeStruct(wmap_f32.shape, BF16),
        ],
        compiler_params=pltpu.CompilerParams(dimension_semantics=("arbitrary",), vmem_limit_bytes=VMEM_LIMIT),
        name="prep_weights",
    )(wt, wt, wa2, wout_f32, wmap_f32)


def kernel(x_prompt, x_sample, state_pool, state_gla, w_in, w_pool_map, pool_scale, w_gla_a2, b_gla_a,
           gla_norm_g, w_out, ln_g, ln_b):
    assert w_in.shape[0] == DEPTH
    win, wout, wmap = _prep_weights(jnp.transpose(w_in[0]), w_gla_a2[0], w_out[0], w_pool_map[0])
    pscale = pool_scale[0].reshape(1, POOL_WIDTH)
    ba = b_gla_a[0].reshape(1, GLA_KEY_WIDTH)
    ng = gla_norm_g[0].reshape(1, GLA_WIDTH)
    lng = ln_g[0].reshape(1, D_MODEL)
    lnb = ln_b[0].reshape(1, D_MODEL)
    params = (win, wmap, pscale, ba, ng, wout, lng, lnb)

    y_p, pool_p, gla_p = _prompt_call(x_prompt, *params)
    nsamp = x_sample.shape[0]
    y_s, pool_s, gla_s = _sample_call(
        x_sample, jnp.transpose(state_pool[0], (1, 0, 2)),
        state_gla[0].reshape(nsamp, GLA_KEY_WIDTH, GLA_DV), *params, start_pos=PAST_LEN)

    nb = x_prompt.shape[0]
    return (y_p, y_s, jnp.transpose(pool_p, (1, 0, 2))[None],
            gla_p.reshape(1, nb, GLA_HEADS, GLA_DK, GLA_DV),
            jnp.transpose(pool_s, (1, 0, 2))[None],
            gla_s.reshape(1, nsamp, GLA_HEADS, GLA_DK, GLA_DV))
```

```python
import functools
import itertools

import jax
import jax.numpy as jnp
from jax import lax
from jax.experimental import pallas as pl
from jax.experimental.pallas import tpu as pltpu

F32 = jnp.float32
BF16 = jnp.bfloat16

D_MODEL = 1024
POOL_WIDTH = 512
POOL_WINDOWS = (2, 4, 8, 16)
POOL_GROUP_DIM = 128
POOL_BUF = 15
HIST_ROWS = 16
GLA_HEADS = 4
GLA_DK = 64
GLA_DV = 128
GLA_KEY_WIDTH = GLA_HEADS * GLA_DK
GLA_WIDTH = GLA_HEADS * GLA_DV
GLA_RANK = 16
GLA_TAU = 16.0
LN_EPS = 1e-5
RMS_EPS = 1e-6
DEPTH = 1
DEEPNORM_ALPHA = (2.0 * DEPTH) ** 0.25
PAST_LEN = 16384

C_U, C_GP, C_Q, C_K, C_V, C_GG, C_GL, C_END = 0, 512, 1024, 1280, 1536, 2048, 2560, 2816

MXU_COLS = 256
PROMPT_TILE = 256
PITCH_PAD = 128
PROMPT_SUBTILE = 256
SUBTILE_TAIL_UNITS = 0
PUMP_ROWS = 256
PUMP_PLAN = {"pool": (0, 1, 0, 1), "cumsum": 0, "scores": (1, 1, 1, 1), "level": (0, 0, 0, 1, 0, 0),
             "state": (0, 0, 0, 0), "epilogue": 1, "layernorm": (1, 1, 1, 0)}
SAFE_LOG_DECAY = 30.0
GLA_CHUNK = 64
SAMPLE_SEQS = 16
PREP_COLS = 512
VMEM_LIMIT = 56 * 1024 * 1024


def _dot(a, b):
    return jnp.dot(a, b, preferred_element_type=F32)


def _dot_nt(a, b):
    return lax.dot_general(a, b, (((1,), (1,)), ((), ())), preferred_element_type=F32)


def _dot_tn(a, b):
    return lax.dot_general(a, b, (((0,), (0,)), ((), ())), preferred_element_type=F32)


def _silu(x):
    h = 0.5 * x
    return h + h * jnp.tanh(h)


def _log_sigmoid(x):
    return jnp.minimum(x, 0.0) - jnp.log(1.0 + jnp.exp(-jnp.abs(x)))


def _split_bf16(x):
    hi = x.astype(BF16)
    lo = (x - hi.astype(F32)).astype(BF16)
    return hi, lo


def _epilogue(x, pool_act, o, gg, ng_ref, wout_ref, lng_ref, lnb_ref, pump=lambda n=None: None, tail=None,
              row_blocks=1):
    pump(PUMP_PLAN["epilogue"])
    normed = []
    for h in range(GLA_HEADS):
        oh = o[:, h * GLA_DV:(h + 1) * GLA_DV]
        ms = jnp.mean(oh * oh, axis=-1, keepdims=True)
        normed.append(oh * lax.rsqrt(ms + RMS_EPS))
    gla_act = jnp.concatenate(normed, axis=1) * ng_ref[...] * _silu(gg)
    mix = jnp.concatenate([pool_act, gla_act], axis=1).astype(BF16)
    y = _dot(mix, wout_ref[:, 0:D_MODEL])
    n = x.shape[0] // row_blocks
    outs = []
    for r in range(row_blocks):
        rows = slice(r * n, (r + 1) * n)
        pump(PUMP_PLAN["layernorm"][r])
        h = DEEPNORM_ALPHA * x[rows] + y[rows]
        mu = jnp.mean(h, axis=-1, keepdims=True)
        d = h - mu
        var = jnp.mean(d * d, axis=-1, keepdims=True)
        outs.append(d * lax.rsqrt(var + LN_EPS) * lng_ref[...] + lnb_ref[...])
    pump(tail)
    return outs[0] if row_blocks == 1 else jnp.concatenate(outs, axis=0)


def _head_block_mask(rows_per_head, cols_per_head, shape):
    r = lax.broadcasted_iota(jnp.int32, shape, 0) // rows_per_head
    c = lax.broadcasted_iota(jnp.int32, shape, 1) // cols_per_head
    return r == c


def _score_masks(rows, n_levels):
    shape = (rows, GLA_HEADS * rows)
    t_idx = lax.broadcasted_iota(jnp.int32, shape, 0)
    s_idx = lax.broadcasted_iota(jnp.int32, shape, 1) % rows
    masks = [t_idx == s_idx]
    for lvl in range(n_levels):
        half = 1 << lvl
        blk = 2 * half
        masks.append((t_idx // blk == s_idx // blk) & (t_idx % blk >= half) & (s_idx % blk < half))
    return masks


def _decayed_scores(q, k, la, b, masks, head_blk, pump=lambda n=None: None):
    rows = q.shape[0]
    r_idx = lax.broadcasted_iota(jnp.int32, q.shape, 0)

    def scores(qt, kt):
        kh = jnp.where(head_blk, jnp.concatenate([kt.astype(BF16)] * GLA_HEADS, axis=0), jnp.zeros((), BF16))
        return _dot_nt(qt.astype(BF16), kh)

    acc = jnp.where(masks[0], scores(q, k), 0.0)
    for lvl in range(len(masks) - 1):
        half = 1 << lvl
        blk = 2 * half
        if lvl == 0:
            neg = jnp.where(r_idx % 2 == 1, la, 0.0)
        elif lvl == 1:
            r4 = r_idx % 4
            nxt = pltpu.roll(la, rows - 1, 0)
            prv = pltpu.roll(la, 1, 0)
            neg = jnp.where(r4 == 0, nxt, jnp.where(r4 == 1, 0.0, jnp.where(r4 == 2, la, la + prv)))
        else:
            ref = jnp.concatenate(
                [jnp.broadcast_to(b[j * blk + half - 1:j * blk + half, :], (blk, b.shape[1]))
                 for j in range(rows // blk)], axis=0)
            d = b - ref
            neg = jnp.minimum(d, -d)
        w = jnp.exp(neg)
        acc = jnp.where(masks[lvl + 1], scores(q * w, k * w), acc)
        pump(PUMP_PLAN["level"][lvl])
    return acc.astype(BF16)


def _pair_block_diag(a0, a1):
    z = jnp.zeros_like(a0)
    return jnp.concatenate(
        [jnp.concatenate([a0, z], axis=1), jnp.concatenate([z, a1], axis=1)], axis=0)


def _make_pump(units, xb_ref, win_ref, proj_ref, unit_rows=PUMP_ROWS):
    units = iter(units)

    def pump(n=None):
        for r, lo in (units if n is None else itertools.islice(units, n)):
            proj_ref[r:r + unit_rows, lo:lo + MXU_COLS] = _dot(
                xb_ref[r:r + unit_rows, 0:D_MODEL], win_ref[:, lo:lo + MXU_COLS])

    return pump


def _naive_scores(q, k, b, causal, head_blk):
    ks = (k * jnp.exp(-b)).astype(BF16)
    kh = jnp.where(head_blk, jnp.concatenate([ks] * GLA_HEADS, axis=0), jnp.zeros((), BF16))
    return jnp.where(causal, _dot_nt((q * jnp.exp(b)).astype(BF16), kh), 0.0).astype(BF16)


def _prompt_stage_b(pos0, proj_ref, xres_ref, la_ref, wmap_ref, pscale_ref, ba_ref, ng_ref, wout_ref, lng_ref,
                    lnb_ref, y_ref, ext_ref, s_ref, pump, next_decay, *, tc, robust, tail):
    def pooling():
        u = proj_ref[:, C_U:C_GP]
        ext_ref[HIST_ROWS:HIST_ROWS + tc, 0:POOL_WIDTH] = u
        pos16 = pos0 + lax.broadcasted_iota(jnp.int32, (HIST_ROWS, POOL_GROUP_DIM), 0)
        mixed = []
        for g, w in enumerate(POOL_WINDOWS):
            sl = slice(g * POOL_GROUP_DIM, (g + 1) * POOL_GROUP_DIM)
            e = ext_ref[:, sl]
            s = e
            sh = 1
            while sh < w:
                s = s + pltpu.roll(s, sh, 0)
                sh *= 2
            wsum = s[HIST_ROWS:]
            self_rows = e[HIST_ROWS:]
            cnt = jnp.minimum(pos16 + 1, w).astype(F32)
            first = wsum[:HIST_ROWS] / cnt - self_rows[:HIST_ROWS]
            rest = wsum[HIST_ROWS:] * (1.0 / w) - self_rows[HIST_ROWS:]
            pooled = jnp.concatenate([first, rest], axis=0)
            mixed.append(_dot(pooled.astype(BF16), wmap_ref[g]))
            pump(PUMP_PLAN["pool"][g])
        return jnp.concatenate(mixed, axis=1) * pscale_ref[...] * _silu(proj_ref[:, C_GP:C_Q])

    cc = GLA_CHUNK
    n_chunks = tc // cc
    rr = lax.broadcasted_iota(jnp.int32, (tc, tc), 0)
    cl = lax.broadcasted_iota(jnp.int32, (tc, tc), 1)
    ltri = jnp.where((rr // cc == cl // cc) & (cl <= rr), 1.0, 0.0).astype(BF16)
    masks = _score_masks(cc, cc.bit_length() - 1 if robust else 0)
    causal = (lax.broadcasted_iota(jnp.int32, (cc, GLA_HEADS * cc), 1) % cc
              <= lax.broadcasted_iota(jnp.int32, (cc, GLA_HEADS * cc), 0))
    blk = _head_block_mask(cc, GLA_DK, (GLA_HEADS * cc, GLA_KEY_WIDTH))

    hi, lo = _split_bf16(la_ref[:, 0:GLA_KEY_WIDTH])
    b_all = _dot(ltri, hi) + _dot(ltri, lo)
    pump(PUMP_PLAN["cumsum"])
    next_decay()
    lhs_list, v_list, upd_list, dcol_list = [], [], [], []
    for c in range(n_chunks):
        rs = slice(c * cc, (c + 1) * cc)
        b = b_all[rs]
        b_end = b[cc - 1:cc, :]
        q_c = proj_ref[rs, C_Q:C_K] * (GLA_DK ** -0.5)
        k_c = proj_ref[rs, C_K:C_V]
        qs = (q_c * jnp.exp(b)).astype(BF16)
        kd = (k_c * jnp.exp(b_end - b)).astype(BF16)
        if robust:
            a_all = _decayed_scores(q_c, k_c, la_ref[rs, 0:GLA_KEY_WIDTH], b, masks, blk, pump)
        else:
            a_all = _naive_scores(q_c, k_c, b, causal, blk)
            pump(PUMP_PLAN["scores"][c])
        vc = proj_ref[rs, C_V:C_GG].astype(BF16)
        upds = []
        for p in range(GLA_HEADS // 2):
            upd = _dot_tn(kd[:, p * 2 * GLA_DK:(p + 1) * 2 * GLA_DK], vc[:, p * 2 * GLA_DV:(p + 1) * 2 * GLA_DV])
            upds.append(upd[0:GLA_DK, 0:GLA_DV])
            upds.append(upd[GLA_DK:2 * GLA_DK, GLA_DV:2 * GLA_DV])
        lhs_list.append([jnp.concatenate([a_all[:, p * 2 * cc:(p + 1) * 2 * cc],
                                          qs[:, p * 2 * GLA_DK:(p + 1) * 2 * GLA_DK]], axis=1)
                         for p in range(GLA_HEADS // 2)])
        v_list.append(vc)
        upd_list.append(jnp.concatenate(upds, axis=0))
        dcol_list.append(jnp.transpose(jnp.broadcast_to(jnp.exp(b_end), (GLA_DV, GLA_KEY_WIDTH))))

    o_chunks = []
    state = s_ref[...]
    for c in range(n_chunks):
        s_bf = state.astype(BF16)
        vc = v_list[c]
        outs = []
        for p in range(GLA_HEADS // 2):
            h0, h1 = 2 * p, 2 * p + 1
            rhs = jnp.concatenate([
                _pair_block_diag(vc[:, h0 * GLA_DV:(h0 + 1) * GLA_DV], vc[:, h1 * GLA_DV:(h1 + 1) * GLA_DV]),
                _pair_block_diag(s_bf[h0 * GLA_DK:(h0 + 1) * GLA_DK], s_bf[h1 * GLA_DK:(h1 + 1) * GLA_DK]),
            ], axis=0)
            outs.append(_dot(lhs_list[c][p], rhs))
        o_chunks.append(jnp.concatenate(outs, axis=1))
        state = state * dcol_list[c] + upd_list[c]
        pump(PUMP_PLAN["state"][c])
    s_ref[...] = state
    o = jnp.concatenate(o_chunks, axis=0)
    pool_act = pooling()

    y_ref[...] = _epilogue(xres_ref[...], pool_act, o, proj_ref[:, C_GG:C_GL], ng_ref, wout_ref, lng_ref, lnb_ref,
                           pump, tail, row_blocks=4)


def _prompt_body(x_ref, xres_ref, win_ref, wmap_ref, pscale_ref, ba_ref, ng_ref, wout_ref, lng_ref, lnb_ref,
                 y_ref, pool_ref, s_out_ref,
                 proj0_ref, proj1_ref, la0_ref, la1_ref, xb_ref, ext_ref, s_ref, steep_ref,
                 *, tc, sub, nj, ntiles):
    i = pl.program_id(0)
    jb = (i + nj - 1) % nj

    @pl.when(i == 0)
    def _():
        steep_ref[0] = 0
        pool_ref[...] = jnp.zeros_like(pool_ref)

    @pl.when(jb == 0)
    def _():
        ext_ref[0:HIST_ROWS, 0:POOL_WIDTH] = jnp.zeros((HIST_ROWS, POOL_WIDTH), F32)
        s_ref[...] = jnp.zeros_like(s_ref)

    units = [(r, lo) for lo in (C_GL, *range(0, C_GL, MXU_COLS)) for r in range(0, tc, PUMP_ROWS)]

    def step(set_a, set_b, robust):
        pump, next_decay = (lambda n=None: None), (lambda: None)
        if set_a is not None:
            proj_a, la_a = set_a
            xb_ref[:, 0:D_MODEL] = x_ref[0].astype(BF16)
            pump = _make_pump(units, xb_ref, win_ref, proj_a)

            def next_decay():
                la = _log_sigmoid(proj_a[:, C_GL:C_END] + ba_ref[...]) * (1.0 / GLA_TAU)
                la_a[:, 0:GLA_KEY_WIDTH] = la
                chunk_decay = [jnp.sum(la[c * GLA_CHUNK:(c + 1) * GLA_CHUNK], axis=0, keepdims=True)
                               for c in range(tc // GLA_CHUNK)]
                lowest = jnp.min(functools.reduce(jnp.minimum, chunk_decay))
                steep_ref[0] = jnp.where(lowest < -SAFE_LOG_DECAY, 1, 0).astype(jnp.int32)

            if set_b is None:
                pump()
                next_decay()
                return
            pump(tc // PUMP_ROWS)

        proj_b, la_b = set_b
        n_sub = tc // sub
        for k in range(n_sub):
            rows = slice(k * sub, (k + 1) * sub)
            last = k == n_sub - 1
            _prompt_stage_b((jb * n_sub + k) * sub, proj_b.at[rows, :], xres_ref.at[0, rows, :], la_b.at[rows, :],
                            wmap_ref, pscale_ref, ba_ref, ng_ref, wout_ref, lng_ref, lnb_ref,
                            y_ref.at[0, rows, :], ext_ref, s_ref, pump, next_decay if k == 0 else (lambda: None),
                            tc=sub, robust=robust, tail=None if last else SUBTILE_TAIL_UNITS)
            if not last:
                ext_ref[0:HIST_ROWS, 0:POOL_WIDTH] = ext_ref[sub:sub + HIST_ROWS, 0:POOL_WIDTH]

    sets = ((proj0_ref, la0_ref), (proj1_ref, la1_ref))
    steep = steep_ref[0] != 0

    @pl.when(i == 0)
    def _():
        step(sets[0], None, False)

    for robust in (False, True):
        for parity in (0, 1):
            @pl.when((i > 0) & (i < ntiles) & (i % 2 == parity) & (steep == robust))
            def _(parity=parity, robust=robust):
                step(sets[parity], sets[1 - parity], robust)

        @pl.when((i == ntiles) & (steep == robust))
        def _(robust=robust):
            step(None, sets[(ntiles - 1) % 2], robust)

    @pl.when((i > 0) & (jb == nj - 1))
    def _():
        mine = lax.broadcasted_iota(jnp.int32, pool_ref.shape[1:], 0) == (i - 1) // nj
        for r in range(POOL_BUF):
            row = ext_ref[sub + 1 + r:sub + 2 + r, 0:POOL_WIDTH]
            pool_ref[r] = jnp.where(mine, jnp.broadcast_to(row, pool_ref.shape[1:]), pool_ref[r])
        s_out_ref[0] = s_ref[...]

    @pl.when(i > 0)
    def _():
        ext_ref[0:HIST_ROWS, 0:POOL_WIDTH] = ext_ref[sub:sub + HIST_ROWS, 0:POOL_WIDTH]


def _prompt_call(x, win, wmap, pscale, ba, ng, wout, lng, lnb):
    nb, t, _ = x.shape
    tc = PROMPT_TILE
    nj = t // tc
    ntiles = nb * nj
    const2 = lambda i: (0, 0)

    def tile_a(i):
        ia = jnp.minimum(i, ntiles - 1)
        return (ia // nj, ia % nj, 0)

    def tile_b(i):
        ib = jnp.maximum(i - 1, 0)
        return (ib // nj, ib % nj, 0)

    def seq_b(i):
        return (jnp.maximum(i - 1, 0) // nj, 0, 0)

    return pl.pallas_call(
        functools.partial(_prompt_body, tc=tc, sub=PROMPT_SUBTILE, nj=nj, ntiles=ntiles),
        grid=(ntiles + 1,),
        in_specs=[
            pl.BlockSpec((1, tc, D_MODEL), tile_a),
            pl.BlockSpec((1, tc, D_MODEL), tile_b),
            pl.BlockSpec(win.shape, const2),
            pl.BlockSpec(wmap.shape, lambda i: (0, 0, 0)),
            pl.BlockSpec(pscale.shape, const2),
            pl.BlockSpec(ba.shape, const2),
            pl.BlockSpec(ng.shape, const2),
            pl.BlockSpec(wout.shape, const2),
            pl.BlockSpec(lng.shape, const2),
            pl.BlockSpec(lnb.shape, const2),
        ],
        out_specs=[
            pl.BlockSpec((1, tc, D_MODEL), tile_b),
            pl.BlockSpec((POOL_BUF, nb, POOL_WIDTH), lambda i: (0, 0, 0)),
            pl.BlockSpec((1, GLA_KEY_WIDTH, GLA_DV), seq_b),
        ],
        out_shape=[
            jax.ShapeDtypeStruct((nb, t, D_MODEL), F32),
            jax.ShapeDtypeStruct((POOL_BUF, nb, POOL_WIDTH), F32),
            jax.ShapeDtypeStruct((nb, GLA_KEY_WIDTH, GLA_DV), F32),
        ],
        scratch_shapes=[
            pltpu.VMEM((tc, C_END + PITCH_PAD), F32),
            pltpu.VMEM((tc, C_END + PITCH_PAD), F32),
            pltpu.VMEM((tc, GLA_KEY_WIDTH + PITCH_PAD), F32),
            pltpu.VMEM((tc, GLA_KEY_WIDTH + PITCH_PAD), F32),
            pltpu.VMEM((tc, D_MODEL + PITCH_PAD), BF16),
            pltpu.VMEM((HIST_ROWS + PROMPT_SUBTILE, POOL_WIDTH + PITCH_PAD), F32),
            pltpu.VMEM((GLA_KEY_WIDTH, GLA_DV), F32),
            pltpu.SMEM((1,), jnp.int32),
        ],
        compiler_params=pltpu.CompilerParams(
            dimension_semantics=("arbitrary",), vmem_limit_bytes=VMEM_LIMIT),
        name="prompt_layer",
    )(x, x, win, wmap, pscale, ba, ng, wout, lng, lnb)


def _sample_stage_b(proj_ref, xres_ref, pool_in_ref, s_in_ref, wmap_ref, pscale_ref, ba_ref, ng_ref, wout_ref,
                    lng_ref, lnb_ref, y_ref, pool_ref, s_out_ref, u_ref, pooled_ref, pump, *, bs, t, start_pos):
    rows = bs * t
    x = xres_ref[:, 0:D_MODEL]

    def proj(lo, hi):
        return proj_ref[:, lo:hi]

    u = proj(C_U, C_GP)
    mixed = []
    for g, w in enumerate(POOL_WINDOWS):
        sl = slice(g * POOL_GROUP_DIM, (g + 1) * POOL_GROUP_DIM)
        u_ref[g] = u[:, sl]
        ext = [pool_in_ref[r, :, sl] for r in range(POOL_BUF)]
        ext += [u_ref[g, pl.ds(tok, bs, stride=t), :] for tok in range(t)]
        for r in range(POOL_BUF):
            pool_ref[r, :, sl] = ext[t + r]
        sums = ext
        sh = 1
        while sh < w:
            first = 2 * sh - 1
            sums = [None] * first + [sums[r] + sums[r - sh] for r in range(first, len(sums))]
            sh *= 2
        for tok in range(t):
            cnt = float(min(start_pos + tok + 1, w))
            pooled_ref[g, pl.ds(tok, bs, stride=t), :] = sums[POOL_BUF + tok] / cnt - ext[POOL_BUF + tok]
        mixed.append(_dot(pooled_ref[g].astype(BF16), wmap_ref[g]))
        pump(g % 2)
    gp = proj(C_GP, C_Q)
    pool_act = jnp.concatenate(mixed, axis=1) * pscale_ref[...] * _silu(gp)

    q = proj(C_Q, C_K) * (GLA_DK ** -0.5)
    k = proj(C_K, C_V)
    vb = proj(C_V, C_GG).astype(BF16)
    gg = proj(C_GG, C_GL)
    la = _log_sigmoid(proj(C_GL, C_END) + ba_ref[...]) * (1.0 / GLA_TAU)

    rr = lax.broadcasted_iota(jnp.int32, (rows, rows), 0)
    cl = lax.broadcasted_iota(jnp.int32, (rows, rows), 1)
    same = (rr // t) == (cl // t)
    ltri = jnp.where(same & (cl <= rr), 1.0, 0.0).astype(BF16)
    lall = jnp.where(same, 1.0, 0.0).astype(BF16)
    hi, lo = _split_bf16(la)
    b = _dot(ltri, hi) + _dot(ltri, lo)
    b_end = _dot(lall, hi) + _dot(lall, lo)
    pump(1)
    qs = (q * jnp.exp(b)).astype(BF16)
    kd = (k * jnp.exp(b_end - b)).astype(BF16)

    assert t & (t - 1) == 0, "tokens per sequence must be a power of two"
    blk = _head_block_mask(rows, GLA_DK, (GLA_HEADS * rows, GLA_KEY_WIDTH))
    a_all = _decayed_scores(q, k, la, b, _score_masks(rows, t.bit_length() - 1), blk)
    pump(2)

    s_old = s_in_ref[...]
    s_bf = s_old.astype(BF16)
    qs3 = qs.reshape(bs, t, GLA_KEY_WIDTH)
    kd3 = kd.reshape(bs, t, GLA_KEY_WIDTH)
    v3 = vb.reshape(bs, t, GLA_WIDTH)
    bdims_nn = (((2,), (1,)), ((0,), (0,)))
    bdims_tn = (((1,), (1,)), ((0,), (0,)))
    assert bs <= GLA_DV
    seq_sel = jnp.where(lax.broadcasted_iota(jnp.int32, (rows, GLA_DV), 0) // t
                        == lax.broadcasted_iota(jnp.int32, (rows, GLA_DV), 1), 1.0, 0.0).astype(BF16)
    decay_t = jnp.exp(_dot_tn(hi, seq_sel) + _dot_tn(lo, seq_sel))
    decay = jnp.stack([jnp.broadcast_to(decay_t[:, j:j + 1], (GLA_KEY_WIDTH, GLA_DV)) for j in range(bs)], axis=0)
    outs, upds = [], []
    for p in range(GLA_HEADS // 2):
        h0, h1 = 2 * p, 2 * p + 1
        intra = _dot(a_all[:, p * 2 * rows:(p + 1) * 2 * rows],
                     _pair_block_diag(vb[:, h0 * GLA_DV:(h0 + 1) * GLA_DV], vb[:, h1 * GLA_DV:(h1 + 1) * GLA_DV]))
        s0 = s_bf[:, h0 * GLA_DK:(h0 + 1) * GLA_DK, :]
        s1 = s_bf[:, h1 * GLA_DK:(h1 + 1) * GLA_DK, :]
        z = jnp.zeros_like(s0)
        s_pair = jnp.concatenate(
            [jnp.concatenate([s0, z], axis=2), jnp.concatenate([z, s1], axis=2)], axis=1)
        inter = lax.dot_general(qs3[:, :, p * 2 * GLA_DK:(p + 1) * 2 * GLA_DK], s_pair, bdims_nn,
                                preferred_element_type=F32)
        outs.append(intra + inter.reshape(rows, 2 * GLA_DV))
        upd = lax.dot_general(kd3[:, :, p * 2 * GLA_DK:(p + 1) * 2 * GLA_DK],
                              v3[:, :, p * 2 * GLA_DV:(p + 1) * 2 * GLA_DV], bdims_tn,
                              preferred_element_type=F32)
        upds.append(upd[:, 0:GLA_DK, 0:GLA_DV])
        upds.append(upd[:, GLA_DK:2 * GLA_DK, GLA_DV:2 * GLA_DV])
        pump(2)
    o = jnp.concatenate(outs, axis=1)
    s_out_ref[...] = s_old * decay + jnp.concatenate(upds, axis=1)

    y = _epilogue(x, pool_act, o, gg, ng_ref, wout_ref, lng_ref, lnb_ref, pump)
    y_ref[...] = y.reshape(bs, t, D_MODEL)


def _sample_body(x_ref, pool_in_ref, s_in_ref, win_ref, wmap_ref, pscale_ref, ba_ref, ng_ref, wout_ref,
                 lng_ref, lnb_ref, y_ref, pool_ref, s_out_ref,
                 proj0_ref, proj1_ref, xkeep0_ref, xkeep1_ref, xb_ref, u_ref, pooled_ref,
                 *, bs, t, start_pos, n_groups):
    i = pl.program_id(0)
    rows = bs * t
    sets = ((proj0_ref, xkeep0_ref), (proj1_ref, xkeep1_ref))
    units = [(0, lo) for lo in range(0, C_END, MXU_COLS)]

    def step(set_a, set_b):
        pump = lambda n=None: None
        if set_a is not None:
            proj_a, xkeep_a = set_a
            x = x_ref[...].reshape(rows, D_MODEL)
            xkeep_a[:, 0:D_MODEL] = x
            xb_ref[:, 0:D_MODEL] = x.astype(BF16)
            pump = _make_pump(units, xb_ref, win_ref, proj_a, rows)
            pump(1 if set_b is not None else None)
        if set_b is not None:
            proj_b, xkeep_b = set_b
            _sample_stage_b(proj_b, xkeep_b, pool_in_ref, s_in_ref, wmap_ref, pscale_ref, ba_ref, ng_ref,
                            wout_ref, lng_ref, lnb_ref, y_ref, pool_ref, s_out_ref, u_ref, pooled_ref, pump,
                            bs=bs, t=t, start_pos=start_pos)

    @pl.when(i == 0)
    def _():
        step(sets[0], None)

    for parity in (0, 1):
        @pl.when((i > 0) & (i < n_groups) & (i % 2 == parity))
        def _(parity=parity):
            step(sets[parity], sets[1 - parity])

    @pl.when(i == n_groups)
    def _():
        step(None, sets[(n_groups - 1) % 2])


def _sample_call(x, pool_in, s_in, win, wmap, pscale, ba, ng, wout, lng, lnb, start_pos):
    nb, t, _ = x.shape
    bs = SAMPLE_SEQS
    n_groups = nb // bs
    const2 = lambda i: (0, 0)
    seq3_a = lambda i: (jnp.minimum(i, n_groups - 1), 0, 0)
    seq3 = lambda i: (jnp.maximum(i - 1, 0), 0, 0)
    hist3 = lambda i: (0, jnp.maximum(i - 1, 0), 0)
    return pl.pallas_call(
        functools.partial(_sample_body, bs=bs, t=t, start_pos=start_pos, n_groups=n_groups),
        grid=(n_groups + 1,),
        in_specs=[
            pl.BlockSpec((bs, t, D_MODEL), seq3_a),
            pl.BlockSpec((POOL_BUF, bs, POOL_WIDTH), hist3),
            pl.BlockSpec((bs, GLA_KEY_WIDTH, GLA_DV), seq3),
            pl.BlockSpec(win.shape, const2),
            pl.BlockSpec(wmap.shape, lambda i: (0, 0, 0)),
            pl.BlockSpec(pscale.shape, const2),
            pl.BlockSpec(ba.shape, const2),
            pl.BlockSpec(ng.shape, const2),
            pl.BlockSpec(wout.shape, const2),
            pl.BlockSpec(lng.shape, const2),
            pl.BlockSpec(lnb.shape, const2),
        ],
        out_specs=[
            pl.BlockSpec((bs, t, D_MODEL), seq3),
            pl.BlockSpec((POOL_BUF, bs, POOL_WIDTH), hist3),
            pl.BlockSpec((bs, GLA_KEY_WIDTH, GLA_DV), seq3),
        ],
        out_shape=[
            jax.ShapeDtypeStruct((nb, t, D_MODEL), F32),
            jax.ShapeDtypeStruct((POOL_BUF, nb, POOL_WIDTH), F32),
            jax.ShapeDtypeStruct((nb, GLA_KEY_WIDTH, GLA_DV), F32),
        ],
        scratch_shapes=[pltpu.VMEM((bs * t, C_END + PITCH_PAD), F32),
                        pltpu.VMEM((bs * t, C_END + PITCH_PAD), F32),
                        pltpu.VMEM((bs * t, D_MODEL + PITCH_PAD), F32),
                        pltpu.VMEM((bs * t, D_MODEL + PITCH_PAD), F32),
                        pltpu.VMEM((bs * t, D_MODEL + PITCH_PAD), BF16),
                        pltpu.VMEM((len(POOL_WINDOWS), bs * t, POOL_GROUP_DIM), F32),
                        pltpu.VMEM((len(POOL_WINDOWS), bs * t, POOL_GROUP_DIM), F32)],
        compiler_params=pltpu.CompilerParams(
            dimension_semantics=("arbitrary",), vmem_limit_bytes=VMEM_LIMIT),
        name="sample_layer",
    )(x, pool_in, s_in, win, wmap, pscale, ba, ng, wout, lng, lnb)


def _prep_body(wt_ref, wfat_ref, wa2_ref, wout_f32_ref, wmap_f32_ref, win_ref, wout_ref, wmap_ref, *, n_plain,
               n_wout):
    i = pl.program_id(0)

    @pl.when(i == 0)
    def _():
        wmap_ref[...] = wmap_f32_ref[...].astype(BF16)

    @pl.when(i < n_wout)
    def _():
        wout_ref[:, 0:D_MODEL] = wout_f32_ref[...].astype(BF16)
        wout_ref[:, D_MODEL:] = jnp.zeros((wout_ref.shape[0], PITCH_PAD), BF16)

    @pl.when(i < n_plain)
    def _():
        win_ref[...] = jnp.transpose(wt_ref[...]).astype(BF16)

    @pl.when(i == n_plain)
    def _():
        fa_hi, fa_lo = _split_bf16(wfat_ref[...])
        a2_hi, a2_lo = _split_bf16(wa2_ref[...])
        acc = _dot_tn(fa_hi, a2_hi) + (_dot_tn(fa_hi, a2_lo) + _dot_tn(fa_lo, a2_hi))
        win_ref[:, 0:GLA_KEY_WIDTH] = acc.astype(BF16)
        win_ref[:, GLA_KEY_WIDTH:] = jnp.zeros((D_MODEL, PREP_COLS - GLA_KEY_WIDTH), BF16)


def _prep_weights(wt, wa2, wout_f32, wmap_f32):
    assert C_GL % PREP_COLS == 0 and C_END + PITCH_PAD - C_GL <= PREP_COLS and C_GL % GLA_RANK == 0
    n_plain = C_GL // PREP_COLS
    n_wout = 4
    wout_rows = wout_f32.shape[0] // n_wout
    assert n_wout <= n_plain + 1
    wout_block = lambda i: (jnp.minimum(i, n_wout - 1), 0)
    return pl.pallas_call(
        functools.partial(_prep_body, n_plain=n_plain, n_wout=n_wout),
        grid=(n_plain + 1,),
        in_specs=[
            pl.BlockSpec((PREP_COLS, D_MODEL), lambda i: (jnp.minimum(i, n_plain - 1), 0)),
            pl.BlockSpec((GLA_RANK, D_MODEL), lambda i: (C_GL // GLA_RANK, 0)),
            pl.BlockSpec(wa2.shape, lambda i: (0, 0)),
            pl.BlockSpec((wout_rows, D_MODEL), wout_block),
            pl.BlockSpec(wmap_f32.shape, lambda i: (0, 0, 0)),
        ],
        out_specs=[
            pl.BlockSpec((D_MODEL, PREP_COLS), lambda i: (0, i)),
            pl.BlockSpec((wout_rows, D_MODEL + PITCH_PAD), wout_block),
            pl.BlockSpec(wmap_f32.shape, lambda i: (0, 0, 0)),
        ],
        out_shape=[
            jax.ShapeDtypeStruct((D_MODEL, C_END + PITCH_PAD), BF16),
            jax.ShapeDtypeStruct((D_MODEL, D_MODEL + PITCH_PAD), BF16),
            jax.ShapeDtypeStruct(wmap_f32.shape, BF16),
        ],
        compiler_params=pltpu.CompilerParams(dimension_semantics=("arbitrary",), vmem_limit_bytes=VMEM_LIMIT),
        name="prep_weights",
    )(wt, wt, wa2, wout_f32, wmap_f32)


def kernel(x_prompt, x_sample, state_pool, state_gla, w_in, w_pool_map, pool_scale, w_gla_a2, b_gla_a,
           gla_norm_g, w_out, ln_g, ln_b):
    assert w_in.shape[0] == DEPTH
    win, wout, wmap = _prep_weights(jnp.transpose(w_in[0]), w_gla_a2[0], w_out[0], w_pool_map[0])
    pscale = pool_scale[0].reshape(1, POOL_WIDTH)
    ba = b_gla_a[0].reshape(1, GLA_KEY_WIDTH)
    ng = gla_norm_g[0].reshape(1, GLA_WIDTH)
    lng = ln_g[0].reshape(1, D_MODEL)
    lnb = ln_b[0].reshape(1, D_MODEL)
    params = (win, wmap, pscale, ba, ng, wout, lng, lnb)

    y_p, pool_p, gla_p = _prompt_call(x_prompt, *params)
    nsamp = x_sample.shape[0]
    y_s, pool_s, gla_s = _sample_call(
        x_sample, jnp.transpose(state_pool[0], (1, 0, 2)),
        state_gla[0].reshape(nsamp, GLA_KEY_WIDTH, GLA_DV), *params, start_pos=PAST_LEN)

    nb = x_prompt.shape[0]
    return (y_p, y_s, jnp.transpose(pool_p, (1, 0, 2))[None],
            gla_p.reshape(1, nb, GLA_HEADS, GLA_DK, GLA_DV),
            jnp.transpose(pool_s, (1, 0, 2))[None],
            gla_s.reshape(1, nsamp, GLA_HEADS, GLA_DK, GLA_DV))
```

```python
import functools
import itertools

import jax
import jax.numpy as jnp
from jax import lax
from jax.experimental import pallas as pl
from jax.experimental.pallas import tpu as pltpu

F32 = jnp.float32
BF16 = jnp.bfloat16

D_MODEL = 1024
POOL_WIDTH = 512
POOL_WINDOWS = (2, 4, 8, 16)
POOL_GROUP_DIM = 128
POOL_BUF = 15
HIST_ROWS = 16
GLA_HEADS = 4
GLA_DK = 64
GLA_DV = 128
GLA_KEY_WIDTH = GLA_HEADS * GLA_DK
GLA_WIDTH = GLA_HEADS * GLA_DV
GLA_RANK = 16
GLA_TAU = 16.0
LN_EPS = 1e-5
RMS_EPS = 1e-6
DEPTH = 1
DEEPNORM_ALPHA = (2.0 * DEPTH) ** 0.25
PAST_LEN = 16384

C_U, C_GP, C_Q, C_K, C_V, C_GG, C_GL, C_END = 0, 512, 1024, 1280, 1536, 2048, 2560, 2816

MXU_COLS = 256
PROMPT_TILE = 256
PITCH_PAD = 128
PROMPT_SUBTILE = 256
SUBTILE_TAIL_UNITS = 0
PUMP_ROWS = 256
PUMP_PLAN = {"pool": (0, 1, 0, 1), "cumsum": 0, "scores": (1, 1, 1, 1), "level": (0, 0, 0, 1, 0, 0),
             "state": (0, 0, 0, 0), "epilogue": 1, "layernorm": (1, 1, 1, 0)}
SAFE_LOG_DECAY = 30.0
GLA_CHUNK = 64
SAMPLE_SEQS = 32
PREP_COLS = 512
VMEM_LIMIT = 56 * 1024 * 1024


def _dot(a, b):
    return jnp.dot(a, b, preferred_element_type=F32)


def _dot_nt(a, b):
    return lax.dot_general(a, b, (((1,), (1,)), ((), ())), preferred_element_type=F32)


def _dot_tn(a, b):
    return lax.dot_general(a, b, (((0,), (0,)), ((), ())), preferred_element_type=F32)


def _silu(x):
    h = 0.5 * x
    return h + h * jnp.tanh(h)


def _log_sigmoid(x):
    return jnp.minimum(x, 0.0) - jnp.log(1.0 + jnp.exp(-jnp.abs(x)))


def _split_bf16(x):
    hi = x.astype(BF16)
    lo = (x - hi.astype(F32)).astype(BF16)
    return hi, lo


def _epilogue(x, pool_act, o, gg, ng_ref, wout_ref, lng_ref, lnb_ref, pump=lambda n=None: None, tail=None,
              row_blocks=1):
    pump(PUMP_PLAN["epilogue"])
    normed = []
    for h in range(GLA_HEADS):
        oh = o[:, h * GLA_DV:(h + 1) * GLA_DV]
        ms = jnp.mean(oh * oh, axis=-1, keepdims=True)
        normed.append(oh * lax.rsqrt(ms + RMS_EPS))
    gla_act = jnp.concatenate(normed, axis=1) * ng_ref[...] * _silu(gg)
    mix = jnp.concatenate([pool_act, gla_act], axis=1).astype(BF16)
    y = _dot(mix, wout_ref[:, 0:D_MODEL])
    n = x.shape[0] // row_blocks
    outs = []
    for r in range(row_blocks):
        rows = slice(r * n, (r + 1) * n)
        pump(PUMP_PLAN["layernorm"][r])
        h = DEEPNORM_ALPHA * x[rows] + y[rows]
        mu = jnp.mean(h, axis=-1, keepdims=True)
        d = h - mu
        var = jnp.mean(d * d, axis=-1, keepdims=True)
        outs.append(d * lax.rsqrt(var + LN_EPS) * lng_ref[...] + lnb_ref[...])
    pump(tail)
    return outs[0] if row_blocks == 1 else jnp.concatenate(outs, axis=0)


def _head_block_mask(rows_per_head, cols_per_head, shape):
    r = lax.broadcasted_iota(jnp.int32, shape, 0) // rows_per_head
    c = lax.broadcasted_iota(jnp.int32, shape, 1) // cols_per_head
    return r == c


def _score_masks(rows, n_levels):
    shape = (rows, GLA_HEADS * rows)
    t_idx = lax.broadcasted_iota(jnp.int32, shape, 0)
    s_idx = lax.broadcasted_iota(jnp.int32, shape, 1) % rows
    masks = [t_idx == s_idx]
    for lvl in range(n_levels):
        half = 1 << lvl
        blk = 2 * half
        masks.append((t_idx // blk == s_idx // blk) & (t_idx % blk >= half) & (s_idx % blk < half))
    return masks


def _decayed_scores(q, k, la, b, masks, head_blk, pump=lambda n=None: None):
    rows = q.shape[0]
    r_idx = lax.broadcasted_iota(jnp.int32, q.shape, 0)

    def scores(qt, kt):
        kh = jnp.where(head_blk, jnp.concatenate([kt.astype(BF16)] * GLA_HEADS, axis=0), jnp.zeros((), BF16))
        return _dot_nt(qt.astype(BF16), kh)

    acc = jnp.where(masks[0], scores(q, k), 0.0)
    for lvl in range(len(masks) - 1):
        half = 1 << lvl
        blk = 2 * half
        if lvl == 0:
            neg = jnp.where(r_idx % 2 == 1, la, 0.0)
        elif lvl == 1:
            r4 = r_idx % 4
            nxt = pltpu.roll(la, rows - 1, 0)
            prv = pltpu.roll(la, 1, 0)
            neg = jnp.where(r4 == 0, nxt, jnp.where(r4 == 1, 0.0, jnp.where(r4 == 2, la, la + prv)))
        else:
            ref = jnp.concatenate(
                [jnp.broadcast_to(b[j * blk + half - 1:j * blk + half, :], (blk, b.shape[1]))
                 for j in range(rows // blk)], axis=0)
            d = b - ref
            neg = jnp.minimum(d, -d)
        w = jnp.exp(neg)
        acc = jnp.where(masks[lvl + 1], scores(q * w, k * w), acc)
        pump(PUMP_PLAN["level"][lvl])
    return acc.astype(BF16)


def _pair_block_diag(a0, a1):
    z = jnp.zeros_like(a0)
    return jnp.concatenate(
        [jnp.concatenate([a0, z], axis=1), jnp.concatenate([z, a1], axis=1)], axis=0)


def _make_pump(units, xb_ref, win_ref, proj_ref, unit_rows=PUMP_ROWS):
    units = iter(units)

    def pump(n=None):
        for r, lo in (units if n is None else itertools.islice(units, n)):
            proj_ref[r:r + unit_rows, lo:lo + MXU_COLS] = _dot(
                xb_ref[r:r + unit_rows, 0:D_MODEL], win_ref[:, lo:lo + MXU_COLS])

    return pump


def _naive_scores(q, k, b, causal, head_blk):
    ks = (k * jnp.exp(-b)).astype(BF16)
    kh = jnp.where(head_blk, jnp.concatenate([ks] * GLA_HEADS, axis=0), jnp.zeros((), BF16))
    return jnp.where(causal, _dot_nt((q * jnp.exp(b)).astype(BF16), kh), 0.0).astype(BF16)


def _prompt_stage_b(pos0, proj_ref, xres_ref, la_ref, wmap_ref, pscale_ref, ba_ref, ng_ref, wout_ref, lng_ref,
                    lnb_ref, y_ref, ext_ref, s_ref, pump, next_decay, *, tc, robust, tail):
    def pooling():
        u = proj_ref[:, C_U:C_GP]
        ext_ref[HIST_ROWS:HIST_ROWS + tc, 0:POOL_WIDTH] = u
        pos16 = pos0 + lax.broadcasted_iota(jnp.int32, (HIST_ROWS, POOL_GROUP_DIM), 0)
        mixed = []
        for g, w in enumerate(POOL_WINDOWS):
            sl = slice(g * POOL_GROUP_DIM, (g + 1) * POOL_GROUP_DIM)
            e = ext_ref[:, sl]
            s = e
            sh = 1
            while sh < w:
                s = s + pltpu.roll(s, sh, 0)
                sh *= 2
            wsum = s[HIST_ROWS:]
            self_rows = e[HIST_ROWS:]
            cnt = jnp.minimum(pos16 + 1, w).astype(F32)
            first = wsum[:HIST_ROWS] / cnt - self_rows[:HIST_ROWS]
            rest = wsum[HIST_ROWS:] * (1.0 / w) - self_rows[HIST_ROWS:]
            pooled = jnp.concatenate([first, rest], axis=0)
            mixed.append(_dot(pooled.astype(BF16), wmap_ref[g]))
            pump(PUMP_PLAN["pool"][g])
        return jnp.concatenate(mixed, axis=1) * pscale_ref[...] * _silu(proj_ref[:, C_GP:C_Q])

    cc = GLA_CHUNK
    n_chunks = tc // cc
    rr = lax.broadcasted_iota(jnp.int32, (tc, tc), 0)
    cl = lax.broadcasted_iota(jnp.int32, (tc, tc), 1)
    ltri = jnp.where((rr // cc == cl // cc) & (cl <= rr), 1.0, 0.0).astype(BF16)
    masks = _score_masks(cc, cc.bit_length() - 1 if robust else 0)
    causal = (lax.broadcasted_iota(jnp.int32, (cc, GLA_HEADS * cc), 1) % cc
              <= lax.broadcasted_iota(jnp.int32, (cc, GLA_HEADS * cc), 0))
    blk = _head_block_mask(cc, GLA_DK, (GLA_HEADS * cc, GLA_KEY_WIDTH))

    hi, lo = _split_bf16(la_ref[:, 0:GLA_KEY_WIDTH])
    b_all = _dot(ltri, hi) + _dot(ltri, lo)
    pump(PUMP_PLAN["cumsum"])
    next_decay()
    lhs_list, v_list, upd_list, dcol_list = [], [], [], []
    for c in range(n_chunks):
        rs = slice(c * cc, (c + 1) * cc)
        b = b_all[rs]
        b_end = b[cc - 1:cc, :]
        q_c = proj_ref[rs, C_Q:C_K] * (GLA_DK ** -0.5)
        k_c = proj_ref[rs, C_K:C_V]
        qs = (q_c * jnp.exp(b)).astype(BF16)
        kd = (k_c * jnp.exp(b_end - b)).astype(BF16)
        if robust:
            a_all = _decayed_scores(q_c, k_c, la_ref[rs, 0:GLA_KEY_WIDTH], b, masks, blk, pump)
        else:
            a_all = _naive_scores(q_c, k_c, b, causal, blk)
            pump(PUMP_PLAN["scores"][c])
        vc = proj_ref[rs, C_V:C_GG].astype(BF16)
        upds = []
        for p in range(GLA_HEADS // 2):
            upd = _dot_tn(kd[:, p * 2 * GLA_DK:(p + 1) * 2 * GLA_DK], vc[:, p * 2 * GLA_DV:(p + 1) * 2 * GLA_DV])
            upds.append(upd[0:GLA_DK, 0:GLA_DV])
            upds.append(upd[GLA_DK:2 * GLA_DK, GLA_DV:2 * GLA_DV])
        lhs_list.append([jnp.concatenate([a_all[:, p * 2 * cc:(p + 1) * 2 * cc],
                                          qs[:, p * 2 * GLA_DK:(p + 1) * 2 * GLA_DK]], axis=1)
                         for p in range(GLA_HEADS // 2)])
        v_list.append(vc)
        upd_list.append(jnp.concatenate(upds, axis=0))
        dcol_list.append(jnp.transpose(jnp.broadcast_to(jnp.exp(b_end), (GLA_DV, GLA_KEY_WIDTH))))

    o_chunks = []
    state = s_ref[...]
    for c in range(n_chunks):
        s_bf = state.astype(BF16)
        vc = v_list[c]
        outs = []
        for p in range(GLA_HEADS // 2):
            h0, h1 = 2 * p, 2 * p + 1
            rhs = jnp.concatenate([
                _pair_block_diag(vc[:, h0 * GLA_DV:(h0 + 1) * GLA_DV], vc[:, h1 * GLA_DV:(h1 + 1) * GLA_DV]),
                _pair_block_diag(s_bf[h0 * GLA_DK:(h0 + 1) * GLA_DK], s_bf[h1 * GLA_DK:(h1 + 1) * GLA_DK]),
            ], axis=0)
            outs.append(_dot(lhs_list[c][p], rhs))
        o_chunks.append(jnp.concatenate(outs, axis=1))
        state = state * dcol_list[c] + upd_list[c]
        pump(PUMP_PLAN["state"][c])
    s_ref[...] = state
    o = jnp.concatenate(o_chunks, axis=0)
    pool_act = pooling()

    y_ref[...] = _epilogue(xres_ref[...], pool_act, o, proj_ref[:, C_GG:C_GL], ng_ref, wout_ref, lng_ref, lnb_ref,
                           pump, tail, row_blocks=4)


def _prompt_body(x_ref, xres_ref, win_ref, wmap_ref, pscale_ref, ba_ref, ng_ref, wout_ref, lng_ref, lnb_ref,
                 y_ref, pool_ref, s_out_ref,
                 proj0_ref, proj1_ref, la0_ref, la1_ref, xb_ref, ext_ref, s_ref, steep_ref,
                 *, tc, sub, nj, ntiles):
    i = pl.program_id(0)
    jb = (i + nj - 1) % nj

    @pl.when(i == 0)
    def _():
        steep_ref[0] = 0
        pool_ref[...] = jnp.zeros_like(pool_ref)

    @pl.when(jb == 0)
    def _():
        ext_ref[0:HIST_ROWS, 0:POOL_WIDTH] = jnp.zeros((HIST_ROWS, POOL_WIDTH), F32)
        s_ref[...] = jnp.zeros_like(s_ref)

    units = [(r, lo) for lo in (C_GL, *range(0, C_GL, MXU_COLS)) for r in range(0, tc, PUMP_ROWS)]

    def step(set_a, set_b, robust):
        pump, next_decay = (lambda n=None: None), (lambda: None)
        if set_a is not None:
            proj_a, la_a = set_a
            xb_ref[:, 0:D_MODEL] = x_ref[0].astype(BF16)
            pump = _make_pump(units, xb_ref, win_ref, proj_a)

            def next_decay():
                la = _log_sigmoid(proj_a[:, C_GL:C_END] + ba_ref[...]) * (1.0 / GLA_TAU)
                la_a[:, 0:GLA_KEY_WIDTH] = la
                chunk_decay = [jnp.sum(la[c * GLA_CHUNK:(c + 1) * GLA_CHUNK], axis=0, keepdims=True)
                               for c in range(tc // GLA_CHUNK)]
                lowest = jnp.min(functools.reduce(jnp.minimum, chunk_decay))
                steep_ref[0] = jnp.where(lowest < -SAFE_LOG_DECAY, 1, 0).astype(jnp.int32)

            if set_b is None:
                pump()
                next_decay()
                return
            pump(tc // PUMP_ROWS)

        proj_b, la_b = set_b
        n_sub = tc // sub
        for k in range(n_sub):
            rows = slice(k * sub, (k + 1) * sub)
            last = k == n_sub - 1
            _prompt_stage_b((jb * n_sub + k) * sub, proj_b.at[rows, :], xres_ref.at[0, rows, :], la_b.at[rows, :],
                            wmap_ref, pscale_ref, ba_ref, ng_ref, wout_ref, lng_ref, lnb_ref,
                            y_ref.at[0, rows, :], ext_ref, s_ref, pump, next_decay if k == 0 else (lambda: None),
                            tc=sub, robust=robust, tail=None if last else SUBTILE_TAIL_UNITS)
            if not last:
                ext_ref[0:HIST_ROWS, 0:POOL_WIDTH] = ext_ref[sub:sub + HIST_ROWS, 0:POOL_WIDTH]

    sets = ((proj0_ref, la0_ref), (proj1_ref, la1_ref))
    steep = steep_ref[0] != 0

    @pl.when(i == 0)
    def _():
        step(sets[0], None, False)

    for robust in (False, True):
        for parity in (0, 1):
            @pl.when((i > 0) & (i < ntiles) & (i % 2 == parity) & (steep == robust))
            def _(parity=parity, robust=robust):
                step(sets[parity], sets[1 - parity], robust)

        @pl.when((i == ntiles) & (steep == robust))
        def _(robust=robust):
            step(None, sets[(ntiles - 1) % 2], robust)

    @pl.when((i > 0) & (jb == nj - 1))
    def _():
        mine = lax.broadcasted_iota(jnp.int32, pool_ref.shape[1:], 0) == (i - 1) // nj
        for r in range(POOL_BUF):
            row = ext_ref[sub + 1 + r:sub + 2 + r, 0:POOL_WIDTH]
            pool_ref[r] = jnp.where(mine, jnp.broadcast_to(row, pool_ref.shape[1:]), pool_ref[r])
        s_out_ref[0] = s_ref[...]

    @pl.when(i > 0)
    def _():
        ext_ref[0:HIST_ROWS, 0:POOL_WIDTH] = ext_ref[sub:sub + HIST_ROWS, 0:POOL_WIDTH]


def _prompt_call(x, win, wmap, pscale, ba, ng, wout, lng, lnb):
    nb, t, _ = x.shape
    tc = PROMPT_TILE
    nj = t // tc
    ntiles = nb * nj
    const2 = lambda i: (0, 0)

    def tile_a(i):
        ia = jnp.minimum(i, ntiles - 1)
        return (ia // nj, ia % nj, 0)

    def tile_b(i):
        ib = jnp.maximum(i - 1, 0)
        return (ib // nj, ib % nj, 0)

    def seq_b(i):
        return (jnp.maximum(i - 1, 0) // nj, 0, 0)

    return pl.pallas_call(
        functools.partial(_prompt_body, tc=tc, sub=PROMPT_SUBTILE, nj=nj, ntiles=ntiles),
        grid=(ntiles + 1,),
        in_specs=[
            pl.BlockSpec((1, tc, D_MODEL), tile_a),
            pl.BlockSpec((1, tc, D_MODEL), tile_b),
            pl.BlockSpec(win.shape, const2),
            pl.BlockSpec(wmap.shape, lambda i: (0, 0, 0)),
            pl.BlockSpec(pscale.shape, const2),
            pl.BlockSpec(ba.shape, const2),
            pl.BlockSpec(ng.shape, const2),
            pl.BlockSpec(wout.shape, const2),
            pl.BlockSpec(lng.shape, const2),
            pl.BlockSpec(lnb.shape, const2),
        ],
        out_specs=[
            pl.BlockSpec((1, tc, D_MODEL), tile_b),
            pl.BlockSpec((POOL_BUF, nb, POOL_WIDTH), lambda i: (0, 0, 0)),
            pl.BlockSpec((1, GLA_KEY_WIDTH, GLA_DV), seq_b),
        ],
        out_shape=[
            jax.ShapeDtypeStruct((nb, t, D_MODEL), F32),
            jax.ShapeDtypeStruct((POOL_BUF, nb, POOL_WIDTH), F32),
            jax.ShapeDtypeStruct((nb, GLA_KEY_WIDTH, GLA_DV), F32),
        ],
        scratch_shapes=[
            pltpu.VMEM((tc, C_END + PITCH_PAD), F32),
            pltpu.VMEM((tc, C_END + PITCH_PAD), F32),
            pltpu.VMEM((tc, GLA_KEY_WIDTH + PITCH_PAD), F32),
            pltpu.VMEM((tc, GLA_KEY_WIDTH + PITCH_PAD), F32),
            pltpu.VMEM((tc, D_MODEL + PITCH_PAD), BF16),
            pltpu.VMEM((HIST_ROWS + PROMPT_SUBTILE, POOL_WIDTH + PITCH_PAD), F32),
            pltpu.VMEM((GLA_KEY_WIDTH, GLA_DV), F32),
            pltpu.SMEM((1,), jnp.int32),
        ],
        compiler_params=pltpu.CompilerParams(
            dimension_semantics=("arbitrary",), vmem_limit_bytes=VMEM_LIMIT),
        name="prompt_layer",
    )(x, x, win, wmap, pscale, ba, ng, wout, lng, lnb)


def _sample_stage_b(proj_ref, xres_ref, pool_in_ref, s_in_ref, wmap_ref, pscale_ref, ba_ref, ng_ref, wout_ref,
                    lng_ref, lnb_ref, y_ref, pool_ref, s_out_ref, u_ref, pooled_ref, pump, *, bs, t, start_pos):
    rows = bs * t
    x = xres_ref[...].reshape(rows, D_MODEL)

    def proj(lo, hi):
        return proj_ref[:, lo:hi]

    u = proj(C_U, C_GP)
    mixed = []
    for g, w in enumerate(POOL_WINDOWS):
        sl = slice(g * POOL_GROUP_DIM, (g + 1) * POOL_GROUP_DIM)
        u_ref[g] = u[:, sl]
        ext = [pool_in_ref[r, :, sl] for r in range(POOL_BUF)]
        ext += [u_ref[g, pl.ds(tok, bs, stride=t), :] for tok in range(t)]
        for r in range(POOL_BUF):
            pool_ref[r, :, sl] = ext[t + r]
        sums = ext
        sh = 1
        while sh < w:
            first = 2 * sh - 1
            sums = [None] * first + [sums[r] + sums[r - sh] for r in range(first, len(sums))]
            sh *= 2
        for tok in range(t):
            cnt = float(min(start_pos + tok + 1, w))
            pooled_ref[g, pl.ds(tok, bs, stride=t), :] = sums[POOL_BUF + tok] / cnt - ext[POOL_BUF + tok]
        mixed.append(_dot(pooled_ref[g].astype(BF16), wmap_ref[g]))
        pump(g % 2)
    gp = proj(C_GP, C_Q)
    pool_act = jnp.concatenate(mixed, axis=1) * pscale_ref[...] * _silu(gp)

    q = proj(C_Q, C_K) * (GLA_DK ** -0.5)
    k = proj(C_K, C_V)
    vb = proj(C_V, C_GG).astype(BF16)
    gg = proj(C_GG, C_GL)
    la = _log_sigmoid(proj(C_GL, C_END) + ba_ref[...]) * (1.0 / GLA_TAU)

    rr = lax.broadcasted_iota(jnp.int32, (rows, rows), 0)
    cl = lax.broadcasted_iota(jnp.int32, (rows, rows), 1)
    same = (rr // t) == (cl // t)
    ltri = jnp.where(same & (cl <= rr), 1.0, 0.0).astype(BF16)
    lall = jnp.where(same, 1.0, 0.0).astype(BF16)
    hi, lo = _split_bf16(la)
    b = _dot(ltri, hi) + _dot(ltri, lo)
    b_end = _dot(lall, hi) + _dot(lall, lo)
    pump(1)
    qs = (q * jnp.exp(b)).astype(BF16)
    kd = (k * jnp.exp(b_end - b)).astype(BF16)

    assert t & (t - 1) == 0, "tokens per sequence must be a power of two"
    blk = _head_block_mask(rows, GLA_DK, (GLA_HEADS * rows, GLA_KEY_WIDTH))
    a_all = _decayed_scores(q, k, la, b, _score_masks(rows, t.bit_length() - 1), blk)
    pump(2)

    s_old = s_in_ref[...]
    s_bf = s_old.astype(BF16)
    qs3 = qs.reshape(bs, t, GLA_KEY_WIDTH)
    kd3 = kd.reshape(bs, t, GLA_KEY_WIDTH)
    v3 = vb.reshape(bs, t, GLA_WIDTH)
    bdims_nn = (((2,), (1,)), ((0,), (0,)))
    bdims_tn = (((1,), (1,)), ((0,), (0,)))
    assert bs <= GLA_DV
    seq_sel = jnp.where(lax.broadcasted_iota(jnp.int32, (rows, GLA_DV), 0) // t
                        == lax.broadcasted_iota(jnp.int32, (rows, GLA_DV), 1), 1.0, 0.0).astype(BF16)
    decay_t = jnp.exp(_dot_tn(hi, seq_sel) + _dot_tn(lo, seq_sel))
    decay = jnp.stack([jnp.broadcast_to(decay_t[:, j:j + 1], (GLA_KEY_WIDTH, GLA_DV)) for j in range(bs)], axis=0)
    outs, upds = [], []
    for p in range(GLA_HEADS // 2):
        h0, h1 = 2 * p, 2 * p + 1
        intra = _dot(a_all[:, p * 2 * rows:(p + 1) * 2 * rows],
                     _pair_block_diag(vb[:, h0 * GLA_DV:(h0 + 1) * GLA_DV], vb[:, h1 * GLA_DV:(h1 + 1) * GLA_DV]))
        s0 = s_bf[:, h0 * GLA_DK:(h0 + 1) * GLA_DK, :]
        s1 = s_bf[:, h1 * GLA_DK:(h1 + 1) * GLA_DK, :]
        z = jnp.zeros_like(s0)
        s_pair = jnp.concatenate(
            [jnp.concatenate([s0, z], axis=2), jnp.concatenate([z, s1], axis=2)], axis=1)
        inter = lax.dot_general(qs3[:, :, p * 2 * GLA_DK:(p + 1) * 2 * GLA_DK], s_pair, bdims_nn,
                                preferred_element_type=F32)
        outs.append(intra + inter.reshape(rows, 2 * GLA_DV))
        upd = lax.dot_general(kd3[:, :, p * 2 * GLA_DK:(p + 1) * 2 * GLA_DK],
                              v3[:, :, p * 2 * GLA_DV:(p + 1) * 2 * GLA_DV], bdims_tn,
                              preferred_element_type=F32)
        upds.append(upd[:, 0:GLA_DK, 0:GLA_DV])
        upds.append(upd[:, GLA_DK:2 * GLA_DK, GLA_DV:2 * GLA_DV])
        pump(2)
    o = jnp.concatenate(outs, axis=1)
    s_out_ref[...] = s_old * decay + jnp.concatenate(upds, axis=1)

    y = _epilogue(x, pool_act, o, gg, ng_ref, wout_ref, lng_ref, lnb_ref, pump)
    y_ref[...] = y.reshape(bs, t, D_MODEL)


def _sample_body(x_ref, xres_ref, pool_in_ref, s_in_ref, win_ref, wmap_ref, pscale_ref, ba_ref, ng_ref, wout_ref,
                 lng_ref, lnb_ref, y_ref, pool_ref, s_out_ref,
                 proj0_ref, proj1_ref, xb_ref, u_ref, pooled_ref,
                 *, bs, t, start_pos, n_groups):
    i = pl.program_id(0)
    rows = bs * t
    sets = (proj0_ref, proj1_ref)
    units = [(0, lo) for lo in range(0, C_END, MXU_COLS)]

    def step(proj_a, proj_b):
        pump = lambda n=None: None
        if proj_a is not None:
            xb_ref[:, 0:D_MODEL] = x_ref[...].reshape(rows, D_MODEL).astype(BF16)
            pump = _make_pump(units, xb_ref, win_ref, proj_a, rows)
            pump(1 if proj_b is not None else None)
        if proj_b is not None:
            _sample_stage_b(proj_b, xres_ref, pool_in_ref, s_in_ref, wmap_ref, pscale_ref, ba_ref, ng_ref,
                            wout_ref, lng_ref, lnb_ref, y_ref, pool_ref, s_out_ref, u_ref, pooled_ref, pump,
                            bs=bs, t=t, start_pos=start_pos)

    @pl.when(i == 0)
    def _():
        step(sets[0], None)

    for parity in (0, 1):
        @pl.when((i > 0) & (i < n_groups) & (i % 2 == parity))
        def _(parity=parity):
            step(sets[parity], sets[1 - parity])

    @pl.when(i == n_groups)
    def _():
        step(None, sets[(n_groups - 1) % 2])


def _sample_call(x, pool_in, s_in, win, wmap, pscale, ba, ng, wout, lng, lnb, start_pos):
    nb, t, _ = x.shape
    bs = SAMPLE_SEQS
    n_groups = nb // bs
    const2 = lambda i: (0, 0)
    seq3_a = lambda i: (jnp.minimum(i, n_groups - 1), 0, 0)
    seq3 = lambda i: (jnp.maximum(i - 1, 0), 0, 0)
    hist3 = lambda i: (0, jnp.maximum(i - 1, 0), 0)
    return pl.pallas_call(
        functools.partial(_sample_body, bs=bs, t=t, start_pos=start_pos, n_groups=n_groups),
        grid=(n_groups + 1,),
        in_specs=[
            pl.BlockSpec((bs, t, D_MODEL), seq3_a),
            pl.BlockSpec((bs, t, D_MODEL), seq3),
            pl.BlockSpec((POOL_BUF, bs, POOL_WIDTH), hist3),
            pl.BlockSpec((bs, GLA_KEY_WIDTH, GLA_DV), seq3),
            pl.BlockSpec(win.shape, const2),
            pl.BlockSpec(wmap.shape, lambda i: (0, 0, 0)),
            pl.BlockSpec(pscale.shape, const2),
            pl.BlockSpec(ba.shape, const2),
            pl.BlockSpec(ng.shape, const2),
            pl.BlockSpec(wout.shape, const2),
            pl.BlockSpec(lng.shape, const2),
            pl.BlockSpec(lnb.shape, const2),
        ],
        out_specs=[
            pl.BlockSpec((bs, t, D_MODEL), seq3),
            pl.BlockSpec((POOL_BUF, bs, POOL_WIDTH), hist3),
            pl.BlockSpec((bs, GLA_KEY_WIDTH, GLA_DV), seq3),
        ],
        out_shape=[
            jax.ShapeDtypeStruct((nb, t, D_MODEL), F32),
            jax.ShapeDtypeStruct((POOL_BUF, nb, POOL_WIDTH), F32),
            jax.ShapeDtypeStruct((nb, GLA_KEY_WIDTH, GLA_DV), F32),
        ],
        scratch_shapes=[pltpu.VMEM((bs * t, C_END + PITCH_PAD), F32),
                        pltpu.VMEM((bs * t, C_END + PITCH_PAD), F32),
                        pltpu.VMEM((bs * t, D_MODEL + PITCH_PAD), BF16),
                        pltpu.VMEM((len(POOL_WINDOWS), bs * t, POOL_GROUP_DIM), F32),
                        pltpu.VMEM((len(POOL_WINDOWS), bs * t, POOL_GROUP_DIM), F32)],
        compiler_params=pltpu.CompilerParams(
            dimension_semantics=("arbitrary",), vmem_limit_bytes=VMEM_LIMIT),
        name="sample_layer",
    )(x, x, pool_in, s_in, win, wmap, pscale, ba, ng, wout, lng, lnb)


def _prep_body(wt_ref, wfat_ref, wa2_ref, wout_f32_ref, wmap_f32_ref, win_ref, wout_ref, wmap_ref, *, n_plain,
               n_wout):
    i = pl.program_id(0)

    @pl.when(i == 0)
    def _():
        wmap_ref[...] = wmap_f32_ref[...].astype(BF16)

    @pl.when(i < n_wout)
    def _():
        wout_ref[:, 0:D_MODEL] = wout_f32_ref[...].astype(BF16)
        wout_ref[:, D_MODEL:] = jnp.zeros((wout_ref.shape[0], PITCH_PAD), BF16)

    @pl.when(i < n_plain)
    def _():
        win_ref[...] = jnp.transpose(wt_ref[...]).astype(BF16)

    @pl.when(i == n_plain)
    def _():
        fa_hi, fa_lo = _split_bf16(wfat_ref[...])
        a2_hi, a2_lo = _split_bf16(wa2_ref[...])
        acc = _dot_tn(fa_hi, a2_hi) + (_dot_tn(fa_hi, a2_lo) + _dot_tn(fa_lo, a2_hi))
        win_ref[:, 0:GLA_KEY_WIDTH] = acc.astype(BF16)
        win_ref[:, GLA_KEY_WIDTH:] = jnp.zeros((D_MODEL, PREP_COLS - GLA_KEY_WIDTH), BF16)


def _prep_weights(wt, wa2, wout_f32, wmap_f32):
    assert C_GL % PREP_COLS == 0 and C_END + PITCH_PAD - C_GL <= PREP_COLS and C_GL % GLA_RANK == 0
    n_plain = C_GL // PREP_COLS
    n_wout = 4
    wout_rows = wout_f32.shape[0] // n_wout
    assert n_wout <= n_plain + 1
    wout_block = lambda i: (jnp.minimum(i, n_wout - 1), 0)
    return pl.pallas_call(
        functools.partial(_prep_body, n_plain=n_plain, n_wout=n_wout),
        grid=(n_plain + 1,),
        in_specs=[
            pl.BlockSpec((PREP_COLS, D_MODEL), lambda i: (jnp.minimum(i, n_plain - 1), 0)),
            pl.BlockSpec((GLA_RANK, D_MODEL), lambda i: (C_GL // GLA_RANK, 0)),
            pl.BlockSpec(wa2.shape, lambda i: (0, 0)),
            pl.BlockSpec((wout_rows, D_MODEL), wout_block),
            pl.BlockSpec(wmap_f32.shape, lambda i: (0, 0, 0)),
        ],
        out_specs=[
            pl.BlockSpec((D_MODEL, PREP_COLS), lambda i: (0, i)),
            pl.BlockSpec((wout_rows, D_MODEL + PITCH_PAD), wout_block),
            pl.BlockSpec(wmap_f32.shape, lambda i: (0, 0, 0)),
        ],
        out_shape=[
            jax.ShapeDtypeStruct((D_MODEL, C_END + PITCH_PAD), BF16),
            jax.ShapeDtypeStruct((D_MODEL, D_MODEL + PITCH_PAD), BF16),
            jax.ShapeDtypeStruct(wmap_f32.shape, BF16),
        ],
        compiler_params=pltpu.CompilerParams(dimension_semantics=("arbitrary",), vmem_limit_bytes=VMEM_LIMIT),
        name="prep_weights",
    )(wt, wt, wa2, wout_f32, wmap_f32)


def kernel(x_prompt, x_sample, state_pool, state_gla, w_in, w_pool_map, pool_scale, w_gla_a2, b_gla_a,
           gla_norm_g, w_out, ln_g, ln_b):
    assert w_in.shape[0] == DEPTH
    win, wout, wmap = _prep_weights(jnp.transpose(w_in[0]), w_gla_a2[0], w_out[0], w_pool_map[0])
    pscale = pool_scale[0].reshape(1, POOL_WIDTH)
    ba = b_gla_a[0].reshape(1, GLA_KEY_WIDTH)
    ng = gla_norm_g[0].reshape(1, GLA_WIDTH)
    lng = ln_g[0].reshape(1, D_MODEL)
    lnb = ln_b[0].reshape(1, D_MODEL)
    params = (win, wmap, pscale, ba, ng, wout, lng, lnb)

    y_p, pool_p, gla_p = _prompt_call(x_prompt, *params)
    nsamp = x_sample.shape[0]
    y_s, pool_s, gla_s = _sample_call(
        x_sample, jnp.transpose(state_pool[0], (1, 0, 2)),
        state_gla[0].reshape(nsamp, GLA_KEY_WIDTH, GLA_DV), *params, start_pos=PAST_LEN)

    nb = x_prompt.shape[0]
    return (y_p, y_s, jnp.transpose(pool_p, (1, 0, 2))[None],
            gla_p.reshape(1, nb, GLA_HEADS, GLA_DK, GLA_DV),
            jnp.transpose(pool_s, (1, 0, 2))[None],
            gla_s.reshape(1, nsamp, GLA_HEADS, GLA_DK, GLA_DV))
```
